```python
import math
import jax, jax.numpy as jnp
from jax import lax
import numpy as np

D_MODEL = 1024
BATCH = 8
SEQ = 8192
DEPTH = 2

D_MIX = 3 * D_MODEL // 2
W_A = D_MIX // 3
W_B = D_MIX // 3
W_C = D_MIX - W_A - W_B
HEAD_DIM = 64
N_HEADS_A = W_A // HEAD_DIM
N_HEADS_B = W_B // HEAD_DIM
POOL_WINDOWS = (2, 4, 8, 16)
N_POOL_GROUPS = len(POOL_WINDOWS)
POOL_GROUP_DIM = W_C // N_POOL_GROUPS
CONV_A_WIDTH = 3
CONV_B_WIDTH = 31
D_IN = 4 * W_A + 3 * W_B + 2 * W_C
DEEPNORM_ALPHA = (2.0 * DEPTH) ** 0.25
DEEPNORM_BETA = (8.0 * DEPTH) ** -0.25
LN_EPS = 1e-5

kernel_name = "hybrid_conv_pool_deepnorm_trunk"


def layer_norm(x, g, b):
    xf = x.astype(jnp.float32)
    mu = jnp.mean(xf, axis=-1, keepdims=True)
    xc = xf - mu
    var = jnp.mean(xc * xc, axis=-1, keepdims=True)
    y = xc * lax.rsqrt(var + LN_EPS) * g.astype(jnp.float32) + b.astype(jnp.float32)
    return y.astype(x.dtype)


def causal_depthwise_conv(u, w, b):
    k, c = w.shape
    y = lax.conv_general_dilated(
        u, w[:, None, :].astype(u.dtype),
        window_strides=(1,), padding=[(k - 1, 0)],
        dimension_numbers=("NWC", "WIO", "NWC"),
        feature_group_count=c)
    return y + b.astype(u.dtype)


def multiscale_causal_pool(u):
    t = u.shape[1]
    uf = u.astype(jnp.float32)
    cs = jnp.cumsum(uf, axis=1)
    pos = jnp.arange(t, dtype=jnp.float32)[None, :, None]
    outs = []
    for g, w in enumerate(POOL_WINDOWS):
        sl = slice(g * POOL_GROUP_DIM, (g + 1) * POOL_GROUP_DIM)
        cs_g = cs[..., sl]
        cs_shift = jnp.pad(cs_g[:, : t - w], ((0, 0), (w, 0), (0, 0)))
        count = jnp.minimum(pos + 1.0, float(w))
        mean = (cs_g - cs_shift) / count
        outs.append(mean - uf[..., sl])
    return jnp.stack(outs, axis=2).astype(u.dtype)


def hybrid_layer(x, w_in, conv_a_w, conv_a_b, conv_b_w, conv_b_b, ln_b_g, ln_b_b,
                 pool_w, pool_b, pool_scale, w_out, ln_g, ln_b):
    bsz, t, _ = x.shape
    h = jnp.einsum("btd,de->bte", x, w_in)
    splits = np.cumsum([W_A, W_A, W_A, W_A, W_B, W_B, W_B, W_C])
    a_bg, a_cg, a_v, a_z, b_v, b_g, b_z, c_u, c_z = jnp.split(h, splits, axis=-1)

    y_a = a_bg * causal_depthwise_conv(a_cg * a_v, conv_a_w, conv_a_b)
    y_a = y_a * jax.nn.silu(a_z)

    u_b = b_v * jax.nn.sigmoid(b_g)
    u_b = causal_depthwise_conv(u_b, conv_b_w, conv_b_b)
    u_b = jax.nn.silu(layer_norm(u_b, ln_b_g, ln_b_b))
    y_b = u_b * jax.nn.silu(b_z)

    p = multiscale_causal_pool(c_u)
    p = jnp.einsum("btgc,gcd->btgd", p, pool_w) + pool_b
    y_c = p.reshape(bsz, t, W_C) * pool_scale
    y_c = y_c * jax.nn.silu(c_z)

    y = jnp.concatenate([y_a, y_b, y_c], axis=-1)
    out = jnp.einsum("bte,ed->btd", y, w_out)
    return layer_norm(DEEPNORM_ALPHA * x + out, ln_g, ln_b)


def _fwd_setup_inputs(seed: int = 0) -> dict:
    key = jax.random.key(seed)
    ks = jax.random.split(key, 16)
    f32 = jnp.float32
    nrm = lambda k, s, sc: jax.random.normal(k, s, f32) * sc
    return {
        "x": jax.random.normal(ks[0], (BATCH, SEQ, D_MODEL), f32),
        "w_in": nrm(ks[1], (DEPTH, D_MODEL, D_IN), D_MODEL ** -0.5),
        "conv_a_w": nrm(ks[2], (DEPTH, CONV_A_WIDTH, W_A), CONV_A_WIDTH ** -0.5),
        "conv_a_b": nrm(ks[3], (DEPTH, W_A), 0.02),
        "conv_b_w": nrm(ks[4], (DEPTH, CONV_B_WIDTH, W_B), CONV_B_WIDTH ** -0.5),
        "conv_b_b": nrm(ks[5], (DEPTH, W_B), 0.02),
        "ln_b_g": 1.0 + nrm(ks[6], (DEPTH, W_B), 0.02),
        "ln_b_b": nrm(ks[7], (DEPTH, W_B), 0.02),
        "pool_w": nrm(ks[8], (DEPTH, N_POOL_GROUPS, POOL_GROUP_DIM, POOL_GROUP_DIM), POOL_GROUP_DIM ** -0.5),
        "pool_b": nrm(ks[9], (DEPTH, N_POOL_GROUPS, POOL_GROUP_DIM), 0.02),
        "pool_scale": 1.0 + nrm(ks[10], (DEPTH, W_C), 0.02),
        "w_out": nrm(ks[11], (DEPTH, D_MIX, D_MODEL), DEEPNORM_BETA * D_MIX ** -0.5),
        "ln_g": 1.0 + nrm(ks[12], (DEPTH, D_MODEL), 0.02),
        "ln_b": nrm(ks[13], (DEPTH, D_MODEL), 0.02),
    }


def _fwd_reference(x, w_in, conv_a_w, conv_a_b, conv_b_w, conv_b_b, ln_b_g, ln_b_b,
              pool_w, pool_b, pool_scale, w_out, ln_g, ln_b):
    for l in range(DEPTH):
        x = hybrid_layer(x, w_in[l], conv_a_w[l], conv_a_b[l], conv_b_w[l], conv_b_b[l],
                         ln_b_g[l], ln_b_b[l], pool_w[l], pool_b[l], pool_scale[l],
                         w_out[l], ln_g[l], ln_b[l])
    return x


import jax as _jax
import jax.numpy as _jnp

TWIN_FORMAT = 'train_step'
FWD_PARAMS = ['x', 'w_in', 'conv_a_w', 'conv_a_b', 'conv_b_w', 'conv_b_b', 'ln_b_g', 'ln_b_b', 'pool_w', 'pool_b', 'pool_scale', 'w_out', 'ln_g', 'ln_b']
TWIN_WEIGHTS = ['w_in', 'conv_a_w', 'conv_a_b', 'conv_b_w', 'conv_b_b', 'ln_b_g', 'ln_b_b', 'pool_w', 'pool_b', 'pool_scale', 'w_out', 'ln_g', 'ln_b']
TWIN_DIFF_INPUT = 'x'
TWIN_INPUTS = ['x', 'w_in', 'conv_a_w', 'conv_a_b', 'conv_b_w', 'conv_b_b', 'ln_b_g', 'ln_b_b', 'pool_w', 'pool_b', 'pool_scale', 'w_out', 'ln_g', 'ln_b', 'loss_target', 'm_w_in', 'm_conv_a_w', 'm_conv_a_b', 'm_conv_b_w', 'm_conv_b_b', 'm_ln_b_g', 'm_ln_b_b', 'm_pool_w', 'm_pool_b', 'm_pool_scale', 'm_w_out', 'm_ln_g', 'm_ln_b', 'v_w_in', 'v_conv_a_w', 'v_conv_a_b', 'v_conv_b_w', 'v_conv_b_b', 'v_ln_b_g', 'v_ln_b_b', 'v_pool_w', 'v_pool_b', 'v_pool_scale', 'v_w_out', 'v_ln_g', 'v_ln_b']
TWIN_OUTPUTS = ['loss', 'grad_x', 'grad_w_in', 'grad_conv_a_w', 'grad_conv_a_b', 'grad_conv_b_w', 'grad_conv_b_b', 'grad_ln_b_g', 'grad_ln_b_b', 'grad_pool_w', 'grad_pool_b', 'grad_pool_scale', 'grad_w_out', 'grad_ln_g', 'grad_ln_b', 'delta_w_in', 'delta_conv_a_w', 'delta_conv_a_b', 'delta_conv_b_w', 'delta_conv_b_b', 'delta_ln_b_g', 'delta_ln_b_b', 'delta_pool_w', 'delta_pool_b', 'delta_pool_scale', 'delta_w_out', 'delta_ln_g', 'delta_ln_b', 'new_m_w_in', 'new_m_conv_a_w', 'new_m_conv_a_b', 'new_m_conv_b_w', 'new_m_conv_b_b', 'new_m_ln_b_g', 'new_m_ln_b_b', 'new_m_pool_w', 'new_m_pool_b', 'new_m_pool_scale', 'new_m_w_out', 'new_m_ln_g', 'new_m_ln_b', 'new_v_w_in', 'new_v_conv_a_w', 'new_v_conv_a_b', 'new_v_conv_b_w', 'new_v_conv_b_b', 'new_v_ln_b_g', 'new_v_ln_b_b', 'new_v_pool_w', 'new_v_pool_b', 'new_v_pool_scale', 'new_v_w_out', 'new_v_ln_g', 'new_v_ln_b']
TWIN_LEAF_KINDS = {'loss': 'loss', 'grad_x': 'grad_x', 'grad_w_in': 'grad_w', 'grad_conv_a_w': 'grad_w', 'grad_conv_a_b': 'grad_w', 'grad_conv_b_w': 'grad_w', 'grad_conv_b_b': 'grad_w', 'grad_ln_b_g': 'grad_w', 'grad_ln_b_b': 'grad_w', 'grad_pool_w': 'grad_w', 'grad_pool_b': 'grad_w', 'grad_pool_scale': 'grad_w', 'grad_w_out': 'grad_w', 'grad_ln_g': 'grad_w', 'grad_ln_b': 'grad_w', 'delta_w_in': 'delta_w', 'delta_conv_a_w': 'delta_w', 'delta_conv_a_b': 'delta_w', 'delta_conv_b_w': 'delta_w', 'delta_conv_b_b': 'delta_w', 'delta_ln_b_g': 'delta_w', 'delta_ln_b_b': 'delta_w', 'delta_pool_w': 'delta_w', 'delta_pool_b': 'delta_w', 'delta_pool_scale': 'delta_w', 'delta_w_out': 'delta_w', 'delta_ln_g': 'delta_w', 'delta_ln_b': 'delta_w', 'new_m_w_in': 'new_m', 'new_m_conv_a_w': 'new_m', 'new_m_conv_a_b': 'new_m', 'new_m_conv_b_w': 'new_m', 'new_m_conv_b_b': 'new_m', 'new_m_ln_b_g': 'new_m', 'new_m_ln_b_b': 'new_m', 'new_m_pool_w': 'new_m', 'new_m_pool_b': 'new_m', 'new_m_pool_scale': 'new_m', 'new_m_w_out': 'new_m', 'new_m_ln_g': 'new_m', 'new_m_ln_b': 'new_m', 'new_v_w_in': 'new_v', 'new_v_conv_a_w': 'new_v', 'new_v_conv_a_b': 'new_v', 'new_v_conv_b_w': 'new_v', 'new_v_conv_b_b': 'new_v', 'new_v_ln_b_g': 'new_v', 'new_v_ln_b_b': 'new_v', 'new_v_pool_w': 'new_v', 'new_v_pool_b': 'new_v', 'new_v_pool_scale': 'new_v', 'new_v_w_out': 'new_v', 'new_v_ln_g': 'new_v', 'new_v_ln_b': 'new_v'}


def _forward(args):
    return _fwd_reference(*[args[k] for k in FWD_PARAMS])


def _output_shape():
    def fwd():
        inp = _fwd_setup_inputs(0)
        return _fwd_reference(*[inp[k] for k in FWD_PARAMS])
    out = _jax.eval_shape(fwd)
    return out.shape, out.dtype

N_MICROBATCH = 1
ADAM_LR = 0.001
ADAM_B1 = 0.9
ADAM_B2 = 0.999
ADAM_EPS = 1e-08
ADAM_WD = 0.01
ADAM_STEP = 10
PER_EXAMPLE_BATCH_AXIS = {'x': 0, 'loss_target': 0}
SHARED_INPUTS = []
_WEIGHT_DTYPES = {'w_in': _jnp.float32, 'conv_a_w': _jnp.float32, 'conv_a_b': _jnp.float32, 'conv_b_w': _jnp.float32, 'conv_b_b': _jnp.float32, 'ln_b_g': _jnp.float32, 'ln_b_b': _jnp.float32, 'pool_w': _jnp.float32, 'pool_b': _jnp.float32, 'pool_scale': _jnp.float32, 'w_out': _jnp.float32, 'ln_g': _jnp.float32, 'ln_b': _jnp.float32}
MOMENT_SCALE = {'w_in': 3.670521e-02, 'conv_a_w': 4.419482e-02, 'conv_a_b': 4.321492e-02, 'conv_b_w': 2.671033e-02, 'conv_b_b': 5.683133e-02, 'ln_b_g': 3.227756e-02, 'ln_b_b': 3.214283e-02, 'pool_w': 3.789891e-02, 'pool_b': 6.508032e-02, 'pool_scale': 3.762664e-02, 'w_out': 8.991041e-02, 'ln_g': 4.529276e+01, 'ln_b': 1.355220e+00}


def _to_microbatches(a, axis):
    t = _jnp.moveaxis(a, axis, 0)
    t = t.reshape((N_MICROBATCH, t.shape[0] // N_MICROBATCH) + t.shape[1:])
    return _jnp.moveaxis(t, 1, axis + 1)


def setup_inputs(seed: int = 0) -> dict:
    inp = _fwd_setup_inputs(seed)
    key = _jax.random.fold_in(_jax.random.key(seed), 7919)
    shape, _ = _output_shape()
    out = dict(inp)
    out["loss_target"] = _jax.random.normal(_jax.random.fold_in(key, 0), shape, _jnp.float32)
    for i, name in enumerate(TWIN_WEIGHTS):
        w = inp[name].astype(_jnp.float32)
        if MOMENT_SCALE is None:
            s = _jnp.sqrt(_jnp.mean(_jnp.square(w)) + 1e-30)
        else:
            s = MOMENT_SCALE[name]
        km, kv = _jax.random.split(_jax.random.fold_in(key, i + 1))
        out[name] = w
        out["m_" + name] = s * _jax.random.normal(km, w.shape, _jnp.float32)
        out["v_" + name] = (s * s) * _jax.random.uniform(kv, w.shape, _jnp.float32, 0.5, 1.5)
    if N_MICROBATCH > 1:
        for name, axis in PER_EXAMPLE_BATCH_AXIS.items():
            out[name] = _to_microbatches(out[name], axis)
    return {'x': out['x'], 'w_in': out['w_in'], 'conv_a_w': out['conv_a_w'], 'conv_a_b': out['conv_a_b'], 'conv_b_w': out['conv_b_w'], 'conv_b_b': out['conv_b_b'], 'ln_b_g': out['ln_b_g'], 'ln_b_b': out['ln_b_b'], 'pool_w': out['pool_w'], 'pool_b': out['pool_b'], 'pool_scale': out['pool_scale'], 'w_out': out['w_out'], 'ln_g': out['ln_g'], 'ln_b': out['ln_b'], 'loss_target': out['loss_target'], 'm_w_in': out['m_w_in'], 'm_conv_a_w': out['m_conv_a_w'], 'm_conv_a_b': out['m_conv_a_b'], 'm_conv_b_w': out['m_conv_b_w'], 'm_conv_b_b': out['m_conv_b_b'], 'm_ln_b_g': out['m_ln_b_g'], 'm_ln_b_b': out['m_ln_b_b'], 'm_pool_w': out['m_pool_w'], 'm_pool_b': out['m_pool_b'], 'm_pool_scale': out['m_pool_scale'], 'm_w_out': out['m_w_out'], 'm_ln_g': out['m_ln_g'], 'm_ln_b': out['m_ln_b'], 'v_w_in': out['v_w_in'], 'v_conv_a_w': out['v_conv_a_w'], 'v_conv_a_b': out['v_conv_a_b'], 'v_conv_b_w': out['v_conv_b_w'], 'v_conv_b_b': out['v_conv_b_b'], 'v_ln_b_g': out['v_ln_b_g'], 'v_ln_b_b': out['v_ln_b_b'], 'v_pool_w': out['v_pool_w'], 'v_pool_b': out['v_pool_b'], 'v_pool_scale': out['v_pool_scale'], 'v_w_out': out['v_w_out'], 'v_ln_g': out['v_ln_g'], 'v_ln_b': out['v_ln_b']}


def _loss(weights, diff, rest, loss_target):
    with _jax.named_scope("forward"):
        args = {**rest, TWIN_DIFF_INPUT: diff, **{k: w.astype(_WEIGHT_DTYPES[k]) for k, w in weights.items()}}
        y = _forward(args)
    with _jax.named_scope("loss_head"):
        err = _jnp.square(y.astype(_jnp.float32) - loss_target)
        return 0.5 * _jnp.sum(_jnp.mean(err, axis=-1)) if err.ndim else 0.5 * err


def _adamw(w, g, m, v):
    m = ADAM_B1 * m + (1.0 - ADAM_B1) * g
    v = ADAM_B2 * v + (1.0 - ADAM_B2) * _jnp.square(g)
    m_hat = m / (1.0 - ADAM_B1 ** ADAM_STEP)
    v_hat = v / (1.0 - ADAM_B2 ** ADAM_STEP)
    delta = -ADAM_LR * (m_hat / (_jnp.sqrt(v_hat) + ADAM_EPS) + ADAM_WD * w)
    return delta, m, v


def reference(x, w_in, conv_a_w, conv_a_b, conv_b_w, conv_b_b, ln_b_g, ln_b_b, pool_w, pool_b, pool_scale, w_out, ln_g, ln_b, loss_target, m_w_in, m_conv_a_w, m_conv_a_b, m_conv_b_w, m_conv_b_b, m_ln_b_g, m_ln_b_b, m_pool_w, m_pool_b, m_pool_scale, m_w_out, m_ln_g, m_ln_b, v_w_in, v_conv_a_w, v_conv_a_b, v_conv_b_w, v_conv_b_b, v_ln_b_g, v_ln_b_b, v_pool_w, v_pool_b, v_pool_scale, v_w_out, v_ln_g, v_ln_b):
    given = dict(x=x, w_in=w_in, conv_a_w=conv_a_w, conv_a_b=conv_a_b, conv_b_w=conv_b_w, conv_b_b=conv_b_b, ln_b_g=ln_b_g, ln_b_b=ln_b_b, pool_w=pool_w, pool_b=pool_b, pool_scale=pool_scale, w_out=w_out, ln_g=ln_g, ln_b=ln_b, loss_target=loss_target, m_w_in=m_w_in, m_conv_a_w=m_conv_a_w, m_conv_a_b=m_conv_a_b, m_conv_b_w=m_conv_b_w, m_conv_b_b=m_conv_b_b, m_ln_b_g=m_ln_b_g, m_ln_b_b=m_ln_b_b, m_pool_w=m_pool_w, m_pool_b=m_pool_b, m_pool_scale=m_pool_scale, m_w_out=m_w_out, m_ln_g=m_ln_g, m_ln_b=m_ln_b, v_w_in=v_w_in, v_conv_a_w=v_conv_a_w, v_conv_a_b=v_conv_a_b, v_conv_b_w=v_conv_b_w, v_conv_b_b=v_conv_b_b, v_ln_b_g=v_ln_b_g, v_ln_b_b=v_ln_b_b, v_pool_w=v_pool_w, v_pool_b=v_pool_b, v_pool_scale=v_pool_scale, v_w_out=v_w_out, v_ln_g=v_ln_g, v_ln_b=v_ln_b)
    weights = {n: given[n] for n in TWIN_WEIGHTS}
    shared = {n: given[n] for n in SHARED_INPUTS}
    per_example = {n: given[n] for n in ['x']}
    grad_fn = _jax.value_and_grad(_loss, argnums=(0, 1))

    def one_microbatch(ex, loss_target):
        ex = dict(ex)
        diff = ex.pop(TWIN_DIFF_INPUT)
        return grad_fn(weights, diff, {**shared, **ex}, loss_target)

    if N_MICROBATCH == 1:
        loss, (grad_w, grad_x) = one_microbatch(per_example, given["loss_target"])
    else:
        def body(carry, xs):
            loss_sum, grad_sum = carry
            l_k, (gw_k, gx_k) = one_microbatch(xs[0], xs[1])
            with _jax.named_scope("update"):
                return (loss_sum + l_k, _jax.tree.map(_jnp.add, grad_sum, gw_k)), gx_k

        init = (_jnp.zeros((), _jnp.float32), _jax.tree.map(_jnp.zeros_like, weights))
        (loss, grad_w), grad_x = _jax.lax.scan(body, init, (per_example, given["loss_target"]))
    with _jax.named_scope("update"):
        delta_w, new_m, new_v = {}, {}, {}
        for n in TWIN_WEIGHTS:
            delta_w[n], new_m[n], new_v[n] = _adamw(weights[n], grad_w[n], given["m_" + n], given["v_" + n])
    return (loss, grad_x, *[grad_w[n] for n in TWIN_WEIGHTS], *[delta_w[n] for n in TWIN_WEIGHTS],
            *[new_m[n] for n in TWIN_WEIGHTS], *[new_v[n] for n in TWIN_WEIGHTS])
```

```python
import functools

import jax
import jax.numpy as jnp
from jax import lax
from jax.experimental import pallas as pl
from jax.experimental.pallas import tpu as pltpu

F32 = jnp.float32
BF16 = jnp.bfloat16

DEPTH = 2
D_MODEL = 1024
W_MIX = 512
D_IN = 9 * W_MIX
D_MIX = 3 * W_MIX
POOL_WINDOWS = (2, 4, 8, 16)
POOL_DIM = 128
KA = 3
KB = 31
ALPHA = (2.0 * DEPTH) ** 0.25
LN_EPS = 1e-5
ADAM_LR, ADAM_B1, ADAM_B2, ADAM_EPS, ADAM_WD, ADAM_STEP = 0.001, 0.9, 0.999, 1e-08, 0.01, 10

N_DEV = 8
N_CHIP = 4
MESH = pl.DeviceIdType.MESH

HALO = 32
CH = 32
SUB = 8
VMEM_LIMIT = 56 * 1024 * 1024

R_CAB, R_CBB, R_LBG, R_LBB, R_PB, R_PS, R_CAW, R_CBW = 0, 1, 2, 3, 4, 5, 6, 9
N_ROWS = R_CBW + KB
LN_ROWS = 16
SMALL_ROWS = N_ROWS + LN_ROWS


def _cparams(sem, **kw):
    return pltpu.CompilerParams(dimension_semantics=sem, vmem_limit_bytes=VMEM_LIMIT, **kw)


def _sigmoid(v):
    return 1.0 / (1.0 + jnp.exp(-v))


def _fold8(a):
    r, c = a.shape
    return a.reshape(r // SUB, SUB, c).sum(axis=0)


def _dot(a, b):
    return jnp.dot(a, b, preferred_element_type=F32)


def _dot_nt(a, b):
    return lax.dot_general(a, b, (((1,), (1,)), ((), ())), preferred_element_type=F32)


def _dot_tn(a, b):
    return lax.dot_general(a, b, (((0,), (0,)), ((), ())), preferred_element_type=F32)


def _conv_rows(ext_ref, w_ref, out_ref, *, nk, off, rows, reverse):
    def body(c, carry):
        r0 = c * CH
        acc = jnp.zeros((CH, W_MIX), F32)
        for k in range(nk):
            kk = nk - 1 - k if reverse else k
            acc = acc + ext_ref[pl.ds(r0 + (off + k), CH), :] * w_ref[kk:kk + 1, :]
        out_ref[pl.ds(r0, CH), :] = acc
        return carry
    for c in range(rows // CH):
        body(c, 0)


def _conv_wgrad(g_ref, ext_ref, acc_ref, *, nk, off, rows, row0):
    def body(c, carry):
        r0 = c * CH
        g = g_ref[pl.ds(r0, CH), :]
        for k in range(nk):
            a = (row0 + k) * SUB
            acc_ref[a:a + SUB, :] += _fold8(g * ext_ref[pl.ds(r0 + (off + k), CH), :])
        return carry
    for c in range(rows // CH):
        body(c, 0)


def _window_sums(ext_ref, out_ref, *, rows, forward):
    def body(c, carry):
        r0 = c * CH
        for g, w in enumerate(POOL_WINDOWS):
            lanes = slice(g * POOL_DIM, (g + 1) * POOL_DIM)
            acc = jnp.zeros((CH, POOL_DIM), F32)
            for j in range(w):
                off = j if forward else HALO - j
                acc = acc + ext_ref[pl.ds(r0 + off, CH), lanes]
            out_ref[pl.ds(r0, CH), lanes] = acc
        return carry
    for c in range(rows // CH):
        body(c, 0)


def _inv_count(row0, rows):
    t1 = (lax.broadcasted_iota(jnp.int32, (rows, POOL_DIM), 0) + (row0 + 1)).astype(F32)
    return jnp.concatenate([1.0 / jnp.minimum(t1, float(w)) for w in POOL_WINDOWS], axis=1)


def _groups(h_ref):
    return [h_ref[:, k * W_MIX:(k + 1) * W_MIX].astype(F32) for k in range(9)]


def _layer_norm(v, g, b):
    mu = jnp.mean(v, axis=-1, keepdims=True)
    vc = v - mu
    var = jnp.mean(vc * vc, axis=-1, keepdims=True)
    rstd = lax.rsqrt(var + LN_EPS)
    vhat = vc * rstd
    return vhat * g + b, vhat, rstd


def _layer_norm_bwd(g_out, vhat, rstd, g):
    gh = g_out * g
    m1 = jnp.mean(gh, axis=-1, keepdims=True)
    m2 = jnp.mean(gh * vhat, axis=-1, keepdims=True)
    return rstd * (gh - m1 - vhat * m2)


def _pool_linear(pooled, pw_ref, pb_ref):
    outs = []
    for g in range(len(POOL_WINDOWS)):
        lanes = slice(g * POOL_DIM, (g + 1) * POOL_DIM)
        outs.append(_dot(pooled[:, lanes].astype(BF16), pw_ref[g].astype(BF16)))
    return jnp.concatenate(outs, axis=1) + pb_ref[...]


def _in_proj(x, w, name):
    t, d = x.shape
    n = w.shape[1]
    tm, tn = min(t, 1024), 1536

    def body(x_ref, w_ref, o_ref, xb_ref):
        @pl.when(pl.program_id(1) == 0)
        def _():
            xb_ref[...] = x_ref[...].astype(BF16)
        o_ref[...] = _dot(xb_ref[...], w_ref[...]).astype(BF16)

    return pl.pallas_call(
        body, name=name, grid=(t // tm, n // tn),
        in_specs=[pl.BlockSpec((tm, d), lambda i, j: (i, 0)), pl.BlockSpec((d, tn), lambda i, j: (0, j))],
        out_specs=pl.BlockSpec((tm, tn), lambda i, j: (i, j)),
        out_shape=jax.ShapeDtypeStruct((t, n), BF16),
        scratch_shapes=[pltpu.VMEM((tm, d), BF16)],
        compiler_params=_cparams(("arbitrary", "arbitrary")),
    )(x, w)


def _mixer_fwd(h, x, caw, cab, cbw, cbb, lbg, lbb, pw, pb, ps, w_out, lng, lnb, name):
    t = h.shape[0]
    tt = min(t, 256)

    def body(h_ref, x_ref, caw_ref, cab_ref, cbw_ref, cbb_ref, lbg_ref, lbb_ref, pw_ref, pb_ref, ps_ref,
             wo_ref, lng_ref, lnb_ref, y_ref, u2_ref, z_ref, xn_ref, exta, extb, extc, tmp):
        i = pl.program_id(0)

        @pl.when(i == 0)
        def _():
            for e in (exta, extb, extc):
                e[0:HALO, :] = jnp.zeros((HALO, W_MIX), F32)

        @pl.when(i > 0)
        def _():
            for e in (exta, extb, extc):
                e[0:HALO, :] = e[tt:tt + HALO, :]

        a_bg, a_cg, a_v, a_z, b_v, b_g, b_z, c_u, c_z = _groups(h_ref)
        exta[HALO:, :] = a_cg * a_v
        extb[HALO:, :] = b_v * _sigmoid(b_g)
        extc[HALO:, :] = c_u

        _conv_rows(exta, caw_ref, tmp, nk=KA, off=HALO - (KA - 1), rows=tt, reverse=False)
        y_a = a_bg * (tmp[...] + cab_ref[...]) * (a_z * _sigmoid(a_z))
        y_ref[:, 0:W_MIX] = y_a.astype(BF16)

        _conv_rows(extb, cbw_ref, tmp, nk=KB, off=HALO - (KB - 1), rows=tt, reverse=False)
        u2 = tmp[...] + cbb_ref[...]
        u2_ref[...] = u2
        ln, _, _ = _layer_norm(u2, lbg_ref[...], lbb_ref[...])
        y_b = (ln * _sigmoid(ln)) * (b_z * _sigmoid(b_z))
        y_ref[:, W_MIX:2 * W_MIX] = y_b.astype(BF16)

        _window_sums(extc, tmp, rows=tt, forward=False)
        pooled = tmp[...] * _inv_count(i * tt, tt) - c_u
        p = _pool_linear(pooled, pw_ref, pb_ref)
        y_c = p * ps_ref[...] * (c_z * _sigmoid(c_z))
        y_ref[:, 2 * W_MIX:3 * W_MIX] = y_c.astype(BF16)

        out = _dot(y_ref[...], wo_ref[...])
        z = ALPHA * x_ref[...] + out
        z_ref[...] = z
        xn, _, _ = _layer_norm(z, lng_ref[...], lnb_ref[...])
        xn_ref[...] = xn

    row = lambda wd: pl.BlockSpec((tt, wd), lambda i: (i, 0))
    full = lambda a: pl.BlockSpec(a.shape, lambda i: (0,) * a.ndim)
    params = (caw, cab, cbw, cbb, lbg, lbb, pw, pb, ps, w_out, lng, lnb)
    return pl.pallas_call(
        body, name=name, grid=(t // tt,),
        in_specs=[row(D_IN), row(D_MODEL)] + [full(a) for a in params],
        out_specs=[row(D_MIX), row(W_MIX), row(D_MODEL), row(D_MODEL)],
        out_shape=[jax.ShapeDtypeStruct((t, D_MIX), BF16), jax.ShapeDtypeStruct((t, W_MIX), F32),
                   jax.ShapeDtypeStruct((t, D_MODEL), F32), jax.ShapeDtypeStruct((t, D_MODEL), F32)],
        scratch_shapes=[pltpu.VMEM((tt + HALO, W_MIX), F32)] * 3 + [pltpu.VMEM((tt, W_MIX), F32)],
        compiler_params=_cparams(("arbitrary",)),
    )(h, x, *params)


def _loss_grad(xn, target, name):
    t, d = xn.shape
    tt = min(t, 512)

    def body(x_ref, t_ref, g_ref, l_ref):
        @pl.when(pl.program_id(0) == 0)
        def _():
            l_ref[...] = jnp.zeros_like(l_ref)
        e = x_ref[...] - t_ref[...]
        g_ref[...] = e * (1.0 / d)
        l_ref[...] += _fold8(e * e) * (0.5 / d)

    row = pl.BlockSpec((tt, d), lambda i: (i, 0))
    return pl.pallas_call(
        body, name=name, grid=(t // tt,), in_specs=[row, row],
        out_specs=[row, pl.BlockSpec((SUB, d), lambda i: (0, 0))],
        out_shape=[jax.ShapeDtypeStruct((t, d), F32), jax.ShapeDtypeStruct((SUB, d), F32)],
        compiler_params=_cparams(("arbitrary",)),
    )(xn, target)


def _out_proj_bwd(g_xn, z, y, w_out, lng, name):
    t = z.shape[0]
    tt = min(t, 256)

    def body(g_ref, z_ref, y_ref, wo_ref, lng_ref, gz_ref, gy_ref, gwo_ref, gln_ref, accg, accb):
        i = pl.program_id(0)

        @pl.when(i == 0)
        def _():
            gwo_ref[...] = jnp.zeros_like(gwo_ref)
            accg[...] = jnp.zeros_like(accg)
            accb[...] = jnp.zeros_like(accb)

        g = g_ref[...]
        _, zhat, rstd = _layer_norm(z_ref[...], lng_ref[...], 0.0)
        accg[...] += _fold8(g * zhat)
        accb[...] += _fold8(g)
        g_z = _layer_norm_bwd(g, zhat, rstd, lng_ref[...])
        gz_ref[...] = g_z
        gzb = g_z.astype(BF16)
        gy_ref[...] = _dot_nt(gzb, wo_ref[...])
        gwo_ref[...] += _dot_tn(y_ref[...], gzb)

        @pl.when(i == pl.num_programs(0) - 1)
        def _():
            gln_ref[...] = jnp.zeros_like(gln_ref)
            gln_ref[0:1, :] = jnp.sum(accg[...], axis=0, keepdims=True)
            gln_ref[1:2, :] = jnp.sum(accb[...], axis=0, keepdims=True)

    row = lambda wd: pl.BlockSpec((tt, wd), lambda i: (i, 0))
    full = lambda shape: pl.BlockSpec(shape, lambda i: (0,) * len(shape))
    return pl.pallas_call(
        body, name=name, grid=(t // tt,),
        in_specs=[row(D_MODEL), row(D_MODEL), row(D_MIX), full(w_out.shape), full(lng.shape)],
        out_specs=[row(D_MODEL), row(D_MIX), full((D_MIX, D_MODEL)), full((SUB, D_MODEL))],
        out_shape=[jax.ShapeDtypeStruct((t, D_MODEL), F32), jax.ShapeDtypeStruct((t, D_MIX), F32),
                   jax.ShapeDtypeStruct((D_MIX, D_MODEL), F32), jax.ShapeDtypeStruct((SUB, D_MODEL), F32)],
        scratch_shapes=[pltpu.VMEM((SUB, D_MODEL), F32)] * 2,
        compiler_params=_cparams(("arbitrary",)),
    )(g_xn, z, y, w_out, lng)


def _mixer_bwd(h, u2, g_y, caw, cab, cbw, lbg, lbb, pw, pb, ps, name):
    t = h.shape[0]
    tt = min(t, 256)
    n = t // tt
    halo_blocks = tt // HALO

    def body(h_ref, hh_ref, u2_ref, gy_ref, caw_ref, cab_ref, cbw_ref, lbg_ref, lbb_ref, pw_ref, pb_ref, ps_ref,
             gh_ref, rows_ref, gpw_ref, exta, extb, extc, gca, gu2, qx, tmp, tmp2, acc):
        s = pl.program_id(0)
        i = n - 1 - s

        @pl.when(s == 0)
        def _():
            acc[...] = jnp.zeros_like(acc)
            gpw_ref[...] = jnp.zeros_like(gpw_ref)
            for e in (gca, gu2, qx):
                e[tt:tt + HALO, :] = jnp.zeros((HALO, W_MIX), F32)

        @pl.when(s > 0)
        def _():
            for e in (gca, gu2, qx):
                e[tt:tt + HALO, :] = e[0:HALO, :]

        live = (i > 0).astype(F32)
        hh = [hh_ref[:, k * W_MIX:(k + 1) * W_MIX].astype(F32) for k in (1, 2, 4, 5, 7)]
        exta[0:HALO, :] = hh[0] * hh[1] * live
        extb[0:HALO, :] = hh[2] * _sigmoid(hh[3]) * live
        extc[0:HALO, :] = hh[4] * live

        a_bg, a_cg, a_v, a_z, b_v, b_g, b_z, c_u, c_z = _groups(h_ref)
        g_ya = gy_ref[:, 0:W_MIX]
        g_yb = gy_ref[:, W_MIX:2 * W_MIX]
        g_yc = gy_ref[:, 2 * W_MIX:3 * W_MIX]

        def add_row(r, v):
            acc[r * SUB:(r + 1) * SUB, :] += _fold8(v)

        exta[HALO:, :] = a_cg * a_v
        _conv_rows(exta, caw_ref, tmp, nk=KA, off=HALO - (KA - 1), rows=tt, reverse=False)
        ca = tmp[...] + cab_ref[...]
        sg = _sigmoid(a_z)
        s_az = a_z * sg
        t_a = g_ya * a_bg
        gh_ref[:, 0:W_MIX] = (g_ya * ca * s_az).astype(BF16)
        gh_ref[:, 3 * W_MIX:4 * W_MIX] = (t_a * ca * (sg * (1.0 + a_z * (1.0 - sg)))).astype(BF16)
        g_ca = t_a * s_az
        gca[0:tt, :] = g_ca
        add_row(R_CAB, g_ca)
        _conv_wgrad(gca, exta, acc, nk=KA, off=HALO - (KA - 1), rows=tt, row0=R_CAW)
        _conv_rows(gca, caw_ref, tmp, nk=KA, off=0, rows=tt, reverse=True)
        g_pa = tmp[...]
        gh_ref[:, W_MIX:2 * W_MIX] = (g_pa * a_v).astype(BF16)
        gh_ref[:, 2 * W_MIX:3 * W_MIX] = (g_pa * a_cg).astype(BF16)

        sgg = _sigmoid(b_g)
        extb[HALO:, :] = b_v * sgg
        ln, u2hat, rstd = _layer_norm(u2_ref[...], lbg_ref[...], lbb_ref[...])
        sl = _sigmoid(ln)
        u3 = ln * sl
        sz = _sigmoid(b_z)
        s_bz = b_z * sz
        gh_ref[:, 6 * W_MIX:7 * W_MIX] = (g_yb * u3 * (sz * (1.0 + b_z * (1.0 - sz)))).astype(BF16)
        g_ln = g_yb * s_bz * (sl * (1.0 + ln * (1.0 - sl)))
        add_row(R_LBG, g_ln * u2hat)
        add_row(R_LBB, g_ln)
        g_u2 = _layer_norm_bwd(g_ln, u2hat, rstd, lbg_ref[...])
        gu2[0:tt, :] = g_u2
        add_row(R_CBB, g_u2)
        _conv_wgrad(gu2, extb, acc, nk=KB, off=HALO - (KB - 1), rows=tt, row0=R_CBW)
        _conv_rows(gu2, cbw_ref, tmp, nk=KB, off=0, rows=tt, reverse=True)
        g_u1 = tmp[...]
        gh_ref[:, 4 * W_MIX:5 * W_MIX] = (g_u1 * sgg).astype(BF16)
        gh_ref[:, 5 * W_MIX:6 * W_MIX] = (g_u1 * b_v * sgg * (1.0 - sgg)).astype(BF16)

        extc[HALO:, :] = c_u
        _window_sums(extc, tmp, rows=tt, forward=False)
        inv = _inv_count(i * tt, tt)
        pooled = tmp[...] * inv - c_u
        p = _pool_linear(pooled, pw_ref, pb_ref)
        sc = _sigmoid(c_z)
        s_cz = c_z * sc
        scale = ps_ref[...]
        gh_ref[:, 8 * W_MIX:9 * W_MIX] = (g_yc * p * scale * (sc * (1.0 + c_z * (1.0 - sc)))).astype(BF16)
        t_c = g_yc * s_cz
        add_row(R_PS, t_c * p)
        g_p = t_c * scale
        add_row(R_PB, g_p)
        g_pooled = []
        for g in range(len(POOL_WINDOWS)):
            lanes = slice(g * POOL_DIM, (g + 1) * POOL_DIM)
            gpg = g_p[:, lanes].astype(BF16)
            gpw_ref[g] += _dot_tn(pooled[:, lanes].astype(BF16), gpg)
            g_pooled.append(_dot_nt(gpg, pw_ref[g].astype(BF16)))
        g_pooled = jnp.concatenate(g_pooled, axis=1)
        qx[0:tt, :] = g_pooled * inv
        _window_sums(qx, tmp2, rows=tt, forward=True)
        gh_ref[:, 7 * W_MIX:8 * W_MIX] = (tmp2[...] - g_pooled).astype(BF16)

        @pl.when(s == n - 1)
        def _():
            for r in range(N_ROWS):
                rows_ref[r:r + 1, :] = jnp.sum(acc[r * SUB:(r + 1) * SUB, :], axis=0, keepdims=True)

    row = lambda wd: pl.BlockSpec((tt, wd), lambda s: (n - 1 - s, 0))
    halo = pl.BlockSpec((HALO, D_IN), lambda s: (jnp.maximum((n - 1 - s) * halo_blocks - 1, 0), 0))
    full = lambda shape: pl.BlockSpec(shape, lambda s: (0,) * len(shape))
    params = (caw, cab, cbw, lbg, lbb, pw, pb, ps)
    return pl.pallas_call(
        body, name=name, grid=(n,),
        in_specs=[row(D_IN), halo, row(W_MIX), row(D_MIX)] + [full(a.shape) for a in params],
        out_specs=[row(D_IN), full((N_ROWS, W_MIX)), full(pw.shape)],
        out_shape=[jax.ShapeDtypeStruct((t, D_IN), BF16), jax.ShapeDtypeStruct((N_ROWS, W_MIX), F32),
                   jax.ShapeDtypeStruct(pw.shape, F32)],
        scratch_shapes=[pltpu.VMEM((tt + HALO, W_MIX), F32)] * 6 + [pltpu.VMEM((tt, W_MIX), F32)] * 2
        + [pltpu.VMEM((N_ROWS * SUB, W_MIX), F32)],
        compiler_params=_cparams(("arbitrary",)),
    )(h, h, u2, g_y, *params)


def _in_proj_wgrad(x, g_h, name):
    t, d = x.shape
    n = g_h.shape[1]
    tk, tn = min(t, 512), 1536

    def body(x_ref, g_ref, o_ref):
        @pl.when(pl.program_id(1) == 0)
        def _():
            o_ref[...] = jnp.zeros_like(o_ref)
        o_ref[...] += _dot_tn(x_ref[...].astype(BF16), g_ref[...])

    return pl.pallas_call(
        body, name=name, grid=(n // tn, t // tk),
        in_specs=[pl.BlockSpec((tk, d), lambda j, k: (k, 0)), pl.BlockSpec((tk, tn), lambda j, k: (k, j))],
        out_specs=pl.BlockSpec((d, tn), lambda j, k: (0, j)),
        out_shape=jax.ShapeDtypeStruct((d, n), F32),
        compiler_params=_cparams(("arbitrary", "arbitrary")),
    )(x, g_h)


def _in_proj_dgrad(g_h, w, g_z, name):
    t, n = g_h.shape
    d = w.shape[0]
    tm, tk = min(t, 1024), 1536

    def body(g_ref, w_ref, gz_ref, o_ref):
        @pl.when(pl.program_id(1) == 0)
        def _():
            o_ref[...] = ALPHA * gz_ref[...]
        o_ref[...] += _dot_nt(g_ref[...], w_ref[...])

    return pl.pallas_call(
        body, name=name, grid=(t // tm, n // tk),
        in_specs=[pl.BlockSpec((tm, tk), lambda i, k: (i, k)), pl.BlockSpec((d, tk), lambda i, k: (0, k)),
                  pl.BlockSpec((tm, d), lambda i, k: (i, 0))],
        out_specs=pl.BlockSpec((tm, d), lambda i, k: (i, 0)),
        out_shape=jax.ShapeDtypeStruct((t, d), F32),
        compiler_params=_cparams(("arbitrary", "arbitrary")),
    )(g_h, w, g_z)


def _adamw(parts, w, m, v, name):
    p, r, c = parts.shape
    tr = r
    for cand in (512, 256, 128, 64, 32, 16, 8):
        if r % cand == 0 and r > cand:
            tr = cand
            break
    bc1 = 1.0 - ADAM_B1 ** ADAM_STEP
    bc2 = 1.0 - ADAM_B2 ** ADAM_STEP

    def body(p_ref, w_ref, m_ref, v_ref, g_ref, d_ref, nm_ref, nv_ref):
        g = p_ref[0].astype(F32)
        for k in range(1, p):
            g = g + p_ref[k].astype(F32)
        nm = ADAM_B1 * m_ref[...] + (1.0 - ADAM_B1) * g
        nv = ADAM_B2 * v_ref[...] + (1.0 - ADAM_B2) * (g * g)
        g_ref[...] = g
        nm_ref[...] = nm
        nv_ref[...] = nv
        d_ref[...] = -ADAM_LR * ((nm / bc1) / (jnp.sqrt(nv / bc2) + ADAM_EPS) + ADAM_WD * w_ref[...])

    blk = pl.BlockSpec((tr, c), lambda i: (i, 0))
    out = jax.ShapeDtypeStruct((r, c), F32)
    return pl.pallas_call(
        body, name=name, grid=(r // tr,),
        in_specs=[pl.BlockSpec((p, tr, c), lambda i: (0, i, 0)), blk, blk, blk],
        out_specs=[blk] * 4, out_shape=[out] * 4,
        compiler_params=_cparams(("arbitrary",)),
    )(parts, w, m, v)


def _position():
    return lax.axis_index("x"), lax.axis_index("y"), lax.axis_index("c")


def _all_gather(arrs, name):
    na = len(arrs)
    any_spec = pl.BlockSpec(memory_space=pl.ANY)

    def body(*refs):
        src, dst = refs[:na], refs[na:2 * na]
        send_sems, recv_sems, local_sems = refs[2 * na:]
        x, y, c = _position()
        me, sibling = (x, y, c), (x, y, 1 - c)
        chips = [(1 - x, y), (x, 1 - y), (1 - x, 1 - y)]

        def slot(a, dev):
            return dst[a].at[4 * dev[0] + 2 * dev[1] + dev[2]]

        def copy(a, k, block, to, from_src=False):
            return pltpu.make_async_remote_copy(
                src_ref=src[a] if from_src else slot(a, block), dst_ref=slot(a, block),
                send_sem=send_sems.at[a, k], recv_sem=recv_sems.at[a, k], device_id=to, device_id_type=MESH)

        mine = [pltpu.make_async_copy(src[a], slot(a, me), local_sems.at[a]) for a in range(na)]
        for cp in mine:
            cp.start()
        first = []
        for a in range(na):
            first.append(copy(a, 0, me, sibling, from_src=True))
            first += [copy(a, 1 + j, me, (*chip, c), from_src=True) for j, chip in enumerate(chips)]
        for cp in first:
            cp.start()
        passed = []
        for j, chip in enumerate(chips):
            for a in range(na):
                copy(a, 1 + j, (*chip, c), me).wait_recv()
                cp = copy(a, 4 + j, (*chip, c), sibling)
                cp.start()
                passed.append(cp)
        for a in range(na):
            copy(a, 0, sibling, me).wait_recv()
            for j, chip in enumerate(chips):
                copy(a, 4 + j, (*chip, 1 - c), me).wait_recv()
        for cp in first + passed:
            cp.wait_send()
        for cp in mine:
            cp.wait()

    return pl.pallas_call(
        body, name=name,
        in_specs=[any_spec] * na, out_specs=[any_spec] * na,
        out_shape=[jax.ShapeDtypeStruct((N_DEV,) + a.shape, a.dtype) for a in arrs],
        scratch_shapes=[pltpu.SemaphoreType.DMA((na, 7)), pltpu.SemaphoreType.DMA((na, 7)),
                        pltpu.SemaphoreType.DMA((na,))],
    )(*arrs)


def _sibling_swap(arrs, name):
    na = len(arrs)
    any_spec = pl.BlockSpec(memory_space=pl.ANY)

    def body(*refs):
        src, dst = refs[:na], refs[na:2 * na]
        send_sems, recv_sems = refs[2 * na:]
        x, y, c = _position()
        sibling = (x, y, 1 - c)
        copies = []
        for a in range(na):
            for k in range(N_CHIP):
                copies.append(pltpu.make_async_remote_copy(
                    src_ref=src[a].at[2 * k + (1 - c)], dst_ref=dst[a].at[k],
                    send_sem=send_sems.at[a, k], recv_sem=recv_sems.at[a, k], device_id=sibling, device_id_type=MESH))
        for cp in copies:
            cp.start()
        for cp in copies:
            cp.wait()

    return pl.pallas_call(
        body, name=name,
        in_specs=[any_spec] * na, out_specs=[any_spec] * na,
        out_shape=[jax.ShapeDtypeStruct((N_CHIP,) + a.shape[1:], a.dtype) for a in arrs],
        scratch_shapes=[pltpu.SemaphoreType.DMA((na, N_CHIP)), pltpu.SemaphoreType.DMA((na, N_CHIP))],
    )(*arrs)


def _pair_sum(mine, got, core, name):
    _, r, c = mine.shape
    tr = r
    for cand in (512, 256, 128, 64, 32, 16, 8):
        if r % cand == 0 and r > cand:
            tr = cand
            break

    def body(core_ref, a_ref, b_ref, o_ref):
        o_ref[...] = a_ref[...] + b_ref[...]

    return pl.pallas_call(
        body, name=name,
        grid_spec=pltpu.PrefetchScalarGridSpec(
            num_scalar_prefetch=1, grid=(N_CHIP, r // tr),
            in_specs=[pl.BlockSpec((1, tr, c), lambda k, i, core_ref: (2 * k + core_ref[0], i, 0)),
                      pl.BlockSpec((1, tr, c), lambda k, i, core_ref: (k, i, 0))],
            out_specs=pl.BlockSpec((1, tr, c), lambda k, i, core_ref: (k, i, 0))),
        out_shape=jax.ShapeDtypeStruct((N_CHIP, r, c), mine.dtype),
        compiler_params=_cparams(("arbitrary", "arbitrary")),
    )(core, mine, got)


def _chip_exchange(arrs, name):
    na = len(arrs)
    any_spec = pl.BlockSpec(memory_space=pl.ANY)

    def body(*refs):
        src, dst = refs[:na], refs[na:2 * na]
        send_sems, recv_sems, local_sems = refs[2 * na:]
        x, y, c = _position()
        my_chip = 2 * x + y
        chips = [(1 - x, y), (x, 1 - y), (1 - x, 1 - y)]
        mine = [pltpu.make_async_copy(src[a].at[my_chip], dst[a].at[my_chip], local_sems.at[a]) for a in range(na)]
        for cp in mine:
            cp.start()
        copies = []
        for a in range(na):
            for j, chip in enumerate(chips):
                copies.append(pltpu.make_async_remote_copy(
                    src_ref=src[a].at[2 * chip[0] + chip[1]], dst_ref=dst[a].at[my_chip],
                    send_sem=send_sems.at[a, j], recv_sem=recv_sems.at[a, j], device_id=(*chip, c),
                    device_id_type=MESH))
        for cp in copies:
            cp.start()
        for cp in copies:
            cp.wait()
        for cp in mine:
            cp.wait()

    return pl.pallas_call(
        body, name=name,
        in_specs=[any_spec] * na, out_specs=[any_spec] * na,
        out_shape=[jax.ShapeDtypeStruct(a.shape, a.dtype) for a in arrs],
        scratch_shapes=[pltpu.SemaphoreType.DMA((na, 3)), pltpu.SemaphoreType.DMA((na, 3)),
                        pltpu.SemaphoreType.DMA((na,))],
    )(*arrs)


def _by_destination(g, axis):
    d0, d1 = g.shape
    if axis == 0:
        return g.reshape(N_DEV, d0 // N_DEV, d1)
    return g.reshape(d0, N_DEV, d1 // N_DEV).transpose(1, 0, 2)


def _pad_rows(a, rows):
    return jnp.pad(a, ((0, rows - a.shape[0]), (0, 0)))


def kernel(x, w_in, conv_a_w, conv_a_b, conv_b_w, conv_b_b, ln_b_g, ln_b_b, pool_w, pool_b, pool_scale, w_out, ln_g, ln_b, loss_target, m_w_in, m_conv_a_w, m_conv_a_b, m_conv_b_w, m_conv_b_b, m_ln_b_g, m_ln_b_b, m_pool_w, m_pool_b, m_pool_scale, m_w_out, m_ln_g, m_ln_b, v_w_in, v_conv_a_w, v_conv_a_b, v_conv_b_w, v_conv_b_b, v_ln_b_g, v_ln_b_b, v_pool_w, v_pool_b, v_pool_scale, v_w_out, v_ln_g, v_ln_b):
    xc, yc, cc = _position()
    dev = 4 * xc + 2 * yc + cc
    core = jnp.reshape(cc, (1,)).astype(jnp.int32)
    depth = w_in.shape[0]
    cs = conv_a_w.shape[2]
    x0 = x[0]
    target = loss_target[0]

    conv_sh = jnp.concatenate([conv_a_w, conv_b_w], axis=1)
    w_in_g, w_out_g, conv_g = _all_gather([w_in.astype(BF16), w_out.astype(BF16), conv_sh], "gather_weights")
    w_in_f = w_in_g.transpose(1, 2, 0, 3).reshape(depth, D_MODEL, D_IN)
    w_out_f = w_out_g.transpose(1, 0, 2, 3).reshape(depth, D_MIX, D_MODEL)
    conv_f = conv_g.transpose(1, 2, 0, 3).reshape(depth, KA + KB, W_MIX)
    caw_f, cbw_f = conv_f[:, :KA], conv_f[:, KA:]

    r2 = lambda a: a.reshape(1, -1)

    xs, hs, ys, u2s, zs = [x0], [], [], [], []
    for l in range(depth):
        h = _in_proj(xs[l], w_in_f[l], f"in_proj_{l}")
        y, u2, z, xn = _mixer_fwd(
            h, xs[l], caw_f[l], r2(conv_a_b[l]), cbw_f[l], r2(conv_b_b[l]), r2(ln_b_g[l]), r2(ln_b_b[l]),
            pool_w[l], r2(pool_b[l]), r2(pool_scale[l]), w_out_f[l], r2(ln_g[l]), r2(ln_b[l]), f"mixer_fwd_{l}")
        hs.append(h), ys.append(y), u2s.append(u2), zs.append(z), xs.append(xn)

    g, loss_rows = _loss_grad(xs[depth], target, "loss_grad")
    loss = lax.psum(jnp.sum(loss_rows), ("x", "y", "c"))

    g_w_in, g_w_out, g_small, g_pool_w = [None] * depth, [None] * depth, [None] * depth, [None] * depth
    for l in reversed(range(depth)):
        g_z, g_y, g_w_out[l], g_ln = _out_proj_bwd(g, zs[l], ys[l], w_out_f[l], r2(ln_g[l]), f"out_proj_bwd_{l}")
        g_h, rows, g_pool_w[l] = _mixer_bwd(
            hs[l], u2s[l], g_y, caw_f[l], r2(conv_a_b[l]), cbw_f[l], r2(ln_b_g[l]), r2(ln_b_b[l]),
            pool_w[l], r2(pool_b[l]), r2(pool_scale[l]), f"mixer_bwd_{l}")
        g_small[l] = jnp.concatenate([rows, g_ln.reshape(LN_ROWS, W_MIX)], axis=0)
        g_w_in[l] = _in_proj_wgrad(xs[l], g_h, f"in_proj_wgrad_{l}")
        g = _in_proj_dgrad(g_h, w_in_f[l], g_z, f"in_proj_dgrad_{l}")
    grad_x = g[None]

    gi = jnp.concatenate([_by_destination(g_w_in[l], 1) for l in range(depth)], axis=1)
    go = jnp.concatenate([_by_destination(g_w_out[l], 0) for l in range(depth)], axis=1)
    gi_sib, go_sib = _sibling_swap([gi, go], "reduce_swap")
    gi_chip = _pair_sum(gi, gi_sib, core, "pair_sum_w_in")
    go_chip = _pair_sum(go, go_sib, core, "pair_sum_w_out")
    gi_parts, go_parts = _chip_exchange([gi_chip, go_chip], "reduce_chips")

    flat = lambda a: a.reshape(-1, a.shape[-1])
    out_in = _adamw(gi_parts, flat(w_in), flat(m_w_in), flat(v_w_in), "adamw_w_in")
    out_out = _adamw(go_parts, flat(w_out), flat(m_w_out), flat(v_w_out), "adamw_w_out")
    out_in = [a.reshape(w_in.shape) for a in out_in]
    out_out = [a.reshape(w_out.shape) for a in out_out]

    small = jnp.concatenate(g_small, axis=0)
    gpw = jnp.concatenate([a.reshape(-1, POOL_DIM) for a in g_pool_w], axis=0)
    small_parts, gpw_parts = _all_gather([small, gpw], "gather_small_grads")

    def pack_small(tree):
        per_layer = []
        for l in range(depth):
            rows = [tree[k][l].reshape(1, W_MIX) for k in ("conv_a_b", "conv_b_b", "ln_b_g", "ln_b_b", "pool_b", "pool_scale")]
            rows.append(jnp.zeros((N_ROWS - R_CAW, W_MIX), F32))
            lnp = jnp.concatenate([tree["ln_g"][l].reshape(1, D_MODEL), tree["ln_b"][l].reshape(1, D_MODEL),
                                   jnp.zeros((SUB - 2, D_MODEL), F32)], axis=0).reshape(LN_ROWS, W_MIX)
            per_layer.append(jnp.concatenate(rows + [lnp], axis=0))
        return jnp.concatenate(per_layer, axis=0)

    names = ("conv_a_b", "conv_b_b", "ln_b_g", "ln_b_b", "pool_b", "pool_scale", "ln_g", "ln_b")
    w_tree = dict(zip(names, (conv_a_b, conv_b_b, ln_b_g, ln_b_b, pool_b.reshape(depth, W_MIX), pool_scale, ln_g, ln_b)))
    m_tree = dict(zip(names, (m_conv_a_b, m_conv_b_b, m_ln_b_g, m_ln_b_b, m_pool_b.reshape(depth, W_MIX), m_pool_scale, m_ln_g, m_ln_b)))
    v_tree = dict(zip(names, (v_conv_a_b, v_conv_b_b, v_ln_b_g, v_ln_b_b, v_pool_b.reshape(depth, W_MIX), v_pool_scale, v_ln_g, v_ln_b)))
    out_small = _adamw(small_parts, pack_small(w_tree), pack_small(m_tree), pack_small(v_tree), "adamw_small")
    out_pw = _adamw(gpw_parts, flat(pool_w), flat(m_pool_w), flat(v_pool_w), "adamw_pool_w")
    out_pw = [a.reshape(pool_w.shape) for a in out_pw]

    def unpack_small(a):
        a = a.reshape(depth, SMALL_ROWS, W_MIX)
        lnp = a[:, N_ROWS:].reshape(depth, SUB, D_MODEL)
        return dict(conv_a_b=a[:, R_CAB], conv_b_b=a[:, R_CBB], ln_b_g=a[:, R_LBG], ln_b_b=a[:, R_LBB],
                    pool_b=a[:, R_PB].reshape(pool_b.shape), pool_scale=a[:, R_PS], ln_g=lnp[:, 0], ln_b=lnp[:, 1],
                    conv=a[:, R_CAW:N_ROWS])
    small_out = [unpack_small(a) for a in out_small]

    g_conv = lax.dynamic_slice_in_dim(small_out[0]["conv"], dev * cs, cs, axis=2)
    conv_rows = depth * (KA + KB)
    conv_pad = -conv_rows % SUB
    pack_conv = lambda a, b: _pad_rows(jnp.concatenate([a, b], axis=1).reshape(conv_rows, cs), conv_rows + conv_pad)
    out_conv = _adamw(_pad_rows(g_conv.reshape(conv_rows, cs), conv_rows + conv_pad)[None],
                      pack_conv(conv_a_w, conv_b_w), pack_conv(m_conv_a_w, m_conv_b_w),
                      pack_conv(v_conv_a_w, v_conv_b_w), "adamw_conv")
    out_conv = [a[:conv_rows].reshape(depth, KA + KB, cs) for a in out_conv]

    outs = []
    for k in range(4):
        s, cv = small_out[k], out_conv[k]
        outs.append([out_in[k], cv[:, :KA], s["conv_a_b"], cv[:, KA:], s["conv_b_b"], s["ln_b_g"], s["ln_b_b"],
                     out_pw[k], s["pool_b"], s["pool_scale"], out_out[k], s["ln_g"], s["ln_b"]])
    return (loss, grad_x, *outs[0], *outs[1], *outs[2], *outs[3])
```

```python
import jax
import jax.numpy as jnp
from jax import lax
from jax.experimental import pallas as pl
from jax.experimental.pallas import tpu as pltpu

F32 = jnp.float32
BF16 = jnp.bfloat16

DEPTH = 2
D_MODEL = 1024
W_MIX = 512
D_IN = 9 * W_MIX
D_MIX = 3 * W_MIX
POOL_WINDOWS = (2, 4, 8, 16)
POOL_DIM = 128
KA = 3
KB = 31
ALPHA = (2.0 * DEPTH) ** 0.25
LN_EPS = 1e-5
ADAM_LR, ADAM_B1, ADAM_B2, ADAM_EPS, ADAM_WD, ADAM_STEP = 0.001, 0.9, 0.999, 1e-08, 0.01, 10

N_DEV = 8
N_CHIP = 4
MESH = pl.DeviceIdType.MESH

HALO = 32
CH = 32
SUB = 8
VMEM_LIMIT = 56 * 1024 * 1024

R_CAB, R_CBB, R_LBG, R_LBB, R_PB, R_PS, R_CAW, R_CBW = 0, 1, 2, 3, 4, 5, 6, 9
N_ROWS = R_CBW + KB
LN_ROWS = 16
SMALL_ROWS = N_ROWS + LN_ROWS


def _cparams(sem, **kw):
    return pltpu.CompilerParams(dimension_semantics=sem, vmem_limit_bytes=VMEM_LIMIT, **kw)


def _sigmoid(v):
    return 1.0 / (1.0 + jnp.exp(-v))


def _fold8(a):
    r, c = a.shape
    return a.reshape(r // SUB, SUB, c).sum(axis=0)


def _dot(a, b):
    return jnp.dot(a, b, preferred_element_type=F32)


def _dot_nt(a, b):
    return lax.dot_general(a, b, (((1,), (1,)), ((), ())), preferred_element_type=F32)


def _dot_tn(a, b):
    return lax.dot_general(a, b, (((0,), (0,)), ((), ())), preferred_element_type=F32)


def _row_tile(r):
    for cand in (512, 256, 128, 64, 32, 16, 8):
        if r % cand == 0 and r > cand:
            return cand
    return r


def _position():
    return lax.axis_index("x"), lax.axis_index("y"), lax.axis_index("c")


class _Gather:
    def __init__(self, arrs):
        self.arrs = list(arrs)
        na = len(self.arrs)
        self.out_shape = [jax.ShapeDtypeStruct((N_DEV,) + a.shape, a.dtype) for a in self.arrs]
        self.scratch = [pltpu.SemaphoreType.DMA((na, 7)), pltpu.SemaphoreType.DMA((na, 7)),
                        pltpu.SemaphoreType.DMA((na,))]

    def _copies(self, src, dst, sems):
        send_sems, recv_sems, local_sems = sems
        na = len(self.arrs)
        x, y, c = _position()
        me, sibling = (x, y, c), (x, y, 1 - c)
        chips = [(1 - x, y), (x, 1 - y), (1 - x, 1 - y)]

        def slot(a, dev):
            return dst[a].at[4 * dev[0] + 2 * dev[1] + dev[2]]

        def copy(a, k, block, to, from_src=False):
            return pltpu.make_async_remote_copy(
                src_ref=src[a] if from_src else slot(a, block), dst_ref=slot(a, block),
                send_sem=send_sems.at[a, k], recv_sem=recv_sems.at[a, k], device_id=to, device_id_type=MESH)

        mine = [pltpu.make_async_copy(src[a], slot(a, me), local_sems.at[a]) for a in range(na)]
        first, landed, passed, last = [], [], [], []
        for a in range(na):
            first.append(copy(a, 0, me, sibling, from_src=True))
            first += [copy(a, 1 + j, me, (*chip, c), from_src=True) for j, chip in enumerate(chips)]
        for j, chip in enumerate(chips):
            for a in range(na):
                landed.append(copy(a, 1 + j, (*chip, c), me))
                passed.append(copy(a, 4 + j, (*chip, c), sibling))
        for a in range(na):
            last.append(copy(a, 0, sibling, me))
            last += [copy(a, 4 + j, (*chip, 1 - c), me) for j, chip in enumerate(chips)]
        return mine, first, landed, passed, last

    def start(self, src, dst, sems):
        mine, first, _, _, _ = self._copies(src, dst, sems)
        for cp in mine + first:
            cp.start()

    def pass_on(self, src, dst, sems):
        _, _, landed, passed, _ = self._copies(src, dst, sems)
        for got, cp in zip(landed, passed):
            got.wait_recv()
            cp.start()

    def finish(self, src, dst, sems):
        mine, first, _, passed, last = self._copies(src, dst, sems)
        for cp in last:
            cp.wait_recv()
        for cp in first + passed:
            cp.wait_send()
        for cp in mine:
            cp.wait()


class _Exchange:
    def __init__(self, arrs):
        self.arrs = list(arrs)
        na = len(self.arrs)
        self.out_shape = [jax.ShapeDtypeStruct(a.shape, a.dtype) for a in self.arrs]
        self.scratch = [pltpu.SemaphoreType.DMA((na, 3)), pltpu.SemaphoreType.DMA((na, 3)),
                        pltpu.SemaphoreType.DMA((na,))]

    def _copies(self, src, dst, sems):
        send_sems, recv_sems, local_sems = sems
        na = len(self.arrs)
        x, y, c = _position()
        my_chip = 2 * x + y
        chips = [(1 - x, y), (x, 1 - y), (1 - x, 1 - y)]
        mine = [pltpu.make_async_copy(src[a].at[my_chip], dst[a].at[my_chip], local_sems.at[a]) for a in range(na)]
        copies = []
        for a in range(na):
            for j, chip in enumerate(chips):
                copies.append(pltpu.make_async_remote_copy(
                    src_ref=src[a].at[2 * chip[0] + chip[1]], dst_ref=dst[a].at[my_chip],
                    send_sem=send_sems.at[a, j], recv_sem=recv_sems.at[a, j], device_id=(*chip, c),
                    device_id_type=MESH))
        return mine, copies

    def start(self, src, dst, sems):
        mine, copies = self._copies(src, dst, sems)
        for cp in mine + copies:
            cp.start()

    def pass_on(self, src, dst, sems):
        pass

    def finish(self, src, dst, sems):
        mine, copies = self._copies(src, dst, sems)
        for cp in copies:
            cp.wait()
        for cp in mine:
            cp.wait()


def _run_plan(plan, name):
    na = len(plan.arrs)
    any_spec = pl.BlockSpec(memory_space=pl.ANY)

    def body(*refs):
        src, dst, sems = refs[:na], refs[na:2 * na], refs[2 * na:]
        plan.start(src, dst, sems)
        plan.pass_on(src, dst, sems)
        plan.finish(src, dst, sems)

    return pl.pallas_call(
        body, name=name, in_specs=[any_spec] * na, out_specs=[any_spec] * na,
        out_shape=plan.out_shape, scratch_shapes=plan.scratch,
    )(*plan.arrs)


class _Hosted:
    def __init__(self, plan, n_in, n_out, n_scratch):
        self.plan, self.n_in, self.n_out, self.n_scratch = plan, n_in, n_out, n_scratch
        any_spec = pl.BlockSpec(memory_space=pl.ANY)
        k = 0 if plan is None else len(plan.arrs)
        self.operands = [] if plan is None else plan.arrs
        self.in_specs = [any_spec] * k
        self.out_specs = [any_spec] * k
        self.out_shape = [] if plan is None else plan.out_shape
        self.scratch = [] if plan is None else plan.scratch

    def wrap(self, body, phase):
        if self.plan is None:
            return body
        plan, k = self.plan, len(self.plan.arrs)
        i0, o0 = self.n_in, self.n_in + k
        o1 = o0 + self.n_out
        s0 = o1 + k
        s1 = s0 + self.n_scratch

        def hosted(*refs):
            src, dst, sems = refs[i0:o0], refs[o1:s0], refs[s1:]
            first, middle, last = phase()
            pl.when(first)(lambda: plan.start(src, dst, sems))
            body(*refs[:i0], *refs[o0:o1], *refs[s0:s1])
            pl.when(middle)(lambda: plan.pass_on(src, dst, sems))
            pl.when(last)(lambda: plan.finish(src, dst, sems))

        return hosted


def _sibling_swap(arrs, name):
    na = len(arrs)
    any_spec = pl.BlockSpec(memory_space=pl.ANY)

    def body(*refs):
        src, dst = refs[:na], refs[na:2 * na]
        send_sems, recv_sems = refs[2 * na:]
        x, y, c = _position()
        sibling = (x, y, 1 - c)
        copies = []
        for a in range(na):
            for k in range(N_CHIP):
                copies.append(pltpu.make_async_remote_copy(
                    src_ref=src[a].at[2 * k + (1 - c)], dst_ref=dst[a].at[k],
                    send_sem=send_sems.at[a, k], recv_sem=recv_sems.at[a, k], device_id=sibling, device_id_type=MESH))
        for cp in copies:
            cp.start()
        for cp in copies:
            cp.wait()

    return pl.pallas_call(
        body, name=name,
        in_specs=[any_spec] * na, out_specs=[any_spec] * na,
        out_shape=[jax.ShapeDtypeStruct((N_CHIP,) + a.shape[1:], a.dtype) for a in arrs],
        scratch_shapes=[pltpu.SemaphoreType.DMA((na, N_CHIP)), pltpu.SemaphoreType.DMA((na, N_CHIP))],
    )(*arrs)


def _pair_sum(mine, got, core, name):
    _, r, c = mine.shape
    tr = _row_tile(r)

    def body(core_ref, a_ref, b_ref, o_ref):
        o_ref[...] = a_ref[...] + b_ref[...]

    return pl.pallas_call(
        body, name=name,
        grid_spec=pltpu.PrefetchScalarGridSpec(
            num_scalar_prefetch=1, grid=(N_CHIP, r // tr),
            in_specs=[pl.BlockSpec((1, tr, c), lambda k, i, core_ref: (2 * k + core_ref[0], i, 0)),
                      pl.BlockSpec((1, tr, c), lambda k, i, core_ref: (k, i, 0))],
            out_specs=pl.BlockSpec((1, tr, c), lambda k, i, core_ref: (k, i, 0))),
        out_shape=jax.ShapeDtypeStruct((N_CHIP, r, c), mine.dtype),
        compiler_params=_cparams(("arbitrary", "arbitrary")),
    )(core, mine, got)


def _conv_rows(ext_ref, w_ref, out_ref, *, nk, off, rows, reverse):
    def body(c, carry):
        r0 = c * CH
        acc = jnp.zeros((CH, W_MIX), F32)
        for k in range(nk):
            kk = nk - 1 - k if reverse else k
            acc = acc + ext_ref[pl.ds(r0 + (off + k), CH), :] * w_ref[kk:kk + 1, :]
        out_ref[pl.ds(r0, CH), :] = acc
        return carry
    for c in range(rows // CH):
        body(c, 0)


def _conv_wgrad(g_ref, ext_ref, acc_ref, *, nk, off, rows, row0):
    def body(c, carry):
        r0 = c * CH
        g = g_ref[pl.ds(r0, CH), :]
        for k in range(nk):
            a = (row0 + k) * SUB
            acc_ref[a:a + SUB, :] += _fold8(g * ext_ref[pl.ds(r0 + (off + k), CH), :])
        return carry
    for c in range(rows // CH):
        body(c, 0)


def _window_sums(ext_ref, out_ref, *, rows, forward):
    def body(c, carry):
        r0 = c * CH
        for g, w in enumerate(POOL_WINDOWS):
            lanes = slice(g * POOL_DIM, (g + 1) * POOL_DIM)
            acc = jnp.zeros((CH, POOL_DIM), F32)
            for j in range(w):
                off = j if forward else HALO - j
                acc = acc + ext_ref[pl.ds(r0 + off, CH), lanes]
            out_ref[pl.ds(r0, CH), lanes] = acc
        return carry
    for c in range(rows // CH):
        body(c, 0)


def _inv_count(row0, rows):
    t1 = (lax.broadcasted_iota(jnp.int32, (rows, POOL_DIM), 0) + (row0 + 1)).astype(F32)
    return jnp.concatenate([1.0 / jnp.minimum(t1, float(w)) for w in POOL_WINDOWS], axis=1)


def _groups(h_ref):
    return [h_ref[:, k * W_MIX:(k + 1) * W_MIX].astype(F32) for k in range(9)]


def _layer_norm(v, g, b):
    mu = jnp.mean(v, axis=-1, keepdims=True)
    vc = v - mu
    var = jnp.mean(vc * vc, axis=-1, keepdims=True)
    rstd = lax.rsqrt(var + LN_EPS)
    vhat = vc * rstd
    return vhat * g + b, vhat, rstd


def _layer_norm_bwd(g_out, vhat, rstd, g):
    gh = g_out * g
    m1 = jnp.mean(gh, axis=-1, keepdims=True)
    m2 = jnp.mean(gh * vhat, axis=-1, keepdims=True)
    return rstd * (gh - m1 - vhat * m2)


def _pool_linear(pooled, pw_ref, pb_ref):
    outs = []
    for g in range(len(POOL_WINDOWS)):
        lanes = slice(g * POOL_DIM, (g + 1) * POOL_DIM)
        outs.append(_dot(pooled[:, lanes].astype(BF16), pw_ref[g].astype(BF16)))
    return jnp.concatenate(outs, axis=1) + pb_ref[...]


def _in_proj(x, w, name):
    t, d = x.shape
    n = w.shape[1]
    tm, tn = min(t, 1024), 1536

    def body(x_ref, w_ref, o_ref, xb_ref):
        @pl.when(pl.program_id(1) == 0)
        def _():
            xb_ref[...] = x_ref[...].astype(BF16)
        o_ref[...] = _dot(xb_ref[...], w_ref[...]).astype(BF16)

    return pl.pallas_call(
        body, name=name, grid=(t // tm, n // tn),
        in_specs=[pl.BlockSpec((tm, d), lambda i, j: (i, 0)), pl.BlockSpec((d, tn), lambda i, j: (0, j))],
        out_specs=pl.BlockSpec((tm, tn), lambda i, j: (i, j)),
        out_shape=jax.ShapeDtypeStruct((t, n), BF16),
        scratch_shapes=[pltpu.VMEM((tm, d), BF16)],
        compiler_params=_cparams(("arbitrary", "arbitrary")),
    )(x, w)


def _mixer_fwd(h, x, caw, cab, cbw, cbb, lbg, lbb, pw, pb, ps, w_out, lng, lnb, name, plan=None):
    t = h.shape[0]
    tt = min(t, 256)
    n = t // tt

    def body(h_ref, x_ref, caw_ref, cab_ref, cbw_ref, cbb_ref, lbg_ref, lbb_ref, pw_ref, pb_ref, ps_ref,
             wo_ref, lng_ref, lnb_ref, y_ref, u2_ref, z_ref, xn_ref, exta, extb, extc, tmp):
        i = pl.program_id(0)

        @pl.when(i == 0)
        def _():
            for e in (exta, extb, extc):
                e[0:HALO, :] = jnp.zeros((HALO, W_MIX), F32)

        @pl.when(i > 0)
        def _():
            for e in (exta, extb, extc):
                e[0:HALO, :] = e[tt:tt + HALO, :]

        a_bg, a_cg, a_v, a_z, b_v, b_g, b_z, c_u, c_z = _groups(h_ref)
        exta[HALO:, :] = a_cg * a_v
        extb[HALO:, :] = b_v * _sigmoid(b_g)
        extc[HALO:, :] = c_u

        _conv_rows(exta, caw_ref, tmp, nk=KA, off=HALO - (KA - 1), rows=tt, reverse=False)
        y_a = a_bg * (tmp[...] + cab_ref[...]) * (a_z * _sigmoid(a_z))
        y_ref[:, 0:W_MIX] = y_a.astype(BF16)

        _conv_rows(extb, cbw_ref, tmp, nk=KB, off=HALO - (KB - 1), rows=tt, reverse=False)
        u2 = tmp[...] + cbb_ref[...]
        u2_ref[...] = u2
        ln, _, _ = _layer_norm(u2, lbg_ref[...], lbb_ref[...])
        y_b = (ln * _sigmoid(ln)) * (b_z * _sigmoid(b_z))
        y_ref[:, W_MIX:2 * W_MIX] = y_b.astype(BF16)

        _window_sums(extc, tmp, rows=tt, forward=False)
        pooled = tmp[...] * _inv_count(i * tt, tt) - c_u
        p = _pool_linear(pooled, pw_ref, pb_ref)
        y_c = p * ps_ref[...] * (c_z * _sigmoid(c_z))
        y_ref[:, 2 * W_MIX:3 * W_MIX] = y_c.astype(BF16)

        out = _dot(y_ref[...], wo_ref[...])
        z = ALPHA * x_ref[...] + out
        z_ref[...] = z
        xn, _, _ = _layer_norm(z, lng_ref[...], lnb_ref[...])
        xn_ref[...] = xn

    def phase():
        i = pl.program_id(0)
        return i == 0, i == n // 2, i == n - 1

    row = lambda wd: pl.BlockSpec((tt, wd), lambda i: (i, 0))
    full = lambda a: pl.BlockSpec(a.shape, lambda i: (0,) * a.ndim)
    params = (caw, cab, cbw, cbb, lbg, lbb, pw, pb, ps, w_out, lng, lnb)
    host = _Hosted(plan, n_in=2 + len(params), n_out=4, n_scratch=4)
    return pl.pallas_call(
        host.wrap(body, phase), name=name, grid=(n,),
        in_specs=[row(D_IN), row(D_MODEL)] + [full(a) for a in params] + host.in_specs,
        out_specs=[row(D_MIX), row(W_MIX), row(D_MODEL), row(D_MODEL)] + host.out_specs,
        out_shape=[jax.ShapeDtypeStruct((t, D_MIX), BF16), jax.ShapeDtypeStruct((t, W_MIX), F32),
                   jax.ShapeDtypeStruct((t, D_MODEL), F32), jax.ShapeDtypeStruct((t, D_MODEL), F32)] + host.out_shape,
        scratch_shapes=[pltpu.VMEM((tt + HALO, W_MIX), F32)] * 3 + [pltpu.VMEM((tt, W_MIX), F32)] + host.scratch,
        compiler_params=_cparams(("arbitrary",)),
    )(h, x, *params, *host.operands)


def _loss_grad(xn, target, name):
    t, d = xn.shape
    tt = min(t, 512)

    def body(x_ref, t_ref, g_ref, l_ref):
        @pl.when(pl.program_id(0) == 0)
        def _():
            l_ref[...] = jnp.zeros_like(l_ref)
        e = x_ref[...] - t_ref[...]
        g_ref[...] = e * (1.0 / d)
        l_ref[...] += _fold8(e * e) * (0.5 / d)

    row = pl.BlockSpec((tt, d), lambda i: (i, 0))
    return pl.pallas_call(
        body, name=name, grid=(t // tt,), in_specs=[row, row],
        out_specs=[row, pl.BlockSpec((SUB, d), lambda i: (0, 0))],
        out_shape=[jax.ShapeDtypeStruct((t, d), F32), jax.ShapeDtypeStruct((SUB, d), F32)],
        compiler_params=_cparams(("arbitrary",)),
    )(xn, target)


def _out_proj_bwd(g_xn, z, y, w_out, lng, name):
    t = z.shape[0]
    tt = min(t, 256)

    def body(g_ref, z_ref, y_ref, wo_ref, lng_ref, gz_ref, gy_ref, gwo_ref, gln_ref, accg, accb):
        i = pl.program_id(0)

        @pl.when(i == 0)
        def _():
            gwo_ref[...] = jnp.zeros_like(gwo_ref)
            accg[...] = jnp.zeros_like(accg)
            accb[...] = jnp.zeros_like(accb)

        g = g_ref[...]
        _, zhat, rstd = _layer_norm(z_ref[...], lng_ref[...], 0.0)
        accg[...] += _fold8(g * zhat)
        accb[...] += _fold8(g)
        g_z = _layer_norm_bwd(g, zhat, rstd, lng_ref[...])
        gz_ref[...] = g_z
        gzb = g_z.astype(BF16)
        gy_ref[...] = _dot_nt(gzb, wo_ref[...])
        gwo_ref[...] += _dot_tn(y_ref[...], gzb)

        @pl.when(i == pl.num_programs(0) - 1)
        def _():
            gln_ref[...] = jnp.zeros_like(gln_ref)
            gln_ref[0:1, :] = jnp.sum(accg[...], axis=0, keepdims=True)
            gln_ref[1:2, :] = jnp.sum(accb[...], axis=0, keepdims=True)

    row = lambda wd: pl.BlockSpec((tt, wd), lambda i: (i, 0))
    full = lambda shape: pl.BlockSpec(shape, lambda i: (0,) * len(shape))
    return pl.pallas_call(
        body, name=name, grid=(t // tt,),
        in_specs=[row(D_MODEL), row(D_MODEL), row(D_MIX), full(w_out.shape), full(lng.shape)],
        out_specs=[row(D_MODEL), row(D_MIX), full((D_MIX, D_MODEL)), full((SUB, D_MODEL))],
        out_shape=[jax.ShapeDtypeStruct((t, D_MODEL), F32), jax.ShapeDtypeStruct((t, D_MIX), F32),
                   jax.ShapeDtypeStruct((D_MIX, D_MODEL), F32), jax.ShapeDtypeStruct((SUB, D_MODEL), F32)],
        scratch_shapes=[pltpu.VMEM((SUB, D_MODEL), F32)] * 2,
        compiler_params=_cparams(("arbitrary",)),
    )(g_xn, z, y, w_out, lng)


def _mixer_bwd(h, u2, g_y, caw, cab, cbw, lbg, lbb, pw, pb, ps, name, plan=None):
    t = h.shape[0]
    tt = min(t, 256)
    n = t // tt
    halo_blocks = tt // HALO

    def body(h_ref, hh_ref, u2_ref, gy_ref, caw_ref, cab_ref, cbw_ref, lbg_ref, lbb_ref, pw_ref, pb_ref, ps_ref,
             gh_ref, rows_ref, gpw_ref, exta, extb, extc, gca, gu2, qx, tmp, tmp2, acc):
        s = pl.program_id(0)
        i = n - 1 - s

        @pl.when(s == 0)
        def _():
            acc[...] = jnp.zeros_like(acc)
            gpw_ref[...] = jnp.zeros_like(gpw_ref)
            for e in (gca, gu2, qx):
                e[tt:tt + HALO, :] = jnp.zeros((HALO, W_MIX), F32)

        @pl.when(s > 0)
        def _():
            for e in (gca, gu2, qx):
                e[tt:tt + HALO, :] = e[0:HALO, :]

        live = (i > 0).astype(F32)
        hh = [hh_ref[:, k * W_MIX:(k + 1) * W_MIX].astype(F32) for k in (1, 2, 4, 5, 7)]
        exta[0:HALO, :] = hh[0] * hh[1] * live
        extb[0:HALO, :] = hh[2] * _sigmoid(hh[3]) * live
        extc[0:HALO, :] = hh[4] * live

        a_bg, a_cg, a_v, a_z, b_v, b_g, b_z, c_u, c_z = _groups(h_ref)
        g_ya = gy_ref[:, 0:W_MIX]
        g_yb = gy_ref[:, W_MIX:2 * W_MIX]
        g_yc = gy_ref[:, 2 * W_MIX:3 * W_MIX]

        def add_row(r, v):
            acc[r * SUB:(r + 1) * SUB, :] += _fold8(v)

        exta[HALO:, :] = a_cg * a_v
        _conv_rows(exta, caw_ref, tmp, nk=KA, off=HALO - (KA - 1), rows=tt, reverse=False)
        ca = tmp[...] + cab_ref[...]
        sg = _sigmoid(a_z)
        s_az = a_z * sg
        t_a = g_ya * a_bg
        gh_ref[:, 0:W_MIX] = (g_ya * ca * s_az).astype(BF16)
        gh_ref[:, 3 * W_MIX:4 * W_MIX] = (t_a * ca * (sg * (1.0 + a_z * (1.0 - sg)))).astype(BF16)
        g_ca = t_a * s_az
        gca[0:tt, :] = g_ca
        add_row(R_CAB, g_ca)
        _conv_wgrad(gca, exta, acc, nk=KA, off=HALO - (KA - 1), rows=tt, row0=R_CAW)
        _conv_rows(gca, caw_ref, tmp, nk=KA, off=0, rows=tt, reverse=True)
        g_pa = tmp[...]
        gh_ref[:, W_MIX:2 * W_MIX] = (g_pa * a_v).astype(BF16)
        gh_ref[:, 2 * W_MIX:3 * W_MIX] = (g_pa * a_cg).astype(BF16)

        sgg = _sigmoid(b_g)
        extb[HALO:, :] = b_v * sgg
        ln, u2hat, rstd = _layer_norm(u2_ref[...], lbg_ref[...], lbb_ref[...])
        sl = _sigmoid(ln)
        u3 = ln * sl
        sz = _sigmoid(b_z)
        s_bz = b_z * sz
        gh_ref[:, 6 * W_MIX:7 * W_MIX] = (g_yb * u3 * (sz * (1.0 + b_z * (1.0 - sz)))).astype(BF16)
        g_ln = g_yb * s_bz * (sl * (1.0 + ln * (1.0 - sl)))
        add_row(R_LBG, g_ln * u2hat)
        add_row(R_LBB, g_ln)
        g_u2 = _layer_norm_bwd(g_ln, u2hat, rstd, lbg_ref[...])
        gu2[0:tt, :] = g_u2
        add_row(R_CBB, g_u2)
        _conv_wgrad(gu2, extb, acc, nk=KB, off=HALO - (KB - 1), rows=tt, row0=R_CBW)
        _conv_rows(gu2, cbw_ref, tmp, nk=KB, off=0, rows=tt, reverse=True)
        g_u1 = tmp[...]
        gh_ref[:, 4 * W_MIX:5 * W_MIX] = (g_u1 * sgg).astype(BF16)
        gh_ref[:, 5 * W_MIX:6 * W_MIX] = (g_u1 * b_v * sgg * (1.0 - sgg)).astype(BF16)

        extc[HALO:, :] = c_u
        _window_sums(extc, tmp, rows=tt, forward=False)
        inv = _inv_count(i * tt, tt)
        pooled = tmp[...] * inv - c_u
        p = _pool_linear(pooled, pw_ref, pb_ref)
        sc = _sigmoid(c_z)
        s_cz = c_z * sc
        scale = ps_ref[...]
        gh_ref[:, 8 * W_MIX:9 * W_MIX] = (g_yc * p * scale * (sc * (1.0 + c_z * (1.0 - sc)))).astype(BF16)
        t_c = g_yc * s_cz
        add_row(R_PS, t_c * p)
        g_p = t_c * scale
        add_row(R_PB, g_p)
        g_pooled = []
        for g in range(len(POOL_WINDOWS)):
            lanes = slice(g * POOL_DIM, (g + 1) * POOL_DIM)
            gpg = g_p[:, lanes].astype(BF16)
            gpw_ref[g] += _dot_tn(pooled[:, lanes].astype(BF16), gpg)
            g_pooled.append(_dot_nt(gpg, pw_ref[g].astype(BF16)))
        g_pooled = jnp.concatenate(g_pooled, axis=1)
        qx[0:tt, :] = g_pooled * inv
        _window_sums(qx, tmp2, rows=tt, forward=True)
        gh_ref[:, 7 * W_MIX:8 * W_MIX] = (tmp2[...] - g_pooled).astype(BF16)

        @pl.when(s == n - 1)
        def _():
            for r in range(N_ROWS):
                rows_ref[r:r + 1, :] = jnp.sum(acc[r * SUB:(r + 1) * SUB, :], axis=0, keepdims=True)

    def phase():
        s = pl.program_id(0)
        return s == 0, s == n // 2, s == n - 1

    row = lambda wd: pl.BlockSpec((tt, wd), lambda s: (n - 1 - s, 0))
    halo = pl.BlockSpec((HALO, D_IN), lambda s: (jnp.maximum((n - 1 - s) * halo_blocks - 1, 0), 0))
    full = lambda shape: pl.BlockSpec(shape, lambda s: (0,) * len(shape))
    params = (caw, cab, cbw, lbg, lbb, pw, pb, ps)
    host = _Hosted(plan, n_in=4 + len(params), n_out=3, n_scratch=9)
    return pl.pallas_call(
        host.wrap(body, phase), name=name, grid=(n,),
        in_specs=[row(D_IN), halo, row(W_MIX), row(D_MIX)] + [full(a.shape) for a in params] + host.in_specs,
        out_specs=[row(D_IN), full((N_ROWS, W_MIX)), full(pw.shape)] + host.out_specs,
        out_shape=[jax.ShapeDtypeStruct((t, D_IN), BF16), jax.ShapeDtypeStruct((N_ROWS, W_MIX), F32),
                   jax.ShapeDtypeStruct(pw.shape, F32)] + host.out_shape,
        scratch_shapes=[pltpu.VMEM((tt + HALO, W_MIX), F32)] * 6 + [pltpu.VMEM((tt, W_MIX), F32)] * 2
        + [pltpu.VMEM((N_ROWS * SUB, W_MIX), F32)] + host.scratch,
        compiler_params=_cparams(("arbitrary",)),
    )(h, h, u2, g_y, *params, *host.operands)


def _in_proj_wgrad(x, g_h, name, plan=None):
    t, d = x.shape
    n = g_h.shape[1]
    nb = n // N_DEV
    per = N_DEV // 2
    tk, tn = min(t, 512), per * nb
    nk = t // tk

    def body(x_ref, g_ref, o_ref, acc):
        k = pl.program_id(1)

        @pl.when(k == 0)
        def _():
            acc[...] = jnp.zeros_like(acc)
        acc[...] += _dot_tn(x_ref[...].astype(BF16), g_ref[...])

        @pl.when(k == nk - 1)
        def _():
            for b in range(per):
                o_ref[b] = acc[:, b * nb:(b + 1) * nb]

    def phase():
        step = pl.program_id(0) * nk + pl.program_id(1)
        return step == 0, step == nk, step == 2 * nk - 1

    host = _Hosted(plan, n_in=2, n_out=1, n_scratch=1)
    return pl.pallas_call(
        host.wrap(body, phase), name=name, grid=(n // tn, nk),
        in_specs=[pl.BlockSpec((tk, d), lambda j, k: (k, 0)), pl.BlockSpec((tk, tn), lambda j, k: (k, j))]
        + host.in_specs,
        out_specs=[pl.BlockSpec((per, d, nb), lambda j, k: (j, 0, 0))] + host.out_specs,
        out_shape=[jax.ShapeDtypeStruct((N_DEV, d, nb), F32)] + host.out_shape,
        scratch_shapes=[pltpu.VMEM((d, tn), F32)] + host.scratch,
        compiler_params=_cparams(("arbitrary", "arbitrary")),
    )(x, g_h, *host.operands)


def _in_proj_dgrad(g_h, w, g_z, name, plan=None):
    t, n = g_h.shape
    d = w.shape[0]
    tm, tk = min(t, 1024), 1536
    nm, nk = t // tm, n // tk

    def body(g_ref, w_ref, gz_ref, o_ref):
        @pl.when(pl.program_id(1) == 0)
        def _():
            o_ref[...] = ALPHA * gz_ref[...]
        o_ref[...] += _dot_nt(g_ref[...], w_ref[...])

    def phase():
        step = pl.program_id(0) * nk + pl.program_id(1)
        return step == 0, step == (nm * nk) // 2, step == nm * nk - 1

    host = _Hosted(plan, n_in=3, n_out=1, n_scratch=0)
    return pl.pallas_call(
        host.wrap(body, phase), name=name, grid=(nm, nk),
        in_specs=[pl.BlockSpec((tm, tk), lambda i, k: (i, k)), pl.BlockSpec((d, tk), lambda i, k: (0, k)),
                  pl.BlockSpec((tm, d), lambda i, k: (i, 0))] + host.in_specs,
        out_specs=[pl.BlockSpec((tm, d), lambda i, k: (i, 0))] + host.out_specs,
        out_shape=[jax.ShapeDtypeStruct((t, d), F32)] + host.out_shape,
        scratch_shapes=host.scratch,
        compiler_params=_cparams(("arbitrary", "arbitrary")),
    )(g_h, w, g_z, *host.operands)


def _adamw(parts, w, m, v, name):
    p, r, c = parts.shape
    tr = _row_tile(r)
    bc1 = 1.0 - ADAM_B1 ** ADAM_STEP
    bc2 = 1.0 - ADAM_B2 ** ADAM_STEP

    def body(p_ref, w_ref, m_ref, v_ref, g_ref, d_ref, nm_ref, nv_ref):
        g = p_ref[0].astype(F32)
        for k in range(1, p):
            g = g + p_ref[k].astype(F32)
        nm = ADAM_B1 * m_ref[...] + (1.0 - ADAM_B1) * g
        nv = ADAM_B2 * v_ref[...] + (1.0 - ADAM_B2) * (g * g)
        g_ref[...] = g
        nm_ref[...] = nm
        nv_ref[...] = nv
        d_ref[...] = -ADAM_LR * ((nm / bc1) / (jnp.sqrt(nv / bc2) + ADAM_EPS) + ADAM_WD * w_ref[...])

    blk = pl.BlockSpec((tr, c), lambda i: (i, 0))
    out = jax.ShapeDtypeStruct((r, c), F32)
    return pl.pallas_call(
        body, name=name, grid=(r // tr,),
        in_specs=[pl.BlockSpec((p, tr, c), lambda i: (0, i, 0)), blk, blk, blk],
        out_specs=[blk] * 4, out_shape=[out] * 4,
        compiler_params=_cparams(("arbitrary",)),
    )(parts, w, m, v)


def _rows_of(vectors, nrows):
    c = vectors[0].shape[0]
    ri = lax.broadcasted_iota(jnp.int32, (nrows, c), 0)
    out = jnp.zeros((nrows, c), F32)
    for r, vec in enumerate(vectors):
        out = jnp.where(ri == r, vec[None, :], out)
    return out


def _pad_rows(a, rows):
    return jnp.pad(a, ((0, rows - a.shape[0]), (0, 0)))


def kernel(x, w_in, conv_a_w, conv_a_b, conv_b_w, conv_b_b, ln_b_g, ln_b_b, pool_w, pool_b, pool_scale, w_out, ln_g, ln_b, loss_target, m_w_in, m_conv_a_w, m_conv_a_b, m_conv_b_w, m_conv_b_b, m_ln_b_g, m_ln_b_b, m_pool_w, m_pool_b, m_pool_scale, m_w_out, m_ln_g, m_ln_b, v_w_in, v_conv_a_w, v_conv_a_b, v_conv_b_w, v_conv_b_b, v_ln_b_g, v_ln_b_b, v_pool_w, v_pool_b, v_pool_scale, v_w_out, v_ln_g, v_ln_b):
    xc, yc, cc = _position()
    dev = 4 * xc + 2 * yc + cc
    core = jnp.reshape(cc, (1,)).astype(jnp.int32)
    depth = w_in.shape[0]
    cs = conv_a_w.shape[2]
    x0 = x[0]
    target = loss_target[0]
    r2 = lambda a: a.reshape(1, -1)

    w_in_b, w_out_b = w_in.astype(BF16), w_out.astype(BF16)
    conv_sh = jnp.concatenate([conv_a_w, conv_b_w], axis=1)
    full_in = lambda g: g.transpose(1, 0, 2).reshape(D_MODEL, D_IN)
    full_out = lambda g: g.reshape(D_MIX, D_MODEL)
    gi0, go0, conv_g = _run_plan(_Gather([w_in_b[0], w_out_b[0], conv_sh]), "gather_weights_0")
    w_in_f, w_out_f = [full_in(gi0)], [full_out(go0)]
    conv_f = conv_g.transpose(1, 2, 0, 3).reshape(depth, KA + KB, W_MIX)
    caw_f, cbw_f = conv_f[:, :KA], conv_f[:, KA:]

    xs, hs, ys, u2s, zs = [x0], [], [], [], []
    for l in range(depth):
        h = _in_proj(xs[l], w_in_f[l], f"in_proj_{l}")
        plan = _Gather([w_in_b[l + 1], w_out_b[l + 1]]) if l + 1 < depth else None
        y, u2, z, xn, *got = _mixer_fwd(
            h, xs[l], caw_f[l], r2(conv_a_b[l]), cbw_f[l], r2(conv_b_b[l]), r2(ln_b_g[l]), r2(ln_b_b[l]),
            pool_w[l], r2(pool_b[l]), r2(pool_scale[l]), w_out_f[l], r2(ln_g[l]), r2(ln_b[l]), f"mixer_fwd_{l}",
            plan=plan)
        if got:
            w_in_f.append(full_in(got[0])), w_out_f.append(full_out(got[1]))
        hs.append(h), ys.append(y), u2s.append(u2), zs.append(z), xs.append(xn)

    g, loss_rows = _loss_grad(xs[depth], target, "loss_grad")
    loss = lax.psum(jnp.sum(loss_rows), ("x", "y", "c"))

    def chip_sums(arrs, tag):
        got = _sibling_swap(arrs, f"reduce_swap_{tag}")
        return [_pair_sum(a, b, core, f"pair_sum_{tag}_{k}") for k, (a, b) in enumerate(zip(arrs, got))]

    gi_parts, go_parts, g_small, g_pool_w = [None] * depth, [None] * depth, [None] * depth, [None] * depth
    waiting = []
    for l in reversed(range(depth)):
        g_z, g_y, g_w_out, g_ln = _out_proj_bwd(g, zs[l], ys[l], w_out_f[l], r2(ln_g[l]), f"out_proj_bwd_{l}")
        go = g_w_out.reshape(N_DEV, D_MIX // N_DEV, D_MODEL)
        if l == 0:
            waiting += chip_sums([go], "w_out_0")
        g_h, rows, g_pool_w[l], *got = _mixer_bwd(
            hs[l], u2s[l], g_y, caw_f[l], r2(conv_a_b[l]), cbw_f[l], r2(ln_b_g[l]), r2(ln_b_b[l]),
            pool_w[l], r2(pool_b[l]), r2(pool_scale[l]), f"mixer_bwd_{l}",
            plan=_Exchange(waiting) if waiting else None)
        if got:
            if l + 1 < depth:
                gi_parts[l + 1], go_parts[l + 1] = got[0], got[1]
            if l == 0:
                go_parts[0] = got[-1]
        g_small[l] = jnp.concatenate([rows, g_ln.reshape(LN_ROWS, W_MIX)], axis=0)
        if l > 0:
            gi = _in_proj_wgrad(xs[l], g_h, f"in_proj_wgrad_{l}")[0]
            waiting = chip_sums([gi, go], f"layer_{l}")
            g = _in_proj_dgrad(g_h, w_in_f[l], g_z, f"in_proj_dgrad_{l}")[0]
        else:
            small = jnp.concatenate(g_small, axis=0)
            gpw = jnp.concatenate([a.reshape(-1, POOL_DIM) for a in g_pool_w], axis=0)
            gi, small_parts, gpw_parts = _in_proj_wgrad(xs[0], g_h, "in_proj_wgrad_0", plan=_Gather([small, gpw]))
            g, gi_parts[0] = _in_proj_dgrad(g_h, w_in_f[0], g_z, "in_proj_dgrad_0",
                                            plan=_Exchange(chip_sums([gi], "w_in_0")))
    grad_x = g[None]

    out_in = [_adamw(gi_parts[l], w_in[l], m_w_in[l], v_w_in[l], f"adamw_w_in_{l}") for l in range(depth)]
    out_out = [_adamw(go_parts[l], w_out[l], m_w_out[l], v_w_out[l], f"adamw_w_out_{l}") for l in range(depth)]
    out_in = [jnp.stack([o[k] for o in out_in]) for k in range(4)]
    out_out = [jnp.stack([o[k] for o in out_out]) for k in range(4)]

    def pack_small(cab_, cbb_, lbg_, lbb_, pb_, ps_, lng_, lnb_):
        per_layer = []
        for l in range(depth):
            top = _rows_of([cab_[l], cbb_[l], lbg_[l], lbb_[l], pb_[l].reshape(W_MIX), ps_[l]], N_ROWS)
            per_layer += [top, _rows_of([lng_[l], lnb_[l]], SUB).reshape(LN_ROWS, W_MIX)]
        return jnp.concatenate(per_layer, axis=0)

    flat = lambda a: a.reshape(-1, a.shape[-1])
    out_small = _adamw(small_parts,
                       pack_small(conv_a_b, conv_b_b, ln_b_g, ln_b_b, pool_b, pool_scale, ln_g, ln_b),
                       pack_small(m_conv_a_b, m_conv_b_b, m_ln_b_g, m_ln_b_b, m_pool_b, m_pool_scale, m_ln_g, m_ln_b),
                       pack_small(v_conv_a_b, v_conv_b_b, v_ln_b_g, v_ln_b_b, v_pool_b, v_pool_scale, v_ln_g, v_ln_b),
                       "adamw_small")
    out_pw = _adamw(gpw_parts, flat(pool_w), flat(m_pool_w), flat(v_pool_w), "adamw_pool_w")
    out_pw = [a.reshape(pool_w.shape) for a in out_pw]

    def unpack_small(a):
        a = a.reshape(depth, SMALL_ROWS, W_MIX)
        lnp = a[:, N_ROWS:].reshape(depth, SUB, D_MODEL)
        return dict(conv_a_b=a[:, R_CAB], conv_b_b=a[:, R_CBB], ln_b_g=a[:, R_LBG], ln_b_b=a[:, R_LBB],
                    pool_b=a[:, R_PB].reshape(pool_b.shape), pool_scale=a[:, R_PS], ln_g=lnp[:, 0], ln_b=lnp[:, 1],
                    conv=a[:, R_CAW:N_ROWS])
    small_out = [unpack_small(a) for a in out_small]

    g_conv = lax.dynamic_slice_in_dim(small_out[0]["conv"], dev * cs, cs, axis=2)
    conv_rows = depth * (KA + KB)
    conv_pad = -conv_rows % SUB
    pack_conv = lambda a, b: _pad_rows(jnp.concatenate([a, b], axis=1).reshape(conv_rows, cs), conv_rows + conv_pad)
    out_conv = _adamw(_pad_rows(g_conv.reshape(conv_rows, cs), conv_rows + conv_pad)[None],
                      pack_conv(conv_a_w, conv_b_w), pack_conv(m_conv_a_w, m_conv_b_w),
                      pack_conv(v_conv_a_w, v_conv_b_w), "adamw_conv")
    out_conv = [a[:conv_rows].reshape(depth, KA + KB, cs) for a in out_conv]

    outs = []
    for k in range(4):
        s, cv = small_out[k], out_conv[k]
        outs.append([out_in[k], cv[:, :KA], s["conv_a_b"], cv[:, KA:], s["conv_b_b"], s["ln_b_g"], s["ln_b_b"],
                     out_pw[k], s["pool_b"], s["pool_scale"], out_out[k], s["ln_g"], s["ln_b"]])
    return (loss, grad_x, *outs[0], *outs[1], *outs[2], *outs[3])
```

```python
import jax
import jax.numpy as jnp
from jax import lax
from jax.experimental import pallas as pl
from jax.experimental.pallas import tpu as pltpu

F32 = jnp.float32
BF16 = jnp.bfloat16

DEPTH = 2
D_MODEL = 1024
W_MIX = 512
D_IN = 9 * W_MIX
D_MIX = 3 * W_MIX
POOL_WINDOWS = (2, 4, 8, 16)
POOL_DIM = 128
KA = 3
KB = 31
ALPHA = (2.0 * DEPTH) ** 0.25
LN_EPS = 1e-5
ADAM_LR, ADAM_B1, ADAM_B2, ADAM_EPS, ADAM_WD, ADAM_STEP = 0.001, 0.9, 0.999, 1e-08, 0.01, 10

N_DEV = 8
N_CHIP = 4
MESH = pl.DeviceIdType.MESH

CH = 32
SUB = 8
VMEM_LIMIT = 56 * 1024 * 1024

R_CAB, R_CBB, R_LBG, R_LBB, R_PB, R_PS, R_CAW, R_CBW = 0, 1, 2, 3, 4, 5, 6, 9
N_ROWS = R_CBW + KB
LN_ROWS = 16
SMALL_ROWS = N_ROWS + LN_ROWS


def _cparams(sem, **kw):
    return pltpu.CompilerParams(dimension_semantics=sem, vmem_limit_bytes=VMEM_LIMIT, **kw)


def _sigmoid(v):
    return 1.0 / (1.0 + jnp.exp(-v))


def _fold8(a):
    r, c = a.shape
    return a.reshape(r // SUB, SUB, c).sum(axis=0)


def _dot(a, b):
    return jnp.dot(a, b, preferred_element_type=F32)


def _dot_nt(a, b):
    return lax.dot_general(a, b, (((1,), (1,)), ((), ())), preferred_element_type=F32)


def _dot_tn(a, b):
    return lax.dot_general(a, b, (((0,), (0,)), ((), ())), preferred_element_type=F32)


def _row_tile(r):
    for cand in (512, 256, 128, 64, 32, 16, 8):
        if r % cand == 0 and r > cand:
            return cand
    return r


def _position():
    return lax.axis_index("x"), lax.axis_index("y"), lax.axis_index("c")


class _Gather:
    def __init__(self, arrs):
        self.arrs = list(arrs)
        na = len(self.arrs)
        self.out_shape = [jax.ShapeDtypeStruct((N_DEV,) + a.shape, a.dtype) for a in self.arrs]
        self.scratch = [pltpu.SemaphoreType.DMA((na, 7)), pltpu.SemaphoreType.DMA((na, 7)),
                        pltpu.SemaphoreType.DMA((na,))]

    def _copies(self, src, dst, sems):
        send_sems, recv_sems, local_sems = sems
        na = len(self.arrs)
        x, y, c = _position()
        me, sibling = (x, y, c), (x, y, 1 - c)
        chips = [(1 - x, y), (x, 1 - y), (1 - x, 1 - y)]

        def slot(a, dev):
            return dst[a].at[4 * dev[0] + 2 * dev[1] + dev[2]]

        def copy(a, k, block, to, from_src=False):
            return pltpu.make_async_remote_copy(
                src_ref=src[a] if from_src else slot(a, block), dst_ref=slot(a, block),
                send_sem=send_sems.at[a, k], recv_sem=recv_sems.at[a, k], device_id=to, device_id_type=MESH)

        mine = [pltpu.make_async_copy(src[a], slot(a, me), local_sems.at[a]) for a in range(na)]
        first, landed, passed, last = [], [], [], []
        for a in range(na):
            first.append(copy(a, 0, me, sibling, from_src=True))
            first += [copy(a, 1 + j, me, (*chip, c), from_src=True) for j, chip in enumerate(chips)]
        for j, chip in enumerate(chips):
            for a in range(na):
                landed.append(copy(a, 1 + j, (*chip, c), me))
                passed.append(copy(a, 4 + j, (*chip, c), sibling))
        for a in range(na):
            last.append(copy(a, 0, sibling, me))
            last += [copy(a, 4 + j, (*chip, 1 - c), me) for j, chip in enumerate(chips)]
        return mine, first, landed, passed, last

    def start(self, src, dst, sems):
        mine, first, _, _, _ = self._copies(src, dst, sems)
        for cp in mine + first:
            cp.start()

    def pass_on(self, src, dst, sems):
        _, _, landed, passed, _ = self._copies(src, dst, sems)
        for got, cp in zip(landed, passed):
            got.wait_recv()
            cp.start()

    def finish(self, src, dst, sems):
        mine, first, _, passed, last = self._copies(src, dst, sems)
        for cp in last:
            cp.wait_recv()
        for cp in first + passed:
            cp.wait_send()
        for cp in mine:
            cp.wait()


class _Exchange:
    def __init__(self, arrs):
        self.arrs = list(arrs)
        na = len(self.arrs)
        self.out_shape = [jax.ShapeDtypeStruct(a.shape, a.dtype) for a in self.arrs]
        self.scratch = [pltpu.SemaphoreType.DMA((na, 3)), pltpu.SemaphoreType.DMA((na, 3)),
                        pltpu.SemaphoreType.DMA((na,))]

    def _copies(self, src, dst, sems):
        send_sems, recv_sems, local_sems = sems
        na = len(self.arrs)
        x, y, c = _position()
        my_chip = 2 * x + y
        chips = [(1 - x, y), (x, 1 - y), (1 - x, 1 - y)]
        mine = [pltpu.make_async_copy(src[a].at[my_chip], dst[a].at[my_chip], local_sems.at[a]) for a in range(na)]
        copies = []
        for a in range(na):
            for j, chip in enumerate(chips):
                copies.append(pltpu.make_async_remote_copy(
                    src_ref=src[a].at[2 * chip[0] + chip[1]], dst_ref=dst[a].at[my_chip],
                    send_sem=send_sems.at[a, j], recv_sem=recv_sems.at[a, j], device_id=(*chip, c),
                    device_id_type=MESH))
        return mine, copies

    def start(self, src, dst, sems):
        mine, copies = self._copies(src, dst, sems)
        for cp in mine + copies:
            cp.start()

    def pass_on(self, src, dst, sems):
        pass

    def finish(self, src, dst, sems):
        mine, copies = self._copies(src, dst, sems)
        for cp in copies:
            cp.wait()
        for cp in mine:
            cp.wait()


def _run_plan(plan, name):
    na = len(plan.arrs)
    any_spec = pl.BlockSpec(memory_space=pl.ANY)

    def body(*refs):
        src, dst, sems = refs[:na], refs[na:2 * na], refs[2 * na:]
        plan.start(src, dst, sems)
        plan.pass_on(src, dst, sems)
        plan.finish(src, dst, sems)

    return pl.pallas_call(
        body, name=name, in_specs=[any_spec] * na, out_specs=[any_spec] * na,
        out_shape=plan.out_shape, scratch_shapes=plan.scratch,
    )(*plan.arrs)


class _Hosted:
    def __init__(self, plan, n_in, n_out, n_scratch):
        self.plan, self.n_in, self.n_out, self.n_scratch = plan, n_in, n_out, n_scratch
        any_spec = pl.BlockSpec(memory_space=pl.ANY)
        k = 0 if plan is None else len(plan.arrs)
        self.operands = [] if plan is None else plan.arrs
        self.in_specs = [any_spec] * k
        self.out_specs = [any_spec] * k
        self.out_shape = [] if plan is None else plan.out_shape
        self.scratch = [] if plan is None else plan.scratch

    def wrap(self, body, phase):
        if self.plan is None:
            return body
        plan, k = self.plan, len(self.plan.arrs)
        i0, o0 = self.n_in, self.n_in + k
        o1 = o0 + self.n_out
        s0 = o1 + k
        s1 = s0 + self.n_scratch

        def hosted(*refs):
            src, dst, sems = refs[i0:o0], refs[o1:s0], refs[s1:]
            first, middle, last = phase()
            pl.when(first)(lambda: plan.start(src, dst, sems))
            body(*refs[:i0], *refs[o0:o1], *refs[s0:s1])
            pl.when(middle)(lambda: plan.pass_on(src, dst, sems))
            pl.when(last)(lambda: plan.finish(src, dst, sems))

        return hosted


def _sibling_swap(arrs, name):
    na = len(arrs)
    any_spec = pl.BlockSpec(memory_space=pl.ANY)

    def body(*refs):
        src, dst = refs[:na], refs[na:2 * na]
        send_sems, recv_sems = refs[2 * na:]
        x, y, c = _position()
        sibling = (x, y, 1 - c)
        copies = []
        for a in range(na):
            for k in range(N_CHIP):
                copies.append(pltpu.make_async_remote_copy(
                    src_ref=src[a].at[2 * k + (1 - c)], dst_ref=dst[a].at[k],
                    send_sem=send_sems.at[a, k], recv_sem=recv_sems.at[a, k], device_id=sibling, device_id_type=MESH))
        for cp in copies:
            cp.start()
        for cp in copies:
            cp.wait()

    return pl.pallas_call(
        body, name=name,
        in_specs=[any_spec] * na, out_specs=[any_spec] * na,
        out_shape=[jax.ShapeDtypeStruct((N_CHIP,) + a.shape[1:], a.dtype) for a in arrs],
        scratch_shapes=[pltpu.SemaphoreType.DMA((na, N_CHIP)), pltpu.SemaphoreType.DMA((na, N_CHIP))],
    )(*arrs)


def _pair_sum(mine, got, core, name):
    _, r, c = mine.shape
    tr = _row_tile(r)

    def body(core_ref, a_ref, b_ref, o_ref):
        o_ref[...] = a_ref[...] + b_ref[...]

    return pl.pallas_call(
        body, name=name,
        grid_spec=pltpu.PrefetchScalarGridSpec(
            num_scalar_prefetch=1, grid=(N_CHIP, r // tr),
            in_specs=[pl.BlockSpec((1, tr, c), lambda k, i, core_ref: (2 * k + core_ref[0], i, 0)),
                      pl.BlockSpec((1, tr, c), lambda k, i, core_ref: (k, i, 0))],
            out_specs=pl.BlockSpec((1, tr, c), lambda k, i, core_ref: (k, i, 0))),
        out_shape=jax.ShapeDtypeStruct((N_CHIP, r, c), mine.dtype),
        compiler_params=_cparams(("arbitrary", "arbitrary")),
    )(core, mine, got)


TT = 256
SEG = TT // SUB


def _to_segments(a):
    t, c = a.shape
    return a.reshape(t // TT, SUB, SEG, c).transpose(0, 2, 1, 3).reshape(t, c)


def _from_segments(a):
    t, c = a.shape
    return a.reshape(t // TT, SEG, SUB, c).transpose(0, 2, 1, 3).reshape(t, c)


def _sublane_is(s):
    return lax.broadcasted_iota(jnp.int32, (TT, W_MIX), 0) % SUB == s


def _look_back(ext_ref, cur, before):
    ext_ref[TT:, :] = cur
    ext_ref[0:TT, :] = jnp.where(_sublane_is(0), before, pltpu.roll(cur, 1, axis=0))


def _last_segment(cur):
    return pltpu.roll(cur, TT - (SUB - 1), axis=0)


def _look_ahead(ext_ref, cur, after):
    ext_ref[0:TT, :] = cur
    ext_ref[TT:, :] = jnp.where(_sublane_is(SUB - 1), after, pltpu.roll(cur, TT - 1, axis=0))


def _first_segment(cur):
    return pltpu.roll(cur, SUB - 1, axis=0)


def _tap_loop(body):
    lax.fori_loop(0, TT // CH, lambda c, carry: body(pl.multiple_of(c * CH, CH), carry), 0)


def _conv_rows(ext_ref, w_ref, out_ref, *, nk, off, reverse):
    def body(r0, carry):
        acc = jnp.zeros((CH, W_MIX), F32)
        for k in range(nk):
            kk = nk - 1 - k if reverse else k
            acc = acc + ext_ref[pl.ds(r0 + (off + k) * SUB, CH), :] * w_ref[kk:kk + 1, :]
        out_ref[pl.ds(r0, CH), :] = acc
        return carry
    _tap_loop(body)


def _conv_wgrad(g_ref, ext_ref, acc_ref, *, nk, off, row0):
    def body(r0, carry):
        g = g_ref[pl.ds(r0, CH), :]
        for k in range(nk):
            a = (row0 + k) * SUB
            acc_ref[a:a + SUB, :] += _fold8(g * ext_ref[pl.ds(r0 + (off + k) * SUB, CH), :])
        return carry
    _tap_loop(body)


def _window_sums(ext_ref, out_ref, *, forward):
    def body(r0, carry):
        for g, w in enumerate(POOL_WINDOWS):
            lanes = slice(g * POOL_DIM, (g + 1) * POOL_DIM)
            acc = jnp.zeros((CH, POOL_DIM), F32)
            for j in range(w):
                off = j if forward else SEG - j
                acc = acc + ext_ref[pl.ds(r0 + off * SUB, CH), lanes]
            out_ref[pl.ds(r0, CH), lanes] = acc
        return carry
    _tap_loop(body)


def _inv_count(tile):
    r = lax.broadcasted_iota(jnp.int32, (TT, POOL_DIM), 0)
    t1 = (tile * TT + (r % SUB) * SEG + r // SUB + 1).astype(F32)
    return jnp.concatenate([1.0 / jnp.minimum(t1, float(w)) for w in POOL_WINDOWS], axis=1)


def _groups(h_ref):
    return [h_ref[:, k * W_MIX:(k + 1) * W_MIX].astype(F32) for k in range(9)]


def _layer_norm(v, g, b):
    mu = jnp.mean(v, axis=-1, keepdims=True)
    vc = v - mu
    var = jnp.mean(vc * vc, axis=-1, keepdims=True)
    rstd = lax.rsqrt(var + LN_EPS)
    vhat = vc * rstd
    return vhat * g + b, vhat, rstd


def _layer_norm_bwd(g_out, vhat, rstd, g):
    gh = g_out * g
    m1 = jnp.mean(gh, axis=-1, keepdims=True)
    m2 = jnp.mean(gh * vhat, axis=-1, keepdims=True)
    return rstd * (gh - m1 - vhat * m2)


def _pool_linear(pooled, pw_ref, pb_ref):
    outs = []
    for g in range(len(POOL_WINDOWS)):
        lanes = slice(g * POOL_DIM, (g + 1) * POOL_DIM)
        outs.append(_dot(pooled[:, lanes].astype(BF16), pw_ref[g].astype(BF16)))
    return jnp.concatenate(outs, axis=1) + pb_ref[...]


def _in_proj(x, w, name):
    t, d = x.shape
    n = w.shape[1]
    tm, tn = min(t, 1024), 1536

    def body(x_ref, w_ref, o_ref, xb_ref):
        @pl.when(pl.program_id(1) == 0)
        def _():
            xb_ref[...] = x_ref[...].astype(BF16)
        o_ref[...] = _dot(xb_ref[...], w_ref[...]).astype(BF16)

    return pl.pallas_call(
        body, name=name, grid=(t // tm, n // tn),
        in_specs=[pl.BlockSpec((tm, d), lambda i, j: (i, 0)), pl.BlockSpec((d, tn), lambda i, j: (0, j))],
        out_specs=pl.BlockSpec((tm, tn), lambda i, j: (i, j)),
        out_shape=jax.ShapeDtypeStruct((t, n), BF16),
        scratch_shapes=[pltpu.VMEM((tm, d), BF16)],
        compiler_params=_cparams(("arbitrary", "arbitrary")),
    )(x, w)


def _mixer_fwd(h, x, caw, cab, cbw, cbb, lbg, lbb, pw, pb, ps, w_out, lng, lnb, name, plan=None):
    t = h.shape[0]
    tt = TT
    n = t // tt

    def body(h_ref, x_ref, caw_ref, cab_ref, cbw_ref, cbb_ref, lbg_ref, lbb_ref, pw_ref, pb_ref, ps_ref,
             wo_ref, lng_ref, lnb_ref, y_ref, u2_ref, z_ref, xn_ref, exta, extb, extc, lasta, lastb, lastc, tmp):
        i = pl.program_id(0)

        @pl.when(i == 0)
        def _():
            for e in (lasta, lastb, lastc):
                e[...] = jnp.zeros_like(e)

        a_bg, a_cg, a_v, a_z, b_v, b_g, b_z, c_u, c_z = _groups(h_ref)
        for ext, last, cur in ((exta, lasta, a_cg * a_v), (extb, lastb, b_v * _sigmoid(b_g)), (extc, lastc, c_u)):
            _look_back(ext, cur, last[...])
            last[...] = _last_segment(cur)

        _conv_rows(exta, caw_ref, tmp, nk=KA, off=SEG - (KA - 1), reverse=False)
        y_a = a_bg * (tmp[...] + cab_ref[...]) * (a_z * _sigmoid(a_z))
        y_ref[:, 0:W_MIX] = y_a.astype(BF16)

        _window_sums(extc, tmp, forward=False)
        pooled = tmp[...] * _inv_count(i) - c_u
        p = _pool_linear(pooled, pw_ref, pb_ref)
        y_c = p * ps_ref[...] * (c_z * _sigmoid(c_z))
        y_ref[:, 2 * W_MIX:3 * W_MIX] = y_c.astype(BF16)

        _conv_rows(extb, cbw_ref, tmp, nk=KB, off=SEG - (KB - 1), reverse=False)
        u2 = tmp[...] + cbb_ref[...]
        u2_ref[...] = u2
        ln, _, _ = _layer_norm(u2, lbg_ref[...], lbb_ref[...])
        y_b = (ln * _sigmoid(ln)) * (b_z * _sigmoid(b_z))
        y_ref[:, W_MIX:2 * W_MIX] = y_b.astype(BF16)

        out = _dot(y_ref[...], wo_ref[...])
        z = ALPHA * x_ref[...] + out
        z_ref[...] = z
        xn, _, _ = _layer_norm(z, lng_ref[...], lnb_ref[...])
        xn_ref[...] = xn

    def phase():
        i = pl.program_id(0)
        return i == 0, i == n // 2, i == n - 1

    row = lambda wd: pl.BlockSpec((tt, wd), lambda i: (i, 0))
    full = lambda a: pl.BlockSpec(a.shape, lambda i: (0,) * a.ndim)
    params = (caw, cab, cbw, cbb, lbg, lbb, pw, pb, ps, w_out, lng, lnb)
    host = _Hosted(plan, n_in=2 + len(params), n_out=4, n_scratch=7)
    return pl.pallas_call(
        host.wrap(body, phase), name=name, grid=(n,),
        in_specs=[row(D_IN), row(D_MODEL)] + [full(a) for a in params] + host.in_specs,
        out_specs=[row(D_MIX), row(W_MIX), row(D_MODEL), row(D_MODEL)] + host.out_specs,
        out_shape=[jax.ShapeDtypeStruct((t, D_MIX), BF16), jax.ShapeDtypeStruct((t, W_MIX), F32),
                   jax.ShapeDtypeStruct((t, D_MODEL), F32), jax.ShapeDtypeStruct((t, D_MODEL), F32)] + host.out_shape,
        scratch_shapes=[pltpu.VMEM((2 * tt, W_MIX), F32)] * 3 + [pltpu.VMEM((tt, W_MIX), F32)] * 4 + host.scratch,
        compiler_params=_cparams(("arbitrary",)),
    )(h, x, *params, *host.operands)


def _loss_grad(xn, target, name):
    t, d = xn.shape
    tt = min(t, 512)

    def body(x_ref, t_ref, g_ref, l_ref):
        @pl.when(pl.program_id(0) == 0)
        def _():
            l_ref[...] = jnp.zeros_like(l_ref)
        e = x_ref[...] - t_ref[...]
        g_ref[...] = e * (1.0 / d)
        l_ref[...] += _fold8(e * e) * (0.5 / d)

    row = pl.BlockSpec((tt, d), lambda i: (i, 0))
    return pl.pallas_call(
        body, name=name, grid=(t // tt,), in_specs=[row, row],
        out_specs=[row, pl.BlockSpec((SUB, d), lambda i: (0, 0))],
        out_shape=[jax.ShapeDtypeStruct((t, d), F32), jax.ShapeDtypeStruct((SUB, d), F32)],
        compiler_params=_cparams(("arbitrary",)),
    )(xn, target)


def _out_proj_bwd(g_xn, z, y, w_out, lng, name):
    t = z.shape[0]
    tt = min(t, 256)

    def body(g_ref, z_ref, y_ref, wo_ref, lng_ref, gz_ref, gy_ref, gwo_ref, gln_ref, accg, accb):
        i = pl.program_id(0)

        @pl.when(i == 0)
        def _():
            gwo_ref[...] = jnp.zeros_like(gwo_ref)
            accg[...] = jnp.zeros_like(accg)
            accb[...] = jnp.zeros_like(accb)

        g = g_ref[...]
        _, zhat, rstd = _layer_norm(z_ref[...], lng_ref[...], 0.0)
        accg[...] += _fold8(g * zhat)
        accb[...] += _fold8(g)
        g_z = _layer_norm_bwd(g, zhat, rstd, lng_ref[...])
        gz_ref[...] = g_z
        gzb = g_z.astype(BF16)
        gy_ref[...] = _dot_nt(gzb, wo_ref[...])
        gwo_ref[...] += _dot_tn(y_ref[...], gzb)

        @pl.when(i == pl.num_programs(0) - 1)
        def _():
            gln_ref[...] = jnp.zeros_like(gln_ref)
            gln_ref[0:1, :] = jnp.sum(accg[...], axis=0, keepdims=True)
            gln_ref[1:2, :] = jnp.sum(accb[...], axis=0, keepdims=True)

    row = lambda wd: pl.BlockSpec((tt, wd), lambda i: (i, 0))
    full = lambda shape: pl.BlockSpec(shape, lambda i: (0,) * len(shape))
    return pl.pallas_call(
        body, name=name, grid=(t // tt,),
        in_specs=[row(D_MODEL), row(D_MODEL), row(D_MIX), full(w_out.shape), full(lng.shape)],
        out_specs=[row(D_MODEL), row(D_MIX), full((D_MIX, D_MODEL)), full((SUB, D_MODEL))],
        out_shape=[jax.ShapeDtypeStruct((t, D_MODEL), F32), jax.ShapeDtypeStruct((t, D_MIX), F32),
                   jax.ShapeDtypeStruct((D_MIX, D_MODEL), F32), jax.ShapeDtypeStruct((SUB, D_MODEL), F32)],
        scratch_shapes=[pltpu.VMEM((SUB, D_MODEL), F32)] * 2,
        compiler_params=_cparams(("arbitrary",)),
    )(g_xn, z, y, w_out, lng)


def _mixer_bwd(h, u2, g_y, caw, cab, cbw, lbg, lbb, pw, pb, ps, name, plan=None):
    t = h.shape[0]
    tt = TT
    n = t // tt
    before_groups = (1, 2, 4, 5, 7)

    def body(h_ref, p_cg, p_av, p_bv, p_bg, p_cu, u2_ref, gy_ref, caw_ref, cab_ref, cbw_ref, lbg_ref, lbb_ref,
             pw_ref, pb_ref, ps_ref, gh_ref, rows_ref, gpw_ref,
             exta, extb, extc, gca, gu2, qx, nexta, nextb, nextc, tmp, tmp2, acc):
        s = pl.program_id(0)
        i = n - 1 - s

        @pl.when(s == 0)
        def _():
            acc[...] = jnp.zeros_like(acc)
            gpw_ref[...] = jnp.zeros_like(gpw_ref)
            for e in (nexta, nextb, nextc):
                e[...] = jnp.zeros_like(e)

        live = (i > 0).astype(F32)
        f32 = lambda ref: ref[...].astype(F32)
        before_a = _last_segment(f32(p_cg) * f32(p_av)) * live
        before_b = _last_segment(f32(p_bv) * _sigmoid(f32(p_bg))) * live
        before_c = _last_segment(f32(p_cu)) * live

        a_bg, a_cg, a_v, a_z, b_v, b_g, b_z, c_u, c_z = _groups(h_ref)
        g_ya = gy_ref[:, 0:W_MIX]
        g_yb = gy_ref[:, W_MIX:2 * W_MIX]
        g_yc = gy_ref[:, 2 * W_MIX:3 * W_MIX]

        def add_row(r, v):
            acc[r * SUB:(r + 1) * SUB, :] += _fold8(v)

        _look_back(exta, a_cg * a_v, before_a)
        _conv_rows(exta, caw_ref, tmp, nk=KA, off=SEG - (KA - 1), reverse=False)
        ca = tmp[...] + cab_ref[...]
        sg = _sigmoid(a_z)
        s_az = a_z * sg
        t_a = g_ya * a_bg
        gh_ref[:, 0:W_MIX] = (g_ya * ca * s_az).astype(BF16)
        gh_ref[:, 3 * W_MIX:4 * W_MIX] = (t_a * ca * (sg * (1.0 + a_z * (1.0 - sg)))).astype(BF16)
        g_ca = t_a * s_az
        _look_ahead(gca, g_ca, nexta[...])
        nexta[...] = _first_segment(g_ca)
        add_row(R_CAB, g_ca)
        _conv_wgrad(gca, exta, acc, nk=KA, off=SEG - (KA - 1), row0=R_CAW)
        _conv_rows(gca, caw_ref, tmp, nk=KA, off=0, reverse=True)
        g_pa = tmp[...]
        gh_ref[:, W_MIX:2 * W_MIX] = (g_pa * a_v).astype(BF16)
        gh_ref[:, 2 * W_MIX:3 * W_MIX] = (g_pa * a_cg).astype(BF16)

        sgg = _sigmoid(b_g)
        _look_back(extb, b_v * sgg, before_b)
        ln, u2hat, rstd = _layer_norm(u2_ref[...], lbg_ref[...], lbb_ref[...])
        sl = _sigmoid(ln)
        u3 = ln * sl
        sz = _sigmoid(b_z)
        s_bz = b_z * sz
        gh_ref[:, 6 * W_MIX:7 * W_MIX] = (g_yb * u3 * (sz * (1.0 + b_z * (1.0 - sz)))).astype(BF16)
        g_ln = g_yb * s_bz * (sl * (1.0 + ln * (1.0 - sl)))
        add_row(R_LBG, g_ln * u2hat)
        add_row(R_LBB, g_ln)
        g_u2 = _layer_norm_bwd(g_ln, u2hat, rstd, lbg_ref[...])
        _look_ahead(gu2, g_u2, nextb[...])
        nextb[...] = _first_segment(g_u2)
        add_row(R_CBB, g_u2)
        _conv_wgrad(gu2, extb, acc, nk=KB, off=SEG - (KB - 1), row0=R_CBW)
        _conv_rows(gu2, cbw_ref, tmp, nk=KB, off=0, reverse=True)
        g_u1 = tmp[...]
        gh_ref[:, 4 * W_MIX:5 * W_MIX] = (g_u1 * sgg).astype(BF16)
        gh_ref[:, 5 * W_MIX:6 * W_MIX] = (g_u1 * b_v * sgg * (1.0 - sgg)).astype(BF16)

        _look_back(extc, c_u, before_c)
        _window_sums(extc, tmp, forward=False)
        inv = _inv_count(i)
        pooled = tmp[...] * inv - c_u
        p = _pool_linear(pooled, pw_ref, pb_ref)
        sc = _sigmoid(c_z)
        s_cz = c_z * sc
        scale = ps_ref[...]
        gh_ref[:, 8 * W_MIX:9 * W_MIX] = (g_yc * p * scale * (sc * (1.0 + c_z * (1.0 - sc)))).astype(BF16)
        t_c = g_yc * s_cz
        add_row(R_PS, t_c * p)
        g_p = t_c * scale
        add_row(R_PB, g_p)
        g_pooled = []
        for g in range(len(POOL_WINDOWS)):
            lanes = slice(g * POOL_DIM, (g + 1) * POOL_DIM)
            gpg = g_p[:, lanes].astype(BF16)
            gpw_ref[g] += _dot_tn(pooled[:, lanes].astype(BF16), gpg)
            g_pooled.append(_dot_nt(gpg, pw_ref[g].astype(BF16)))
        g_pooled = jnp.concatenate(g_pooled, axis=1)
        q = g_pooled * inv
        _look_ahead(qx, q, nextc[...])
        nextc[...] = _first_segment(q)
        _window_sums(qx, tmp2, forward=True)
        gh_ref[:, 7 * W_MIX:8 * W_MIX] = (tmp2[...] - g_pooled).astype(BF16)

        @pl.when(s == n - 1)
        def _():
            for r in range(N_ROWS):
                rows_ref[r:r + 1, :] = jnp.sum(acc[r * SUB:(r + 1) * SUB, :], axis=0, keepdims=True)

    def phase():
        s = pl.program_id(0)
        return s == 0, s == n // 2, s == n - 1

    row = lambda wd: pl.BlockSpec((tt, wd), lambda s: (n - 1 - s, 0))
    before = [pl.BlockSpec((tt, W_MIX), lambda s, k=k: (jnp.maximum(n - 2 - s, 0), k)) for k in before_groups]
    full = lambda shape: pl.BlockSpec(shape, lambda s: (0,) * len(shape))
    params = (caw, cab, cbw, lbg, lbb, pw, pb, ps)
    host = _Hosted(plan, n_in=3 + len(before) + len(params), n_out=3, n_scratch=12)
    return pl.pallas_call(
        host.wrap(body, phase), name=name, grid=(n,),
        in_specs=[row(D_IN)] + before + [row(W_MIX), row(D_MIX)] + [full(a.shape) for a in params] + host.in_specs,
        out_specs=[row(D_IN), full((N_ROWS, W_MIX)), full(pw.shape)] + host.out_specs,
        out_shape=[jax.ShapeDtypeStruct((t, D_IN), BF16), jax.ShapeDtypeStruct((N_ROWS, W_MIX), F32),
                   jax.ShapeDtypeStruct(pw.shape, F32)] + host.out_shape,
        scratch_shapes=[pltpu.VMEM((2 * tt, W_MIX), F32)] * 6 + [pltpu.VMEM((tt, W_MIX), F32)] * 5
        + [pltpu.VMEM((N_ROWS * SUB, W_MIX), F32)] + host.scratch,
        compiler_params=_cparams(("arbitrary",)),
    )(h, *([h] * len(before)), u2, g_y, *params, *host.operands)


def _in_proj_wgrad(x, g_h, name, plan=None):
    t, d = x.shape
    n = g_h.shape[1]
    nb = n // N_DEV
    per = N_DEV // 2
    tk, tn = min(t, 512), per * nb
    nk = t // tk

    def body(x_ref, g_ref, o_ref, acc):
        k = pl.program_id(1)

        @pl.when(k == 0)
        def _():
            acc[...] = jnp.zeros_like(acc)
        acc[...] += _dot_tn(x_ref[...].astype(BF16), g_ref[...])

        @pl.when(k == nk - 1)
        def _():
            for b in range(per):
                o_ref[b] = acc[:, b * nb:(b + 1) * nb]

    def phase():
        step = pl.program_id(0) * nk + pl.program_id(1)
        return step == 0, step == nk, step == 2 * nk - 1

    host = _Hosted(plan, n_in=2, n_out=1, n_scratch=1)
    return pl.pallas_call(
        host.wrap(body, phase), name=name, grid=(n // tn, nk),
        in_specs=[pl.BlockSpec((tk, d), lambda j, k: (k, 0)), pl.BlockSpec((tk, tn), lambda j, k: (k, j))]
        + host.in_specs,
        out_specs=[pl.BlockSpec((per, d, nb), lambda j, k: (j, 0, 0))] + host.out_specs,
        out_shape=[jax.ShapeDtypeStruct((N_DEV, d, nb), F32)] + host.out_shape,
        scratch_shapes=[pltpu.VMEM((d, tn), F32)] + host.scratch,
        compiler_params=_cparams(("arbitrary", "arbitrary")),
    )(x, g_h, *host.operands)


def _in_proj_dgrad(g_h, w, g_z, name, plan=None):
    t, n = g_h.shape
    d = w.shape[0]
    tm, tk = min(t, 1024), 1536
    nm, nk = t // tm, n // tk

    def body(g_ref, w_ref, gz_ref, o_ref):
        @pl.when(pl.program_id(1) == 0)
        def _():
            o_ref[...] = ALPHA * gz_ref[...]
        o_ref[...] += _dot_nt(g_ref[...], w_ref[...])

    def phase():
        step = pl.program_id(0) * nk + pl.program_id(1)
        return step == 0, step == (nm * nk) // 2, step == nm * nk - 1

    host = _Hosted(plan, n_in=3, n_out=1, n_scratch=0)
    return pl.pallas_call(
        host.wrap(body, phase), name=name, grid=(nm, nk),
        in_specs=[pl.BlockSpec((tm, tk), lambda i, k: (i, k)), pl.BlockSpec((d, tk), lambda i, k: (0, k)),
                  pl.BlockSpec((tm, d), lambda i, k: (i, 0))] + host.in_specs,
        out_specs=[pl.BlockSpec((tm, d), lambda i, k: (i, 0))] + host.out_specs,
        out_shape=[jax.ShapeDtypeStruct((t, d), F32)] + host.out_shape,
        scratch_shapes=host.scratch,
        compiler_params=_cparams(("arbitrary", "arbitrary")),
    )(g_h, w, g_z, *host.operands)


def _adamw(parts, w, m, v, name):
    p, r, c = parts.shape
    tr = _row_tile(r)
    bc1 = 1.0 - ADAM_B1 ** ADAM_STEP
    bc2 = 1.0 - ADAM_B2 ** ADAM_STEP

    def body(p_ref, w_ref, m_ref, v_ref, g_ref, d_ref, nm_ref, nv_ref):
        g = p_ref[0].astype(F32)
        for k in range(1, p):
            g = g + p_ref[k].astype(F32)
        nm = ADAM_B1 * m_ref[...] + (1.0 - ADAM_B1) * g
        nv = ADAM_B2 * v_ref[...] + (1.0 - ADAM_B2) * (g * g)
        g_ref[...] = g
        nm_ref[...] = nm
        nv_ref[...] = nv
        d_ref[...] = -ADAM_LR * ((nm / bc1) / (jnp.sqrt(nv / bc2) + ADAM_EPS) + ADAM_WD * w_ref[...])

    blk = pl.BlockSpec((tr, c), lambda i: (i, 0))
    out = jax.ShapeDtypeStruct((r, c), F32)
    return pl.pallas_call(
        body, name=name, grid=(r // tr,),
        in_specs=[pl.BlockSpec((p, tr, c), lambda i: (0, i, 0)), blk, blk, blk],
        out_specs=[blk] * 4, out_shape=[out] * 4,
        compiler_params=_cparams(("arbitrary",)),
    )(parts, w, m, v)


def _rows_of(vectors, nrows):
    c = vectors[0].shape[0]
    ri = lax.broadcasted_iota(jnp.int32, (nrows, c), 0)
    out = jnp.zeros((nrows, c), F32)
    for r, vec in enumerate(vectors):
        out = jnp.where(ri == r, vec[None, :], out)
    return out


def _pad_rows(a, rows):
    return jnp.pad(a, ((0, rows - a.shape[0]), (0, 0)))


def kernel(x, w_in, conv_a_w, conv_a_b, conv_b_w, conv_b_b, ln_b_g, ln_b_b, pool_w, pool_b, pool_scale, w_out, ln_g, ln_b, loss_target, m_w_in, m_conv_a_w, m_conv_a_b, m_conv_b_w, m_conv_b_b, m_ln_b_g, m_ln_b_b, m_pool_w, m_pool_b, m_pool_scale, m_w_out, m_ln_g, m_ln_b, v_w_in, v_conv_a_w, v_conv_a_b, v_conv_b_w, v_conv_b_b, v_ln_b_g, v_ln_b_b, v_pool_w, v_pool_b, v_pool_scale, v_w_out, v_ln_g, v_ln_b):
    xc, yc, cc = _position()
    dev = 4 * xc + 2 * yc + cc
    core = jnp.reshape(cc, (1,)).astype(jnp.int32)
    depth = w_in.shape[0]
    cs = conv_a_w.shape[2]
    x0 = _to_segments(x[0])
    target = _to_segments(loss_target[0])
    r2 = lambda a: a.reshape(1, -1)

    w_in_b, w_out_b = w_in.astype(BF16), w_out.astype(BF16)
    conv_sh = jnp.concatenate([conv_a_w, conv_b_w], axis=1)
    full_in = lambda g: g.transpose(1, 0, 2).reshape(D_MODEL, D_IN)
    full_out = lambda g: g.reshape(D_MIX, D_MODEL)
    gi0, go0, conv_g = _run_plan(_Gather([w_in_b[0], w_out_b[0], conv_sh]), "gather_weights_0")
    w_in_f, w_out_f = [full_in(gi0)], [full_out(go0)]
    conv_f = conv_g.transpose(1, 2, 0, 3).reshape(depth, KA + KB, W_MIX)
    caw_f, cbw_f = conv_f[:, :KA], conv_f[:, KA:]

    xs, hs, ys, u2s, zs = [x0], [], [], [], []
    for l in range(depth):
        h = _in_proj(xs[l], w_in_f[l], f"in_proj_{l}")
        plan = _Gather([w_in_b[l + 1], w_out_b[l + 1]]) if l + 1 < depth else None
        y, u2, z, xn, *got = _mixer_fwd(
            h, xs[l], caw_f[l], r2(conv_a_b[l]), cbw_f[l], r2(conv_b_b[l]), r2(ln_b_g[l]), r2(ln_b_b[l]),
            pool_w[l], r2(pool_b[l]), r2(pool_scale[l]), w_out_f[l], r2(ln_g[l]), r2(ln_b[l]), f"mixer_fwd_{l}",
            plan=plan)
        if got:
            w_in_f.append(full_in(got[0])), w_out_f.append(full_out(got[1]))
        hs.append(h), ys.append(y), u2s.append(u2), zs.append(z), xs.append(xn)

    g, loss_rows = _loss_grad(xs[depth], target, "loss_grad")
    loss = lax.psum(jnp.sum(loss_rows), ("x", "y", "c"))

    def chip_sums(arrs, tag):
        got = _sibling_swap(arrs, f"reduce_swap_{tag}")
        return [_pair_sum(a, b, core, f"pair_sum_{tag}_{k}") for k, (a, b) in enumerate(zip(arrs, got))]

    gi_parts, go_parts, g_small, g_pool_w = [None] * depth, [None] * depth, [None] * depth, [None] * depth
    waiting = []
    for l in reversed(range(depth)):
        g_z, g_y, g_w_out, g_ln = _out_proj_bwd(g, zs[l], ys[l], w_out_f[l], r2(ln_g[l]), f"out_proj_bwd_{l}")
        go = g_w_out.reshape(N_DEV, D_MIX // N_DEV, D_MODEL)
        if l == 0:
            waiting += chip_sums([go], "w_out_0")
        g_h, rows, g_pool_w[l], *got = _mixer_bwd(
            hs[l], u2s[l], g_y, caw_f[l], r2(conv_a_b[l]), cbw_f[l], r2(ln_b_g[l]), r2(ln_b_b[l]),
            pool_w[l], r2(pool_b[l]), r2(pool_scale[l]), f"mixer_bwd_{l}",
            plan=_Exchange(waiting) if waiting else None)
        if got:
            if l + 1 < depth:
                gi_parts[l + 1], go_parts[l + 1] = got[0], got[1]
            if l == 0:
                go_parts[0] = got[-1]
        g_small[l] = jnp.concatenate([rows, g_ln.reshape(LN_ROWS, W_MIX)], axis=0)
        if l > 0:
            gi = _in_proj_wgrad(xs[l], g_h, f"in_proj_wgrad_{l}")[0]
            waiting = chip_sums([gi, go], f"layer_{l}")
            g = _in_proj_dgrad(g_h, w_in_f[l], g_z, f"in_proj_dgrad_{l}")[0]
        else:
            small = jnp.concatenate(g_small, axis=0)
            gpw = jnp.concatenate([a.reshape(-1, POOL_DIM) for a in g_pool_w], axis=0)
            gi, small_parts, gpw_parts = _in_proj_wgrad(xs[0], g_h, "in_proj_wgrad_0", plan=_Gather([small, gpw]))
            g, gi_parts[0] = _in_proj_dgrad(g_h, w_in_f[0], g_z, "in_proj_dgrad_0",
                                            plan=_Exchange(chip_sums([gi], "w_in_0")))
    grad_x = _from_segments(g)[None]

    out_in = [_adamw(gi_parts[l], w_in[l], m_w_in[l], v_w_in[l], f"adamw_w_in_{l}") for l in range(depth)]
    out_out = [_adamw(go_parts[l], w_out[l], m_w_out[l], v_w_out[l], f"adamw_w_out_{l}") for l in range(depth)]
    out_in = [jnp.stack([o[k] for o in out_in]) for k in range(4)]
    out_out = [jnp.stack([o[k] for o in out_out]) for k in range(4)]

    def pack_small(cab_, cbb_, lbg_, lbb_, pb_, ps_, lng_, lnb_):
        per_layer = []
        for l in range(depth):
            top = _rows_of([cab_[l], cbb_[l], lbg_[l], lbb_[l], pb_[l].reshape(W_MIX), ps_[l]], N_ROWS)
            per_layer += [top, _rows_of([lng_[l], lnb_[l]], SUB).reshape(LN_ROWS, W_MIX)]
        return jnp.concatenate(per_layer, axis=0)

    flat = lambda a: a.reshape(-1, a.shape[-1])
    out_small = _adamw(small_parts,
                       pack_small(conv_a_b, conv_b_b, ln_b_g, ln_b_b, pool_b, pool_scale, ln_g, ln_b),
                       pack_small(m_conv_a_b, m_conv_b_b, m_ln_b_g, m_ln_b_b, m_pool_b, m_pool_scale, m_ln_g, m_ln_b),
                       pack_small(v_conv_a_b, v_conv_b_b, v_ln_b_g, v_ln_b_b, v_pool_b, v_pool_scale, v_ln_g, v_ln_b),
                       "adamw_small")
    out_pw = _adamw(gpw_parts, flat(pool_w), flat(m_pool_w), flat(v_pool_w), "adamw_pool_w")
    out_pw = [a.reshape(pool_w.shape) for a in out_pw]

    def unpack_small(a):
        a = a.reshape(depth, SMALL_ROWS, W_MIX)
        lnp = a[:, N_ROWS:].reshape(depth, SUB, D_MODEL)
        return dict(conv_a_b=a[:, R_CAB], conv_b_b=a[:, R_CBB], ln_b_g=a[:, R_LBG], ln_b_b=a[:, R_LBB],
                    pool_b=a[:, R_PB].reshape(pool_b.shape), pool_scale=a[:, R_PS], ln_g=lnp[:, 0], ln_b=lnp[:, 1],
                    conv=a[:, R_CAW:N_ROWS])
    small_out = [unpack_small(a) for a in out_small]

    g_conv = lax.dynamic_slice_in_dim(small_out[0]["conv"], dev * cs, cs, axis=2)
    conv_rows = depth * (KA + KB)
    conv_pad = -conv_rows % SUB
    pack_conv = lambda a, b: _pad_rows(jnp.concatenate([a, b], axis=1).reshape(conv_rows, cs), conv_rows + conv_pad)
    out_conv = _adamw(_pad_rows(g_conv.reshape(conv_rows, cs), conv_rows + conv_pad)[None],
                      pack_conv(conv_a_w, conv_b_w), pack_conv(m_conv_a_w, m_conv_b_w),
                      pack_conv(v_conv_a_w, v_conv_b_w), "adamw_conv")
    out_conv = [a[:conv_rows].reshape(depth, KA + KB, cs) for a in out_conv]

    outs = []
    for k in range(4):
        s, cv = small_out[k], out_conv[k]
        outs.append([out_in[k], cv[:, :KA], s["conv_a_b"], cv[:, KA:], s["conv_b_b"], s["ln_b_g"], s["ln_b_b"],
                     out_pw[k], s["pool_b"], s["pool_scale"], out_out[k], s["ln_g"], s["ln_b"]])
    return (loss, grad_x, *outs[0], *outs[1], *outs[2], *outs[3])
```

```python
import jax
import jax.numpy as jnp
from jax import lax
from jax.experimental import pallas as pl
from jax.experimental.pallas import tpu as pltpu

F32 = jnp.float32
BF16 = jnp.bfloat16

DEPTH = 2
D_MODEL = 1024
W_MIX = 512
D_IN = 9 * W_MIX
D_MIX = 3 * W_MIX
POOL_WINDOWS = (2, 4, 8, 16)
POOL_DIM = 128
KA = 3
KB = 31
ALPHA = (2.0 * DEPTH) ** 0.25
LN_EPS = 1e-5
ADAM_LR, ADAM_B1, ADAM_B2, ADAM_EPS, ADAM_WD, ADAM_STEP = 0.001, 0.9, 0.999, 1e-08, 0.01, 10

N_DEV = 8
N_CHIP = 4
MESH = pl.DeviceIdType.MESH

CH = 32
SUB = 8
VMEM_LIMIT = 56 * 1024 * 1024

R_CAB, R_CBB, R_LBG, R_LBB, R_PB, R_PS, R_CAW, R_CBW = 0, 1, 2, 3, 4, 5, 6, 9
N_ROWS = R_CBW + KB
LN_ROWS = 16
SMALL_ROWS = N_ROWS + LN_ROWS


def _cparams(sem, **kw):
    return pltpu.CompilerParams(dimension_semantics=sem, vmem_limit_bytes=VMEM_LIMIT, **kw)


def _sigmoid(v):
    return 1.0 / (1.0 + jnp.exp(-v))


def _fold8(a):
    r, c = a.shape
    return a.reshape(r // SUB, SUB, c).sum(axis=0)


def _dot(a, b):
    return jnp.dot(a, b, preferred_element_type=F32)


def _dot_nt(a, b):
    return lax.dot_general(a, b, (((1,), (1,)), ((), ())), preferred_element_type=F32)


def _dot_tn(a, b):
    return lax.dot_general(a, b, (((0,), (0,)), ((), ())), preferred_element_type=F32)


def _row_tile(r):
    for cand in (512, 256, 128, 64, 32, 16, 8):
        if r % cand == 0 and r > cand:
            return cand
    return r


def _position():
    return lax.axis_index("x"), lax.axis_index("y"), lax.axis_index("c")


class _Gather:
    def __init__(self, arrs):
        self.arrs = list(arrs)
        na = len(self.arrs)
        self.out_shape = [jax.ShapeDtypeStruct((N_DEV,) + a.shape, a.dtype) for a in self.arrs]
        self.scratch = [pltpu.SemaphoreType.DMA((na, 7)), pltpu.SemaphoreType.DMA((na, 7)),
                        pltpu.SemaphoreType.DMA((na,))]

    def _copies(self, src, dst, sems):
        send_sems, recv_sems, local_sems = sems
        na = len(self.arrs)
        x, y, c = _position()
        me, sibling = (x, y, c), (x, y, 1 - c)
        chips = [(1 - x, y), (x, 1 - y), (1 - x, 1 - y)]

        def slot(a, dev):
            return dst[a].at[4 * dev[0] + 2 * dev[1] + dev[2]]

        def copy(a, k, block, to, from_src=False):
            return pltpu.make_async_remote_copy(
                src_ref=src[a] if from_src else slot(a, block), dst_ref=slot(a, block),
                send_sem=send_sems.at[a, k], recv_sem=recv_sems.at[a, k], device_id=to, device_id_type=MESH)

        mine = [pltpu.make_async_copy(src[a], slot(a, me), local_sems.at[a]) for a in range(na)]
        first, landed, passed, last = [], [], [], []
        for a in range(na):
            first.append(copy(a, 0, me, sibling, from_src=True))
            first += [copy(a, 1 + j, me, (*chip, c), from_src=True) for j, chip in enumerate(chips)]
        for j, chip in enumerate(chips):
            for a in range(na):
                landed.append(copy(a, 1 + j, (*chip, c), me))
                passed.append(copy(a, 4 + j, (*chip, c), sibling))
        for a in range(na):
            last.append(copy(a, 0, sibling, me))
            last += [copy(a, 4 + j, (*chip, 1 - c), me) for j, chip in enumerate(chips)]
        return mine, first, landed, passed, last

    def start(self, src, dst, sems):
        mine, first, _, _, _ = self._copies(src, dst, sems)
        for cp in mine + first:
            cp.start()

    def pass_on(self, src, dst, sems):
        _, _, landed, passed, _ = self._copies(src, dst, sems)
        for got, cp in zip(landed, passed):
            got.wait_recv()
            cp.start()

    def finish(self, src, dst, sems):
        mine, first, _, passed, last = self._copies(src, dst, sems)
        for cp in last:
            cp.wait_recv()
        for cp in first + passed:
            cp.wait_send()
        for cp in mine:
            cp.wait()


class _Exchange:
    def __init__(self, arrs):
        self.arrs = list(arrs)
        na = len(self.arrs)
        self.out_shape = [jax.ShapeDtypeStruct(a.shape, a.dtype) for a in self.arrs]
        self.scratch = [pltpu.SemaphoreType.DMA((na, 3)), pltpu.SemaphoreType.DMA((na, 3)),
                        pltpu.SemaphoreType.DMA((na,))]

    def _copies(self, src, dst, sems):
        send_sems, recv_sems, local_sems = sems
        na = len(self.arrs)
        x, y, c = _position()
        my_chip = 2 * x + y
        chips = [(1 - x, y), (x, 1 - y), (1 - x, 1 - y)]
        mine = [pltpu.make_async_copy(src[a].at[my_chip], dst[a].at[my_chip], local_sems.at[a]) for a in range(na)]
        copies = []
        for a in range(na):
            for j, chip in enumerate(chips):
                copies.append(pltpu.make_async_remote_copy(
                    src_ref=src[a].at[2 * chip[0] + chip[1]], dst_ref=dst[a].at[my_chip],
                    send_sem=send_sems.at[a, j], recv_sem=recv_sems.at[a, j], device_id=(*chip, c),
                    device_id_type=MESH))
        return mine, copies

    def start(self, src, dst, sems):
        mine, copies = self._copies(src, dst, sems)
        for cp in mine + copies:
            cp.start()

    def pass_on(self, src, dst, sems):
        pass

    def finish(self, src, dst, sems):
        mine, copies = self._copies(src, dst, sems)
        for cp in copies:
            cp.wait()
        for cp in mine:
            cp.wait()


def _run_plan(plan, name):
    na = len(plan.arrs)
    any_spec = pl.BlockSpec(memory_space=pl.ANY)

    def body(*refs):
        src, dst, sems = refs[:na], refs[na:2 * na], refs[2 * na:]
        plan.start(src, dst, sems)
        plan.pass_on(src, dst, sems)
        plan.finish(src, dst, sems)

    return pl.pallas_call(
        body, name=name, in_specs=[any_spec] * na, out_specs=[any_spec] * na,
        out_shape=plan.out_shape, scratch_shapes=plan.scratch,
    )(*plan.arrs)


class _Hosted:
    def __init__(self, plan, n_in, n_out, n_scratch):
        self.plan, self.n_in, self.n_out, self.n_scratch = plan, n_in, n_out, n_scratch
        any_spec = pl.BlockSpec(memory_space=pl.ANY)
        k = 0 if plan is None else len(plan.arrs)
        self.operands = [] if plan is None else plan.arrs
        self.in_specs = [any_spec] * k
        self.out_specs = [any_spec] * k
        self.out_shape = [] if plan is None else plan.out_shape
        self.scratch = [] if plan is None else plan.scratch

    def wrap(self, body, phase):
        if self.plan is None:
            return body
        plan, k = self.plan, len(self.plan.arrs)
        i0, o0 = self.n_in, self.n_in + k
        o1 = o0 + self.n_out
        s0 = o1 + k
        s1 = s0 + self.n_scratch

        def hosted(*refs):
            src, dst, sems = refs[i0:o0], refs[o1:s0], refs[s1:]
            first, middle, last = phase()
            pl.when(first)(lambda: plan.start(src, dst, sems))
            body(*refs[:i0], *refs[o0:o1], *refs[s0:s1])
            pl.when(middle)(lambda: plan.pass_on(src, dst, sems))
            pl.when(last)(lambda: plan.finish(src, dst, sems))

        return hosted


def _sibling_swap(arrs, name):
    na = len(arrs)
    any_spec = pl.BlockSpec(memory_space=pl.ANY)

    def body(*refs):
        src, dst = refs[:na], refs[na:2 * na]
        send_sems, recv_sems = refs[2 * na:]
        x, y, c = _position()
        sibling = (x, y, 1 - c)
        copies = []
        for a in range(na):
            for k in range(N_CHIP):
                copies.append(pltpu.make_async_remote_copy(
                    src_ref=src[a].at[2 * k + (1 - c)], dst_ref=dst[a].at[k],
                    send_sem=send_sems.at[a, k], recv_sem=recv_sems.at[a, k], device_id=sibling, device_id_type=MESH))
        for cp in copies:
            cp.start()
        for cp in copies:
            cp.wait()

    return pl.pallas_call(
        body, name=name,
        in_specs=[any_spec] * na, out_specs=[any_spec] * na,
        out_shape=[jax.ShapeDtypeStruct((N_CHIP,) + a.shape[1:], a.dtype) for a in arrs],
        scratch_shapes=[pltpu.SemaphoreType.DMA((na, N_CHIP)), pltpu.SemaphoreType.DMA((na, N_CHIP))],
    )(*arrs)


def _pair_sum(mine, got, core, name):
    _, r, c = mine.shape
    tr = _row_tile(r)

    def body(core_ref, a_ref, b_ref, o_ref):
        o_ref[...] = a_ref[...] + b_ref[...]

    return pl.pallas_call(
        body, name=name,
        grid_spec=pltpu.PrefetchScalarGridSpec(
            num_scalar_prefetch=1, grid=(N_CHIP, r // tr),
            in_specs=[pl.BlockSpec((1, tr, c), lambda k, i, core_ref: (2 * k + core_ref[0], i, 0)),
                      pl.BlockSpec((1, tr, c), lambda k, i, core_ref: (k, i, 0))],
            out_specs=pl.BlockSpec((1, tr, c), lambda k, i, core_ref: (k, i, 0))),
        out_shape=jax.ShapeDtypeStruct((N_CHIP, r, c), mine.dtype),
        compiler_params=_cparams(("arbitrary", "arbitrary")),
    )(core, mine, got)


TT = 256
SEG = TT // SUB


def _to_segments(a):
    t, c = a.shape
    return a.reshape(t // TT, SUB, SEG, c).transpose(0, 2, 1, 3).reshape(t, c)


def _from_segments(a):
    t, c = a.shape
    return a.reshape(t // TT, SEG, SUB, c).transpose(0, 2, 1, 3).reshape(t, c)


def _sublane_is(s):
    return lax.broadcasted_iota(jnp.int32, (TT, W_MIX), 0) % SUB == s


def _look_back(ext_ref, cur, before):
    ext_ref[TT:, :] = cur
    ext_ref[0:TT, :] = jnp.where(_sublane_is(0), before, pltpu.roll(cur, 1, axis=0))


def _last_segment(cur):
    return pltpu.roll(cur, TT - (SUB - 1), axis=0)


def _look_ahead(ext_ref, cur, after):
    ext_ref[0:TT, :] = cur
    ext_ref[TT:, :] = jnp.where(_sublane_is(SUB - 1), after, pltpu.roll(cur, TT - 1, axis=0))


def _first_segment(cur):
    return pltpu.roll(cur, SUB - 1, axis=0)


def _tap_loop(body):
    lax.fori_loop(0, TT // CH, lambda c, carry: body(pl.multiple_of(c * CH, CH), carry), 0)


def _conv_rows(ext_ref, w_ref, out_ref, *, nk, off, reverse):
    def body(r0, carry):
        acc = jnp.zeros((CH, W_MIX), F32)
        for k in range(nk):
            kk = nk - 1 - k if reverse else k
            acc = acc + ext_ref[pl.ds(r0 + (off + k) * SUB, CH), :] * w_ref[kk:kk + 1, :]
        out_ref[pl.ds(r0, CH), :] = acc
        return carry
    _tap_loop(body)


def _conv_wgrad(g_ref, ext_ref, acc_ref, *, nk, off, row0):
    def body(r0, carry):
        g = g_ref[pl.ds(r0, CH), :]
        for k in range(nk):
            a = (row0 + k) * SUB
            acc_ref[a:a + SUB, :] += _fold8(g * ext_ref[pl.ds(r0 + (off + k) * SUB, CH), :])
        return carry
    _tap_loop(body)


def _window_sums(ext_ref, out_ref, *, forward):
    def body(r0, carry):
        for g, w in enumerate(POOL_WINDOWS):
            lanes = slice(g * POOL_DIM, (g + 1) * POOL_DIM)
            acc = jnp.zeros((CH, POOL_DIM), F32)
            for j in range(w):
                off = j if forward else SEG - j
                acc = acc + ext_ref[pl.ds(r0 + off * SUB, CH), lanes]
            out_ref[pl.ds(r0, CH), lanes] = acc
        return carry
    _tap_loop(body)


def _inv_count(tile):
    r = lax.broadcasted_iota(jnp.int32, (TT, POOL_DIM), 0)
    t1 = (tile * TT + (r % SUB) * SEG + r // SUB + 1).astype(F32)
    return jnp.concatenate([1.0 / jnp.minimum(t1, float(w)) for w in POOL_WINDOWS], axis=1)


def _groups(h_ref):
    return [h_ref[:, k * W_MIX:(k + 1) * W_MIX].astype(F32) for k in range(9)]


def _layer_norm(v, g, b):
    mu = jnp.mean(v, axis=-1, keepdims=True)
    vc = v - mu
    var = jnp.mean(vc * vc, axis=-1, keepdims=True)
    rstd = lax.rsqrt(var + LN_EPS)
    vhat = vc * rstd
    return vhat * g + b, vhat, rstd


def _layer_norm_bwd(g_out, vhat, rstd, g):
    gh = g_out * g
    m1 = jnp.mean(gh, axis=-1, keepdims=True)
    m2 = jnp.mean(gh * vhat, axis=-1, keepdims=True)
    return rstd * (gh - m1 - vhat * m2)


def _pool_linear(pooled, pw_ref, pb_ref):
    outs = []
    for g in range(len(POOL_WINDOWS)):
        lanes = slice(g * POOL_DIM, (g + 1) * POOL_DIM)
        outs.append(_dot(pooled[:, lanes].astype(BF16), pw_ref[g].astype(BF16)))
    return jnp.concatenate(outs, axis=1) + pb_ref[...]


def _in_proj(x, w, name, plan=None):
    t, d = x.shape
    n = w.shape[1]
    tm, tn = min(t, 1024), 1536
    nm, nn = t // tm, n // tn

    def body(x_ref, w_ref, o_ref, xb_ref):
        @pl.when(pl.program_id(1) == 0)
        def _():
            xb_ref[...] = x_ref[...].astype(BF16)
        o_ref[...] = _dot(xb_ref[...], w_ref[...]).astype(BF16)

    def phase():
        step = pl.program_id(0) * nn + pl.program_id(1)
        return step == 0, step == (nm * nn) // 2, step == nm * nn - 1

    host = _Hosted(plan, n_in=2, n_out=1, n_scratch=1)
    return pl.pallas_call(
        host.wrap(body, phase), name=name, grid=(nm, nn),
        in_specs=[pl.BlockSpec((tm, d), lambda i, j: (i, 0)), pl.BlockSpec((d, tn), lambda i, j: (0, j))]
        + host.in_specs,
        out_specs=[pl.BlockSpec((tm, tn), lambda i, j: (i, j))] + host.out_specs,
        out_shape=[jax.ShapeDtypeStruct((t, n), BF16)] + host.out_shape,
        scratch_shapes=[pltpu.VMEM((tm, d), BF16)] + host.scratch,
        compiler_params=_cparams(("arbitrary", "arbitrary")),
    )(x, w, *host.operands)


def _mixer_fwd(h, x, caw, cab, cbw, cbb, lbg, lbb, pw, pb, ps, w_out, lng, lnb, name, plan=None, target=None):
    t = h.shape[0]
    tt = TT
    n = t // tt
    with_loss = target is not None

    def body(h_ref, x_ref, *refs):
        if with_loss:
            t_ref, refs = refs[0], refs[1:]
        (caw_ref, cab_ref, cbw_ref, cbb_ref, lbg_ref, lbb_ref, pw_ref, pb_ref, ps_ref, wo_ref, lng_ref, lnb_ref,
         y_ref, u2_ref, z_ref, xn_ref) = refs[:16]
        refs = refs[16:]
        if with_loss:
            l_ref, refs = refs[0], refs[1:]
        exta, extb, extc, lasta, lastb, lastc, tmp, inv_ref = refs
        i = pl.program_id(0)

        @pl.when(i == 0)
        def _():
            for e in (lasta, lastb, lastc):
                e[...] = jnp.zeros_like(e)
            if with_loss:
                l_ref[...] = jnp.zeros_like(l_ref)

        @pl.when(i <= 1)
        def _():
            inv_ref[...] = _inv_count(i)

        a_bg, a_cg, a_v, a_z, b_v, b_g, b_z, c_u, c_z = _groups(h_ref)
        for ext, last, cur in ((exta, lasta, a_cg * a_v), (extb, lastb, b_v * _sigmoid(b_g)), (extc, lastc, c_u)):
            _look_back(ext, cur, last[...])
            last[...] = _last_segment(cur)

        _conv_rows(exta, caw_ref, tmp, nk=KA, off=SEG - (KA - 1), reverse=False)
        y_a = a_bg * (tmp[...] + cab_ref[...]) * (a_z * _sigmoid(a_z))
        y_ref[:, 0:W_MIX] = y_a.astype(BF16)

        _window_sums(extc, tmp, forward=False)
        pooled = tmp[...] * inv_ref[...] - c_u
        p = _pool_linear(pooled, pw_ref, pb_ref)
        y_c = p * ps_ref[...] * (c_z * _sigmoid(c_z))
        y_ref[:, 2 * W_MIX:3 * W_MIX] = y_c.astype(BF16)

        _conv_rows(extb, cbw_ref, tmp, nk=KB, off=SEG - (KB - 1), reverse=False)
        u2 = tmp[...] + cbb_ref[...]
        u2_ref[...] = u2
        ln, _, _ = _layer_norm(u2, lbg_ref[...], lbb_ref[...])
        y_b = (ln * _sigmoid(ln)) * (b_z * _sigmoid(b_z))
        y_ref[:, W_MIX:2 * W_MIX] = y_b.astype(BF16)

        out = _dot(y_ref[...], wo_ref[...])
        z = ALPHA * x_ref[...] + out
        z_ref[...] = z
        xn, _, _ = _layer_norm(z, lng_ref[...], lnb_ref[...])
        if with_loss:
            e = xn - t_ref[...]
            xn_ref[...] = e * (1.0 / D_MODEL)
            l_ref[...] += _fold8(e * e) * (0.5 / D_MODEL)
        else:
            xn_ref[...] = xn

    def phase():
        i = pl.program_id(0)
        return i == 0, i == n // 2, i == n - 1

    row = lambda wd: pl.BlockSpec((tt, wd), lambda i: (i, 0))
    full = lambda a: pl.BlockSpec(a.shape, lambda i: (0,) * a.ndim)
    params = (caw, cab, cbw, cbb, lbg, lbb, pw, pb, ps, w_out, lng, lnb)
    extra_in = [target] if with_loss else []
    extra_out = [jax.ShapeDtypeStruct((SUB, D_MODEL), F32)] if with_loss else []
    host = _Hosted(plan, n_in=2 + len(extra_in) + len(params), n_out=4 + len(extra_out), n_scratch=8)
    return pl.pallas_call(
        host.wrap(body, phase), name=name, grid=(n,),
        in_specs=[row(D_IN), row(D_MODEL)] + [row(D_MODEL)] * len(extra_in) + [full(a) for a in params]
        + host.in_specs,
        out_specs=[row(D_MIX), row(W_MIX), row(D_MODEL), row(D_MODEL)] + [full(o) for o in extra_out]
        + host.out_specs,
        out_shape=[jax.ShapeDtypeStruct((t, D_MIX), BF16), jax.ShapeDtypeStruct((t, W_MIX), F32),
                   jax.ShapeDtypeStruct((t, D_MODEL), F32), jax.ShapeDtypeStruct((t, D_MODEL), F32)]
        + extra_out + host.out_shape,
        scratch_shapes=[pltpu.VMEM((2 * tt, W_MIX), F32)] * 3 + [pltpu.VMEM((tt, W_MIX), F32)] * 5 + host.scratch,
        compiler_params=_cparams(("arbitrary",)),
    )(h, x, *extra_in, *params, *host.operands)


def _out_proj_bwd(g_xn, z, y, w_out, lng, name):
    t = z.shape[0]
    tt = min(t, 256)

    def body(g_ref, z_ref, y_ref, wo_ref, lng_ref, gz_ref, gy_ref, gwo_ref, gln_ref, accg, accb):
        i = pl.program_id(0)

        @pl.when(i == 0)
        def _():
            gwo_ref[...] = jnp.zeros_like(gwo_ref)
            accg[...] = jnp.zeros_like(accg)
            accb[...] = jnp.zeros_like(accb)

        g = g_ref[...]
        _, zhat, rstd = _layer_norm(z_ref[...], lng_ref[...], 0.0)
        accg[...] += _fold8(g * zhat)
        accb[...] += _fold8(g)
        g_z = _layer_norm_bwd(g, zhat, rstd, lng_ref[...])
        gz_ref[...] = g_z
        gzb = g_z.astype(BF16)
        gy_ref[...] = _dot_nt(gzb, wo_ref[...])
        gwo_ref[...] += _dot_tn(y_ref[...], gzb)

        @pl.when(i == pl.num_programs(0) - 1)
        def _():
            gln_ref[...] = jnp.zeros_like(gln_ref)
            gln_ref[0:1, :] = jnp.sum(accg[...], axis=0, keepdims=True)
            gln_ref[1:2, :] = jnp.sum(accb[...], axis=0, keepdims=True)

    row = lambda wd: pl.BlockSpec((tt, wd), lambda i: (i, 0))
    full = lambda shape: pl.BlockSpec(shape, lambda i: (0,) * len(shape))
    return pl.pallas_call(
        body, name=name, grid=(t // tt,),
        in_specs=[row(D_MODEL), row(D_MODEL), row(D_MIX), full(w_out.shape), full(lng.shape)],
        out_specs=[row(D_MODEL), row(D_MIX), full((D_MIX, D_MODEL)), full((SUB, D_MODEL))],
        out_shape=[jax.ShapeDtypeStruct((t, D_MODEL), F32), jax.ShapeDtypeStruct((t, D_MIX), F32),
                   jax.ShapeDtypeStruct((D_MIX, D_MODEL), F32), jax.ShapeDtypeStruct((SUB, D_MODEL), F32)],
        scratch_shapes=[pltpu.VMEM((SUB, D_MODEL), F32)] * 2,
        compiler_params=_cparams(("arbitrary",)),
    )(g_xn, z, y, w_out, lng)


def _mixer_bwd(h, u2, g_y, caw, cab, cbw, lbg, lbb, pw, pb, ps, name, plan=None):
    t = h.shape[0]
    tt = TT
    n = t // tt
    before_groups = (1, 2, 4, 5, 7)

    def body(h_ref, p_cg, p_av, p_bv, p_bg, p_cu, u2_ref, gy_ref, caw_ref, cab_ref, cbw_ref, lbg_ref, lbb_ref,
             pw_ref, pb_ref, ps_ref, gh_ref, rows_ref, gpw_ref,
             exta, extb, extc, gca, gu2, qx, nexta, nextb, nextc, tmp, tmp2, inv_ref, acc):
        s = pl.program_id(0)
        i = n - 1 - s

        @pl.when(s == 0)
        def _():
            acc[...] = jnp.zeros_like(acc)
            gpw_ref[...] = jnp.zeros_like(gpw_ref)
            for e in (nexta, nextb, nextc):
                e[...] = jnp.zeros_like(e)

        live = (i > 0).astype(F32)
        f32 = lambda ref: ref[...].astype(F32)
        before_a = _last_segment(f32(p_cg) * f32(p_av)) * live
        before_b = _last_segment(f32(p_bv) * _sigmoid(f32(p_bg))) * live
        before_c = _last_segment(f32(p_cu)) * live

        a_bg, a_cg, a_v, a_z, b_v, b_g, b_z, c_u, c_z = _groups(h_ref)
        g_ya = gy_ref[:, 0:W_MIX]
        g_yb = gy_ref[:, W_MIX:2 * W_MIX]
        g_yc = gy_ref[:, 2 * W_MIX:3 * W_MIX]

        def add_row(r, v):
            acc[r * SUB:(r + 1) * SUB, :] += _fold8(v)

        _look_back(exta, a_cg * a_v, before_a)
        _conv_rows(exta, caw_ref, tmp, nk=KA, off=SEG - (KA - 1), reverse=False)
        ca = tmp[...] + cab_ref[...]
        sg = _sigmoid(a_z)
        s_az = a_z * sg
        t_a = g_ya * a_bg
        gh_ref[:, 0:W_MIX] = (g_ya * ca * s_az).astype(BF16)
        gh_ref[:, 3 * W_MIX:4 * W_MIX] = (t_a * ca * (sg * (1.0 + a_z * (1.0 - sg)))).astype(BF16)
        g_ca = t_a * s_az
        _look_ahead(gca, g_ca, nexta[...])
        nexta[...] = _first_segment(g_ca)
        add_row(R_CAB, g_ca)
        _conv_wgrad(gca, exta, acc, nk=KA, off=SEG - (KA - 1), row0=R_CAW)
        _conv_rows(gca, caw_ref, tmp, nk=KA, off=0, reverse=True)
        g_pa = tmp[...]
        gh_ref[:, W_MIX:2 * W_MIX] = (g_pa * a_v).astype(BF16)
        gh_ref[:, 2 * W_MIX:3 * W_MIX] = (g_pa * a_cg).astype(BF16)

        sgg = _sigmoid(b_g)
        _look_back(extb, b_v * sgg, before_b)
        ln, u2hat, rstd = _layer_norm(u2_ref[...], lbg_ref[...], lbb_ref[...])
        sl = _sigmoid(ln)
        u3 = ln * sl
        sz = _sigmoid(b_z)
        s_bz = b_z * sz
        gh_ref[:, 6 * W_MIX:7 * W_MIX] = (g_yb * u3 * (sz * (1.0 + b_z * (1.0 - sz)))).astype(BF16)
        g_ln = g_yb * s_bz * (sl * (1.0 + ln * (1.0 - sl)))
        add_row(R_LBG, g_ln * u2hat)
        add_row(R_LBB, g_ln)
        g_u2 = _layer_norm_bwd(g_ln, u2hat, rstd, lbg_ref[...])
        _look_ahead(gu2, g_u2, nextb[...])
        nextb[...] = _first_segment(g_u2)
        add_row(R_CBB, g_u2)
        _conv_wgrad(gu2, extb, acc, nk=KB, off=SEG - (KB - 1), row0=R_CBW)
        _conv_rows(gu2, cbw_ref, tmp, nk=KB, off=0, reverse=True)
        g_u1 = tmp[...]
        gh_ref[:, 4 * W_MIX:5 * W_MIX] = (g_u1 * sgg).astype(BF16)
        gh_ref[:, 5 * W_MIX:6 * W_MIX] = (g_u1 * b_v * sgg * (1.0 - sgg)).astype(BF16)

        _look_back(extc, c_u, before_c)
        _window_sums(extc, tmp, forward=False)
        @pl.when((s == 0) | (i == 0))
        def _():
            inv_ref[...] = _inv_count(i)
        inv = inv_ref[...]
        pooled = tmp[...] * inv - c_u
        p = _pool_linear(pooled, pw_ref, pb_ref)
        sc = _sigmoid(c_z)
        s_cz = c_z * sc
        scale = ps_ref[...]
        gh_ref[:, 8 * W_MIX:9 * W_MIX] = (g_yc * p * scale * (sc * (1.0 + c_z * (1.0 - sc)))).astype(BF16)
        t_c = g_yc * s_cz
        add_row(R_PS, t_c * p)
        g_p = t_c * scale
        add_row(R_PB, g_p)
        g_pooled = []
        for g in range(len(POOL_WINDOWS)):
            lanes = slice(g * POOL_DIM, (g + 1) * POOL_DIM)
            gpg = g_p[:, lanes].astype(BF16)
            gpw_ref[g] += _dot_tn(pooled[:, lanes].astype(BF16), gpg)
            g_pooled.append(_dot_nt(gpg, pw_ref[g].astype(BF16)))
        g_pooled = jnp.concatenate(g_pooled, axis=1)
        q = g_pooled * inv
        _look_ahead(qx, q, nextc[...])
        nextc[...] = _first_segment(q)
        _window_sums(qx, tmp2, forward=True)
        gh_ref[:, 7 * W_MIX:8 * W_MIX] = (tmp2[...] - g_pooled).astype(BF16)

        @pl.when(s == n - 1)
        def _():
            for r in range(N_ROWS):
                rows_ref[r:r + 1, :] = jnp.sum(acc[r * SUB:(r + 1) * SUB, :], axis=0, keepdims=True)

    def phase():
        s = pl.program_id(0)
        return s == 0, s == n // 2, s == n - 1

    row = lambda wd: pl.BlockSpec((tt, wd), lambda s: (n - 1 - s, 0))
    before = [pl.BlockSpec((tt, W_MIX), lambda s, k=k: (jnp.maximum(n - 2 - s, 0), k)) for k in before_groups]
    full = lambda shape: pl.BlockSpec(shape, lambda s: (0,) * len(shape))
    params = (caw, cab, cbw, lbg, lbb, pw, pb, ps)
    host = _Hosted(plan, n_in=3 + len(before) + len(params), n_out=3, n_scratch=13)
    return pl.pallas_call(
        host.wrap(body, phase), name=name, grid=(n,),
        in_specs=[row(D_IN)] + before + [row(W_MIX), row(D_MIX)] + [full(a.shape) for a in params] + host.in_specs,
        out_specs=[row(D_IN), full((N_ROWS, W_MIX)), full(pw.shape)] + host.out_specs,
        out_shape=[jax.ShapeDtypeStruct((t, D_IN), BF16), jax.ShapeDtypeStruct((N_ROWS, W_MIX), F32),
                   jax.ShapeDtypeStruct(pw.shape, F32)] + host.out_shape,
        scratch_shapes=[pltpu.VMEM((2 * tt, W_MIX), F32)] * 6 + [pltpu.VMEM((tt, W_MIX), F32)] * 6
        + [pltpu.VMEM((N_ROWS * SUB, W_MIX), F32)] + host.scratch,
        compiler_params=_cparams(("arbitrary",)),
    )(h, *([h] * len(before)), u2, g_y, *params, *host.operands)


def _in_proj_wgrad(x, g_h, name, plan=None):
    t, d = x.shape
    n = g_h.shape[1]
    nb = n // N_DEV
    per = N_DEV // 2
    tk, tn = min(t, 512), per * nb
    nk = t // tk

    def body(x_ref, g_ref, o_ref, acc):
        k = pl.program_id(1)

        @pl.when(k == 0)
        def _():
            acc[...] = jnp.zeros_like(acc)
        acc[...] += _dot_tn(x_ref[...].astype(BF16), g_ref[...])

        @pl.when(k == nk - 1)
        def _():
            for b in range(per):
                o_ref[b] = acc[:, b * nb:(b + 1) * nb]

    def phase():
        step = pl.program_id(0) * nk + pl.program_id(1)
        return step == 0, step == nk, step == 2 * nk - 1

    host = _Hosted(plan, n_in=2, n_out=1, n_scratch=1)
    return pl.pallas_call(
        host.wrap(body, phase), name=name, grid=(n // tn, nk),
        in_specs=[pl.BlockSpec((tk, d), lambda j, k: (k, 0)), pl.BlockSpec((tk, tn), lambda j, k: (k, j))]
        + host.in_specs,
        out_specs=[pl.BlockSpec((per, d, nb), lambda j, k: (j, 0, 0))] + host.out_specs,
        out_shape=[jax.ShapeDtypeStruct((N_DEV, d, nb), F32)] + host.out_shape,
        scratch_shapes=[pltpu.VMEM((d, tn), F32)] + host.scratch,
        compiler_params=_cparams(("arbitrary", "arbitrary")),
    )(x, g_h, *host.operands)


def _in_proj_dgrad(g_h, w, g_z, name, plan=None):
    t, n = g_h.shape
    d = w.shape[0]
    tm, tk = min(t, 1024), 1536
    nm, nk = t // tm, n // tk

    def body(g_ref, w_ref, gz_ref, o_ref):
        @pl.when(pl.program_id(1) == 0)
        def _():
            o_ref[...] = ALPHA * gz_ref[...]
        o_ref[...] += _dot_nt(g_ref[...], w_ref[...])

    def phase():
        step = pl.program_id(0) * nk + pl.program_id(1)
        return step == 0, step == (nm * nk) // 2, step == nm * nk - 1

    host = _Hosted(plan, n_in=3, n_out=1, n_scratch=0)
    return pl.pallas_call(
        host.wrap(body, phase), name=name, grid=(nm, nk),
        in_specs=[pl.BlockSpec((tm, tk), lambda i, k: (i, k)), pl.BlockSpec((d, tk), lambda i, k: (0, k)),
                  pl.BlockSpec((tm, d), lambda i, k: (i, 0))] + host.in_specs,
        out_specs=[pl.BlockSpec((tm, d), lambda i, k: (i, 0))] + host.out_specs,
        out_shape=[jax.ShapeDtypeStruct((t, d), F32)] + host.out_shape,
        scratch_shapes=host.scratch,
        compiler_params=_cparams(("arbitrary", "arbitrary")),
    )(g_h, w, g_z, *host.operands)


BC1 = 1.0 - ADAM_B1 ** ADAM_STEP
BC2 = 1.0 - ADAM_B2 ** ADAM_STEP


def _adamw_math(g, w, m, v):
    nm = ADAM_B1 * m + (1.0 - ADAM_B1) * g
    nv = ADAM_B2 * v + (1.0 - ADAM_B2) * (g * g)
    delta = -ADAM_LR * ((nm / BC1) / (jnp.sqrt(nv / BC2) + ADAM_EPS) + ADAM_WD * w)
    return delta, nm, nv


def _total(ref):
    g = ref[0].astype(F32)
    for k in range(1, ref.shape[0]):
        g = g + ref[k].astype(F32)
    return g


def _adamw_layers(parts, w, m, v, name):
    depth, r, c = w.shape
    p = parts[0].shape[0]
    tr = _row_tile(r)
    nr = r // tr

    def body(*refs):
        p_refs = refs[:depth]
        w_ref, m_ref, v_ref, g_ref, d_ref, nm_ref, nv_ref = refs[depth:]
        for l in range(depth):
            @pl.when(pl.program_id(0) == l)
            def _(l=l):
                g = _total(p_refs[l])
                delta, nm, nv = _adamw_math(g, w_ref[0], m_ref[0], v_ref[0])
                g_ref[0], d_ref[0], nm_ref[0], nv_ref[0] = g, delta, nm, nv

    def part_spec(l):
        return pl.BlockSpec((p, tr, c), lambda li, i: (0, jnp.where(li == l, i, jnp.where(li < l, 0, nr - 1)), 0))

    blk = pl.BlockSpec((1, tr, c), lambda li, i: (li, i, 0))
    out = jax.ShapeDtypeStruct((depth, r, c), F32)
    return pl.pallas_call(
        body, name=name, grid=(depth, nr),
        in_specs=[part_spec(l) for l in range(depth)] + [blk] * 3,
        out_specs=[blk] * 4, out_shape=[out] * 4,
        compiler_params=_cparams(("arbitrary", "arbitrary")),
    )(*parts, w, m, v)


SMALL_PARAMS = ("conv_a_w", "conv_a_b", "conv_b_w", "conv_b_b", "ln_b_g", "ln_b_b", "pool_w", "pool_b", "pool_scale",
                "ln_g", "ln_b")


def _adamw_small(rows_parts, gln_parts, gpw_parts, loss_parts, params, name):
    depth = len(rows_parts)
    cs = params["conv_a_w"][0].shape[2]
    operands = [*rows_parts, *gln_parts, *gpw_parts, loss_parts] + [a for n in SMALL_PARAMS for a in params[n]]
    n_in = len(operands)
    out_shape = [jax.ShapeDtypeStruct((1, 1), F32)]
    out_shape += [jax.ShapeDtypeStruct(params[n][0].shape, F32) for n in SMALL_PARAMS for _ in range(4)]

    def body(*refs):
        rows_p, gln_p, gpw_p = refs[:depth], refs[depth:2 * depth], refs[2 * depth:3 * depth]
        loss_p = refs[3 * depth]
        prm, outs = refs[3 * depth + 1:n_in], refs[n_in + 1:]
        refs[n_in][...] = jnp.sum(_total(loss_p)).reshape(1, 1)
        x, y, c = _position()
        to_front = (W_MIX - (4 * x + 2 * y + c) * cs) % W_MIX

        def update(name, g, at):
            k = SMALL_PARAMS.index(name)
            w_ref, m_ref, v_ref = prm[3 * k:3 * k + 3]
            g_ref, d_ref, nm_ref, nv_ref = outs[4 * k:4 * k + 4]
            delta, nm, nv = _adamw_math(g, w_ref[at], m_ref[at], v_ref[at])
            g_ref[at], d_ref[at], nm_ref[at], nv_ref[at] = g, delta, nm, nv

        for l in range(depth):
            rows = _total(rows_p[l])
            mine = pltpu.roll(rows, to_front, axis=1)
            gln = _total(gln_p[l])
            one = (slice(l, l + 1), slice(None))
            for name, r in (("conv_a_b", R_CAB), ("conv_b_b", R_CBB), ("ln_b_g", R_LBG), ("ln_b_b", R_LBB),
                            ("pool_scale", R_PS)):
                update(name, rows[r:r + 1, :], one)
            for g in range(len(POOL_WINDOWS)):
                update("pool_b", rows[R_PB:R_PB + 1, g * POOL_DIM:(g + 1) * POOL_DIM], (l, slice(g, g + 1), slice(None)))
            update("ln_g", gln[0:1, :], one)
            update("ln_b", gln[1:2, :], one)
            update("conv_a_w", mine[R_CAW:R_CAW + KA, 0:cs], (l,))
            update("conv_b_w", mine[R_CBW:R_CBW + KB, 0:cs], (l,))
            update("pool_w", _total(gpw_p[l]), (l,))

    vmem = pl.BlockSpec(memory_space=pltpu.VMEM)
    return pl.pallas_call(
        body, name=name, in_specs=[vmem] * n_in, out_specs=[vmem] * len(out_shape), out_shape=out_shape,
        compiler_params=pltpu.CompilerParams(vmem_limit_bytes=VMEM_LIMIT),
    )(*operands)


def kernel(x, w_in, conv_a_w, conv_a_b, conv_b_w, conv_b_b, ln_b_g, ln_b_b, pool_w, pool_b, pool_scale, w_out, ln_g, ln_b, loss_target, m_w_in, m_conv_a_w, m_conv_a_b, m_conv_b_w, m_conv_b_b, m_ln_b_g, m_ln_b_b, m_pool_w, m_pool_b, m_pool_scale, m_w_out, m_ln_g, m_ln_b, v_w_in, v_conv_a_w, v_conv_a_b, v_conv_b_w, v_conv_b_b, v_ln_b_g, v_ln_b_b, v_pool_w, v_pool_b, v_pool_scale, v_w_out, v_ln_g, v_ln_b):
    core = jnp.reshape(lax.axis_index("c"), (1,)).astype(jnp.int32)
    depth = w_in.shape[0]
    x0 = _to_segments(x[0])
    target = _to_segments(loss_target[0])
    r2 = lambda a: a.reshape(1, -1)

    w_in_b, w_out_b = w_in.astype(BF16), w_out.astype(BF16)
    conv_sh = jnp.concatenate([conv_a_w, conv_b_w], axis=1)
    full_in = lambda g: g.transpose(1, 0, 2).reshape(D_MODEL, D_IN)
    full_out = lambda g: g.reshape(D_MIX, D_MODEL)
    w_in_f = [full_in(_run_plan(_Gather([w_in_b[0]]), "gather_w_in_0")[0])]
    w_out_f = []

    xs, hs, ys, u2s, zs = [x0], [], [], [], []
    for l in range(depth):
        h, *got = _in_proj(xs[l], w_in_f[l], f"in_proj_{l}",
                           plan=_Gather([w_out_b[0], conv_sh]) if l == 0 else None)
        if got:
            w_out_f.append(full_out(got[0]))
            conv_f = got[1].transpose(1, 2, 0, 3).reshape(depth, KA + KB, W_MIX)
            caw_f, cbw_f = conv_f[:, :KA], conv_f[:, KA:]
        last = l + 1 == depth
        y, u2, z, xn, *got = _mixer_fwd(
            h, xs[l], caw_f[l], r2(conv_a_b[l]), cbw_f[l], r2(conv_b_b[l]), r2(ln_b_g[l]), r2(ln_b_b[l]),
            pool_w[l], r2(pool_b[l]), r2(pool_scale[l]), w_out_f[l], r2(ln_g[l]), r2(ln_b[l]), f"mixer_fwd_{l}",
            plan=None if last else _Gather([w_in_b[l + 1], w_out_b[l + 1]]), target=target if last else None)
        if last:
            g, loss_rows = xn, got[0]
        else:
            w_in_f.append(full_in(got[0])), w_out_f.append(full_out(got[1]))
        hs.append(h), ys.append(y), u2s.append(u2), zs.append(z), xs.append(xn)

    def chip_sums(arrs, tag):
        got = _sibling_swap(arrs, f"reduce_swap_{tag}")
        return [_pair_sum(a, b, core, f"pair_sum_{tag}_{k}") for k, (a, b) in enumerate(zip(arrs, got))]

    gi_parts, go_parts, g_rows, g_lns, g_pool_w = ([None] * depth for _ in range(5))
    waiting = []
    for l in reversed(range(depth)):
        g_z, g_y, g_w_out, g_lns[l] = _out_proj_bwd(g, zs[l], ys[l], w_out_f[l], r2(ln_g[l]), f"out_proj_bwd_{l}")
        go = g_w_out.reshape(N_DEV, D_MIX // N_DEV, D_MODEL)
        if l == 0:
            waiting += chip_sums([go], "w_out_0")
        g_h, g_rows[l], g_pool_w[l], *got = _mixer_bwd(
            hs[l], u2s[l], g_y, caw_f[l], r2(conv_a_b[l]), cbw_f[l], r2(ln_b_g[l]), r2(ln_b_b[l]),
            pool_w[l], r2(pool_b[l]), r2(pool_scale[l]), f"mixer_bwd_{l}",
            plan=_Exchange(waiting) if waiting else None)
        if got:
            if l + 1 < depth:
                gi_parts[l + 1], go_parts[l + 1] = got[0], got[1]
            if l == 0:
                go_parts[0] = got[-1]
        if l > 0:
            gi = _in_proj_wgrad(xs[l], g_h, f"in_proj_wgrad_{l}")[0]
            waiting = chip_sums([gi, go], f"layer_{l}")
            g = _in_proj_dgrad(g_h, w_in_f[l], g_z, f"in_proj_dgrad_{l}")[0]
        else:
            small = [*g_rows, *g_lns, *g_pool_w, loss_rows]
            gi, *small_parts = _in_proj_wgrad(xs[0], g_h, "in_proj_wgrad_0", plan=_Gather(small))
            g, gi_parts[0] = _in_proj_dgrad(g_h, w_in_f[0], g_z, "in_proj_dgrad_0",
                                            plan=_Exchange(chip_sums([gi], "w_in_0")))
    grad_x = _from_segments(g)[None]

    out_in = _adamw_layers(gi_parts, w_in, m_w_in, v_w_in, "adamw_w_in")
    out_out = _adamw_layers(go_parts, w_out, m_w_out, v_w_out, "adamw_w_out")
    params = dict(
        conv_a_w=(conv_a_w, m_conv_a_w, v_conv_a_w), conv_a_b=(conv_a_b, m_conv_a_b, v_conv_a_b),
        conv_b_w=(conv_b_w, m_conv_b_w, v_conv_b_w), conv_b_b=(conv_b_b, m_conv_b_b, v_conv_b_b),
        ln_b_g=(ln_b_g, m_ln_b_g, v_ln_b_g), ln_b_b=(ln_b_b, m_ln_b_b, v_ln_b_b),
        pool_w=(pool_w, m_pool_w, v_pool_w), pool_b=(pool_b, m_pool_b, v_pool_b),
        pool_scale=(pool_scale, m_pool_scale, v_pool_scale), ln_g=(ln_g, m_ln_g, v_ln_g), ln_b=(ln_b, m_ln_b, v_ln_b))
    loss, *small_out = _adamw_small(small_parts[:depth], small_parts[depth:2 * depth], small_parts[2 * depth:3 * depth],
                                    small_parts[3 * depth], params, "adamw_small")
    small_out = {n: small_out[4 * k:4 * k + 4] for k, n in enumerate(SMALL_PARAMS)}

    order = ("w_in", "conv_a_w", "conv_a_b", "conv_b_w", "conv_b_b", "ln_b_g", "ln_b_b", "pool_w", "pool_b",
             "pool_scale", "w_out", "ln_g", "ln_b")
    outs = []
    for k in range(4):
        outs += [out_in[k] if n == "w_in" else out_out[k] if n == "w_out" else small_out[n][k] for n in order]
    return (loss.reshape(()), grad_x, *outs)
```

```python
import jax
import jax.numpy as jnp
from jax import lax
from jax.experimental import pallas as pl
from jax.experimental.pallas import tpu as pltpu

F32 = jnp.float32
BF16 = jnp.bfloat16

DEPTH = 2
D_MODEL = 1024
W_MIX = 512
D_IN = 9 * W_MIX
D_MIX = 3 * W_MIX
POOL_WINDOWS = (2, 4, 8, 16)
POOL_DIM = 128
KA = 3
KB = 31
ALPHA = (2.0 * DEPTH) ** 0.25
LN_EPS = 1e-5
ADAM_LR, ADAM_B1, ADAM_B2, ADAM_EPS, ADAM_WD, ADAM_STEP = 0.001, 0.9, 0.999, 1e-08, 0.01, 10

N_DEV = 8
N_CHIP = 4
MESH = pl.DeviceIdType.MESH

CH = 32
SUB = 8
VMEM_LIMIT = 56 * 1024 * 1024

R_CAB, R_CBB, R_LBG, R_LBB, R_PB, R_PS, R_CAW, R_CBW = 0, 1, 2, 3, 4, 5, 6, 9
N_ROWS = R_CBW + KB
LN_ROWS = 16
SMALL_ROWS = N_ROWS + LN_ROWS


def _cparams(sem, **kw):
    return pltpu.CompilerParams(dimension_semantics=sem, vmem_limit_bytes=VMEM_LIMIT, **kw)


def _sigmoid(v):
    return 1.0 / (1.0 + jnp.exp(-v))


def _fold8(a):
    r, c = a.shape
    return a.reshape(r // SUB, SUB, c).sum(axis=0)


def _dot(a, b):
    return jnp.dot(a, b, preferred_element_type=F32)


def _dot_nt(a, b):
    return lax.dot_general(a, b, (((1,), (1,)), ((), ())), preferred_element_type=F32)


def _dot_tn(a, b):
    return lax.dot_general(a, b, (((0,), (0,)), ((), ())), preferred_element_type=F32)


def _row_tile(r):
    for cand in (512, 256, 128, 64, 32, 16, 8):
        if r % cand == 0 and r > cand:
            return cand
    return r


def _position():
    return lax.axis_index("x"), lax.axis_index("y"), lax.axis_index("c")


class _Gather:
    def __init__(self, arrs):
        self.arrs = list(arrs)
        na = len(self.arrs)
        self.out_shape = [jax.ShapeDtypeStruct((N_DEV,) + a.shape, a.dtype) for a in self.arrs]
        self.scratch = [pltpu.SemaphoreType.DMA((na, 7)), pltpu.SemaphoreType.DMA((na, 7)),
                        pltpu.SemaphoreType.DMA((na,))]

    def _copies(self, src, dst, sems):
        send_sems, recv_sems, local_sems = sems
        na = len(self.arrs)
        x, y, c = _position()
        me, sibling = (x, y, c), (x, y, 1 - c)
        chips = [(1 - x, y), (x, 1 - y), (1 - x, 1 - y)]

        def slot(a, dev):
            return dst[a].at[4 * dev[0] + 2 * dev[1] + dev[2]]

        def copy(a, k, block, to, from_src=False):
            return pltpu.make_async_remote_copy(
                src_ref=src[a] if from_src else slot(a, block), dst_ref=slot(a, block),
                send_sem=send_sems.at[a, k], recv_sem=recv_sems.at[a, k], device_id=to, device_id_type=MESH)

        mine = [pltpu.make_async_copy(src[a], slot(a, me), local_sems.at[a]) for a in range(na)]
        first, landed, passed, last = [], [], [], []
        for a in range(na):
            first.append(copy(a, 0, me, sibling, from_src=True))
            first += [copy(a, 1 + j, me, (*chip, c), from_src=True) for j, chip in enumerate(chips)]
        for j, chip in enumerate(chips):
            for a in range(na):
                landed.append(copy(a, 1 + j, (*chip, c), me))
                passed.append(copy(a, 4 + j, (*chip, c), sibling))
        for a in range(na):
            last.append(copy(a, 0, sibling, me))
            last += [copy(a, 4 + j, (*chip, 1 - c), me) for j, chip in enumerate(chips)]
        return mine, first, landed, passed, last

    def start(self, src, dst, sems):
        mine, first, _, _, _ = self._copies(src, dst, sems)
        for cp in mine + first:
            cp.start()

    def pass_on(self, src, dst, sems):
        _, _, landed, passed, _ = self._copies(src, dst, sems)
        for got, cp in zip(landed, passed):
            got.wait_recv()
            cp.start()

    def finish(self, src, dst, sems):
        mine, first, _, passed, last = self._copies(src, dst, sems)
        for cp in last:
            cp.wait_recv()
        for cp in first + passed:
            cp.wait_send()
        for cp in mine:
            cp.wait()


class _Exchange:
    def __init__(self, arrs):
        self.arrs = list(arrs)
        na = len(self.arrs)
        self.out_shape = [jax.ShapeDtypeStruct(a.shape, a.dtype) for a in self.arrs]
        self.scratch = [pltpu.SemaphoreType.DMA((na, N_DEV - 1)), pltpu.SemaphoreType.DMA((na, N_DEV - 1)),
                        pltpu.SemaphoreType.DMA((na,))]

    def _copies(self, src, dst, sems):
        send_sems, recv_sems, local_sems = sems
        na = len(self.arrs)
        x, y, c = _position()
        me = 4 * x + 2 * y + c
        mine = [pltpu.make_async_copy(src[a].at[me], dst[a].at[me], local_sems.at[a]) for a in range(na)]
        copies = []
        for a in range(na):
            for k in range(N_DEV - 1):
                flip = k + 1
                px, py, pc = x ^ (flip >> 2), y ^ ((flip >> 1) & 1), c ^ (flip & 1)
                copies.append(pltpu.make_async_remote_copy(
                    src_ref=src[a].at[4 * px + 2 * py + pc], dst_ref=dst[a].at[me],
                    send_sem=send_sems.at[a, k], recv_sem=recv_sems.at[a, k], device_id=(px, py, pc),
                    device_id_type=MESH))
        return mine, copies

    def start(self, src, dst, sems):
        mine, copies = self._copies(src, dst, sems)
        for cp in mine + copies:
            cp.start()

    def pass_on(self, src, dst, sems):
        pass

    def finish(self, src, dst, sems):
        mine, copies = self._copies(src, dst, sems)
        for cp in copies:
            cp.wait()
        for cp in mine:
            cp.wait()


def _run_plan(plan, name):
    na = len(plan.arrs)
    any_spec = pl.BlockSpec(memory_space=pl.ANY)

    def body(*refs):
        src, dst, sems = refs[:na], refs[na:2 * na], refs[2 * na:]
        plan.start(src, dst, sems)
        plan.pass_on(src, dst, sems)
        plan.finish(src, dst, sems)

    return pl.pallas_call(
        body, name=name, in_specs=[any_spec] * na, out_specs=[any_spec] * na,
        out_shape=plan.out_shape, scratch_shapes=plan.scratch,
    )(*plan.arrs)


class _Hosted:
    def __init__(self, plan, n_in, n_out, n_scratch):
        self.plan, self.n_in, self.n_out, self.n_scratch = plan, n_in, n_out, n_scratch
        any_spec = pl.BlockSpec(memory_space=pl.ANY)
        k = 0 if plan is None else len(plan.arrs)
        self.operands = [] if plan is None else plan.arrs
        self.in_specs = [any_spec] * k
        self.out_specs = [any_spec] * k
        self.out_shape = [] if plan is None else plan.out_shape
        self.scratch = [] if plan is None else plan.scratch

    def wrap(self, body, phase):
        if self.plan is None:
            return body
        plan, k = self.plan, len(self.plan.arrs)
        i0, o0 = self.n_in, self.n_in + k
        o1 = o0 + self.n_out
        s0 = o1 + k
        s1 = s0 + self.n_scratch

        def hosted(*refs):
            src, dst, sems = refs[i0:o0], refs[o1:s0], refs[s1:]
            first, middle, last = phase()
            pl.when(first)(lambda: plan.start(src, dst, sems))
            body(*refs[:i0], *refs[o0:o1], *refs[s0:s1])
            pl.when(middle)(lambda: plan.pass_on(src, dst, sems))
            pl.when(last)(lambda: plan.finish(src, dst, sems))

        return hosted


TT = 256
SEG = TT // SUB


def _to_segments(a):
    t, c = a.shape
    return a.reshape(t // TT, SUB, SEG, c).transpose(0, 2, 1, 3).reshape(t, c)


def _from_segments(a):
    t, c = a.shape
    return a.reshape(t // TT, SEG, SUB, c).transpose(0, 2, 1, 3).reshape(t, c)


def _sublane_is(s):
    return lax.broadcasted_iota(jnp.int32, (TT, W_MIX), 0) % SUB == s


def _look_back(ext_ref, cur, before):
    ext_ref[TT:, :] = cur
    ext_ref[0:TT, :] = jnp.where(_sublane_is(0), before, pltpu.roll(cur, 1, axis=0))


def _last_segment(cur):
    return pltpu.roll(cur, TT - (SUB - 1), axis=0)


def _look_ahead(ext_ref, cur, after):
    ext_ref[0:TT, :] = cur
    ext_ref[TT:, :] = jnp.where(_sublane_is(SUB - 1), after, pltpu.roll(cur, TT - 1, axis=0))


def _first_segment(cur):
    return pltpu.roll(cur, SUB - 1, axis=0)


def _tap_loop(body):
    lax.fori_loop(0, TT // CH, lambda c, carry: body(pl.multiple_of(c * CH, CH), carry), 0)


def _conv_rows(ext_ref, w_ref, out_ref, *, nk, off, reverse):
    def body(r0, carry):
        acc = jnp.zeros((CH, W_MIX), F32)
        for k in range(nk):
            kk = nk - 1 - k if reverse else k
            acc = acc + ext_ref[pl.ds(r0 + (off + k) * SUB, CH), :] * w_ref[kk:kk + 1, :]
        out_ref[pl.ds(r0, CH), :] = acc
        return carry
    _tap_loop(body)


def _conv_wgrad(g_ref, ext_ref, acc_ref, *, nk, off, row0):
    def body(r0, carry):
        g = g_ref[pl.ds(r0, CH), :]
        for k in range(nk):
            a = (row0 + k) * SUB
            acc_ref[a:a + SUB, :] += _fold8(g * ext_ref[pl.ds(r0 + (off + k) * SUB, CH), :])
        return carry
    _tap_loop(body)


def _window_sums(ext_ref, out_ref, *, forward):
    def body(r0, carry):
        for g, w in enumerate(POOL_WINDOWS):
            lanes = slice(g * POOL_DIM, (g + 1) * POOL_DIM)
            acc = jnp.zeros((CH, POOL_DIM), F32)
            for j in range(w):
                off = j if forward else SEG - j
                acc = acc + ext_ref[pl.ds(r0 + off * SUB, CH), lanes]
            out_ref[pl.ds(r0, CH), lanes] = acc
        return carry
    _tap_loop(body)


def _inv_count(tile):
    r = lax.broadcasted_iota(jnp.int32, (TT, POOL_DIM), 0)
    t1 = (tile * TT + (r % SUB) * SEG + r // SUB + 1).astype(F32)
    return jnp.concatenate([1.0 / jnp.minimum(t1, float(w)) for w in POOL_WINDOWS], axis=1)


def _groups(h_ref):
    return [h_ref[:, k * W_MIX:(k + 1) * W_MIX].astype(F32) for k in range(9)]


def _layer_norm(v, g, b):
    mu = jnp.mean(v, axis=-1, keepdims=True)
    vc = v - mu
    var = jnp.mean(vc * vc, axis=-1, keepdims=True)
    rstd = lax.rsqrt(var + LN_EPS)
    vhat = vc * rstd
    return vhat * g + b, vhat, rstd


def _layer_norm_bwd(g_out, vhat, rstd, g):
    gh = g_out * g
    m1 = jnp.mean(gh, axis=-1, keepdims=True)
    m2 = jnp.mean(gh * vhat, axis=-1, keepdims=True)
    return rstd * (gh - m1 - vhat * m2)


def _pool_linear(pooled, pw_ref, pb_ref):
    outs = []
    for g in range(len(POOL_WINDOWS)):
        lanes = slice(g * POOL_DIM, (g + 1) * POOL_DIM)
        outs.append(_dot(pooled[:, lanes].astype(BF16), pw_ref[g].astype(BF16)))
    return jnp.concatenate(outs, axis=1) + pb_ref[...]


def _in_proj(x, w, name, plan=None):
    t, d = x.shape
    n = w.shape[1]
    tm, tn = min(t, 1024), 1536
    nm, nn = t // tm, n // tn

    def body(x_ref, w_ref, o_ref, xb_ref):
        @pl.when(pl.program_id(1) == 0)
        def _():
            xb_ref[...] = x_ref[...].astype(BF16)
        o_ref[...] = _dot(xb_ref[...], w_ref[...]).astype(BF16)

    def phase():
        step = pl.program_id(0) * nn + pl.program_id(1)
        return step == 0, step == (nm * nn) // 2, step == nm * nn - 1

    host = _Hosted(plan, n_in=2, n_out=1, n_scratch=1)
    return pl.pallas_call(
        host.wrap(body, phase), name=name, grid=(nm, nn),
        in_specs=[pl.BlockSpec((tm, d), lambda i, j: (i, 0)), pl.BlockSpec((d, tn), lambda i, j: (0, j))]
        + host.in_specs,
        out_specs=[pl.BlockSpec((tm, tn), lambda i, j: (i, j))] + host.out_specs,
        out_shape=[jax.ShapeDtypeStruct((t, n), BF16)] + host.out_shape,
        scratch_shapes=[pltpu.VMEM((tm, d), BF16)] + host.scratch,
        compiler_params=_cparams(("arbitrary", "arbitrary")),
    )(x, w, *host.operands)


def _mixer_fwd(h, x, caw, cab, cbw, cbb, lbg, lbb, pw, pb, ps, w_out, lng, lnb, name, plan=None, target=None):
    t = h.shape[0]
    tt = TT
    n = t // tt
    with_loss = target is not None

    def body(h_ref, x_ref, *refs):
        if with_loss:
            t_ref, refs = refs[0], refs[1:]
        (caw_ref, cab_ref, cbw_ref, cbb_ref, lbg_ref, lbb_ref, pw_ref, pb_ref, ps_ref, wo_ref, lng_ref, lnb_ref,
         y_ref, u2_ref, z_ref, xn_ref) = refs[:16]
        refs = refs[16:]
        if with_loss:
            l_ref, refs = refs[0], refs[1:]
        exta, extb, extc, lasta, lastb, lastc, tmp, inv_ref = refs
        i = pl.program_id(0)

        @pl.when(i == 0)
        def _():
            for e in (lasta, lastb, lastc):
                e[...] = jnp.zeros_like(e)
            if with_loss:
                l_ref[...] = jnp.zeros_like(l_ref)

        @pl.when(i <= 1)
        def _():
            inv_ref[...] = _inv_count(i)

        a_bg, a_cg, a_v, a_z, b_v, b_g, b_z, c_u, c_z = _groups(h_ref)
        for ext, last, cur in ((exta, lasta, a_cg * a_v), (extb, lastb, b_v * _sigmoid(b_g)), (extc, lastc, c_u)):
            _look_back(ext, cur, last[...])
            last[...] = _last_segment(cur)

        _conv_rows(exta, caw_ref, tmp, nk=KA, off=SEG - (KA - 1), reverse=False)
        y_a = a_bg * (tmp[...] + cab_ref[...]) * (a_z * _sigmoid(a_z))
        y_ref[:, 0:W_MIX] = y_a.astype(BF16)

        _window_sums(extc, tmp, forward=False)
        pooled = tmp[...] * inv_ref[...] - c_u
        p = _pool_linear(pooled, pw_ref, pb_ref)
        y_c = p * ps_ref[...] * (c_z * _sigmoid(c_z))
        y_ref[:, 2 * W_MIX:3 * W_MIX] = y_c.astype(BF16)

        _conv_rows(extb, cbw_ref, tmp, nk=KB, off=SEG - (KB - 1), reverse=False)
        u2 = tmp[...] + cbb_ref[...]
        u2_ref[...] = u2
        ln, _, _ = _layer_norm(u2, lbg_ref[...], lbb_ref[...])
        y_b = (ln * _sigmoid(ln)) * (b_z * _sigmoid(b_z))
        y_ref[:, W_MIX:2 * W_MIX] = y_b.astype(BF16)

        out = _dot(y_ref[...], wo_ref[...])
        z = ALPHA * x_ref[...] + out
        z_ref[...] = z
        xn, _, _ = _layer_norm(z, lng_ref[...], lnb_ref[...])
        if with_loss:
            e = xn - t_ref[...]
            xn_ref[...] = e * (1.0 / D_MODEL)
            l_ref[...] += _fold8(e * e) * (0.5 / D_MODEL)
        else:
            xn_ref[...] = xn

    def phase():
        i = pl.program_id(0)
        return i == 0, i == n // 2, i == n - 1

    row = lambda wd: pl.BlockSpec((tt, wd), lambda i: (i, 0))
    full = lambda a: pl.BlockSpec(a.shape, lambda i: (0,) * a.ndim)
    params = (caw, cab, cbw, cbb, lbg, lbb, pw, pb, ps, w_out, lng, lnb)
    extra_in = [target] if with_loss else []
    extra_out = [jax.ShapeDtypeStruct((SUB, D_MODEL), F32)] if with_loss else []
    host = _Hosted(plan, n_in=2 + len(extra_in) + len(params), n_out=4 + len(extra_out), n_scratch=8)
    return pl.pallas_call(
        host.wrap(body, phase), name=name, grid=(n,),
        in_specs=[row(D_IN), row(D_MODEL)] + [row(D_MODEL)] * len(extra_in) + [full(a) for a in params]
        + host.in_specs,
        out_specs=[row(D_MIX), row(W_MIX), row(D_MODEL), row(D_MODEL)] + [full(o) for o in extra_out]
        + host.out_specs,
        out_shape=[jax.ShapeDtypeStruct((t, D_MIX), BF16), jax.ShapeDtypeStruct((t, W_MIX), F32),
                   jax.ShapeDtypeStruct((t, D_MODEL), F32), jax.ShapeDtypeStruct((t, D_MODEL), F32)]
        + extra_out + host.out_shape,
        scratch_shapes=[pltpu.VMEM((2 * tt, W_MIX), F32)] * 3 + [pltpu.VMEM((tt, W_MIX), F32)] * 5 + host.scratch,
        compiler_params=_cparams(("arbitrary",)),
    )(h, x, *extra_in, *params, *host.operands)


def _out_proj_bwd(g_xn, z, y, w_out, lng, name):
    t = z.shape[0]
    tt = min(t, 256)

    def body(g_ref, z_ref, y_ref, wo_ref, lng_ref, gz_ref, gy_ref, gwo_ref, gln_ref, accg, accb, accw):
        i = pl.program_id(0)

        @pl.when(i == 0)
        def _():
            accw[...] = jnp.zeros_like(accw)
            accg[...] = jnp.zeros_like(accg)
            accb[...] = jnp.zeros_like(accb)

        g = g_ref[...]
        _, zhat, rstd = _layer_norm(z_ref[...], lng_ref[...], 0.0)
        accg[...] += _fold8(g * zhat)
        accb[...] += _fold8(g)
        g_z = _layer_norm_bwd(g, zhat, rstd, lng_ref[...])
        gz_ref[...] = g_z
        gzb = g_z.astype(BF16)
        gy_ref[...] = _dot_nt(gzb, wo_ref[...])
        accw[...] += _dot_tn(y_ref[...], gzb)

        @pl.when(i == pl.num_programs(0) - 1)
        def _():
            gwo_ref[...] = accw[...].astype(BF16)
            gln_ref[...] = jnp.zeros_like(gln_ref)
            gln_ref[0:1, :] = jnp.sum(accg[...], axis=0, keepdims=True)
            gln_ref[1:2, :] = jnp.sum(accb[...], axis=0, keepdims=True)

    row = lambda wd: pl.BlockSpec((tt, wd), lambda i: (i, 0))
    full = lambda shape: pl.BlockSpec(shape, lambda i: (0,) * len(shape))
    return pl.pallas_call(
        body, name=name, grid=(t // tt,),
        in_specs=[row(D_MODEL), row(D_MODEL), row(D_MIX), full(w_out.shape), full(lng.shape)],
        out_specs=[row(D_MODEL), row(D_MIX), full((D_MIX, D_MODEL)), full((SUB, D_MODEL))],
        out_shape=[jax.ShapeDtypeStruct((t, D_MODEL), F32), jax.ShapeDtypeStruct((t, D_MIX), F32),
                   jax.ShapeDtypeStruct((D_MIX, D_MODEL), BF16), jax.ShapeDtypeStruct((SUB, D_MODEL), F32)],
        scratch_shapes=[pltpu.VMEM((SUB, D_MODEL), F32)] * 2 + [pltpu.VMEM((D_MIX, D_MODEL), F32)],
        compiler_params=_cparams(("arbitrary",)),
    )(g_xn, z, y, w_out, lng)


def _mixer_bwd(h, u2, g_y, caw, cab, cbw, lbg, lbb, pw, pb, ps, name, plan=None):
    t = h.shape[0]
    tt = TT
    n = t // tt
    before_groups = (1, 2, 4, 5, 7)

    def body(h_ref, p_cg, p_av, p_bv, p_bg, p_cu, u2_ref, gy_ref, caw_ref, cab_ref, cbw_ref, lbg_ref, lbb_ref,
             pw_ref, pb_ref, ps_ref, gh_ref, rows_ref, gpw_ref,
             exta, extb, extc, gca, gu2, qx, nexta, nextb, nextc, tmp, tmp2, inv_ref, acc):
        s = pl.program_id(0)
        i = n - 1 - s

        @pl.when(s == 0)
        def _():
            acc[...] = jnp.zeros_like(acc)
            gpw_ref[...] = jnp.zeros_like(gpw_ref)
            for e in (nexta, nextb, nextc):
                e[...] = jnp.zeros_like(e)

        live = (i > 0).astype(F32)
        f32 = lambda ref: ref[...].astype(F32)
        before_a = _last_segment(f32(p_cg) * f32(p_av)) * live
        before_b = _last_segment(f32(p_bv) * _sigmoid(f32(p_bg))) * live
        before_c = _last_segment(f32(p_cu)) * live

        a_bg, a_cg, a_v, a_z, b_v, b_g, b_z, c_u, c_z = _groups(h_ref)
        g_ya = gy_ref[:, 0:W_MIX]
        g_yb = gy_ref[:, W_MIX:2 * W_MIX]
        g_yc = gy_ref[:, 2 * W_MIX:3 * W_MIX]

        def add_row(r, v):
            acc[r * SUB:(r + 1) * SUB, :] += _fold8(v)

        _look_back(exta, a_cg * a_v, before_a)
        _conv_rows(exta, caw_ref, tmp, nk=KA, off=SEG - (KA - 1), reverse=False)
        ca = tmp[...] + cab_ref[...]
        sg = _sigmoid(a_z)
        s_az = a_z * sg
        t_a = g_ya * a_bg
        gh_ref[:, 0:W_MIX] = (g_ya * ca * s_az).astype(BF16)
        gh_ref[:, 3 * W_MIX:4 * W_MIX] = (t_a * ca * (sg * (1.0 + a_z * (1.0 - sg)))).astype(BF16)
        g_ca = t_a * s_az
        _look_ahead(gca, g_ca, nexta[...])
        nexta[...] = _first_segment(g_ca)
        add_row(R_CAB, g_ca)
        _conv_wgrad(gca, exta, acc, nk=KA, off=SEG - (KA - 1), row0=R_CAW)
        _conv_rows(gca, caw_ref, tmp, nk=KA, off=0, reverse=True)
        g_pa = tmp[...]
        gh_ref[:, W_MIX:2 * W_MIX] = (g_pa * a_v).astype(BF16)
        gh_ref[:, 2 * W_MIX:3 * W_MIX] = (g_pa * a_cg).astype(BF16)

        sgg = _sigmoid(b_g)
        _look_back(extb, b_v * sgg, before_b)
        ln, u2hat, rstd = _layer_norm(u2_ref[...], lbg_ref[...], lbb_ref[...])
        sl = _sigmoid(ln)
        u3 = ln * sl
        sz = _sigmoid(b_z)
        s_bz = b_z * sz
        gh_ref[:, 6 * W_MIX:7 * W_MIX] = (g_yb * u3 * (sz * (1.0 + b_z * (1.0 - sz)))).astype(BF16)
        g_ln = g_yb * s_bz * (sl * (1.0 + ln * (1.0 - sl)))
        add_row(R_LBG, g_ln * u2hat)
        add_row(R_LBB, g_ln)
        g_u2 = _layer_norm_bwd(g_ln, u2hat, rstd, lbg_ref[...])
        _look_ahead(gu2, g_u2, nextb[...])
        nextb[...] = _first_segment(g_u2)
        add_row(R_CBB, g_u2)
        _conv_wgrad(gu2, extb, acc, nk=KB, off=SEG - (KB - 1), row0=R_CBW)
        _conv_rows(gu2, cbw_ref, tmp, nk=KB, off=0, reverse=True)
        g_u1 = tmp[...]
        gh_ref[:, 4 * W_MIX:5 * W_MIX] = (g_u1 * sgg).astype(BF16)
        gh_ref[:, 5 * W_MIX:6 * W_MIX] = (g_u1 * b_v * sgg * (1.0 - sgg)).astype(BF16)

        _look_back(extc, c_u, before_c)
        _window_sums(extc, tmp, forward=False)
        @pl.when((s == 0) | (i == 0))
        def _():
            inv_ref[...] = _inv_count(i)
        inv = inv_ref[...]
        pooled = tmp[...] * inv - c_u
        p = _pool_linear(pooled, pw_ref, pb_ref)
        sc = _sigmoid(c_z)
        s_cz = c_z * sc
        scale = ps_ref[...]
        gh_ref[:, 8 * W_MIX:9 * W_MIX] = (g_yc * p * scale * (sc * (1.0 + c_z * (1.0 - sc)))).astype(BF16)
        t_c = g_yc * s_cz
        add_row(R_PS, t_c * p)
        g_p = t_c * scale
        add_row(R_PB, g_p)
        g_pooled = []
        for g in range(len(POOL_WINDOWS)):
            lanes = slice(g * POOL_DIM, (g + 1) * POOL_DIM)
            gpg = g_p[:, lanes].astype(BF16)
            gpw_ref[g] += _dot_tn(pooled[:, lanes].astype(BF16), gpg)
            g_pooled.append(_dot_nt(gpg, pw_ref[g].astype(BF16)))
        g_pooled = jnp.concatenate(g_pooled, axis=1)
        q = g_pooled * inv
        _look_ahead(qx, q, nextc[...])
        nextc[...] = _first_segment(q)
        _window_sums(qx, tmp2, forward=True)
        gh_ref[:, 7 * W_MIX:8 * W_MIX] = (tmp2[...] - g_pooled).astype(BF16)

        @pl.when(s == n - 1)
        def _():
            for r in range(N_ROWS):
                rows_ref[r:r + 1, :] = jnp.sum(acc[r * SUB:(r + 1) * SUB, :], axis=0, keepdims=True)

    def phase():
        s = pl.program_id(0)
        return s == 0, s == n // 2, s == n - 1

    row = lambda wd: pl.BlockSpec((tt, wd), lambda s: (n - 1 - s, 0))
    before = [pl.BlockSpec((tt, W_MIX), lambda s, k=k: (jnp.maximum(n - 2 - s, 0), k)) for k in before_groups]
    full = lambda shape: pl.BlockSpec(shape, lambda s: (0,) * len(shape))
    params = (caw, cab, cbw, lbg, lbb, pw, pb, ps)
    host = _Hosted(plan, n_in=3 + len(before) + len(params), n_out=3, n_scratch=13)
    return pl.pallas_call(
        host.wrap(body, phase), name=name, grid=(n,),
        in_specs=[row(D_IN)] + before + [row(W_MIX), row(D_MIX)] + [full(a.shape) for a in params] + host.in_specs,
        out_specs=[row(D_IN), full((N_ROWS, W_MIX)), full(pw.shape)] + host.out_specs,
        out_shape=[jax.ShapeDtypeStruct((t, D_IN), BF16), jax.ShapeDtypeStruct((N_ROWS, W_MIX), F32),
                   jax.ShapeDtypeStruct(pw.shape, F32)] + host.out_shape,
        scratch_shapes=[pltpu.VMEM((2 * tt, W_MIX), F32)] * 6 + [pltpu.VMEM((tt, W_MIX), F32)] * 6
        + [pltpu.VMEM((N_ROWS * SUB, W_MIX), F32)] + host.scratch,
        compiler_params=_cparams(("arbitrary",)),
    )(h, *([h] * len(before)), u2, g_y, *params, *host.operands)


def _in_proj_wgrad(x, g_h, name, plan=None):
    t, d = x.shape
    n = g_h.shape[1]
    nb = n // N_DEV
    per = N_DEV // 2
    tk, tn = min(t, 512), per * nb
    nk = t // tk

    def body(x_ref, g_ref, o_ref, acc):
        k = pl.program_id(1)

        @pl.when(k == 0)
        def _():
            acc[...] = jnp.zeros_like(acc)
        acc[...] += _dot_tn(x_ref[...].astype(BF16), g_ref[...])

        @pl.when(k == nk - 1)
        def _():
            for b in range(per):
                o_ref[b] = acc[:, b * nb:(b + 1) * nb].astype(BF16)

    def phase():
        step = pl.program_id(0) * nk + pl.program_id(1)
        return step == 0, step == nk, step == 2 * nk - 1

    host = _Hosted(plan, n_in=2, n_out=1, n_scratch=1)
    return pl.pallas_call(
        host.wrap(body, phase), name=name, grid=(n // tn, nk),
        in_specs=[pl.BlockSpec((tk, d), lambda j, k: (k, 0)), pl.BlockSpec((tk, tn), lambda j, k: (k, j))]
        + host.in_specs,
        out_specs=[pl.BlockSpec((per, d, nb), lambda j, k: (j, 0, 0))] + host.out_specs,
        out_shape=[jax.ShapeDtypeStruct((N_DEV, d, nb), BF16)] + host.out_shape,
        scratch_shapes=[pltpu.VMEM((d, tn), F32)] + host.scratch,
        compiler_params=_cparams(("arbitrary", "arbitrary")),
    )(x, g_h, *host.operands)


def _in_proj_dgrad(g_h, w, g_z, name, plan=None):
    t, n = g_h.shape
    d = w.shape[0]
    tm, tk = min(t, 1024), 1536
    nm, nk = t // tm, n // tk

    def body(g_ref, w_ref, gz_ref, o_ref):
        @pl.when(pl.program_id(1) == 0)
        def _():
            o_ref[...] = ALPHA * gz_ref[...]
        o_ref[...] += _dot_nt(g_ref[...], w_ref[...])

    def phase():
        step = pl.program_id(0) * nk + pl.program_id(1)
        return step == 0, step == (nm * nk) // 2, step == nm * nk - 1

    host = _Hosted(plan, n_in=3, n_out=1, n_scratch=0)
    return pl.pallas_call(
        host.wrap(body, phase), name=name, grid=(nm, nk),
        in_specs=[pl.BlockSpec((tm, tk), lambda i, k: (i, k)), pl.BlockSpec((d, tk), lambda i, k: (0, k)),
                  pl.BlockSpec((tm, d), lambda i, k: (i, 0))] + host.in_specs,
        out_specs=[pl.BlockSpec((tm, d), lambda i, k: (i, 0))] + host.out_specs,
        out_shape=[jax.ShapeDtypeStruct((t, d), F32)] + host.out_shape,
        scratch_shapes=host.scratch,
        compiler_params=_cparams(("arbitrary", "arbitrary")),
    )(g_h, w, g_z, *host.operands)


BC1 = 1.0 - ADAM_B1 ** ADAM_STEP
BC2 = 1.0 - ADAM_B2 ** ADAM_STEP


def _adamw_math(g, w, m, v):
    nm = ADAM_B1 * m + (1.0 - ADAM_B1) * g
    nv = ADAM_B2 * v + (1.0 - ADAM_B2) * (g * g)
    delta = -ADAM_LR * ((nm / BC1) / (jnp.sqrt(nv / BC2) + ADAM_EPS) + ADAM_WD * w)
    return delta, nm, nv


def _total(ref):
    g = ref[0].astype(F32)
    for k in range(1, ref.shape[0]):
        g = g + ref[k].astype(F32)
    return g


def _adamw_layers(parts, w, m, v, name):
    depth, r, c = w.shape
    p = parts[0].shape[0]
    tr = _row_tile(r)
    nr = r // tr

    def body(*refs):
        p_refs = refs[:depth]
        w_ref, m_ref, v_ref, g_ref, d_ref, nm_ref, nv_ref = refs[depth:]
        for l in range(depth):
            @pl.when(pl.program_id(0) == l)
            def _(l=l):
                g = _total(p_refs[l])
                delta, nm, nv = _adamw_math(g, w_ref[0], m_ref[0], v_ref[0])
                g_ref[0], d_ref[0], nm_ref[0], nv_ref[0] = g, delta, nm, nv

    def part_spec(l):
        return pl.BlockSpec((p, tr, c), lambda li, i: (0, jnp.where(li == l, i, jnp.where(li < l, 0, nr - 1)), 0))

    blk = pl.BlockSpec((1, tr, c), lambda li, i: (li, i, 0))
    out = jax.ShapeDtypeStruct((depth, r, c), F32)
    return pl.pallas_call(
        body, name=name, grid=(depth, nr),
        in_specs=[part_spec(l) for l in range(depth)] + [blk] * 3,
        out_specs=[blk] * 4, out_shape=[out] * 4,
        compiler_params=_cparams(("arbitrary", "arbitrary")),
    )(*parts, w, m, v)


SMALL_PARAMS = ("conv_a_w", "conv_a_b", "conv_b_w", "conv_b_b", "ln_b_g", "ln_b_b", "pool_w", "pool_b", "pool_scale",
                "ln_g", "ln_b")


def _adamw_small(rows_parts, gln_parts, gpw_parts, loss_parts, params, name):
    depth = len(rows_parts)
    cs = params["conv_a_w"][0].shape[2]
    operands = [*rows_parts, *gln_parts, *gpw_parts, loss_parts] + [a for n in SMALL_PARAMS for a in params[n]]
    n_in = len(operands)
    out_shape = [jax.ShapeDtypeStruct((1, 1), F32)]
    out_shape += [jax.ShapeDtypeStruct(params[n][0].shape, F32) for n in SMALL_PARAMS for _ in range(4)]

    def body(*refs):
        rows_p, gln_p, gpw_p = refs[:depth], refs[depth:2 * depth], refs[2 * depth:3 * depth]
        loss_p = refs[3 * depth]
        prm, outs = refs[3 * depth + 1:n_in], refs[n_in + 1:]
        refs[n_in][...] = jnp.sum(_total(loss_p)).reshape(1, 1)
        x, y, c = _position()
        to_front = (W_MIX - (4 * x + 2 * y + c) * cs) % W_MIX

        def update(name, g, at):
            k = SMALL_PARAMS.index(name)
            w_ref, m_ref, v_ref = prm[3 * k:3 * k + 3]
            g_ref, d_ref, nm_ref, nv_ref = outs[4 * k:4 * k + 4]
            delta, nm, nv = _adamw_math(g, w_ref[at], m_ref[at], v_ref[at])
            g_ref[at], d_ref[at], nm_ref[at], nv_ref[at] = g, delta, nm, nv

        for l in range(depth):
            rows = _total(rows_p[l])
            mine = pltpu.roll(rows, to_front, axis=1)
            gln = _total(gln_p[l])
            one = (slice(l, l + 1), slice(None))
            for name, r in (("conv_a_b", R_CAB), ("conv_b_b", R_CBB), ("ln_b_g", R_LBG), ("ln_b_b", R_LBB),
                            ("pool_scale", R_PS)):
                update(name, rows[r:r + 1, :], one)
            for g in range(len(POOL_WINDOWS)):
                update("pool_b", rows[R_PB:R_PB + 1, g * POOL_DIM:(g + 1) * POOL_DIM], (l, slice(g, g + 1), slice(None)))
            update("ln_g", gln[0:1, :], one)
            update("ln_b", gln[1:2, :], one)
            update("conv_a_w", mine[R_CAW:R_CAW + KA, 0:cs], (l,))
            update("conv_b_w", mine[R_CBW:R_CBW + KB, 0:cs], (l,))
            update("pool_w", _total(gpw_p[l]), (l,))

    vmem = pl.BlockSpec(memory_space=pltpu.VMEM)
    return pl.pallas_call(
        body, name=name, in_specs=[vmem] * n_in, out_specs=[vmem] * len(out_shape), out_shape=out_shape,
        compiler_params=pltpu.CompilerParams(vmem_limit_bytes=VMEM_LIMIT),
    )(*operands)


def kernel(x, w_in, conv_a_w, conv_a_b, conv_b_w, conv_b_b, ln_b_g, ln_b_b, pool_w, pool_b, pool_scale, w_out, ln_g, ln_b, loss_target, m_w_in, m_conv_a_w, m_conv_a_b, m_conv_b_w, m_conv_b_b, m_ln_b_g, m_ln_b_b, m_pool_w, m_pool_b, m_pool_scale, m_w_out, m_ln_g, m_ln_b, v_w_in, v_conv_a_w, v_conv_a_b, v_conv_b_w, v_conv_b_b, v_ln_b_g, v_ln_b_b, v_pool_w, v_pool_b, v_pool_scale, v_w_out, v_ln_g, v_ln_b):
    depth = w_in.shape[0]
    x0 = _to_segments(x[0])
    target = _to_segments(loss_target[0])
    r2 = lambda a: a.reshape(1, -1)

    w_in_b, w_out_b = w_in.astype(BF16), w_out.astype(BF16)
    conv_sh = jnp.concatenate([conv_a_w, conv_b_w], axis=1)
    full_in = lambda g: g.transpose(1, 0, 2).reshape(D_MODEL, D_IN)
    full_out = lambda g: g.reshape(D_MIX, D_MODEL)
    w_in_f = [full_in(_run_plan(_Gather([w_in_b[0]]), "gather_w_in_0")[0])]
    w_out_f = []

    xs, hs, ys, u2s, zs = [x0], [], [], [], []
    for l in range(depth):
        h, *got = _in_proj(xs[l], w_in_f[l], f"in_proj_{l}",
                           plan=_Gather([w_out_b[0], conv_sh]) if l == 0 else None)
        if got:
            w_out_f.append(full_out(got[0]))
            conv_f = got[1].transpose(1, 2, 0, 3).reshape(depth, KA + KB, W_MIX)
            caw_f, cbw_f = conv_f[:, :KA], conv_f[:, KA:]
        last = l + 1 == depth
        y, u2, z, xn, *got = _mixer_fwd(
            h, xs[l], caw_f[l], r2(conv_a_b[l]), cbw_f[l], r2(conv_b_b[l]), r2(ln_b_g[l]), r2(ln_b_b[l]),
            pool_w[l], r2(pool_b[l]), r2(pool_scale[l]), w_out_f[l], r2(ln_g[l]), r2(ln_b[l]), f"mixer_fwd_{l}",
            plan=None if last else _Gather([w_in_b[l + 1], w_out_b[l + 1]]), target=target if last else None)
        if last:
            g, loss_rows = xn, got[0]
        else:
            w_in_f.append(full_in(got[0])), w_out_f.append(full_out(got[1]))
        hs.append(h), ys.append(y), u2s.append(u2), zs.append(z), xs.append(xn)


    gi_parts, go_parts, g_rows, g_lns, g_pool_w = ([None] * depth for _ in range(5))
    waiting = []
    for l in reversed(range(depth)):
        g_z, g_y, g_w_out, g_lns[l] = _out_proj_bwd(g, zs[l], ys[l], w_out_f[l], r2(ln_g[l]), f"out_proj_bwd_{l}")
        go = g_w_out.reshape(N_DEV, D_MIX // N_DEV, D_MODEL)
        if l == 0:
            waiting += [go]
        g_h, g_rows[l], g_pool_w[l], *got = _mixer_bwd(
            hs[l], u2s[l], g_y, caw_f[l], r2(conv_a_b[l]), cbw_f[l], r2(ln_b_g[l]), r2(ln_b_b[l]),
            pool_w[l], r2(pool_b[l]), r2(pool_scale[l]), f"mixer_bwd_{l}",
            plan=_Exchange(waiting) if waiting else None)
        if got:
            if l + 1 < depth:
                gi_parts[l + 1], go_parts[l + 1] = got[0], got[1]
            if l == 0:
                go_parts[0] = got[-1]
        if l > 0:
            gi = _in_proj_wgrad(xs[l], g_h, f"in_proj_wgrad_{l}")[0]
            waiting = [gi, go]
            g = _in_proj_dgrad(g_h, w_in_f[l], g_z, f"in_proj_dgrad_{l}")[0]
        else:
            small = [*g_rows, *g_lns, *g_pool_w, loss_rows]
            gi, *small_parts = _in_proj_wgrad(xs[0], g_h, "in_proj_wgrad_0", plan=_Gather(small))
            g, gi_parts[0] = _in_proj_dgrad(g_h, w_in_f[0], g_z, "in_proj_dgrad_0",
                                            plan=_Exchange([gi]))
    grad_x = _from_segments(g)[None]

    out_in = _adamw_layers(gi_parts, w_in, m_w_in, v_w_in, "adamw_w_in")
    out_out = _adamw_layers(go_parts, w_out, m_w_out, v_w_out, "adamw_w_out")
    params = dict(
        conv_a_w=(conv_a_w, m_conv_a_w, v_conv_a_w), conv_a_b=(conv_a_b, m_conv_a_b, v_conv_a_b),
        conv_b_w=(conv_b_w, m_conv_b_w, v_conv_b_w), conv_b_b=(conv_b_b, m_conv_b_b, v_conv_b_b),
        ln_b_g=(ln_b_g, m_ln_b_g, v_ln_b_g), ln_b_b=(ln_b_b, m_ln_b_b, v_ln_b_b),
        pool_w=(pool_w, m_pool_w, v_pool_w), pool_b=(pool_b, m_pool_b, v_pool_b),
        pool_scale=(pool_scale, m_pool_scale, v_pool_scale), ln_g=(ln_g, m_ln_g, v_ln_g), ln_b=(ln_b, m_ln_b, v_ln_b))
    loss, *small_out = _adamw_small(small_parts[:depth], small_parts[depth:2 * depth], small_parts[2 * depth:3 * depth],
                                    small_parts[3 * depth], params, "adamw_small")
    small_out = {n: small_out[4 * k:4 * k + 4] for k, n in enumerate(SMALL_PARAMS)}

    order = ("w_in", "conv_a_w", "conv_a_b", "conv_b_w", "conv_b_b", "ln_b_g", "ln_b_b", "pool_w", "pool_b",
             "pool_scale", "w_out", "ln_g", "ln_b")
    outs = []
    for k in range(4):
        outs += [out_in[k] if n == "w_in" else out_out[k] if n == "w_out" else small_out[n][k] for n in order]
    return (loss.reshape(()), grad_x, *outs)
```

```python
import jax
import jax.numpy as jnp
from jax import lax
from jax.experimental import pallas as pl
from jax.experimental.pallas import tpu as pltpu

F32 = jnp.float32
BF16 = jnp.bfloat16

DEPTH = 2
D_MODEL = 1024
W_MIX = 512
D_IN = 9 * W_MIX
D_MIX = 3 * W_MIX
POOL_WINDOWS = (2, 4, 8, 16)
POOL_DIM = 128
KA = 3
KB = 31
ALPHA = (2.0 * DEPTH) ** 0.25
LN_EPS = 1e-5
ADAM_LR, ADAM_B1, ADAM_B2, ADAM_EPS, ADAM_WD, ADAM_STEP = 0.001, 0.9, 0.999, 1e-08, 0.01, 10

N_DEV = 8
N_CHIP = 4
MESH = pl.DeviceIdType.MESH

CH = 32
SUB = 8
VMEM_LIMIT = 56 * 1024 * 1024

R_CAB, R_CBB, R_LBG, R_LBB, R_PB, R_PS, R_CAW, R_CBW = 0, 1, 2, 3, 4, 5, 6, 9
N_ROWS = R_CBW + KB
LN_ROWS = 16
SMALL_ROWS = N_ROWS + LN_ROWS


def _cparams(sem, **kw):
    return pltpu.CompilerParams(dimension_semantics=sem, vmem_limit_bytes=VMEM_LIMIT, **kw)


def _sigmoid(v):
    return 1.0 / (1.0 + jnp.exp(-v))


def _fold8(a):
    r, c = a.shape
    return a.reshape(r // SUB, SUB, c).sum(axis=0)


def _dot(a, b):
    return jnp.dot(a, b, preferred_element_type=F32)


def _dot_nt(a, b):
    return lax.dot_general(a, b, (((1,), (1,)), ((), ())), preferred_element_type=F32)


def _dot_tn(a, b):
    return lax.dot_general(a, b, (((0,), (0,)), ((), ())), preferred_element_type=F32)


def _row_tile(r):
    for cand in (512, 256, 128, 64, 32, 16, 8):
        if r % cand == 0 and r > cand:
            return cand
    return r


def _position():
    return lax.axis_index("x"), lax.axis_index("y"), lax.axis_index("c")


class _Gather:
    def __init__(self, arrs):
        self.arrs = list(arrs)
        na = len(self.arrs)
        self.out_shape = [jax.ShapeDtypeStruct((N_DEV,) + a.shape, a.dtype) for a in self.arrs]
        self.scratch = [pltpu.SemaphoreType.DMA((na, 7)), pltpu.SemaphoreType.DMA((na, 7)),
                        pltpu.SemaphoreType.DMA((na,))]

    def _copies(self, src, dst, sems):
        send_sems, recv_sems, local_sems = sems
        na = len(self.arrs)
        x, y, c = _position()
        me, sibling = (x, y, c), (x, y, 1 - c)
        chips = [(1 - x, y), (x, 1 - y), (1 - x, 1 - y)]

        def slot(a, dev):
            return dst[a].at[4 * dev[0] + 2 * dev[1] + dev[2]]

        def copy(a, k, block, to, from_src=False):
            return pltpu.make_async_remote_copy(
                src_ref=src[a] if from_src else slot(a, block), dst_ref=slot(a, block),
                send_sem=send_sems.at[a, k], recv_sem=recv_sems.at[a, k], device_id=to, device_id_type=MESH)

        mine = [pltpu.make_async_copy(src[a], slot(a, me), local_sems.at[a]) for a in range(na)]
        first, landed, passed, last = [], [], [], []
        for a in range(na):
            first.append(copy(a, 0, me, sibling, from_src=True))
            first += [copy(a, 1 + j, me, (*chip, c), from_src=True) for j, chip in enumerate(chips)]
        for j, chip in enumerate(chips):
            for a in range(na):
                landed.append(copy(a, 1 + j, (*chip, c), me))
                passed.append(copy(a, 4 + j, (*chip, c), sibling))
        for a in range(na):
            last.append(copy(a, 0, sibling, me))
            last += [copy(a, 4 + j, (*chip, 1 - c), me) for j, chip in enumerate(chips)]
        return mine, first, landed, passed, last

    def start(self, src, dst, sems):
        mine, first, _, _, _ = self._copies(src, dst, sems)
        for cp in mine + first:
            cp.start()

    def pass_on(self, src, dst, sems):
        _, _, landed, passed, _ = self._copies(src, dst, sems)
        for got, cp in zip(landed, passed):
            got.wait_recv()
            cp.start()

    def finish(self, src, dst, sems):
        mine, first, _, passed, last = self._copies(src, dst, sems)
        for cp in last:
            cp.wait_recv()
        for cp in first + passed:
            cp.wait_send()
        for cp in mine:
            cp.wait()


class _Exchange:
    def __init__(self, arrs):
        self.arrs = list(arrs)
        na = len(self.arrs)
        self.out_shape = [jax.ShapeDtypeStruct(a.shape, a.dtype) for a in self.arrs]
        self.scratch = [pltpu.SemaphoreType.DMA((na, N_DEV - 1)), pltpu.SemaphoreType.DMA((na, N_DEV - 1)),
                        pltpu.SemaphoreType.DMA((na,))]

    def _copies(self, src, dst, sems):
        send_sems, recv_sems, local_sems = sems
        na = len(self.arrs)
        x, y, c = _position()
        me = 4 * x + 2 * y + c
        mine = [pltpu.make_async_copy(src[a].at[me], dst[a].at[me], local_sems.at[a]) for a in range(na)]
        copies = []
        for a in range(na):
            for k in range(N_DEV - 1):
                flip = k + 1
                px, py, pc = x ^ (flip >> 2), y ^ ((flip >> 1) & 1), c ^ (flip & 1)
                copies.append(pltpu.make_async_remote_copy(
                    src_ref=src[a].at[4 * px + 2 * py + pc], dst_ref=dst[a].at[me],
                    send_sem=send_sems.at[a, k], recv_sem=recv_sems.at[a, k], device_id=(px, py, pc),
                    device_id_type=MESH))
        return mine, copies

    def start(self, src, dst, sems):
        mine, copies = self._copies(src, dst, sems)
        for cp in mine + copies:
            cp.start()

    def pass_on(self, src, dst, sems):
        pass

    def finish(self, src, dst, sems):
        mine, copies = self._copies(src, dst, sems)
        for cp in copies:
            cp.wait()
        for cp in mine:
            cp.wait()


def _run_plan(plan, name):
    na = len(plan.arrs)
    any_spec = pl.BlockSpec(memory_space=pl.ANY)

    def body(*refs):
        src, dst, sems = refs[:na], refs[na:2 * na], refs[2 * na:]
        plan.start(src, dst, sems)
        plan.pass_on(src, dst, sems)
        plan.finish(src, dst, sems)

    return pl.pallas_call(
        body, name=name, in_specs=[any_spec] * na, out_specs=[any_spec] * na,
        out_shape=plan.out_shape, scratch_shapes=plan.scratch,
    )(*plan.arrs)


class _Hosted:
    def __init__(self, plan, n_in, n_out, n_scratch):
        self.plan, self.n_in, self.n_out, self.n_scratch = plan, n_in, n_out, n_scratch
        any_spec = pl.BlockSpec(memory_space=pl.ANY)
        k = 0 if plan is None else len(plan.arrs)
        self.operands = [] if plan is None else plan.arrs
        self.in_specs = [any_spec] * k
        self.out_specs = [any_spec] * k
        self.out_shape = [] if plan is None else plan.out_shape
        self.scratch = [] if plan is None else plan.scratch

    def wrap(self, body, phase):
        if self.plan is None:
            return body
        plan, k = self.plan, len(self.plan.arrs)
        i0, o0 = self.n_in, self.n_in + k
        o1 = o0 + self.n_out
        s0 = o1 + k
        s1 = s0 + self.n_scratch

        def hosted(*refs):
            src, dst, sems = refs[i0:o0], refs[o1:s0], refs[s1:]
            first, middle, last = phase()
            pl.when(first)(lambda: plan.start(src, dst, sems))
            body(*refs[:i0], *refs[o0:o1], *refs[s0:s1])
            pl.when(middle)(lambda: plan.pass_on(src, dst, sems))
            pl.when(last)(lambda: plan.finish(src, dst, sems))

        return hosted


TT = 256
SEG = TT // SUB


def _to_segments(a):
    t, c = a.shape
    return a.reshape(t // TT, SUB, SEG, c).transpose(0, 2, 1, 3).reshape(t, c)


def _from_segments(a):
    t, c = a.shape
    return a.reshape(t // TT, SEG, SUB, c).transpose(0, 2, 1, 3).reshape(t, c)


def _sublane_is(s):
    return lax.broadcasted_iota(jnp.int32, (TT, W_MIX), 0) % SUB == s


def _look_back(ext_ref, cur, before):
    ext_ref[TT:, :] = cur
    ext_ref[0:TT, :] = jnp.where(_sublane_is(0), before, pltpu.roll(cur, 1, axis=0))


def _last_segment(cur):
    return pltpu.roll(cur, TT - (SUB - 1), axis=0)


def _look_ahead(ext_ref, cur, after):
    ext_ref[0:TT, :] = cur
    ext_ref[TT:, :] = jnp.where(_sublane_is(SUB - 1), after, pltpu.roll(cur, TT - 1, axis=0))


def _first_segment(cur):
    return pltpu.roll(cur, SUB - 1, axis=0)


def _tap_loop(body):
    lax.fori_loop(0, TT // CH, lambda c, carry: body(pl.multiple_of(c * CH, CH), carry), 0)


def _conv_rows(ext_ref, w_ref, out_ref, *, nk, off, reverse):
    def body(r0, carry):
        acc = jnp.zeros((CH, W_MIX), F32)
        for k in range(nk):
            kk = nk - 1 - k if reverse else k
            acc = acc + ext_ref[pl.ds(r0 + (off + k) * SUB, CH), :] * w_ref[kk:kk + 1, :]
        out_ref[pl.ds(r0, CH), :] = acc
        return carry
    _tap_loop(body)


def _conv_wgrad(g_ref, ext_ref, acc_ref, *, nk, off, row0):
    def body(r0, carry):
        g = g_ref[pl.ds(r0, CH), :]
        for k in range(nk):
            a = (row0 + k) * SUB
            acc_ref[a:a + SUB, :] += _fold8(g * ext_ref[pl.ds(r0 + (off + k) * SUB, CH), :])
        return carry
    _tap_loop(body)


def _window_sums(ext_ref, out_ref, *, forward):
    def body(r0, carry):
        for g, w in enumerate(POOL_WINDOWS):
            lanes = slice(g * POOL_DIM, (g + 1) * POOL_DIM)
            acc = jnp.zeros((CH, POOL_DIM), F32)
            for j in range(w):
                off = j if forward else SEG - j
                acc = acc + ext_ref[pl.ds(r0 + off * SUB, CH), lanes]
            out_ref[pl.ds(r0, CH), lanes] = acc
        return carry
    _tap_loop(body)


def _inv_count(tile):
    r = lax.broadcasted_iota(jnp.int32, (TT, POOL_DIM), 0)
    t1 = (tile * TT + (r % SUB) * SEG + r // SUB + 1).astype(F32)
    return jnp.concatenate([1.0 / jnp.minimum(t1, float(w)) for w in POOL_WINDOWS], axis=1)


def _groups(h_ref):
    return [h_ref[:, k * W_MIX:(k + 1) * W_MIX].astype(F32) for k in range(9)]


def _layer_norm(v, g, b):
    mu = jnp.mean(v, axis=-1, keepdims=True)
    vc = v - mu
    var = jnp.mean(vc * vc, axis=-1, keepdims=True)
    rstd = lax.rsqrt(var + LN_EPS)
    vhat = vc * rstd
    return vhat * g + b, vhat, rstd


def _layer_norm_bwd(g_out, vhat, rstd, g):
    gh = g_out * g
    m1 = jnp.mean(gh, axis=-1, keepdims=True)
    m2 = jnp.mean(gh * vhat, axis=-1, keepdims=True)
    return rstd * (gh - m1 - vhat * m2)


def _pool_linear(pooled, pw_ref, pb_ref):
    outs = []
    for g in range(len(POOL_WINDOWS)):
        lanes = slice(g * POOL_DIM, (g + 1) * POOL_DIM)
        outs.append(_dot(pooled[:, lanes].astype(BF16), pw_ref[g].astype(BF16)))
    return jnp.concatenate(outs, axis=1) + pb_ref[...]


def _in_proj(x, w, name, plan=None):
    t, d = x.shape
    n = w.shape[0]
    tm, tn = min(t, 1024), 1536
    nm, nn = t // tm, n // tn

    def body(x_ref, w_ref, o_ref, xb_ref):
        @pl.when(pl.program_id(1) == 0)
        def _():
            xb_ref[...] = x_ref[...].astype(BF16)
        o_ref[...] = _dot_nt(xb_ref[...], w_ref[...]).astype(BF16)

    def phase():
        step = pl.program_id(0) * nn + pl.program_id(1)
        return step == 0, step == (nm * nn) // 2, step == nm * nn - 1

    host = _Hosted(plan, n_in=2, n_out=1, n_scratch=1)
    return pl.pallas_call(
        host.wrap(body, phase), name=name, grid=(nm, nn),
        in_specs=[pl.BlockSpec((tm, d), lambda i, j: (i, 0)), pl.BlockSpec((tn, d), lambda i, j: (j, 0))]
        + host.in_specs,
        out_specs=[pl.BlockSpec((tm, tn), lambda i, j: (i, j))] + host.out_specs,
        out_shape=[jax.ShapeDtypeStruct((t, n), BF16)] + host.out_shape,
        scratch_shapes=[pltpu.VMEM((tm, d), BF16)] + host.scratch,
        compiler_params=_cparams(("arbitrary", "arbitrary")),
    )(x, w, *host.operands)


def _mixer_fwd(h, x, caw, cab, cbw, cbb, lbg, lbb, pw, pb, ps, w_out, lng, lnb, name, plan=None, target=None):
    t = h.shape[0]
    tt = TT
    n = t // tt
    with_loss = target is not None

    def body(h_ref, x_ref, *refs):
        if with_loss:
            t_ref, refs = refs[0], refs[1:]
        (caw_ref, cab_ref, cbw_ref, cbb_ref, lbg_ref, lbb_ref, pw_ref, pb_ref, ps_ref, wo_ref, lng_ref, lnb_ref,
         y_ref, u2_ref, z_ref, xn_ref) = refs[:16]
        refs = refs[16:]
        if with_loss:
            l_ref, refs = refs[0], refs[1:]
        exta, extb, extc, lasta, lastb, lastc, tmp, inv_ref = refs
        i = pl.program_id(0)

        @pl.when(i == 0)
        def _():
            for e in (lasta, lastb, lastc):
                e[...] = jnp.zeros_like(e)
            if with_loss:
                l_ref[...] = jnp.zeros_like(l_ref)

        @pl.when(i <= 1)
        def _():
            inv_ref[...] = _inv_count(i)

        a_bg, a_cg, a_v, a_z, b_v, b_g, b_z, c_u, c_z = _groups(h_ref)
        for ext, last, cur in ((exta, lasta, a_cg * a_v), (extb, lastb, b_v * _sigmoid(b_g)), (extc, lastc, c_u)):
            _look_back(ext, cur, last[...])
            last[...] = _last_segment(cur)

        _conv_rows(exta, caw_ref, tmp, nk=KA, off=SEG - (KA - 1), reverse=False)
        y_a = a_bg * (tmp[...] + cab_ref[...]) * (a_z * _sigmoid(a_z))
        y_ref[:, 0:W_MIX] = y_a.astype(BF16)

        _window_sums(extc, tmp, forward=False)
        pooled = tmp[...] * inv_ref[...] - c_u
        p = _pool_linear(pooled, pw_ref, pb_ref)
        y_c = p * ps_ref[...] * (c_z * _sigmoid(c_z))
        y_ref[:, 2 * W_MIX:3 * W_MIX] = y_c.astype(BF16)

        _conv_rows(extb, cbw_ref, tmp, nk=KB, off=SEG - (KB - 1), reverse=False)
        u2 = tmp[...] + cbb_ref[...]
        u2_ref[...] = u2
        ln, _, _ = _layer_norm(u2, lbg_ref[...], lbb_ref[...])
        y_b = (ln * _sigmoid(ln)) * (b_z * _sigmoid(b_z))
        y_ref[:, W_MIX:2 * W_MIX] = y_b.astype(BF16)

        out = _dot(y_ref[...], wo_ref[...])
        z = ALPHA * x_ref[...] + out
        z_ref[...] = z
        xn, _, _ = _layer_norm(z, lng_ref[...], lnb_ref[...])
        if with_loss:
            e = xn - t_ref[...]
            xn_ref[...] = e * (1.0 / D_MODEL)
            l_ref[...] += _fold8(e * e) * (0.5 / D_MODEL)
        else:
            xn_ref[...] = xn

    def phase():
        i = pl.program_id(0)
        return i == 0, i == n // 2, i == n - 1

    row = lambda wd: pl.BlockSpec((tt, wd), lambda i: (i, 0))
    full = lambda a: pl.BlockSpec(a.shape, lambda i: (0,) * a.ndim)
    params = (caw, cab, cbw, cbb, lbg, lbb, pw, pb, ps, w_out, lng, lnb)
    extra_in = [target] if with_loss else []
    extra_out = [jax.ShapeDtypeStruct((SUB, D_MODEL), F32)] if with_loss else []
    host = _Hosted(plan, n_in=2 + len(extra_in) + len(params), n_out=4 + len(extra_out), n_scratch=8)
    return pl.pallas_call(
        host.wrap(body, phase), name=name, grid=(n,),
        in_specs=[row(D_IN), row(D_MODEL)] + [row(D_MODEL)] * len(extra_in) + [full(a) for a in params]
        + host.in_specs,
        out_specs=[row(D_MIX), row(W_MIX), row(D_MODEL), row(D_MODEL)] + [full(o) for o in extra_out]
        + host.out_specs,
        out_shape=[jax.ShapeDtypeStruct((t, D_MIX), BF16), jax.ShapeDtypeStruct((t, W_MIX), F32),
                   jax.ShapeDtypeStruct((t, D_MODEL), F32), jax.ShapeDtypeStruct((t, D_MODEL), F32)]
        + extra_out + host.out_shape,
        scratch_shapes=[pltpu.VMEM((2 * tt, W_MIX), F32)] * 3 + [pltpu.VMEM((tt, W_MIX), F32)] * 5 + host.scratch,
        compiler_params=_cparams(("arbitrary",)),
    )(h, x, *extra_in, *params, *host.operands)


def _out_proj_bwd(g_xn, z, y, w_out, lng, name):
    t = z.shape[0]
    tt = min(t, 256)

    def body(g_ref, z_ref, y_ref, wo_ref, lng_ref, gz_ref, gy_ref, gwo_ref, gln_ref, accg, accb, accw):
        i = pl.program_id(0)

        @pl.when(i == 0)
        def _():
            accw[...] = jnp.zeros_like(accw)
            accg[...] = jnp.zeros_like(accg)
            accb[...] = jnp.zeros_like(accb)

        g = g_ref[...]
        _, zhat, rstd = _layer_norm(z_ref[...], lng_ref[...], 0.0)
        accg[...] += _fold8(g * zhat)
        accb[...] += _fold8(g)
        g_z = _layer_norm_bwd(g, zhat, rstd, lng_ref[...])
        gz_ref[...] = g_z
        gzb = g_z.astype(BF16)
        gy_ref[...] = _dot_nt(gzb, wo_ref[...])
        accw[...] += _dot_tn(y_ref[...], gzb)

        @pl.when(i == pl.num_programs(0) - 1)
        def _():
            gwo_ref[...] = accw[...].astype(BF16)
            gln_ref[...] = jnp.zeros_like(gln_ref)
            gln_ref[0:1, :] = jnp.sum(accg[...], axis=0, keepdims=True)
            gln_ref[1:2, :] = jnp.sum(accb[...], axis=0, keepdims=True)

    row = lambda wd: pl.BlockSpec((tt, wd), lambda i: (i, 0))
    full = lambda shape: pl.BlockSpec(shape, lambda i: (0,) * len(shape))
    return pl.pallas_call(
        body, name=name, grid=(t // tt,),
        in_specs=[row(D_MODEL), row(D_MODEL), row(D_MIX), full(w_out.shape), full(lng.shape)],
        out_specs=[row(D_MODEL), row(D_MIX), full((D_MIX, D_MODEL)), full((SUB, D_MODEL))],
        out_shape=[jax.ShapeDtypeStruct((t, D_MODEL), F32), jax.ShapeDtypeStruct((t, D_MIX), F32),
                   jax.ShapeDtypeStruct((D_MIX, D_MODEL), BF16), jax.ShapeDtypeStruct((SUB, D_MODEL), F32)],
        scratch_shapes=[pltpu.VMEM((SUB, D_MODEL), F32)] * 2 + [pltpu.VMEM((D_MIX, D_MODEL), F32)],
        compiler_params=_cparams(("arbitrary",)),
    )(g_xn, z, y, w_out, lng)


def _mixer_bwd(h, u2, g_y, caw, cab, cbw, lbg, lbb, pw, pb, ps, name, plan=None):
    t = h.shape[0]
    tt = TT
    n = t // tt
    before_groups = (1, 2, 4, 5, 7)

    def body(h_ref, p_cg, p_av, p_bv, p_bg, p_cu, u2_ref, gy_ref, caw_ref, cab_ref, cbw_ref, lbg_ref, lbb_ref,
             pw_ref, pb_ref, ps_ref, gh_ref, rows_ref, gpw_ref,
             exta, extb, extc, gca, gu2, qx, nexta, nextb, nextc, tmp, tmp2, inv_ref, acc):
        s = pl.program_id(0)
        i = n - 1 - s

        @pl.when(s == 0)
        def _():
            acc[...] = jnp.zeros_like(acc)
            gpw_ref[...] = jnp.zeros_like(gpw_ref)
            for e in (nexta, nextb, nextc):
                e[...] = jnp.zeros_like(e)

        live = (i > 0).astype(F32)
        f32 = lambda ref: ref[...].astype(F32)
        before_a = _last_segment(f32(p_cg) * f32(p_av)) * live
        before_b = _last_segment(f32(p_bv) * _sigmoid(f32(p_bg))) * live
        before_c = _last_segment(f32(p_cu)) * live

        a_bg, a_cg, a_v, a_z, b_v, b_g, b_z, c_u, c_z = _groups(h_ref)
        g_ya = gy_ref[:, 0:W_MIX]
        g_yb = gy_ref[:, W_MIX:2 * W_MIX]
        g_yc = gy_ref[:, 2 * W_MIX:3 * W_MIX]

        def add_row(r, v):
            acc[r * SUB:(r + 1) * SUB, :] += _fold8(v)

        _look_back(exta, a_cg * a_v, before_a)
        _conv_rows(exta, caw_ref, tmp, nk=KA, off=SEG - (KA - 1), reverse=False)
        ca = tmp[...] + cab_ref[...]
        sg = _sigmoid(a_z)
        s_az = a_z * sg
        t_a = g_ya * a_bg
        gh_ref[:, 0:W_MIX] = (g_ya * ca * s_az).astype(BF16)
        gh_ref[:, 3 * W_MIX:4 * W_MIX] = (t_a * ca * (sg * (1.0 + a_z * (1.0 - sg)))).astype(BF16)
        g_ca = t_a * s_az
        _look_ahead(gca, g_ca, nexta[...])
        nexta[...] = _first_segment(g_ca)
        add_row(R_CAB, g_ca)
        _conv_wgrad(gca, exta, acc, nk=KA, off=SEG - (KA - 1), row0=R_CAW)
        _conv_rows(gca, caw_ref, tmp, nk=KA, off=0, reverse=True)
        g_pa = tmp[...]
        gh_ref[:, W_MIX:2 * W_MIX] = (g_pa * a_v).astype(BF16)
        gh_ref[:, 2 * W_MIX:3 * W_MIX] = (g_pa * a_cg).astype(BF16)

        sgg = _sigmoid(b_g)
        _look_back(extb, b_v * sgg, before_b)
        ln, u2hat, rstd = _layer_norm(u2_ref[...], lbg_ref[...], lbb_ref[...])
        sl = _sigmoid(ln)
        u3 = ln * sl
        sz = _sigmoid(b_z)
        s_bz = b_z * sz
        gh_ref[:, 6 * W_MIX:7 * W_MIX] = (g_yb * u3 * (sz * (1.0 + b_z * (1.0 - sz)))).astype(BF16)
        g_ln = g_yb * s_bz * (sl * (1.0 + ln * (1.0 - sl)))
        add_row(R_LBG, g_ln * u2hat)
        add_row(R_LBB, g_ln)
        g_u2 = _layer_norm_bwd(g_ln, u2hat, rstd, lbg_ref[...])
        _look_ahead(gu2, g_u2, nextb[...])
        nextb[...] = _first_segment(g_u2)
        add_row(R_CBB, g_u2)
        _conv_wgrad(gu2, extb, acc, nk=KB, off=SEG - (KB - 1), row0=R_CBW)
        _conv_rows(gu2, cbw_ref, tmp, nk=KB, off=0, reverse=True)
        g_u1 = tmp[...]
        gh_ref[:, 4 * W_MIX:5 * W_MIX] = (g_u1 * sgg).astype(BF16)
        gh_ref[:, 5 * W_MIX:6 * W_MIX] = (g_u1 * b_v * sgg * (1.0 - sgg)).astype(BF16)

        _look_back(extc, c_u, before_c)
        _window_sums(extc, tmp, forward=False)
        @pl.when((s == 0) | (i == 0))
        def _():
            inv_ref[...] = _inv_count(i)
        inv = inv_ref[...]
        pooled = tmp[...] * inv - c_u
        p = _pool_linear(pooled, pw_ref, pb_ref)
        sc = _sigmoid(c_z)
        s_cz = c_z * sc
        scale = ps_ref[...]
        gh_ref[:, 8 * W_MIX:9 * W_MIX] = (g_yc * p * scale * (sc * (1.0 + c_z * (1.0 - sc)))).astype(BF16)
        t_c = g_yc * s_cz
        add_row(R_PS, t_c * p)
        g_p = t_c * scale
        add_row(R_PB, g_p)
        g_pooled = []
        for g in range(len(POOL_WINDOWS)):
            lanes = slice(g * POOL_DIM, (g + 1) * POOL_DIM)
            gpg = g_p[:, lanes].astype(BF16)
            gpw_ref[g] += _dot_tn(pooled[:, lanes].astype(BF16), gpg)
            g_pooled.append(_dot_nt(gpg, pw_ref[g].astype(BF16)))
        g_pooled = jnp.concatenate(g_pooled, axis=1)
        q = g_pooled * inv
        _look_ahead(qx, q, nextc[...])
        nextc[...] = _first_segment(q)
        _window_sums(qx, tmp2, forward=True)
        gh_ref[:, 7 * W_MIX:8 * W_MIX] = (tmp2[...] - g_pooled).astype(BF16)

        @pl.when(s == n - 1)
        def _():
            for r in range(N_ROWS):
                rows_ref[r:r + 1, :] = jnp.sum(acc[r * SUB:(r + 1) * SUB, :], axis=0, keepdims=True)

    def phase():
        s = pl.program_id(0)
        return s == 0, s == n // 2, s == n - 1

    row = lambda wd: pl.BlockSpec((tt, wd), lambda s: (n - 1 - s, 0))
    before = [pl.BlockSpec((tt, W_MIX), lambda s, k=k: (jnp.maximum(n - 2 - s, 0), k)) for k in before_groups]
    full = lambda shape: pl.BlockSpec(shape, lambda s: (0,) * len(shape))
    params = (caw, cab, cbw, lbg, lbb, pw, pb, ps)
    host = _Hosted(plan, n_in=3 + len(before) + len(params), n_out=3, n_scratch=13)
    return pl.pallas_call(
        host.wrap(body, phase), name=name, grid=(n,),
        in_specs=[row(D_IN)] + before + [row(W_MIX), row(D_MIX)] + [full(a.shape) for a in params] + host.in_specs,
        out_specs=[row(D_IN), full((N_ROWS, W_MIX)), full(pw.shape)] + host.out_specs,
        out_shape=[jax.ShapeDtypeStruct((t, D_IN), BF16), jax.ShapeDtypeStruct((N_ROWS, W_MIX), F32),
                   jax.ShapeDtypeStruct(pw.shape, F32)] + host.out_shape,
        scratch_shapes=[pltpu.VMEM((2 * tt, W_MIX), F32)] * 6 + [pltpu.VMEM((tt, W_MIX), F32)] * 6
        + [pltpu.VMEM((N_ROWS * SUB, W_MIX), F32)] + host.scratch,
        compiler_params=_cparams(("arbitrary",)),
    )(h, *([h] * len(before)), u2, g_y, *params, *host.operands)


def _in_proj_wgrad(x, g_h, name, plan=None):
    t, d = x.shape
    n = g_h.shape[1]
    tk, tn = min(t, 1024), n // 2
    nk = t // tk

    def body(x_ref, g_ref, o_ref, acc):
        k = pl.program_id(1)

        @pl.when(k == 0)
        def _():
            acc[...] = jnp.zeros_like(acc)
        acc[...] += _dot_tn(x_ref[...].astype(BF16), g_ref[...])

        @pl.when(k == nk - 1)
        def _():
            o_ref[...] = acc[...].T.astype(BF16)

    def phase():
        step = pl.program_id(0) * nk + pl.program_id(1)
        return step == 0, step == nk, step == 2 * nk - 1

    host = _Hosted(plan, n_in=2, n_out=1, n_scratch=1)
    return pl.pallas_call(
        host.wrap(body, phase), name=name, grid=(n // tn, nk),
        in_specs=[pl.BlockSpec((tk, d), lambda j, k: (k, 0)), pl.BlockSpec((tk, tn), lambda j, k: (k, j))]
        + host.in_specs,
        out_specs=[pl.BlockSpec((tn, d), lambda j, k: (j, 0))] + host.out_specs,
        out_shape=[jax.ShapeDtypeStruct((n, d), BF16)] + host.out_shape,
        scratch_shapes=[pltpu.VMEM((d, tn), F32)] + host.scratch,
        compiler_params=_cparams(("arbitrary", "arbitrary")),
    )(x, g_h, *host.operands)


def _in_proj_dgrad(g_h, w, g_z, name, plan=None):
    t, n = g_h.shape
    d = w.shape[1]
    tm, tk = min(t, 1024), 1536
    nm, nk = t // tm, n // tk

    def body(g_ref, w_ref, gz_ref, o_ref):
        @pl.when(pl.program_id(1) == 0)
        def _():
            o_ref[...] = ALPHA * gz_ref[...]
        o_ref[...] += _dot(g_ref[...], w_ref[...])

    def phase():
        step = pl.program_id(0) * nk + pl.program_id(1)
        return step == 0, step == (nm * nk) // 2, step == nm * nk - 1

    host = _Hosted(plan, n_in=3, n_out=1, n_scratch=0)
    return pl.pallas_call(
        host.wrap(body, phase), name=name, grid=(nm, nk),
        in_specs=[pl.BlockSpec((tm, tk), lambda i, k: (i, k)), pl.BlockSpec((tk, d), lambda i, k: (k, 0)),
                  pl.BlockSpec((tm, d), lambda i, k: (i, 0))] + host.in_specs,
        out_specs=[pl.BlockSpec((tm, d), lambda i, k: (i, 0))] + host.out_specs,
        out_shape=[jax.ShapeDtypeStruct((t, d), F32)] + host.out_shape,
        scratch_shapes=host.scratch,
        compiler_params=_cparams(("arbitrary", "arbitrary")),
    )(g_h, w, g_z, *host.operands)


BC1 = 1.0 - ADAM_B1 ** ADAM_STEP
BC2 = 1.0 - ADAM_B2 ** ADAM_STEP


def _adamw_math(g, w, m, v):
    nm = ADAM_B1 * m + (1.0 - ADAM_B1) * g
    nv = ADAM_B2 * v + (1.0 - ADAM_B2) * (g * g)
    delta = -ADAM_LR * ((nm / BC1) / (jnp.sqrt(nv / BC2) + ADAM_EPS) + ADAM_WD * w)
    return delta, nm, nv


def _total(ref):
    g = ref[0].astype(F32)
    for k in range(1, ref.shape[0]):
        g = g + ref[k].astype(F32)
    return g


def _adamw_layers(parts, w, m, v, name):
    depth, r, c = w.shape
    p = parts[0].shape[0]
    tr = _row_tile(r)
    nr = r // tr

    def body(*refs):
        p_refs = refs[:depth]
        w_ref, m_ref, v_ref, g_ref, d_ref, nm_ref, nv_ref = refs[depth:]
        for l in range(depth):
            @pl.when(pl.program_id(0) == l)
            def _(l=l):
                g = _total(p_refs[l])
                delta, nm, nv = _adamw_math(g, w_ref[0], m_ref[0], v_ref[0])
                g_ref[0], d_ref[0], nm_ref[0], nv_ref[0] = g, delta, nm, nv

    def part_spec(l):
        return pl.BlockSpec((p, tr, c), lambda li, i: (0, jnp.where(li == l, i, jnp.where(li < l, 0, nr - 1)), 0))

    blk = pl.BlockSpec((1, tr, c), lambda li, i: (li, i, 0))
    out = jax.ShapeDtypeStruct((depth, r, c), F32)
    return pl.pallas_call(
        body, name=name, grid=(depth, nr),
        in_specs=[part_spec(l) for l in range(depth)] + [blk] * 3,
        out_specs=[blk] * 4, out_shape=[out] * 4,
        compiler_params=_cparams(("arbitrary", "arbitrary")),
    )(*parts, w, m, v)


SMALL_PARAMS = ("conv_a_w", "conv_a_b", "conv_b_w", "conv_b_b", "ln_b_g", "ln_b_b", "pool_w", "pool_b", "pool_scale",
                "ln_g", "ln_b")


def _adamw_small(rows_parts, gln_parts, gpw_parts, loss_parts, params, name):
    depth = len(rows_parts)
    cs = params["conv_a_w"][0].shape[2]
    operands = [*rows_parts, *gln_parts, *gpw_parts, loss_parts] + [a for n in SMALL_PARAMS for a in params[n]]
    n_in = len(operands)
    out_shape = [jax.ShapeDtypeStruct((1, 1), F32)]
    out_shape += [jax.ShapeDtypeStruct(params[n][0].shape, F32) for n in SMALL_PARAMS for _ in range(4)]

    def body(*refs):
        rows_p, gln_p, gpw_p = refs[:depth], refs[depth:2 * depth], refs[2 * depth:3 * depth]
        loss_p = refs[3 * depth]
        prm, outs = refs[3 * depth + 1:n_in], refs[n_in + 1:]
        refs[n_in][...] = jnp.sum(_total(loss_p)).reshape(1, 1)
        x, y, c = _position()
        to_front = (W_MIX - (4 * x + 2 * y + c) * cs) % W_MIX

        def update(name, g, at):
            k = SMALL_PARAMS.index(name)
            w_ref, m_ref, v_ref = prm[3 * k:3 * k + 3]
            g_ref, d_ref, nm_ref, nv_ref = outs[4 * k:4 * k + 4]
            delta, nm, nv = _adamw_math(g, w_ref[at], m_ref[at], v_ref[at])
            g_ref[at], d_ref[at], nm_ref[at], nv_ref[at] = g, delta, nm, nv

        for l in range(depth):
            rows = _total(rows_p[l])
            mine = pltpu.roll(rows, to_front, axis=1)
            gln = _total(gln_p[l])
            one = (slice(l, l + 1), slice(None))
            for name, r in (("conv_a_b", R_CAB), ("conv_b_b", R_CBB), ("ln_b_g", R_LBG), ("ln_b_b", R_LBB),
                            ("pool_scale", R_PS)):
                update(name, rows[r:r + 1, :], one)
            for g in range(len(POOL_WINDOWS)):
                update("pool_b", rows[R_PB:R_PB + 1, g * POOL_DIM:(g + 1) * POOL_DIM], (l, slice(g, g + 1), slice(None)))
            update("ln_g", gln[0:1, :], one)
            update("ln_b", gln[1:2, :], one)
            update("conv_a_w", mine[R_CAW:R_CAW + KA, 0:cs], (l,))
            update("conv_b_w", mine[R_CBW:R_CBW + KB, 0:cs], (l,))
            update("pool_w", _total(gpw_p[l]), (l,))

    vmem = pl.BlockSpec(memory_space=pltpu.VMEM)
    return pl.pallas_call(
        body, name=name, in_specs=[vmem] * n_in, out_specs=[vmem] * len(out_shape), out_shape=out_shape,
        compiler_params=pltpu.CompilerParams(vmem_limit_bytes=VMEM_LIMIT),
    )(*operands)


def kernel(x, w_in, conv_a_w, conv_a_b, conv_b_w, conv_b_b, ln_b_g, ln_b_b, pool_w, pool_b, pool_scale, w_out, ln_g, ln_b, loss_target, m_w_in, m_conv_a_w, m_conv_a_b, m_conv_b_w, m_conv_b_b, m_ln_b_g, m_ln_b_b, m_pool_w, m_pool_b, m_pool_scale, m_w_out, m_ln_g, m_ln_b, v_w_in, v_conv_a_w, v_conv_a_b, v_conv_b_w, v_conv_b_b, v_ln_b_g, v_ln_b_b, v_pool_w, v_pool_b, v_pool_scale, v_w_out, v_ln_g, v_ln_b):
    depth = w_in.shape[0]
    x0 = _to_segments(x[0])
    target = _to_segments(loss_target[0])
    r2 = lambda a: a.reshape(1, -1)

    tr = lambda a: jnp.swapaxes(a, 1, 2)
    w_in_t, m_w_in_t, v_w_in_t = tr(w_in), tr(m_w_in), tr(v_w_in)
    w_in_b, w_out_b = w_in_t.astype(BF16), w_out.astype(BF16)
    conv_sh = jnp.concatenate([conv_a_w, conv_b_w], axis=1)
    full_in = lambda g: g.reshape(D_IN, D_MODEL)
    full_out = lambda g: g.reshape(D_MIX, D_MODEL)
    w_in_f = [full_in(_run_plan(_Gather([w_in_b[0]]), "gather_w_in_0")[0])]
    w_out_f = []

    xs, hs, ys, u2s, zs = [x0], [], [], [], []
    for l in range(depth):
        h, *got = _in_proj(xs[l], w_in_f[l], f"in_proj_{l}",
                           plan=_Gather([w_out_b[0], conv_sh]) if l == 0 else None)
        if got:
            w_out_f.append(full_out(got[0]))
            conv_f = got[1].transpose(1, 2, 0, 3).reshape(depth, KA + KB, W_MIX)
            caw_f, cbw_f = conv_f[:, :KA], conv_f[:, KA:]
        last = l + 1 == depth
        y, u2, z, xn, *got = _mixer_fwd(
            h, xs[l], caw_f[l], r2(conv_a_b[l]), cbw_f[l], r2(conv_b_b[l]), r2(ln_b_g[l]), r2(ln_b_b[l]),
            pool_w[l], r2(pool_b[l]), r2(pool_scale[l]), w_out_f[l], r2(ln_g[l]), r2(ln_b[l]), f"mixer_fwd_{l}",
            plan=None if last else _Gather([w_in_b[l + 1], w_out_b[l + 1]]), target=target if last else None)
        if last:
            g, loss_rows = xn, got[0]
        else:
            w_in_f.append(full_in(got[0])), w_out_f.append(full_out(got[1]))
        hs.append(h), ys.append(y), u2s.append(u2), zs.append(z), xs.append(xn)


    gi_parts, go_parts, g_rows, g_lns, g_pool_w = ([None] * depth for _ in range(5))
    waiting = []
    for l in reversed(range(depth)):
        g_z, g_y, g_w_out, g_lns[l] = _out_proj_bwd(g, zs[l], ys[l], w_out_f[l], r2(ln_g[l]), f"out_proj_bwd_{l}")
        go = g_w_out.reshape(N_DEV, D_MIX // N_DEV, D_MODEL)
        if l == 0:
            waiting += [go]
        g_h, g_rows[l], g_pool_w[l], *got = _mixer_bwd(
            hs[l], u2s[l], g_y, caw_f[l], r2(conv_a_b[l]), cbw_f[l], r2(ln_b_g[l]), r2(ln_b_b[l]),
            pool_w[l], r2(pool_b[l]), r2(pool_scale[l]), f"mixer_bwd_{l}",
            plan=_Exchange(waiting) if waiting else None)
        if got:
            if l + 1 < depth:
                gi_parts[l + 1], go_parts[l + 1] = got[0], got[1]
            if l == 0:
                go_parts[0] = got[-1]
        if l > 0:
            gi = _in_proj_wgrad(xs[l], g_h, f"in_proj_wgrad_{l}")[0]
            waiting = [gi.reshape(N_DEV, D_IN // N_DEV, D_MODEL), go]
            g = _in_proj_dgrad(g_h, w_in_f[l], g_z, f"in_proj_dgrad_{l}")[0]
        else:
            small = [*g_rows, *g_lns, *g_pool_w, loss_rows]
            gi, *small_parts = _in_proj_wgrad(xs[0], g_h, "in_proj_wgrad_0", plan=_Gather(small))
            g, gi_parts[0] = _in_proj_dgrad(g_h, w_in_f[0], g_z, "in_proj_dgrad_0",
                                            plan=_Exchange([gi.reshape(N_DEV, D_IN // N_DEV, D_MODEL)]))
    grad_x = _from_segments(g)[None]

    out_in = [tr(a) for a in _adamw_layers(gi_parts, w_in_t, m_w_in_t, v_w_in_t, "adamw_w_in")]
    out_out = _adamw_layers(go_parts, w_out, m_w_out, v_w_out, "adamw_w_out")
    params = dict(
        conv_a_w=(conv_a_w, m_conv_a_w, v_conv_a_w), conv_a_b=(conv_a_b, m_conv_a_b, v_conv_a_b),
        conv_b_w=(conv_b_w, m_conv_b_w, v_conv_b_w), conv_b_b=(conv_b_b, m_conv_b_b, v_conv_b_b),
        ln_b_g=(ln_b_g, m_ln_b_g, v_ln_b_g), ln_b_b=(ln_b_b, m_ln_b_b, v_ln_b_b),
        pool_w=(pool_w, m_pool_w, v_pool_w), pool_b=(pool_b, m_pool_b, v_pool_b),
        pool_scale=(pool_scale, m_pool_scale, v_pool_scale), ln_g=(ln_g, m_ln_g, v_ln_g), ln_b=(ln_b, m_ln_b, v_ln_b))
    loss, *small_out = _adamw_small(small_parts[:depth], small_parts[depth:2 * depth], small_parts[2 * depth:3 * depth],
                                    small_parts[3 * depth], params, "adamw_small")
    small_out = {n: small_out[4 * k:4 * k + 4] for k, n in enumerate(SMALL_PARAMS)}

    order = ("w_in", "conv_a_w", "conv_a_b", "conv_b_w", "conv_b_b", "ln_b_g", "ln_b_b", "pool_w", "pool_b",
             "pool_scale", "w_out", "ln_g", "ln_b")
    outs = []
    for k in range(4):
        outs += [out_in[k] if n == "w_in" else out_out[k] if n == "w_out" else small_out[n][k] for n in order]
    return (loss.reshape(()), grad_x, *outs)
```

```python
import jax
import jax.numpy as jnp
from jax import lax
from jax.experimental import pallas as pl
from jax.experimental.pallas import tpu as pltpu

F32 = jnp.float32
BF16 = jnp.bfloat16

DEPTH = 2
D_MODEL = 1024
W_MIX = 512
D_IN = 9 * W_MIX
D_MIX = 3 * W_MIX
POOL_WINDOWS = (2, 4, 8, 16)
POOL_DIM = 128
KA = 3
KB = 31
ALPHA = (2.0 * DEPTH) ** 0.25
LN_EPS = 1e-5
ADAM_LR, ADAM_B1, ADAM_B2, ADAM_EPS, ADAM_WD, ADAM_STEP = 0.001, 0.9, 0.999, 1e-08, 0.01, 10

N_DEV = 8
N_CHIP = 4
MESH = pl.DeviceIdType.MESH

CH = 32
SUB = 8
VMEM_LIMIT = 56 * 1024 * 1024

R_CAB, R_CBB, R_LBG, R_LBB, R_PB, R_PS, R_CAW, R_CBW = 0, 1, 2, 3, 4, 5, 6, 9
N_ROWS = R_CBW + KB
LN_ROWS = 16
SMALL_ROWS = N_ROWS + LN_ROWS


def _cparams(sem, **kw):
    return pltpu.CompilerParams(dimension_semantics=sem, vmem_limit_bytes=VMEM_LIMIT, **kw)


def _sigmoid(v):
    return 1.0 / (1.0 + jnp.exp(-v))


def _fold8(a):
    r, c = a.shape
    return a.reshape(r // SUB, SUB, c).sum(axis=0)


def _dot(a, b):
    return jnp.dot(a, b, preferred_element_type=F32)


def _dot_nt(a, b):
    return lax.dot_general(a, b, (((1,), (1,)), ((), ())), preferred_element_type=F32)


def _dot_tn(a, b):
    return lax.dot_general(a, b, (((0,), (0,)), ((), ())), preferred_element_type=F32)


def _row_tile(r):
    for cand in (512, 256, 128, 64, 32, 16, 8):
        if r % cand == 0 and r > cand:
            return cand
    return r


def _position():
    return lax.axis_index("x"), lax.axis_index("y"), lax.axis_index("c")


class _Gather:
    def __init__(self, arrs):
        self.arrs = list(arrs)
        na = len(self.arrs)
        self.out_shape = [jax.ShapeDtypeStruct((N_DEV,) + a.shape, a.dtype) for a in self.arrs]
        self.scratch = [pltpu.SemaphoreType.DMA((na, 7)), pltpu.SemaphoreType.DMA((na, 7)),
                        pltpu.SemaphoreType.DMA((na,))]

    def _copies(self, src, dst, sems):
        send_sems, recv_sems, local_sems = sems
        na = len(self.arrs)
        x, y, c = _position()
        me, sibling = (x, y, c), (x, y, 1 - c)
        chips = [(1 - x, y), (x, 1 - y), (1 - x, 1 - y)]

        def slot(a, dev):
            return dst[a].at[4 * dev[0] + 2 * dev[1] + dev[2]]

        def copy(a, k, block, to, from_src=False):
            return pltpu.make_async_remote_copy(
                src_ref=src[a] if from_src else slot(a, block), dst_ref=slot(a, block),
                send_sem=send_sems.at[a, k], recv_sem=recv_sems.at[a, k], device_id=to, device_id_type=MESH)

        mine = [pltpu.make_async_copy(src[a], slot(a, me), local_sems.at[a]) for a in range(na)]
        first, landed, passed, last = [], [], [], []
        for a in range(na):
            first.append(copy(a, 0, me, sibling, from_src=True))
            first += [copy(a, 1 + j, me, (*chip, c), from_src=True) for j, chip in enumerate(chips)]
        for j, chip in enumerate(chips):
            for a in range(na):
                landed.append(copy(a, 1 + j, (*chip, c), me))
                passed.append(copy(a, 4 + j, (*chip, c), sibling))
        for a in range(na):
            last.append(copy(a, 0, sibling, me))
            last += [copy(a, 4 + j, (*chip, 1 - c), me) for j, chip in enumerate(chips)]
        return mine, first, landed, passed, last

    def start(self, src, dst, sems):
        mine, first, _, _, _ = self._copies(src, dst, sems)
        for cp in mine + first:
            cp.start()

    def pass_on(self, src, dst, sems):
        _, _, landed, passed, _ = self._copies(src, dst, sems)
        for got, cp in zip(landed, passed):
            got.wait_recv()
            cp.start()

    def finish(self, src, dst, sems):
        mine, first, _, passed, last = self._copies(src, dst, sems)
        for cp in last:
            cp.wait_recv()
        for cp in first + passed:
            cp.wait_send()
        for cp in mine:
            cp.wait()


class _Exchange:
    def __init__(self, arrs):
        self.arrs = list(arrs)
        na = len(self.arrs)
        self.out_shape = [jax.ShapeDtypeStruct(a.shape, a.dtype) for a in self.arrs]
        self.scratch = [pltpu.SemaphoreType.DMA((na, N_DEV - 1)), pltpu.SemaphoreType.DMA((na, N_DEV - 1)),
                        pltpu.SemaphoreType.DMA((na,))]

    def _copies(self, src, dst, sems):
        send_sems, recv_sems, local_sems = sems
        na = len(self.arrs)
        x, y, c = _position()
        me = 4 * x + 2 * y + c
        mine = [pltpu.make_async_copy(src[a].at[me], dst[a].at[me], local_sems.at[a]) for a in range(na)]
        copies = []
        for a in range(na):
            for k in range(N_DEV - 1):
                flip = k + 1
                px, py, pc = x ^ (flip >> 2), y ^ ((flip >> 1) & 1), c ^ (flip & 1)
                copies.append(pltpu.make_async_remote_copy(
                    src_ref=src[a].at[4 * px + 2 * py + pc], dst_ref=dst[a].at[me],
                    send_sem=send_sems.at[a, k], recv_sem=recv_sems.at[a, k], device_id=(px, py, pc),
                    device_id_type=MESH))
        return mine, copies

    def start(self, src, dst, sems):
        mine, copies = self._copies(src, dst, sems)
        for cp in mine + copies:
            cp.start()

    def pass_on(self, src, dst, sems):
        pass

    def finish(self, src, dst, sems):
        mine, copies = self._copies(src, dst, sems)
        for cp in copies:
            cp.wait()
        for cp in mine:
            cp.wait()


def _run_plan(plan, name):
    na = len(plan.arrs)
    any_spec = pl.BlockSpec(memory_space=pl.ANY)

    def body(*refs):
        src, dst, sems = refs[:na], refs[na:2 * na], refs[2 * na:]
        plan.start(src, dst, sems)
        plan.pass_on(src, dst, sems)
        plan.finish(src, dst, sems)

    return pl.pallas_call(
        body, name=name, in_specs=[any_spec] * na, out_specs=[any_spec] * na,
        out_shape=plan.out_shape, scratch_shapes=plan.scratch,
    )(*plan.arrs)


class _Hosted:
    def __init__(self, plan, n_in, n_out, n_scratch):
        self.plan, self.n_in, self.n_out, self.n_scratch = plan, n_in, n_out, n_scratch
        any_spec = pl.BlockSpec(memory_space=pl.ANY)
        k = 0 if plan is None else len(plan.arrs)
        self.operands = [] if plan is None else plan.arrs
        self.in_specs = [any_spec] * k
        self.out_specs = [any_spec] * k
        self.out_shape = [] if plan is None else plan.out_shape
        self.scratch = [] if plan is None else plan.scratch

    def wrap(self, body, phase):
        if self.plan is None:
            return body
        plan, k = self.plan, len(self.plan.arrs)
        i0, o0 = self.n_in, self.n_in + k
        o1 = o0 + self.n_out
        s0 = o1 + k
        s1 = s0 + self.n_scratch

        def hosted(*refs):
            src, dst, sems = refs[i0:o0], refs[o1:s0], refs[s1:]
            first, middle, last = phase()
            pl.when(first)(lambda: plan.start(src, dst, sems))
            body(*refs[:i0], *refs[o0:o1], *refs[s0:s1])
            pl.when(middle)(lambda: plan.pass_on(src, dst, sems))
            pl.when(last)(lambda: plan.finish(src, dst, sems))

        return hosted


TT = 256
SEG = TT // SUB


def _to_segments(a):
    t, c = a.shape
    return a.reshape(t // TT, SUB, SEG, c).transpose(0, 2, 1, 3).reshape(t, c)


def _from_segments(a):
    t, c = a.shape
    return a.reshape(t // TT, SEG, SUB, c).transpose(0, 2, 1, 3).reshape(t, c)


def _sublane_is(s):
    return lax.broadcasted_iota(jnp.int32, (TT, W_MIX), 0) % SUB == s


def _look_back(ext_ref, cur, before):
    ext_ref[TT:, :] = cur
    ext_ref[0:TT, :] = jnp.where(_sublane_is(0), before, pltpu.roll(cur, 1, axis=0))


def _last_segment(cur):
    return pltpu.roll(cur, TT - (SUB - 1), axis=0)


def _look_ahead(ext_ref, cur, after):
    ext_ref[0:TT, :] = cur
    ext_ref[TT:, :] = jnp.where(_sublane_is(SUB - 1), after, pltpu.roll(cur, TT - 1, axis=0))


def _first_segment(cur):
    return pltpu.roll(cur, SUB - 1, axis=0)


def _tap_loop(body):
    lax.fori_loop(0, TT // CH, lambda c, carry: body(pl.multiple_of(c * CH, CH), carry), 0)


def _conv_rows(ext_ref, w_ref, out_ref, *, nk, off, reverse):
    def body(r0, carry):
        acc = jnp.zeros((CH, W_MIX), F32)
        for k in range(nk):
            kk = nk - 1 - k if reverse else k
            acc = acc + ext_ref[pl.ds(r0 + (off + k) * SUB, CH), :] * w_ref[kk:kk + 1, :]
        out_ref[pl.ds(r0, CH), :] = acc
        return carry
    _tap_loop(body)


def _conv_wgrad(g_ref, ext_ref, acc_ref, *, nk, off, row0):
    def body(r0, carry):
        g = g_ref[pl.ds(r0, CH), :]
        for k in range(nk):
            a = (row0 + k) * SUB
            acc_ref[a:a + SUB, :] += _fold8(g * ext_ref[pl.ds(r0 + (off + k) * SUB, CH), :])
        return carry
    _tap_loop(body)


def _window_sums(ext_ref, out_ref, *, forward):
    def body(r0, carry):
        for g, w in enumerate(POOL_WINDOWS):
            lanes = slice(g * POOL_DIM, (g + 1) * POOL_DIM)
            acc = jnp.zeros((CH, POOL_DIM), F32)
            for j in range(w):
                off = j if forward else SEG - j
                acc = acc + ext_ref[pl.ds(r0 + off * SUB, CH), lanes]
            out_ref[pl.ds(r0, CH), lanes] = acc
        return carry
    _tap_loop(body)


def _inv_count(tile):
    r = lax.broadcasted_iota(jnp.int32, (TT, POOL_DIM), 0)
    t1 = (tile * TT + (r % SUB) * SEG + r // SUB + 1).astype(F32)
    return jnp.concatenate([1.0 / jnp.minimum(t1, float(w)) for w in POOL_WINDOWS], axis=1)


def _groups(h_ref):
    return [h_ref[:, k * W_MIX:(k + 1) * W_MIX].astype(F32) for k in range(9)]


def _layer_norm(v, g, b):
    mu = jnp.mean(v, axis=-1, keepdims=True)
    vc = v - mu
    var = jnp.mean(vc * vc, axis=-1, keepdims=True)
    rstd = lax.rsqrt(var + LN_EPS)
    vhat = vc * rstd
    return vhat * g + b, vhat, rstd


def _layer_norm_bwd(g_out, vhat, rstd, g):
    gh = g_out * g
    m1 = jnp.mean(gh, axis=-1, keepdims=True)
    m2 = jnp.mean(gh * vhat, axis=-1, keepdims=True)
    return rstd * (gh - m1 - vhat * m2)


def _pool_linear(pooled, pw_ref, pb_ref):
    outs = []
    for g in range(len(POOL_WINDOWS)):
        lanes = slice(g * POOL_DIM, (g + 1) * POOL_DIM)
        outs.append(_dot(pooled[:, lanes].astype(BF16), pw_ref[g].astype(BF16)))
    return jnp.concatenate(outs, axis=1) + pb_ref[...]


def _in_proj(x, w, name, plan=None):
    t, d = x.shape
    n = w.shape[0]
    tm, tn = min(t, 1024), 1536
    nm, nn = t // tm, n // tn

    def body(x_ref, w_ref, o_ref, xb_ref):
        @pl.when(pl.program_id(1) == 0)
        def _():
            xb_ref[...] = x_ref[...].astype(BF16)
        o_ref[...] = _dot_nt(xb_ref[...], w_ref[...]).astype(BF16)

    def phase():
        step = pl.program_id(0) * nn + pl.program_id(1)
        return step == 0, step == (nm * nn) // 2, step == nm * nn - 1

    host = _Hosted(plan, n_in=2, n_out=1, n_scratch=1)
    return pl.pallas_call(
        host.wrap(body, phase), name=name, grid=(nm, nn),
        in_specs=[pl.BlockSpec((tm, d), lambda i, j: (i, 0)), pl.BlockSpec((tn, d), lambda i, j: (j, 0))]
        + host.in_specs,
        out_specs=[pl.BlockSpec((tm, tn), lambda i, j: (i, j))] + host.out_specs,
        out_shape=[jax.ShapeDtypeStruct((t, n), BF16)] + host.out_shape,
        scratch_shapes=[pltpu.VMEM((tm, d), BF16)] + host.scratch,
        compiler_params=_cparams(("arbitrary", "arbitrary")),
    )(x, w, *host.operands)


def _mixer_fwd(h, x, caw, cab, cbw, cbb, lbg, lbb, pw, pb, ps, w_out, lng, lnb, name, plan=None, target=None):
    t = h.shape[0]
    tt = TT
    n = t // tt
    with_loss = target is not None

    def body(h_ref, x_ref, *refs):
        if with_loss:
            t_ref, refs = refs[0], refs[1:]
        (caw_ref, cab_ref, cbw_ref, cbb_ref, lbg_ref, lbb_ref, pw_ref, pb_ref, ps_ref, wo_ref, lng_ref, lnb_ref,
         y_ref, u2_ref, z_ref, xn_ref) = refs[:16]
        refs = refs[16:]
        if with_loss:
            l_ref, refs = refs[0], refs[1:]
        exta, extb, extc, lasta, lastb, lastc, tmp, inv_ref = refs
        i = pl.program_id(0)

        @pl.when(i == 0)
        def _():
            for e in (lasta, lastb, lastc):
                e[...] = jnp.zeros_like(e)
            if with_loss:
                l_ref[...] = jnp.zeros_like(l_ref)

        @pl.when(i <= 1)
        def _():
            inv_ref[...] = _inv_count(i)

        a_bg, a_cg, a_v, a_z, b_v, b_g, b_z, c_u, c_z = _groups(h_ref)
        for ext, last, cur in ((exta, lasta, a_cg * a_v), (extb, lastb, b_v * _sigmoid(b_g)), (extc, lastc, c_u)):
            _look_back(ext, cur, last[...])
            last[...] = _last_segment(cur)

        _conv_rows(exta, caw_ref, tmp, nk=KA, off=SEG - (KA - 1), reverse=False)
        y_a = a_bg * (tmp[...] + cab_ref[...]) * (a_z * _sigmoid(a_z))
        y_ref[:, 0:W_MIX] = y_a.astype(BF16)

        _window_sums(extc, tmp, forward=False)
        pooled = tmp[...] * inv_ref[...] - c_u
        p = _pool_linear(pooled, pw_ref, pb_ref)
        y_c = p * ps_ref[...] * (c_z * _sigmoid(c_z))
        y_ref[:, 2 * W_MIX:3 * W_MIX] = y_c.astype(BF16)

        _conv_rows(extb, cbw_ref, tmp, nk=KB, off=SEG - (KB - 1), reverse=False)
        u2 = tmp[...] + cbb_ref[...]
        u2_ref[...] = u2
        ln, _, _ = _layer_norm(u2, lbg_ref[...], lbb_ref[...])
        y_b = (ln * _sigmoid(ln)) * (b_z * _sigmoid(b_z))
        y_ref[:, W_MIX:2 * W_MIX] = y_b.astype(BF16)

        out = _dot(y_ref[...], wo_ref[...])
        z = ALPHA * x_ref[...] + out
        z_ref[...] = z
        xn, _, _ = _layer_norm(z, lng_ref[...], lnb_ref[...])
        if with_loss:
            e = xn - t_ref[...]
            xn_ref[...] = e * (1.0 / D_MODEL)
            l_ref[...] += _fold8(e * e) * (0.5 / D_MODEL)
        else:
            xn_ref[...] = xn

    def phase():
        i = pl.program_id(0)
        return i == 0, i == n // 2, i == n - 1

    row = lambda wd: pl.BlockSpec((tt, wd), lambda i: (i, 0))
    full = lambda a: pl.BlockSpec(a.shape, lambda i: (0,) * a.ndim)
    params = (caw, cab, cbw, cbb, lbg, lbb, pw, pb, ps, w_out, lng, lnb)
    extra_in = [target] if with_loss else []
    extra_out = [jax.ShapeDtypeStruct((SUB, D_MODEL), F32)] if with_loss else []
    host = _Hosted(plan, n_in=2 + len(extra_in) + len(params), n_out=4 + len(extra_out), n_scratch=8)
    return pl.pallas_call(
        host.wrap(body, phase), name=name, grid=(n,),
        in_specs=[row(D_IN), row(D_MODEL)] + [row(D_MODEL)] * len(extra_in) + [full(a) for a in params]
        + host.in_specs,
        out_specs=[row(D_MIX), row(W_MIX), row(D_MODEL), row(D_MODEL)] + [full(o) for o in extra_out]
        + host.out_specs,
        out_shape=[jax.ShapeDtypeStruct((t, D_MIX), BF16), jax.ShapeDtypeStruct((t, W_MIX), F32),
                   jax.ShapeDtypeStruct((t, D_MODEL), F32), jax.ShapeDtypeStruct((t, D_MODEL), F32)]
        + extra_out + host.out_shape,
        scratch_shapes=[pltpu.VMEM((2 * tt, W_MIX), F32)] * 3 + [pltpu.VMEM((tt, W_MIX), F32)] * 5 + host.scratch,
        compiler_params=_cparams(("arbitrary",)),
    )(h, x, *extra_in, *params, *host.operands)


def _out_proj_bwd(g_xn, z, y, w_out_t, lng, name):
    t = z.shape[0]
    tt = min(t, 512)
    n = t // tt

    def body(g_ref, z_ref, y_ref, wo_ref, lng_ref, gz_ref, gy_ref, gwo_ref, gln_ref, accg, accb, accw, gzb):
        i = pl.program_id(0)

        @pl.when(i == 0)
        def _():
            accw[...] = jnp.zeros_like(accw)
            accg[...] = jnp.zeros_like(accg)
            accb[...] = jnp.zeros_like(accb)
            gzb[...] = jnp.zeros_like(gzb)

        before = gzb[(i + 1) % 2]
        gy_ref[...] = _dot(before, wo_ref[...])
        accw[...] += _dot_tn(y_ref[...], before)

        counts = (i < n).astype(F32)
        g = g_ref[...]
        _, zhat, rstd = _layer_norm(z_ref[...], lng_ref[...], 0.0)
        accg[...] += _fold8(g * zhat) * counts
        accb[...] += _fold8(g) * counts
        g_z = _layer_norm_bwd(g, zhat, rstd, lng_ref[...])
        gz_ref[...] = g_z
        gzb[i % 2] = g_z.astype(BF16)

        @pl.when(i == n)
        def _():
            gwo_ref[...] = accw[...].astype(BF16)
            gln_ref[...] = jnp.zeros_like(gln_ref)
            gln_ref[0:1, :] = jnp.sum(accg[...], axis=0, keepdims=True)
            gln_ref[1:2, :] = jnp.sum(accb[...], axis=0, keepdims=True)

    this = lambda wd: pl.BlockSpec((tt, wd), lambda i: (jnp.minimum(i, n - 1), 0))
    last = lambda wd: pl.BlockSpec((tt, wd), lambda i: (jnp.maximum(i - 1, 0), 0))
    full = lambda shape: pl.BlockSpec(shape, lambda i: (0,) * len(shape))
    return pl.pallas_call(
        body, name=name, grid=(n + 1,),
        in_specs=[this(D_MODEL), this(D_MODEL), last(D_MIX), full(w_out_t.shape), full(lng.shape)],
        out_specs=[this(D_MODEL), last(D_MIX), full((D_MIX, D_MODEL)), full((SUB, D_MODEL))],
        out_shape=[jax.ShapeDtypeStruct((t, D_MODEL), F32), jax.ShapeDtypeStruct((t, D_MIX), F32),
                   jax.ShapeDtypeStruct((D_MIX, D_MODEL), BF16), jax.ShapeDtypeStruct((SUB, D_MODEL), F32)],
        scratch_shapes=[pltpu.VMEM((SUB, D_MODEL), F32)] * 2 + [pltpu.VMEM((D_MIX, D_MODEL), F32),
                                                                 pltpu.VMEM((2, tt, D_MODEL), BF16)],
        compiler_params=_cparams(("arbitrary",)),
    )(g_xn, z, y, w_out_t, lng)


def _mixer_bwd(h, u2, g_y, caw, cab, cbw, lbg, lbb, pw, pb, ps, name, plan=None):
    t = h.shape[0]
    tt = TT
    n = t // tt
    before_groups = (1, 2, 4, 5, 7)

    def body(h_ref, p_cg, p_av, p_bv, p_bg, p_cu, u2_ref, gy_ref, caw_ref, cab_ref, cbw_ref, lbg_ref, lbb_ref,
             pw_ref, pb_ref, ps_ref, gh_ref, rows_ref, gpw_ref,
             exta, extb, extc, gca, gu2, qx, nexta, nextb, nextc, tmp, tmp2, inv_ref, acc):
        s = pl.program_id(0)
        i = n - 1 - s

        @pl.when(s == 0)
        def _():
            acc[...] = jnp.zeros_like(acc)
            gpw_ref[...] = jnp.zeros_like(gpw_ref)
            for e in (nexta, nextb, nextc):
                e[...] = jnp.zeros_like(e)

        live = (i > 0).astype(F32)
        f32 = lambda ref: ref[...].astype(F32)
        before_a = _last_segment(f32(p_cg) * f32(p_av)) * live
        before_b = _last_segment(f32(p_bv) * _sigmoid(f32(p_bg))) * live
        before_c = _last_segment(f32(p_cu)) * live

        a_bg, a_cg, a_v, a_z, b_v, b_g, b_z, c_u, c_z = _groups(h_ref)
        g_ya = gy_ref[:, 0:W_MIX]
        g_yb = gy_ref[:, W_MIX:2 * W_MIX]
        g_yc = gy_ref[:, 2 * W_MIX:3 * W_MIX]

        def add_row(r, v):
            acc[r * SUB:(r + 1) * SUB, :] += _fold8(v)

        _look_back(exta, a_cg * a_v, before_a)
        _conv_rows(exta, caw_ref, tmp, nk=KA, off=SEG - (KA - 1), reverse=False)
        ca = tmp[...] + cab_ref[...]
        sg = _sigmoid(a_z)
        s_az = a_z * sg
        t_a = g_ya * a_bg
        gh_ref[:, 0:W_MIX] = (g_ya * ca * s_az).astype(BF16)
        gh_ref[:, 3 * W_MIX:4 * W_MIX] = (t_a * ca * (sg * (1.0 + a_z * (1.0 - sg)))).astype(BF16)
        g_ca = t_a * s_az
        _look_ahead(gca, g_ca, nexta[...])
        nexta[...] = _first_segment(g_ca)
        add_row(R_CAB, g_ca)
        _conv_wgrad(gca, exta, acc, nk=KA, off=SEG - (KA - 1), row0=R_CAW)
        _conv_rows(gca, caw_ref, tmp, nk=KA, off=0, reverse=True)
        g_pa = tmp[...]
        gh_ref[:, W_MIX:2 * W_MIX] = (g_pa * a_v).astype(BF16)
        gh_ref[:, 2 * W_MIX:3 * W_MIX] = (g_pa * a_cg).astype(BF16)

        sgg = _sigmoid(b_g)
        _look_back(extb, b_v * sgg, before_b)
        ln, u2hat, rstd = _layer_norm(u2_ref[...], lbg_ref[...], lbb_ref[...])
        sl = _sigmoid(ln)
        u3 = ln * sl
        sz = _sigmoid(b_z)
        s_bz = b_z * sz
        gh_ref[:, 6 * W_MIX:7 * W_MIX] = (g_yb * u3 * (sz * (1.0 + b_z * (1.0 - sz)))).astype(BF16)
        g_ln = g_yb * s_bz * (sl * (1.0 + ln * (1.0 - sl)))
        add_row(R_LBG, g_ln * u2hat)
        add_row(R_LBB, g_ln)
        g_u2 = _layer_norm_bwd(g_ln, u2hat, rstd, lbg_ref[...])
        _look_ahead(gu2, g_u2, nextb[...])
        nextb[...] = _first_segment(g_u2)
        add_row(R_CBB, g_u2)
        _conv_wgrad(gu2, extb, acc, nk=KB, off=SEG - (KB - 1), row0=R_CBW)
        _conv_rows(gu2, cbw_ref, tmp, nk=KB, off=0, reverse=True)
        g_u1 = tmp[...]
        gh_ref[:, 4 * W_MIX:5 * W_MIX] = (g_u1 * sgg).astype(BF16)
        gh_ref[:, 5 * W_MIX:6 * W_MIX] = (g_u1 * b_v * sgg * (1.0 - sgg)).astype(BF16)

        _look_back(extc, c_u, before_c)
        _window_sums(extc, tmp, forward=False)
        @pl.when((s == 0) | (i == 0))
        def _():
            inv_ref[...] = _inv_count(i)
        inv = inv_ref[...]
        pooled = tmp[...] * inv - c_u
        p = _pool_linear(pooled, pw_ref, pb_ref)
        sc = _sigmoid(c_z)
        s_cz = c_z * sc
        scale = ps_ref[...]
        gh_ref[:, 8 * W_MIX:9 * W_MIX] = (g_yc * p * scale * (sc * (1.0 + c_z * (1.0 - sc)))).astype(BF16)
        t_c = g_yc * s_cz
        add_row(R_PS, t_c * p)
        g_p = t_c * scale
        add_row(R_PB, g_p)
        g_pooled = []
        for g in range(len(POOL_WINDOWS)):
            lanes = slice(g * POOL_DIM, (g + 1) * POOL_DIM)
            gpg = g_p[:, lanes].astype(BF16)
            gpw_ref[g] += _dot_tn(pooled[:, lanes].astype(BF16), gpg)
            g_pooled.append(_dot_nt(gpg, pw_ref[g].astype(BF16)))
        g_pooled = jnp.concatenate(g_pooled, axis=1)
        q = g_pooled * inv
        _look_ahead(qx, q, nextc[...])
        nextc[...] = _first_segment(q)
        _window_sums(qx, tmp2, forward=True)
        gh_ref[:, 7 * W_MIX:8 * W_MIX] = (tmp2[...] - g_pooled).astype(BF16)

        @pl.when(s == n - 1)
        def _():
            for r in range(N_ROWS):
                rows_ref[r:r + 1, :] = jnp.sum(acc[r * SUB:(r + 1) * SUB, :], axis=0, keepdims=True)

    def phase():
        s = pl.program_id(0)
        return s == 0, s == n // 2, s == n - 1

    row = lambda wd: pl.BlockSpec((tt, wd), lambda s: (n - 1 - s, 0))
    before = [pl.BlockSpec((tt, W_MIX), lambda s, k=k: (jnp.maximum(n - 2 - s, 0), k)) for k in before_groups]
    full = lambda shape: pl.BlockSpec(shape, lambda s: (0,) * len(shape))
    params = (caw, cab, cbw, lbg, lbb, pw, pb, ps)
    host = _Hosted(plan, n_in=3 + len(before) + len(params), n_out=3, n_scratch=13)
    return pl.pallas_call(
        host.wrap(body, phase), name=name, grid=(n,),
        in_specs=[row(D_IN)] + before + [row(W_MIX), row(D_MIX)] + [full(a.shape) for a in params] + host.in_specs,
        out_specs=[row(D_IN), full((N_ROWS, W_MIX)), full(pw.shape)] + host.out_specs,
        out_shape=[jax.ShapeDtypeStruct((t, D_IN), BF16), jax.ShapeDtypeStruct((N_ROWS, W_MIX), F32),
                   jax.ShapeDtypeStruct(pw.shape, F32)] + host.out_shape,
        scratch_shapes=[pltpu.VMEM((2 * tt, W_MIX), F32)] * 6 + [pltpu.VMEM((tt, W_MIX), F32)] * 6
        + [pltpu.VMEM((N_ROWS * SUB, W_MIX), F32)] + host.scratch,
        compiler_params=_cparams(("arbitrary",)),
    )(h, *([h] * len(before)), u2, g_y, *params, *host.operands)


def _in_proj_wgrad(x, g_h, name, plan=None):
    t, d = x.shape
    n = g_h.shape[1]
    tk, tn = min(t, 1024), n // 2
    nk = t // tk

    def body(x_ref, g_ref, o_ref, acc):
        k = pl.program_id(1)

        @pl.when(k == 0)
        def _():
            acc[...] = jnp.zeros_like(acc)
        acc[...] += _dot_tn(x_ref[...].astype(BF16), g_ref[...])

        @pl.when(k == nk - 1)
        def _():
            o_ref[...] = acc[...].T.astype(BF16)

    def phase():
        step = pl.program_id(0) * nk + pl.program_id(1)
        return step == 0, step == nk, step == 2 * nk - 1

    host = _Hosted(plan, n_in=2, n_out=1, n_scratch=1)
    return pl.pallas_call(
        host.wrap(body, phase), name=name, grid=(n // tn, nk),
        in_specs=[pl.BlockSpec((tk, d), lambda j, k: (k, 0)), pl.BlockSpec((tk, tn), lambda j, k: (k, j))]
        + host.in_specs,
        out_specs=[pl.BlockSpec((tn, d), lambda j, k: (j, 0))] + host.out_specs,
        out_shape=[jax.ShapeDtypeStruct((n, d), BF16)] + host.out_shape,
        scratch_shapes=[pltpu.VMEM((d, tn), F32)] + host.scratch,
        compiler_params=_cparams(("arbitrary", "arbitrary")),
    )(x, g_h, *host.operands)


def _in_proj_dgrad(g_h, w, g_z, name, plan=None):
    t, n = g_h.shape
    d = w.shape[1]
    tm, tk = min(t, 1024), 1536
    nm, nk = t // tm, n // tk

    def body(g_ref, w_ref, gz_ref, o_ref):
        @pl.when(pl.program_id(1) == 0)
        def _():
            o_ref[...] = ALPHA * gz_ref[...]
        o_ref[...] += _dot(g_ref[...], w_ref[...])

    def phase():
        step = pl.program_id(0) * nk + pl.program_id(1)
        return step == 0, step == (nm * nk) // 2, step == nm * nk - 1

    host = _Hosted(plan, n_in=3, n_out=1, n_scratch=0)
    return pl.pallas_call(
        host.wrap(body, phase), name=name, grid=(nm, nk),
        in_specs=[pl.BlockSpec((tm, tk), lambda i, k: (i, k)), pl.BlockSpec((tk, d), lambda i, k: (k, 0)),
                  pl.BlockSpec((tm, d), lambda i, k: (i, 0))] + host.in_specs,
        out_specs=[pl.BlockSpec((tm, d), lambda i, k: (i, 0))] + host.out_specs,
        out_shape=[jax.ShapeDtypeStruct((t, d), F32)] + host.out_shape,
        scratch_shapes=host.scratch,
        compiler_params=_cparams(("arbitrary", "arbitrary")),
    )(g_h, w, g_z, *host.operands)


BC1 = 1.0 - ADAM_B1 ** ADAM_STEP
BC2 = 1.0 - ADAM_B2 ** ADAM_STEP


def _adamw_math(g, w, m, v):
    nm = ADAM_B1 * m + (1.0 - ADAM_B1) * g
    nv = ADAM_B2 * v + (1.0 - ADAM_B2) * (g * g)
    delta = -ADAM_LR * ((nm / BC1) / (jnp.sqrt(nv / BC2) + ADAM_EPS) + ADAM_WD * w)
    return delta, nm, nv


def _total(ref):
    g = ref[0].astype(F32)
    for k in range(1, ref.shape[0]):
        g = g + ref[k].astype(F32)
    return g


def _adamw_layers(parts, w, m, v, name):
    depth, r, c = w.shape
    p = parts[0].shape[0]
    tr = _row_tile(r)
    nr = r // tr

    def body(*refs):
        p_refs = refs[:depth]
        w_ref, m_ref, v_ref, g_ref, d_ref, nm_ref, nv_ref = refs[depth:]
        for l in range(depth):
            @pl.when(pl.program_id(0) == l)
            def _(l=l):
                g = _total(p_refs[l])
                delta, nm, nv = _adamw_math(g, w_ref[0], m_ref[0], v_ref[0])
                g_ref[0], d_ref[0], nm_ref[0], nv_ref[0] = g, delta, nm, nv

    def part_spec(l):
        return pl.BlockSpec((p, tr, c), lambda li, i: (0, jnp.where(li == l, i, jnp.where(li < l, 0, nr - 1)), 0))

    blk = pl.BlockSpec((1, tr, c), lambda li, i: (li, i, 0))
    out = jax.ShapeDtypeStruct((depth, r, c), F32)
    return pl.pallas_call(
        body, name=name, grid=(depth, nr),
        in_specs=[part_spec(l) for l in range(depth)] + [blk] * 3,
        out_specs=[blk] * 4, out_shape=[out] * 4,
        compiler_params=_cparams(("arbitrary", "arbitrary")),
    )(*parts, w, m, v)


SMALL_PARAMS = ("conv_a_w", "conv_a_b", "conv_b_w", "conv_b_b", "ln_b_g", "ln_b_b", "pool_w", "pool_b", "pool_scale",
                "ln_g", "ln_b")


def _adamw_small(rows_parts, gln_parts, gpw_parts, loss_parts, params, name):
    depth = len(rows_parts)
    cs = params["conv_a_w"][0].shape[2]
    operands = [*rows_parts, *gln_parts, *gpw_parts, loss_parts] + [a for n in SMALL_PARAMS for a in params[n]]
    n_in = len(operands)
    out_shape = [jax.ShapeDtypeStruct((1, 1), F32)]
    out_shape += [jax.ShapeDtypeStruct(params[n][0].shape, F32) for n in SMALL_PARAMS for _ in range(4)]

    def body(*refs):
        rows_p, gln_p, gpw_p = refs[:depth], refs[depth:2 * depth], refs[2 * depth:3 * depth]
        loss_p = refs[3 * depth]
        prm, outs = refs[3 * depth + 1:n_in], refs[n_in + 1:]
        refs[n_in][...] = jnp.sum(_total(loss_p)).reshape(1, 1)
        x, y, c = _position()
        to_front = (W_MIX - (4 * x + 2 * y + c) * cs) % W_MIX

        def update(name, g, at):
            k = SMALL_PARAMS.index(name)
            w_ref, m_ref, v_ref = prm[3 * k:3 * k + 3]
            g_ref, d_ref, nm_ref, nv_ref = outs[4 * k:4 * k + 4]
            delta, nm, nv = _adamw_math(g, w_ref[at], m_ref[at], v_ref[at])
            g_ref[at], d_ref[at], nm_ref[at], nv_ref[at] = g, delta, nm, nv

        for l in range(depth):
            rows = _total(rows_p[l])
            mine = pltpu.roll(rows, to_front, axis=1)
            gln = _total(gln_p[l])
            one = (slice(l, l + 1), slice(None))
            for name, r in (("conv_a_b", R_CAB), ("conv_b_b", R_CBB), ("ln_b_g", R_LBG), ("ln_b_b", R_LBB),
                            ("pool_scale", R_PS)):
                update(name, rows[r:r + 1, :], one)
            for g in range(len(POOL_WINDOWS)):
                update("pool_b", rows[R_PB:R_PB + 1, g * POOL_DIM:(g + 1) * POOL_DIM], (l, slice(g, g + 1), slice(None)))
            update("ln_g", gln[0:1, :], one)
            update("ln_b", gln[1:2, :], one)
            update("conv_a_w", mine[R_CAW:R_CAW + KA, 0:cs], (l,))
            update("conv_b_w", mine[R_CBW:R_CBW + KB, 0:cs], (l,))
            update("pool_w", _total(gpw_p[l]), (l,))

    vmem = pl.BlockSpec(memory_space=pltpu.VMEM)
    return pl.pallas_call(
        body, name=name, in_specs=[vmem] * n_in, out_specs=[vmem] * len(out_shape), out_shape=out_shape,
        compiler_params=pltpu.CompilerParams(vmem_limit_bytes=VMEM_LIMIT),
    )(*operands)


def kernel(x, w_in, conv_a_w, conv_a_b, conv_b_w, conv_b_b, ln_b_g, ln_b_b, pool_w, pool_b, pool_scale, w_out, ln_g, ln_b, loss_target, m_w_in, m_conv_a_w, m_conv_a_b, m_conv_b_w, m_conv_b_b, m_ln_b_g, m_ln_b_b, m_pool_w, m_pool_b, m_pool_scale, m_w_out, m_ln_g, m_ln_b, v_w_in, v_conv_a_w, v_conv_a_b, v_conv_b_w, v_conv_b_b, v_ln_b_g, v_ln_b_b, v_pool_w, v_pool_b, v_pool_scale, v_w_out, v_ln_g, v_ln_b):
    depth = w_in.shape[0]
    x0 = _to_segments(x[0])
    target = _to_segments(loss_target[0])
    r2 = lambda a: a.reshape(1, -1)

    tr = lambda a: jnp.swapaxes(a, 1, 2)
    w_in_t, m_w_in_t, v_w_in_t = tr(w_in), tr(m_w_in), tr(v_w_in)
    w_in_b, w_out_b = w_in_t.astype(BF16), w_out.astype(BF16)
    conv_sh = jnp.concatenate([conv_a_w, conv_b_w], axis=1)
    full_in = lambda g: g.reshape(D_IN, D_MODEL)
    full_out = lambda g: g.reshape(D_MIX, D_MODEL)
    w_in_f = [full_in(_run_plan(_Gather([w_in_b[0]]), "gather_w_in_0")[0])]
    w_out_f = []

    xs, hs, ys, u2s, zs = [x0], [], [], [], []
    for l in range(depth):
        h, *got = _in_proj(xs[l], w_in_f[l], f"in_proj_{l}",
                           plan=_Gather([w_out_b[0], conv_sh]) if l == 0 else None)
        if got:
            w_out_f.append(full_out(got[0]))
            conv_f = got[1].transpose(1, 2, 0, 3).reshape(depth, KA + KB, W_MIX)
            caw_f, cbw_f = conv_f[:, :KA], conv_f[:, KA:]
        last = l + 1 == depth
        y, u2, z, xn, *got = _mixer_fwd(
            h, xs[l], caw_f[l], r2(conv_a_b[l]), cbw_f[l], r2(conv_b_b[l]), r2(ln_b_g[l]), r2(ln_b_b[l]),
            pool_w[l], r2(pool_b[l]), r2(pool_scale[l]), w_out_f[l], r2(ln_g[l]), r2(ln_b[l]), f"mixer_fwd_{l}",
            plan=None if last else _Gather([w_in_b[l + 1], w_out_b[l + 1]]), target=target if last else None)
        if last:
            g, loss_rows = xn, got[0]
        else:
            w_in_f.append(full_in(got[0])), w_out_f.append(full_out(got[1]))
        hs.append(h), ys.append(y), u2s.append(u2), zs.append(z), xs.append(xn)


    gi_parts, go_parts, g_rows, g_lns, g_pool_w = ([None] * depth for _ in range(5))
    waiting = []
    for l in reversed(range(depth)):
        g_z, g_y, g_w_out, g_lns[l] = _out_proj_bwd(g, zs[l], ys[l], w_out_f[l].T, r2(ln_g[l]), f"out_proj_bwd_{l}")
        go = g_w_out.reshape(N_DEV, D_MIX // N_DEV, D_MODEL)
        if l == 0:
            waiting += [go]
        g_h, g_rows[l], g_pool_w[l], *got = _mixer_bwd(
            hs[l], u2s[l], g_y, caw_f[l], r2(conv_a_b[l]), cbw_f[l], r2(ln_b_g[l]), r2(ln_b_b[l]),
            pool_w[l], r2(pool_b[l]), r2(pool_scale[l]), f"mixer_bwd_{l}",
            plan=_Exchange(waiting) if waiting else None)
        if got:
            if l + 1 < depth:
                gi_parts[l + 1], go_parts[l + 1] = got[0], got[1]
            if l == 0:
                go_parts[0] = got[-1]
        if l > 0:
            gi = _in_proj_wgrad(xs[l], g_h, f"in_proj_wgrad_{l}")[0]
            waiting = [gi.reshape(N_DEV, D_IN // N_DEV, D_MODEL), go]
            g = _in_proj_dgrad(g_h, w_in_f[l], g_z, f"in_proj_dgrad_{l}")[0]
        else:
            small = [*g_rows, *g_lns, *g_pool_w, loss_rows]
            gi, *small_parts = _in_proj_wgrad(xs[0], g_h, "in_proj_wgrad_0", plan=_Gather(small))
            g, gi_parts[0] = _in_proj_dgrad(g_h, w_in_f[0], g_z, "in_proj_dgrad_0",
                                            plan=_Exchange([gi.reshape(N_DEV, D_IN // N_DEV, D_MODEL)]))
    grad_x = _from_segments(g)[None]

    out_in = [tr(a) for a in _adamw_layers(gi_parts, w_in_t, m_w_in_t, v_w_in_t, "adamw_w_in")]
    out_out = _adamw_layers(go_parts, w_out, m_w_out, v_w_out, "adamw_w_out")
    params = dict(
        conv_a_w=(conv_a_w, m_conv_a_w, v_conv_a_w), conv_a_b=(conv_a_b, m_conv_a_b, v_conv_a_b),
        conv_b_w=(conv_b_w, m_conv_b_w, v_conv_b_w), conv_b_b=(conv_b_b, m_conv_b_b, v_conv_b_b),
        ln_b_g=(ln_b_g, m_ln_b_g, v_ln_b_g), ln_b_b=(ln_b_b, m_ln_b_b, v_ln_b_b),
        pool_w=(pool_w, m_pool_w, v_pool_w), pool_b=(pool_b, m_pool_b, v_pool_b),
        pool_scale=(pool_scale, m_pool_scale, v_pool_scale), ln_g=(ln_g, m_ln_g, v_ln_g), ln_b=(ln_b, m_ln_b, v_ln_b))
    loss, *small_out = _adamw_small(small_parts[:depth], small_parts[depth:2 * depth], small_parts[2 * depth:3 * depth],
                                    small_parts[3 * depth], params, "adamw_small")
    small_out = {n: small_out[4 * k:4 * k + 4] for k, n in enumerate(SMALL_PARAMS)}

    order = ("w_in", "conv_a_w", "conv_a_b", "conv_b_w", "conv_b_b", "ln_b_g", "ln_b_b", "pool_w", "pool_b",
             "pool_scale", "w_out", "ln_g", "ln_b")
    outs = []
    for k in range(4):
        outs += [out_in[k] if n == "w_in" else out_out[k] if n == "w_out" else small_out[n][k] for n in order]
    return (loss.reshape(()), grad_x, *outs)
```

```python
import jax
import jax.numpy as jnp
from jax import lax
from jax.experimental import pallas as pl
from jax.experimental.pallas import tpu as pltpu

F32 = jnp.float32
BF16 = jnp.bfloat16

DEPTH = 2
D_MODEL = 1024
W_MIX = 512
D_IN = 9 * W_MIX
D_MIX = 3 * W_MIX
POOL_WINDOWS = (2, 4, 8, 16)
POOL_DIM = 128
KA = 3
KB = 31
ALPHA = (2.0 * DEPTH) ** 0.25
LN_EPS = 1e-5
ADAM_LR, ADAM_B1, ADAM_B2, ADAM_EPS, ADAM_WD, ADAM_STEP = 0.001, 0.9, 0.999, 1e-08, 0.01, 10

N_DEV = 8
N_CHIP = 4
MESH = pl.DeviceIdType.MESH

CH = 32
SUB = 8
VMEM_LIMIT = 56 * 1024 * 1024

R_CAB, R_CBB, R_LBG, R_LBB, R_PB, R_PS, R_CAW, R_CBW = 0, 1, 2, 3, 4, 5, 6, 9
N_ROWS = R_CBW + KB
LN_ROWS = 16
SMALL_ROWS = N_ROWS + LN_ROWS


def _cparams(sem, **kw):
    return pltpu.CompilerParams(dimension_semantics=sem, vmem_limit_bytes=VMEM_LIMIT, **kw)


def _sigmoid(v):
    return 1.0 / (1.0 + jnp.exp(-v))


def _fold8(a):
    r, c = a.shape
    return a.reshape(r // SUB, SUB, c).sum(axis=0)


def _dot(a, b):
    return jnp.dot(a, b, preferred_element_type=F32)


def _dot_nt(a, b):
    return lax.dot_general(a, b, (((1,), (1,)), ((), ())), preferred_element_type=F32)


def _dot_tn(a, b):
    return lax.dot_general(a, b, (((0,), (0,)), ((), ())), preferred_element_type=F32)


def _row_tile(r):
    for cand in (512, 256, 128, 64, 32, 16, 8):
        if r % cand == 0 and r > cand:
            return cand
    return r


def _position():
    return lax.axis_index("x"), lax.axis_index("y"), lax.axis_index("c")


class _Gather:
    def __init__(self, arrs):
        self.arrs = list(arrs)
        na = len(self.arrs)
        self.out_shape = [jax.ShapeDtypeStruct((N_DEV,) + a.shape, a.dtype) for a in self.arrs]
        self.scratch = [pltpu.SemaphoreType.DMA((na, 7)), pltpu.SemaphoreType.DMA((na, 7)),
                        pltpu.SemaphoreType.DMA((na,))]

    def _copies(self, src, dst, sems):
        send_sems, recv_sems, local_sems = sems
        na = len(self.arrs)
        x, y, c = _position()
        me, sibling = (x, y, c), (x, y, 1 - c)
        chips = [(1 - x, y), (x, 1 - y), (1 - x, 1 - y)]

        def slot(a, dev):
            return dst[a].at[4 * dev[0] + 2 * dev[1] + dev[2]]

        def copy(a, k, block, to, from_src=False):
            return pltpu.make_async_remote_copy(
                src_ref=src[a] if from_src else slot(a, block), dst_ref=slot(a, block),
                send_sem=send_sems.at[a, k], recv_sem=recv_sems.at[a, k], device_id=to, device_id_type=MESH)

        mine = [pltpu.make_async_copy(src[a], slot(a, me), local_sems.at[a]) for a in range(na)]
        first, landed, passed, last = [], [], [], []
        for a in range(na):
            first.append(copy(a, 0, me, sibling, from_src=True))
            first += [copy(a, 1 + j, me, (*chip, c), from_src=True) for j, chip in enumerate(chips)]
        for j, chip in enumerate(chips):
            for a in range(na):
                landed.append(copy(a, 1 + j, (*chip, c), me))
                passed.append(copy(a, 4 + j, (*chip, c), sibling))
        for a in range(na):
            last.append(copy(a, 0, sibling, me))
            last += [copy(a, 4 + j, (*chip, 1 - c), me) for j, chip in enumerate(chips)]
        return mine, first, landed, passed, last

    def start(self, src, dst, sems):
        mine, first, _, _, _ = self._copies(src, dst, sems)
        for cp in mine + first:
            cp.start()

    def pass_on(self, src, dst, sems):
        _, _, landed, passed, _ = self._copies(src, dst, sems)
        for got, cp in zip(landed, passed):
            got.wait_recv()
            cp.start()

    def finish(self, src, dst, sems):
        mine, first, _, passed, last = self._copies(src, dst, sems)
        for cp in last:
            cp.wait_recv()
        for cp in first + passed:
            cp.wait_send()
        for cp in mine:
            cp.wait()


class _Exchange:
    def __init__(self, arrs):
        self.arrs = list(arrs)
        na = len(self.arrs)
        self.out_shape = [jax.ShapeDtypeStruct(a.shape, a.dtype) for a in self.arrs]
        self.scratch = [pltpu.SemaphoreType.DMA((na, N_DEV - 1)), pltpu.SemaphoreType.DMA((na, N_DEV - 1)),
                        pltpu.SemaphoreType.DMA((na,))]

    def _copies(self, src, dst, sems):
        send_sems, recv_sems, local_sems = sems
        na = len(self.arrs)
        x, y, c = _position()
        me = 4 * x + 2 * y + c
        mine = [pltpu.make_async_copy(src[a].at[me], dst[a].at[me], local_sems.at[a]) for a in range(na)]
        copies = []
        for a in range(na):
            for k in range(N_DEV - 1):
                flip = k + 1
                px, py, pc = x ^ (flip >> 2), y ^ ((flip >> 1) & 1), c ^ (flip & 1)
                copies.append(pltpu.make_async_remote_copy(
                    src_ref=src[a].at[4 * px + 2 * py + pc], dst_ref=dst[a].at[me],
                    send_sem=send_sems.at[a, k], recv_sem=recv_sems.at[a, k], device_id=(px, py, pc),
                    device_id_type=MESH))
        return mine, copies

    def start(self, src, dst, sems):
        mine, copies = self._copies(src, dst, sems)
        for cp in mine + copies:
            cp.start()

    def pass_on(self, src, dst, sems):
        pass

    def finish(self, src, dst, sems):
        mine, copies = self._copies(src, dst, sems)
        for cp in copies:
            cp.wait()
        for cp in mine:
            cp.wait()


def _run_plan(plan, name):
    na = len(plan.arrs)
    any_spec = pl.BlockSpec(memory_space=pl.ANY)

    def body(*refs):
        src, dst, sems = refs[:na], refs[na:2 * na], refs[2 * na:]
        plan.start(src, dst, sems)
        plan.pass_on(src, dst, sems)
        plan.finish(src, dst, sems)

    return pl.pallas_call(
        body, name=name, in_specs=[any_spec] * na, out_specs=[any_spec] * na,
        out_shape=plan.out_shape, scratch_shapes=plan.scratch,
    )(*plan.arrs)


class _Hosted:
    def __init__(self, plan, n_in, n_out, n_scratch):
        self.plan, self.n_in, self.n_out, self.n_scratch = plan, n_in, n_out, n_scratch
        any_spec = pl.BlockSpec(memory_space=pl.ANY)
        k = 0 if plan is None else len(plan.arrs)
        self.operands = [] if plan is None else plan.arrs
        self.in_specs = [any_spec] * k
        self.out_specs = [any_spec] * k
        self.out_shape = [] if plan is None else plan.out_shape
        self.scratch = [] if plan is None else plan.scratch

    def wrap(self, body, phase):
        if self.plan is None:
            return body
        plan, k = self.plan, len(self.plan.arrs)
        i0, o0 = self.n_in, self.n_in + k
        o1 = o0 + self.n_out
        s0 = o1 + k
        s1 = s0 + self.n_scratch

        def hosted(*refs):
            src, dst, sems = refs[i0:o0], refs[o1:s0], refs[s1:]
            first, middle, last = phase()
            pl.when(first)(lambda: plan.start(src, dst, sems))
            body(*refs[:i0], *refs[o0:o1], *refs[s0:s1])
            pl.when(middle)(lambda: plan.pass_on(src, dst, sems))
            pl.when(last)(lambda: plan.finish(src, dst, sems))

        return hosted


TT = 256
SEG = TT // SUB


def _to_segments(a):
    t, c = a.shape
    return a.reshape(t // TT, SUB, SEG, c).transpose(0, 2, 1, 3).reshape(t, c)


def _from_segments(a):
    t, c = a.shape
    return a.reshape(t // TT, SEG, SUB, c).transpose(0, 2, 1, 3).reshape(t, c)


def _sublane_is(s):
    return lax.broadcasted_iota(jnp.int32, (TT, W_MIX), 0) % SUB == s


def _look_back(ext_ref, cur, before):
    ext_ref[TT:, :] = cur
    ext_ref[0:TT, :] = jnp.where(_sublane_is(0), before, pltpu.roll(cur, 1, axis=0))


def _last_segment(cur):
    return pltpu.roll(cur, TT - (SUB - 1), axis=0)


def _look_ahead(ext_ref, cur, after):
    ext_ref[0:TT, :] = cur
    ext_ref[TT:, :] = jnp.where(_sublane_is(SUB - 1), after, pltpu.roll(cur, TT - 1, axis=0))


def _first_segment(cur):
    return pltpu.roll(cur, SUB - 1, axis=0)


def _tap_loop(body):
    lax.fori_loop(0, TT // CH, lambda c, carry: body(pl.multiple_of(c * CH, CH), carry), 0)


def _conv_rows(ext_ref, w_ref, out_ref, *, nk, off, reverse):
    def body(r0, carry):
        acc = jnp.zeros((CH, W_MIX), F32)
        for k in range(nk):
            kk = nk - 1 - k if reverse else k
            acc = acc + ext_ref[pl.ds(r0 + (off + k) * SUB, CH), :] * w_ref[kk:kk + 1, :]
        out_ref[pl.ds(r0, CH), :] = acc
        return carry
    _tap_loop(body)


def _conv_wgrad(g_ref, ext_ref, acc_ref, *, nk, off, row0):
    def body(r0, carry):
        g = g_ref[pl.ds(r0, CH), :]
        for k in range(nk):
            a = (row0 + k) * SUB
            acc_ref[a:a + SUB, :] += _fold8(g * ext_ref[pl.ds(r0 + (off + k) * SUB, CH), :])
        return carry
    _tap_loop(body)


def _window_sums(ext_ref, out_ref, *, forward):
    def body(r0, carry):
        for g, w in enumerate(POOL_WINDOWS):
            lanes = slice(g * POOL_DIM, (g + 1) * POOL_DIM)
            acc = jnp.zeros((CH, POOL_DIM), F32)
            for j in range(w):
                off = j if forward else SEG - j
                acc = acc + ext_ref[pl.ds(r0 + off * SUB, CH), lanes]
            out_ref[pl.ds(r0, CH), lanes] = acc
        return carry
    _tap_loop(body)


def _inv_count(tile):
    r = lax.broadcasted_iota(jnp.int32, (TT, POOL_DIM), 0)
    t1 = (tile * TT + (r % SUB) * SEG + r // SUB + 1).astype(F32)
    return jnp.concatenate([1.0 / jnp.minimum(t1, float(w)) for w in POOL_WINDOWS], axis=1)


KEPT = ("u2", "sigmoid(a_z)", "ca", "sigmoid(b_g)", "sigmoid(ln)", "sigmoid(b_z)", "sigmoid(c_z)", "pooled", "p")


def _groups(h_ref):
    return [h_ref[:, k * W_MIX:(k + 1) * W_MIX].astype(F32) for k in range(9)]


def _layer_norm(v, g, b):
    mu = jnp.mean(v, axis=-1, keepdims=True)
    vc = v - mu
    var = jnp.mean(vc * vc, axis=-1, keepdims=True)
    rstd = lax.rsqrt(var + LN_EPS)
    vhat = vc * rstd
    return vhat * g + b, vhat, rstd


def _layer_norm_bwd(g_out, vhat, rstd, g):
    gh = g_out * g
    m1 = jnp.mean(gh, axis=-1, keepdims=True)
    m2 = jnp.mean(gh * vhat, axis=-1, keepdims=True)
    return rstd * (gh - m1 - vhat * m2)


def _pool_linear(pooled, pw_ref, pb_ref):
    outs = []
    for g in range(len(POOL_WINDOWS)):
        lanes = slice(g * POOL_DIM, (g + 1) * POOL_DIM)
        outs.append(_dot(pooled[:, lanes].astype(BF16), pw_ref[g].astype(BF16)))
    return jnp.concatenate(outs, axis=1) + pb_ref[...]


def _in_proj(x, w, name, plan=None):
    t, d = x.shape
    n = w.shape[0]
    tm, tn = min(t, 1024), 1536
    nm, nn = t // tm, n // tn

    def body(x_ref, w_ref, o_ref, xb_ref):
        @pl.when(pl.program_id(1) == 0)
        def _():
            xb_ref[...] = x_ref[...].astype(BF16)
        o_ref[...] = _dot_nt(xb_ref[...], w_ref[...]).astype(BF16)

    def phase():
        step = pl.program_id(0) * nn + pl.program_id(1)
        return step == 0, step == (nm * nn) // 2, step == nm * nn - 1

    host = _Hosted(plan, n_in=2, n_out=1, n_scratch=1)
    return pl.pallas_call(
        host.wrap(body, phase), name=name, grid=(nm, nn),
        in_specs=[pl.BlockSpec((tm, d), lambda i, j: (i, 0)), pl.BlockSpec((tn, d), lambda i, j: (j, 0))]
        + host.in_specs,
        out_specs=[pl.BlockSpec((tm, tn), lambda i, j: (i, j))] + host.out_specs,
        out_shape=[jax.ShapeDtypeStruct((t, n), BF16)] + host.out_shape,
        scratch_shapes=[pltpu.VMEM((tm, d), BF16)] + host.scratch,
        compiler_params=_cparams(("arbitrary", "arbitrary")),
    )(x, w, *host.operands)


def _mixer_fwd(h, x, caw, cab, cbw, cbb, lbg, lbb, pw, pb, ps, w_out, lng, lnb, name, plan=None, target=None):
    t = h.shape[0]
    tt = TT
    n = t // tt
    with_loss = target is not None

    def body(h_ref, x_ref, *refs):
        if with_loss:
            t_ref, refs = refs[0], refs[1:]
        (caw_ref, cab_ref, cbw_ref, cbb_ref, lbg_ref, lbb_ref, pw_ref, pb_ref, ps_ref, wo_ref, lng_ref, lnb_ref,
         y_ref, u2_ref, z_ref, xn_ref) = refs[:16]
        refs = refs[16:]
        if with_loss:
            l_ref, refs = refs[0], refs[1:]
        exta, extb, extc, lasta, lastb, lastc, tmp, inv_ref = refs
        i = pl.program_id(0)

        @pl.when(i == 0)
        def _():
            for e in (lasta, lastb, lastc):
                e[...] = jnp.zeros_like(e)
            if with_loss:
                l_ref[...] = jnp.zeros_like(l_ref)

        @pl.when(i <= 1)
        def _():
            inv_ref[...] = _inv_count(i)

        a_bg, a_cg, a_v, a_z, b_v, b_g, b_z, c_u, c_z = _groups(h_ref)
        sgg = _sigmoid(b_g)
        for ext, last, cur in ((exta, lasta, a_cg * a_v), (extb, lastb, b_v * sgg), (extc, lastc, c_u)):
            _look_back(ext, cur, last[...])
            last[...] = _last_segment(cur)

        _conv_rows(exta, caw_ref, tmp, nk=KA, off=SEG - (KA - 1), reverse=False)
        ca = tmp[...] + cab_ref[...]
        sga = _sigmoid(a_z)
        y_ref[:, 0:W_MIX] = (a_bg * ca * (a_z * sga)).astype(BF16)

        _window_sums(extc, tmp, forward=False)
        pooled = tmp[...] * inv_ref[...] - c_u
        p = _pool_linear(pooled, pw_ref, pb_ref)
        sc = _sigmoid(c_z)
        y_ref[:, 2 * W_MIX:3 * W_MIX] = (p * ps_ref[...] * (c_z * sc)).astype(BF16)

        _conv_rows(extb, cbw_ref, tmp, nk=KB, off=SEG - (KB - 1), reverse=False)
        u2 = tmp[...] + cbb_ref[...]
        ln, _, _ = _layer_norm(u2, lbg_ref[...], lbb_ref[...])
        sl, sz = _sigmoid(ln), _sigmoid(b_z)
        y_ref[:, W_MIX:2 * W_MIX] = ((ln * sl) * (b_z * sz)).astype(BF16)

        for k, kept in enumerate((u2, sga, ca, sgg, sl, sz, sc, pooled, p)):
            u2_ref[:, k * W_MIX:(k + 1) * W_MIX] = kept

        out = _dot(y_ref[...], wo_ref[...])
        z = ALPHA * x_ref[...] + out
        z_ref[...] = z
        xn, _, _ = _layer_norm(z, lng_ref[...], lnb_ref[...])
        if with_loss:
            e = xn - t_ref[...]
            xn_ref[...] = e * (1.0 / D_MODEL)
            l_ref[...] += _fold8(e * e) * (0.5 / D_MODEL)
        else:
            xn_ref[...] = xn

    def phase():
        i = pl.program_id(0)
        return i == 0, i == n // 2, i == n - 1

    row = lambda wd: pl.BlockSpec((tt, wd), lambda i: (i, 0))
    full = lambda a: pl.BlockSpec(a.shape, lambda i: (0,) * a.ndim)
    params = (caw, cab, cbw, cbb, lbg, lbb, pw, pb, ps, w_out, lng, lnb)
    extra_in = [target] if with_loss else []
    extra_out = [jax.ShapeDtypeStruct((SUB, D_MODEL), F32)] if with_loss else []
    host = _Hosted(plan, n_in=2 + len(extra_in) + len(params), n_out=4 + len(extra_out), n_scratch=8)
    return pl.pallas_call(
        host.wrap(body, phase), name=name, grid=(n,),
        in_specs=[row(D_IN), row(D_MODEL)] + [row(D_MODEL)] * len(extra_in) + [full(a) for a in params]
        + host.in_specs,
        out_specs=[row(D_MIX), row(len(KEPT) * W_MIX), row(D_MODEL), row(D_MODEL)] + [full(o) for o in extra_out]
        + host.out_specs,
        out_shape=[jax.ShapeDtypeStruct((t, D_MIX), BF16), jax.ShapeDtypeStruct((t, len(KEPT) * W_MIX), F32),
                   jax.ShapeDtypeStruct((t, D_MODEL), F32), jax.ShapeDtypeStruct((t, D_MODEL), F32)]
        + extra_out + host.out_shape,
        scratch_shapes=[pltpu.VMEM((2 * tt, W_MIX), F32)] * 3 + [pltpu.VMEM((tt, W_MIX), F32)] * 5 + host.scratch,
        compiler_params=_cparams(("arbitrary",)),
    )(h, x, *extra_in, *params, *host.operands)


def _out_proj_bwd(g_xn, z, y, w_out_t, lng, name):
    t = z.shape[0]
    tt = min(t, 512)
    n = t // tt

    def body(g_ref, z_ref, y_ref, wo_ref, lng_ref, gz_ref, gy_ref, gwo_ref, gln_ref, accg, accb, accw, gzb):
        i = pl.program_id(0)

        @pl.when(i == 0)
        def _():
            accw[...] = jnp.zeros_like(accw)
            accg[...] = jnp.zeros_like(accg)
            accb[...] = jnp.zeros_like(accb)
            gzb[...] = jnp.zeros_like(gzb)

        before = gzb[(i + 1) % 2]
        gy_ref[...] = _dot(before, wo_ref[...])
        accw[...] += _dot_tn(y_ref[...], before)

        counts = (i < n).astype(F32)
        g = g_ref[...]
        _, zhat, rstd = _layer_norm(z_ref[...], lng_ref[...], 0.0)
        accg[...] += _fold8(g * zhat) * counts
        accb[...] += _fold8(g) * counts
        g_z = _layer_norm_bwd(g, zhat, rstd, lng_ref[...])
        gz_ref[...] = g_z
        gzb[i % 2] = g_z.astype(BF16)

        @pl.when(i == n)
        def _():
            gwo_ref[...] = accw[...].astype(BF16)
            gln_ref[...] = jnp.zeros_like(gln_ref)
            gln_ref[0:1, :] = jnp.sum(accg[...], axis=0, keepdims=True)
            gln_ref[1:2, :] = jnp.sum(accb[...], axis=0, keepdims=True)

    this = lambda wd: pl.BlockSpec((tt, wd), lambda i: (jnp.minimum(i, n - 1), 0))
    last = lambda wd: pl.BlockSpec((tt, wd), lambda i: (jnp.maximum(i - 1, 0), 0))
    full = lambda shape: pl.BlockSpec(shape, lambda i: (0,) * len(shape))
    return pl.pallas_call(
        body, name=name, grid=(n + 1,),
        in_specs=[this(D_MODEL), this(D_MODEL), last(D_MIX), full(w_out_t.shape), full(lng.shape)],
        out_specs=[this(D_MODEL), last(D_MIX), full((D_MIX, D_MODEL)), full((SUB, D_MODEL))],
        out_shape=[jax.ShapeDtypeStruct((t, D_MODEL), F32), jax.ShapeDtypeStruct((t, D_MIX), F32),
                   jax.ShapeDtypeStruct((D_MIX, D_MODEL), BF16), jax.ShapeDtypeStruct((SUB, D_MODEL), F32)],
        scratch_shapes=[pltpu.VMEM((SUB, D_MODEL), F32)] * 2 + [pltpu.VMEM((D_MIX, D_MODEL), F32),
                                                                 pltpu.VMEM((2, tt, D_MODEL), BF16)],
        compiler_params=_cparams(("arbitrary",)),
    )(g_xn, z, y, w_out_t, lng)


def _mixer_bwd(h, u2, g_y, caw, cab, cbw, lbg, lbb, pw, pb, ps, name, plan=None):
    t = h.shape[0]
    tt = TT
    n = t // tt
    before_groups = (1, 2, 4, 5)

    def body(h_ref, p_cg, p_av, p_bv, p_bg, u2_ref, gy_ref, caw_ref, cab_ref, cbw_ref, lbg_ref, lbb_ref,
             pw_ref, pb_ref, ps_ref, gh_ref, rows_ref, gpw_ref,
             exta, extb, gca, gu2, qx, nexta, nextb, nextc, tmp, tmp2, inv_ref, acc):
        s = pl.program_id(0)
        i = n - 1 - s

        @pl.when(s == 0)
        def _():
            acc[...] = jnp.zeros_like(acc)
            gpw_ref[...] = jnp.zeros_like(gpw_ref)
            for e in (nexta, nextb, nextc):
                e[...] = jnp.zeros_like(e)

        live = (i > 0).astype(F32)
        f32 = lambda ref: ref[...].astype(F32)
        before_a = _last_segment(f32(p_cg) * f32(p_av)) * live
        before_b = _last_segment(f32(p_bv) * _sigmoid(f32(p_bg))) * live

        a_bg, a_cg, a_v, a_z, b_v, b_g, b_z, c_u, c_z = _groups(h_ref)
        u2, sg, ca, sgg, sl, sz, sc, pooled, p = (u2_ref[:, k * W_MIX:(k + 1) * W_MIX] for k in range(len(KEPT)))
        g_ya = gy_ref[:, 0:W_MIX]
        g_yb = gy_ref[:, W_MIX:2 * W_MIX]
        g_yc = gy_ref[:, 2 * W_MIX:3 * W_MIX]

        def add_row(r, v):
            acc[r * SUB:(r + 1) * SUB, :] += _fold8(v)

        _look_back(exta, a_cg * a_v, before_a)
        s_az = a_z * sg
        t_a = g_ya * a_bg
        gh_ref[:, 0:W_MIX] = (g_ya * ca * s_az).astype(BF16)
        gh_ref[:, 3 * W_MIX:4 * W_MIX] = (t_a * ca * (sg * (1.0 + a_z * (1.0 - sg)))).astype(BF16)
        g_ca = t_a * s_az
        _look_ahead(gca, g_ca, nexta[...])
        nexta[...] = _first_segment(g_ca)
        add_row(R_CAB, g_ca)
        _conv_wgrad(gca, exta, acc, nk=KA, off=SEG - (KA - 1), row0=R_CAW)
        _conv_rows(gca, caw_ref, tmp, nk=KA, off=0, reverse=True)
        g_pa = tmp[...]
        gh_ref[:, W_MIX:2 * W_MIX] = (g_pa * a_v).astype(BF16)
        gh_ref[:, 2 * W_MIX:3 * W_MIX] = (g_pa * a_cg).astype(BF16)

        _look_back(extb, b_v * sgg, before_b)
        ln, u2hat, rstd = _layer_norm(u2, lbg_ref[...], lbb_ref[...])
        u3 = ln * sl
        s_bz = b_z * sz
        gh_ref[:, 6 * W_MIX:7 * W_MIX] = (g_yb * u3 * (sz * (1.0 + b_z * (1.0 - sz)))).astype(BF16)
        g_ln = g_yb * s_bz * (sl * (1.0 + ln * (1.0 - sl)))
        add_row(R_LBG, g_ln * u2hat)
        add_row(R_LBB, g_ln)
        g_u2 = _layer_norm_bwd(g_ln, u2hat, rstd, lbg_ref[...])
        _look_ahead(gu2, g_u2, nextb[...])
        nextb[...] = _first_segment(g_u2)
        add_row(R_CBB, g_u2)
        _conv_wgrad(gu2, extb, acc, nk=KB, off=SEG - (KB - 1), row0=R_CBW)
        _conv_rows(gu2, cbw_ref, tmp, nk=KB, off=0, reverse=True)
        g_u1 = tmp[...]
        gh_ref[:, 4 * W_MIX:5 * W_MIX] = (g_u1 * sgg).astype(BF16)
        gh_ref[:, 5 * W_MIX:6 * W_MIX] = (g_u1 * b_v * sgg * (1.0 - sgg)).astype(BF16)

        @pl.when((s == 0) | (i == 0))
        def _():
            inv_ref[...] = _inv_count(i)
        inv = inv_ref[...]
        s_cz = c_z * sc
        scale = ps_ref[...]
        gh_ref[:, 8 * W_MIX:9 * W_MIX] = (g_yc * p * scale * (sc * (1.0 + c_z * (1.0 - sc)))).astype(BF16)
        t_c = g_yc * s_cz
        add_row(R_PS, t_c * p)
        g_p = t_c * scale
        add_row(R_PB, g_p)
        g_pooled = []
        for g in range(len(POOL_WINDOWS)):
            lanes = slice(g * POOL_DIM, (g + 1) * POOL_DIM)
            gpg = g_p[:, lanes].astype(BF16)
            gpw_ref[g] += _dot_tn(pooled[:, lanes].astype(BF16), gpg)
            g_pooled.append(_dot_nt(gpg, pw_ref[g].astype(BF16)))
        g_pooled = jnp.concatenate(g_pooled, axis=1)
        q = g_pooled * inv
        _look_ahead(qx, q, nextc[...])
        nextc[...] = _first_segment(q)
        _window_sums(qx, tmp2, forward=True)
        gh_ref[:, 7 * W_MIX:8 * W_MIX] = (tmp2[...] - g_pooled).astype(BF16)

        @pl.when(s == n - 1)
        def _():
            for r in range(N_ROWS):
                rows_ref[r:r + 1, :] = jnp.sum(acc[r * SUB:(r + 1) * SUB, :], axis=0, keepdims=True)

    def phase():
        s = pl.program_id(0)
        return s == 0, s == n // 2, s == n - 1

    row = lambda wd: pl.BlockSpec((tt, wd), lambda s: (n - 1 - s, 0))
    before = [pl.BlockSpec((tt, W_MIX), lambda s, k=k: (jnp.maximum(n - 2 - s, 0), k)) for k in before_groups]
    full = lambda shape: pl.BlockSpec(shape, lambda s: (0,) * len(shape))
    params = (caw, cab, cbw, lbg, lbb, pw, pb, ps)
    host = _Hosted(plan, n_in=3 + len(before) + len(params), n_out=3, n_scratch=12)
    return pl.pallas_call(
        host.wrap(body, phase), name=name, grid=(n,),
        in_specs=[row(D_IN)] + before + [row(len(KEPT) * W_MIX), row(D_MIX)] + [full(a.shape) for a in params]
        + host.in_specs,
        out_specs=[row(D_IN), full((N_ROWS, W_MIX)), full(pw.shape)] + host.out_specs,
        out_shape=[jax.ShapeDtypeStruct((t, D_IN), BF16), jax.ShapeDtypeStruct((N_ROWS, W_MIX), F32),
                   jax.ShapeDtypeStruct(pw.shape, F32)] + host.out_shape,
        scratch_shapes=[pltpu.VMEM((2 * tt, W_MIX), F32)] * 5 + [pltpu.VMEM((tt, W_MIX), F32)] * 6
        + [pltpu.VMEM((N_ROWS * SUB, W_MIX), F32)] + host.scratch,
        compiler_params=_cparams(("arbitrary",)),
    )(h, *([h] * len(before)), u2, g_y, *params, *host.operands)


def _in_proj_wgrad(x, g_h, name, plan=None):
    t, d = x.shape
    n = g_h.shape[1]
    tk, tn = min(t, 1024), n // 2
    nk = t // tk

    def body(x_ref, g_ref, o_ref, acc):
        k = pl.program_id(1)

        @pl.when(k == 0)
        def _():
            acc[...] = jnp.zeros_like(acc)
        acc[...] += _dot_tn(x_ref[...].astype(BF16), g_ref[...])

        @pl.when(k == nk - 1)
        def _():
            o_ref[...] = acc[...].T.astype(BF16)

    def phase():
        step = pl.program_id(0) * nk + pl.program_id(1)
        return step == 0, step == nk, step == 2 * nk - 1

    host = _Hosted(plan, n_in=2, n_out=1, n_scratch=1)
    return pl.pallas_call(
        host.wrap(body, phase), name=name, grid=(n // tn, nk),
        in_specs=[pl.BlockSpec((tk, d), lambda j, k: (k, 0)), pl.BlockSpec((tk, tn), lambda j, k: (k, j))]
        + host.in_specs,
        out_specs=[pl.BlockSpec((tn, d), lambda j, k: (j, 0))] + host.out_specs,
        out_shape=[jax.ShapeDtypeStruct((n, d), BF16)] + host.out_shape,
        scratch_shapes=[pltpu.VMEM((d, tn), F32)] + host.scratch,
        compiler_params=_cparams(("arbitrary", "arbitrary")),
    )(x, g_h, *host.operands)


def _in_proj_dgrad(g_h, w, g_z, name, plan=None):
    t, n = g_h.shape
    d = w.shape[1]
    tm, tk = min(t, 1024), 1536
    nm, nk = t // tm, n // tk

    def body(g_ref, w_ref, gz_ref, o_ref):
        @pl.when(pl.program_id(1) == 0)
        def _():
            o_ref[...] = ALPHA * gz_ref[...]
        o_ref[...] += _dot(g_ref[...], w_ref[...])

    def phase():
        step = pl.program_id(0) * nk + pl.program_id(1)
        return step == 0, step == (nm * nk) // 2, step == nm * nk - 1

    host = _Hosted(plan, n_in=3, n_out=1, n_scratch=0)
    return pl.pallas_call(
        host.wrap(body, phase), name=name, grid=(nm, nk),
        in_specs=[pl.BlockSpec((tm, tk), lambda i, k: (i, k)), pl.BlockSpec((tk, d), lambda i, k: (k, 0)),
                  pl.BlockSpec((tm, d), lambda i, k: (i, 0))] + host.in_specs,
        out_specs=[pl.BlockSpec((tm, d), lambda i, k: (i, 0))] + host.out_specs,
        out_shape=[jax.ShapeDtypeStruct((t, d), F32)] + host.out_shape,
        scratch_shapes=host.scratch,
        compiler_params=_cparams(("arbitrary", "arbitrary")),
    )(g_h, w, g_z, *host.operands)


BC1 = 1.0 - ADAM_B1 ** ADAM_STEP
BC2 = 1.0 - ADAM_B2 ** ADAM_STEP


def _adamw_math(g, w, m, v):
    nm = ADAM_B1 * m + (1.0 - ADAM_B1) * g
    nv = ADAM_B2 * v + (1.0 - ADAM_B2) * (g * g)
    delta = -ADAM_LR * ((nm / BC1) / (jnp.sqrt(nv / BC2) + ADAM_EPS) + ADAM_WD * w)
    return delta, nm, nv


def _total(ref):
    g = ref[0].astype(F32)
    for k in range(1, ref.shape[0]):
        g = g + ref[k].astype(F32)
    return g


def _adamw_layers(parts, w, m, v, name):
    depth, r, c = w.shape
    p = parts[0].shape[0]
    tr = _row_tile(r)
    nr = r // tr

    def body(*refs):
        p_refs = refs[:depth]
        w_ref, m_ref, v_ref, g_ref, d_ref, nm_ref, nv_ref = refs[depth:]
        for l in range(depth):
            @pl.when(pl.program_id(0) == l)
            def _(l=l):
                g = _total(p_refs[l])
                delta, nm, nv = _adamw_math(g, w_ref[0], m_ref[0], v_ref[0])
                g_ref[0], d_ref[0], nm_ref[0], nv_ref[0] = g, delta, nm, nv

    def part_spec(l):
        return pl.BlockSpec((p, tr, c), lambda li, i: (0, jnp.where(li == l, i, jnp.where(li < l, 0, nr - 1)), 0))

    blk = pl.BlockSpec((1, tr, c), lambda li, i: (li, i, 0))
    out = jax.ShapeDtypeStruct((depth, r, c), F32)
    return pl.pallas_call(
        body, name=name, grid=(depth, nr),
        in_specs=[part_spec(l) for l in range(depth)] + [blk] * 3,
        out_specs=[blk] * 4, out_shape=[out] * 4,
        compiler_params=_cparams(("arbitrary", "arbitrary")),
    )(*parts, w, m, v)


SMALL_PARAMS = ("conv_a_w", "conv_a_b", "conv_b_w", "conv_b_b", "ln_b_g", "ln_b_b", "pool_w", "pool_b", "pool_scale",
                "ln_g", "ln_b")


def _adamw_small(rows_parts, gln_parts, gpw_parts, loss_parts, params, name):
    depth = len(rows_parts)
    cs = params["conv_a_w"][0].shape[2]
    operands = [*rows_parts, *gln_parts, *gpw_parts, loss_parts] + [a for n in SMALL_PARAMS for a in params[n]]
    n_in = len(operands)
    out_shape = [jax.ShapeDtypeStruct((1, 1), F32)]
    out_shape += [jax.ShapeDtypeStruct(params[n][0].shape, F32) for n in SMALL_PARAMS for _ in range(4)]

    def body(*refs):
        rows_p, gln_p, gpw_p = refs[:depth], refs[depth:2 * depth], refs[2 * depth:3 * depth]
        loss_p = refs[3 * depth]
        prm, outs = refs[3 * depth + 1:n_in], refs[n_in + 1:]
        refs[n_in][...] = jnp.sum(_total(loss_p)).reshape(1, 1)
        x, y, c = _position()
        to_front = (W_MIX - (4 * x + 2 * y + c) * cs) % W_MIX

        def update(name, g, at):
            k = SMALL_PARAMS.index(name)
            w_ref, m_ref, v_ref = prm[3 * k:3 * k + 3]
            g_ref, d_ref, nm_ref, nv_ref = outs[4 * k:4 * k + 4]
            delta, nm, nv = _adamw_math(g, w_ref[at], m_ref[at], v_ref[at])
            g_ref[at], d_ref[at], nm_ref[at], nv_ref[at] = g, delta, nm, nv

        for l in range(depth):
            rows = _total(rows_p[l])
            mine = pltpu.roll(rows, to_front, axis=1)
            gln = _total(gln_p[l])
            one = (slice(l, l + 1), slice(None))
            for name, r in (("conv_a_b", R_CAB), ("conv_b_b", R_CBB), ("ln_b_g", R_LBG), ("ln_b_b", R_LBB),
                            ("pool_scale", R_PS)):
                update(name, rows[r:r + 1, :], one)
            for g in range(len(POOL_WINDOWS)):
                update("pool_b", rows[R_PB:R_PB + 1, g * POOL_DIM:(g + 1) * POOL_DIM], (l, slice(g, g + 1), slice(None)))
            update("ln_g", gln[0:1, :], one)
            update("ln_b", gln[1:2, :], one)
            update("conv_a_w", mine[R_CAW:R_CAW + KA, 0:cs], (l,))
            update("conv_b_w", mine[R_CBW:R_CBW + KB, 0:cs], (l,))
            update("pool_w", _total(gpw_p[l]), (l,))

    vmem = pl.BlockSpec(memory_space=pltpu.VMEM)
    return pl.pallas_call(
        body, name=name, in_specs=[vmem] * n_in, out_specs=[vmem] * len(out_shape), out_shape=out_shape,
        compiler_params=pltpu.CompilerParams(vmem_limit_bytes=VMEM_LIMIT),
    )(*operands)


def kernel(x, w_in, conv_a_w, conv_a_b, conv_b_w, conv_b_b, ln_b_g, ln_b_b, pool_w, pool_b, pool_scale, w_out, ln_g, ln_b, loss_target, m_w_in, m_conv_a_w, m_conv_a_b, m_conv_b_w, m_conv_b_b, m_ln_b_g, m_ln_b_b, m_pool_w, m_pool_b, m_pool_scale, m_w_out, m_ln_g, m_ln_b, v_w_in, v_conv_a_w, v_conv_a_b, v_conv_b_w, v_conv_b_b, v_ln_b_g, v_ln_b_b, v_pool_w, v_pool_b, v_pool_scale, v_w_out, v_ln_g, v_ln_b):
    depth = w_in.shape[0]
    x0 = _to_segments(x[0])
    target = _to_segments(loss_target[0])
    r2 = lambda a: a.reshape(1, -1)

    tr = lambda a: jnp.swapaxes(a, 1, 2)
    w_in_t, m_w_in_t, v_w_in_t = tr(w_in), tr(m_w_in), tr(v_w_in)
    w_in_b, w_out_b = w_in_t.astype(BF16), w_out.astype(BF16)
    conv_sh = jnp.concatenate([conv_a_w, conv_b_w], axis=1)
    full_in = lambda g: g.reshape(D_IN, D_MODEL)
    full_out = lambda g: g.reshape(D_MIX, D_MODEL)
    w_in_f = [full_in(_run_plan(_Gather([w_in_b[0]]), "gather_w_in_0")[0])]
    w_out_f = []

    xs, hs, ys, u2s, zs = [x0], [], [], [], []
    for l in range(depth):
        h, *got = _in_proj(xs[l], w_in_f[l], f"in_proj_{l}",
                           plan=_Gather([w_out_b[0], conv_sh]) if l == 0 else None)
        if got:
            w_out_f.append(full_out(got[0]))
            conv_f = got[1].transpose(1, 2, 0, 3).reshape(depth, KA + KB, W_MIX)
            caw_f, cbw_f = conv_f[:, :KA], conv_f[:, KA:]
        last = l + 1 == depth
        y, u2, z, xn, *got = _mixer_fwd(
            h, xs[l], caw_f[l], r2(conv_a_b[l]), cbw_f[l], r2(conv_b_b[l]), r2(ln_b_g[l]), r2(ln_b_b[l]),
            pool_w[l], r2(pool_b[l]), r2(pool_scale[l]), w_out_f[l], r2(ln_g[l]), r2(ln_b[l]), f"mixer_fwd_{l}",
            plan=None if last else _Gather([w_in_b[l + 1], w_out_b[l + 1]]), target=target if last else None)
        if last:
            g, loss_rows = xn, got[0]
        else:
            w_in_f.append(full_in(got[0])), w_out_f.append(full_out(got[1]))
        hs.append(h), ys.append(y), u2s.append(u2), zs.append(z), xs.append(xn)


    gi_parts, go_parts, g_rows, g_lns, g_pool_w = ([None] * depth for _ in range(5))
    waiting = []
    for l in reversed(range(depth)):
        g_z, g_y, g_w_out, g_lns[l] = _out_proj_bwd(g, zs[l], ys[l], w_out_f[l].T, r2(ln_g[l]), f"out_proj_bwd_{l}")
        go = g_w_out.reshape(N_DEV, D_MIX // N_DEV, D_MODEL)
        if l == 0:
            waiting += [go]
        g_h, g_rows[l], g_pool_w[l], *got = _mixer_bwd(
            hs[l], u2s[l], g_y, caw_f[l], r2(conv_a_b[l]), cbw_f[l], r2(ln_b_g[l]), r2(ln_b_b[l]),
            pool_w[l], r2(pool_b[l]), r2(pool_scale[l]), f"mixer_bwd_{l}",
            plan=_Exchange(waiting) if waiting else None)
        if got:
            if l + 1 < depth:
                gi_parts[l + 1], go_parts[l + 1] = got[0], got[1]
            if l == 0:
                go_parts[0] = got[-1]
        if l > 0:
            gi = _in_proj_wgrad(xs[l], g_h, f"in_proj_wgrad_{l}")[0]
            waiting = [gi.reshape(N_DEV, D_IN // N_DEV, D_MODEL), go]
            g = _in_proj_dgrad(g_h, w_in_f[l], g_z, f"in_proj_dgrad_{l}")[0]
        else:
            small = [*g_rows, *g_lns, *g_pool_w, loss_rows]
            gi, *small_parts = _in_proj_wgrad(xs[0], g_h, "in_proj_wgrad_0", plan=_Gather(small))
            g, gi_parts[0] = _in_proj_dgrad(g_h, w_in_f[0], g_z, "in_proj_dgrad_0",
                                            plan=_Exchange([gi.reshape(N_DEV, D_IN // N_DEV, D_MODEL)]))
    grad_x = _from_segments(g)[None]

    out_in = [tr(a) for a in _adamw_layers(gi_parts, w_in_t, m_w_in_t, v_w_in_t, "adamw_w_in")]
    out_out = _adamw_layers(go_parts, w_out, m_w_out, v_w_out, "adamw_w_out")
    params = dict(
        conv_a_w=(conv_a_w, m_conv_a_w, v_conv_a_w), conv_a_b=(conv_a_b, m_conv_a_b, v_conv_a_b),
        conv_b_w=(conv_b_w, m_conv_b_w, v_conv_b_w), conv_b_b=(conv_b_b, m_conv_b_b, v_conv_b_b),
        ln_b_g=(ln_b_g, m_ln_b_g, v_ln_b_g), ln_b_b=(ln_b_b, m_ln_b_b, v_ln_b_b),
        pool_w=(pool_w, m_pool_w, v_pool_w), pool_b=(pool_b, m_pool_b, v_pool_b),
        pool_scale=(pool_scale, m_pool_scale, v_pool_scale), ln_g=(ln_g, m_ln_g, v_ln_g), ln_b=(ln_b, m_ln_b, v_ln_b))
    loss, *small_out = _adamw_small(small_parts[:depth], small_parts[depth:2 * depth], small_parts[2 * depth:3 * depth],
                                    small_parts[3 * depth], params, "adamw_small")
    small_out = {n: small_out[4 * k:4 * k + 4] for k, n in enumerate(SMALL_PARAMS)}

    order = ("w_in", "conv_a_w", "conv_a_b", "conv_b_w", "conv_b_b", "ln_b_g", "ln_b_b", "pool_w", "pool_b",
             "pool_scale", "w_out", "ln_g", "ln_b")
    outs = []
    for k in range(4):
        outs += [out_in[k] if n == "w_in" else out_out[k] if n == "w_out" else small_out[n][k] for n in order]
    return (loss.reshape(()), grad_x, *outs)
```

```python
import jax
import jax.numpy as jnp
from jax import lax
from jax.experimental import pallas as pl
from jax.experimental.pallas import tpu as pltpu

F32 = jnp.float32
BF16 = jnp.bfloat16

DEPTH = 2
D_MODEL = 1024
W_MIX = 512
D_IN = 9 * W_MIX
D_MIX = 3 * W_MIX
POOL_WINDOWS = (2, 4, 8, 16)
POOL_DIM = 128
KA = 3
KB = 31
ALPHA = (2.0 * DEPTH) ** 0.25
LN_EPS = 1e-5
ADAM_LR, ADAM_B1, ADAM_B2, ADAM_EPS, ADAM_WD, ADAM_STEP = 0.001, 0.9, 0.999, 1e-08, 0.01, 10

N_DEV = 8
N_CHIP = 4
MESH = pl.DeviceIdType.MESH

CH = 32
SUB = 8
VMEM_LIMIT = 56 * 1024 * 1024

R_CAB, R_CBB, R_LBG, R_LBB, R_PB, R_PS, R_CAW, R_CBW = 0, 1, 2, 3, 4, 5, 6, 9
N_ROWS = R_CBW + KB
LN_ROWS = 16
SMALL_ROWS = N_ROWS + LN_ROWS


def _cparams(sem, **kw):
    return pltpu.CompilerParams(dimension_semantics=sem, vmem_limit_bytes=VMEM_LIMIT, **kw)


def _sigmoid(v):
    return 1.0 / (1.0 + jnp.exp(-v))


def _fold8(a):
    r, c = a.shape
    return a.reshape(r // SUB, SUB, c).sum(axis=0)


def _dot(a, b):
    return jnp.dot(a, b, preferred_element_type=F32)


def _dot_nt(a, b):
    return lax.dot_general(a, b, (((1,), (1,)), ((), ())), preferred_element_type=F32)


def _dot_tn(a, b):
    return lax.dot_general(a, b, (((0,), (0,)), ((), ())), preferred_element_type=F32)


def _row_tile(r):
    for cand in (512, 256, 128, 64, 32, 16, 8):
        if r % cand == 0 and r > cand:
            return cand
    return r


def _position():
    return lax.axis_index("x"), lax.axis_index("y"), lax.axis_index("c")


class _Gather:
    def __init__(self, arrs):
        self.arrs = list(arrs)
        na = len(self.arrs)
        self.out_shape = [jax.ShapeDtypeStruct((N_DEV,) + a.shape, a.dtype) for a in self.arrs]
        self.scratch = [pltpu.SemaphoreType.DMA((na, 7)), pltpu.SemaphoreType.DMA((na, 7)),
                        pltpu.SemaphoreType.DMA((na,))]

    def _copies(self, src, dst, sems):
        send_sems, recv_sems, local_sems = sems
        na = len(self.arrs)
        x, y, c = _position()
        me, sibling = (x, y, c), (x, y, 1 - c)
        chips = [(1 - x, y), (x, 1 - y), (1 - x, 1 - y)]

        def slot(a, dev):
            return dst[a].at[4 * dev[0] + 2 * dev[1] + dev[2]]

        def copy(a, k, block, to, from_src=False):
            return pltpu.make_async_remote_copy(
                src_ref=src[a] if from_src else slot(a, block), dst_ref=slot(a, block),
                send_sem=send_sems.at[a, k], recv_sem=recv_sems.at[a, k], device_id=to, device_id_type=MESH)

        mine = [pltpu.make_async_copy(src[a], slot(a, me), local_sems.at[a]) for a in range(na)]
        first, landed, passed, last = [], [], [], []
        for a in range(na):
            first.append(copy(a, 0, me, sibling, from_src=True))
            first += [copy(a, 1 + j, me, (*chip, c), from_src=True) for j, chip in enumerate(chips)]
        for j, chip in enumerate(chips):
            for a in range(na):
                landed.append(copy(a, 1 + j, (*chip, c), me))
                passed.append(copy(a, 4 + j, (*chip, c), sibling))
        for a in range(na):
            last.append(copy(a, 0, sibling, me))
            last += [copy(a, 4 + j, (*chip, 1 - c), me) for j, chip in enumerate(chips)]
        return mine, first, landed, passed, last

    def start(self, src, dst, sems):
        mine, first, _, _, _ = self._copies(src, dst, sems)
        for cp in mine + first:
            cp.start()

    def pass_on(self, src, dst, sems):
        _, _, landed, passed, _ = self._copies(src, dst, sems)
        for got, cp in zip(landed, passed):
            got.wait_recv()
            cp.start()

    def finish(self, src, dst, sems):
        mine, first, _, passed, last = self._copies(src, dst, sems)
        for cp in last:
            cp.wait_recv()
        for cp in first + passed:
            cp.wait_send()
        for cp in mine:
            cp.wait()

    def chip_ready(self, j, src, dst, sems):
        na = len(self.arrs)
        mine, _, landed, passed, last = self._copies(src, dst, sems)
        if j == 0:
            for cp in mine:
                cp.wait()
            for a in range(na):
                last[4 * a].wait_recv()
        else:
            for a in range(na):
                landed[(j - 1) * na + a].wait_recv()
                passed[(j - 1) * na + a].start()
            for a in range(na):
                last[4 * a + j].wait_recv()

    def sent(self, src, dst, sems):
        _, first, _, passed, _ = self._copies(src, dst, sems)
        for cp in first + passed:
            cp.wait_send()


class _Exchange:
    def __init__(self, arrs):
        self.arrs = list(arrs)
        na = len(self.arrs)
        self.out_shape = [jax.ShapeDtypeStruct(a.shape, a.dtype) for a in self.arrs]
        self.scratch = [pltpu.SemaphoreType.DMA((na, N_DEV - 1)), pltpu.SemaphoreType.DMA((na, N_DEV - 1)),
                        pltpu.SemaphoreType.DMA((na,))]

    def _copies(self, src, dst, sems):
        send_sems, recv_sems, local_sems = sems
        na = len(self.arrs)
        x, y, c = _position()
        me = 4 * x + 2 * y + c
        mine = [pltpu.make_async_copy(src[a].at[me], dst[a].at[me], local_sems.at[a]) for a in range(na)]
        copies = []
        for a in range(na):
            for k in range(N_DEV - 1):
                flip = k + 1
                px, py, pc = x ^ (flip >> 2), y ^ ((flip >> 1) & 1), c ^ (flip & 1)
                copies.append(pltpu.make_async_remote_copy(
                    src_ref=src[a].at[4 * px + 2 * py + pc], dst_ref=dst[a].at[me],
                    send_sem=send_sems.at[a, k], recv_sem=recv_sems.at[a, k], device_id=(px, py, pc),
                    device_id_type=MESH))
        return mine, copies

    def start(self, src, dst, sems):
        mine, copies = self._copies(src, dst, sems)
        for cp in mine + copies:
            cp.start()

    def pass_on(self, src, dst, sems):
        pass

    def finish(self, src, dst, sems):
        mine, copies = self._copies(src, dst, sems)
        for cp in copies:
            cp.wait()
        for cp in mine:
            cp.wait()


def _run_plan(plan, name):
    na = len(plan.arrs)
    any_spec = pl.BlockSpec(memory_space=pl.ANY)

    def body(*refs):
        src, dst, sems = refs[:na], refs[na:2 * na], refs[2 * na:]
        plan.start(src, dst, sems)
        plan.pass_on(src, dst, sems)
        plan.finish(src, dst, sems)

    return pl.pallas_call(
        body, name=name, in_specs=[any_spec] * na, out_specs=[any_spec] * na,
        out_shape=plan.out_shape, scratch_shapes=plan.scratch,
    )(*plan.arrs)


class _Hosted:
    def __init__(self, plan, n_in, n_out, n_scratch):
        self.plan, self.n_in, self.n_out, self.n_scratch = plan, n_in, n_out, n_scratch
        any_spec = pl.BlockSpec(memory_space=pl.ANY)
        k = 0 if plan is None else len(plan.arrs)
        self.operands = [] if plan is None else plan.arrs
        self.in_specs = [any_spec] * k
        self.out_specs = [any_spec] * k
        self.out_shape = [] if plan is None else plan.out_shape
        self.scratch = [] if plan is None else plan.scratch

    def wrap(self, body, phase):
        if self.plan is None:
            return body
        plan, k = self.plan, len(self.plan.arrs)
        i0, o0 = self.n_in, self.n_in + k
        o1 = o0 + self.n_out
        s0 = o1 + k
        s1 = s0 + self.n_scratch

        def hosted(*refs):
            src, dst, sems = refs[i0:o0], refs[o1:s0], refs[s1:]
            first, middle, last = phase()
            pl.when(first)(lambda: plan.start(src, dst, sems))
            body(*refs[:i0], *refs[o0:o1], *refs[s0:s1])
            pl.when(middle)(lambda: plan.pass_on(src, dst, sems))
            pl.when(last)(lambda: plan.finish(src, dst, sems))

        return hosted


TT = 256
SEG = TT // SUB


def _to_segments(a):
    t, c = a.shape
    return a.reshape(t // TT, SUB, SEG, c).transpose(0, 2, 1, 3).reshape(t, c)


def _from_segments(a):
    t, c = a.shape
    return a.reshape(t // TT, SEG, SUB, c).transpose(0, 2, 1, 3).reshape(t, c)


def _sublane_is(s):
    return lax.broadcasted_iota(jnp.int32, (TT, W_MIX), 0) % SUB == s


def _look_back(ext_ref, cur, before):
    ext_ref[TT:, :] = cur
    ext_ref[0:TT, :] = jnp.where(_sublane_is(0), before, pltpu.roll(cur, 1, axis=0))


def _last_segment(cur):
    return pltpu.roll(cur, TT - (SUB - 1), axis=0)


def _look_ahead(ext_ref, cur, after):
    ext_ref[0:TT, :] = cur
    ext_ref[TT:, :] = jnp.where(_sublane_is(SUB - 1), after, pltpu.roll(cur, TT - 1, axis=0))


def _first_segment(cur):
    return pltpu.roll(cur, SUB - 1, axis=0)


def _tap_loop(body):
    lax.fori_loop(0, TT // CH, lambda c, carry: body(pl.multiple_of(c * CH, CH), carry), 0)


def _conv_rows(ext_ref, w_ref, out_ref, *, nk, off, reverse):
    def body(r0, carry):
        acc = jnp.zeros((CH, W_MIX), F32)
        for k in range(nk):
            kk = nk - 1 - k if reverse else k
            acc = acc + ext_ref[pl.ds(r0 + (off + k) * SUB, CH), :] * w_ref[kk:kk + 1, :]
        out_ref[pl.ds(r0, CH), :] = acc
        return carry
    _tap_loop(body)


def _conv_wgrad(g_ref, ext_ref, acc_ref, *, nk, off, row0):
    def body(r0, carry):
        g = g_ref[pl.ds(r0, CH), :]
        for k in range(nk):
            a = (row0 + k) * SUB
            acc_ref[a:a + SUB, :] += _fold8(g * ext_ref[pl.ds(r0 + (off + k) * SUB, CH), :])
        return carry
    _tap_loop(body)


def _window_sums(ext_ref, out_ref, *, forward):
    def body(r0, carry):
        for g, w in enumerate(POOL_WINDOWS):
            lanes = slice(g * POOL_DIM, (g + 1) * POOL_DIM)
            acc = jnp.zeros((CH, POOL_DIM), F32)
            for j in range(w):
                off = j if forward else SEG - j
                acc = acc + ext_ref[pl.ds(r0 + off * SUB, CH), lanes]
            out_ref[pl.ds(r0, CH), lanes] = acc
        return carry
    _tap_loop(body)


def _inv_count(tile):
    r = lax.broadcasted_iota(jnp.int32, (TT, POOL_DIM), 0)
    t1 = (tile * TT + (r % SUB) * SEG + r // SUB + 1).astype(F32)
    return jnp.concatenate([1.0 / jnp.minimum(t1, float(w)) for w in POOL_WINDOWS], axis=1)


KEPT = ("u2", "sigmoid(a_z)", "ca", "sigmoid(b_g)", "sigmoid(ln)", "sigmoid(b_z)", "sigmoid(c_z)", "pooled", "p")


def _groups(h_ref):
    return [h_ref[:, k * W_MIX:(k + 1) * W_MIX].astype(F32) for k in range(9)]


def _layer_norm(v, g, b):
    mu = jnp.mean(v, axis=-1, keepdims=True)
    vc = v - mu
    var = jnp.mean(vc * vc, axis=-1, keepdims=True)
    rstd = lax.rsqrt(var + LN_EPS)
    vhat = vc * rstd
    return vhat * g + b, vhat, rstd


def _layer_norm_bwd(g_out, vhat, rstd, g):
    gh = g_out * g
    m1 = jnp.mean(gh, axis=-1, keepdims=True)
    m2 = jnp.mean(gh * vhat, axis=-1, keepdims=True)
    return rstd * (gh - m1 - vhat * m2)


def _pool_linear(pooled, pw_ref, pb_ref):
    outs = []
    for g in range(len(POOL_WINDOWS)):
        lanes = slice(g * POOL_DIM, (g + 1) * POOL_DIM)
        outs.append(_dot(pooled[:, lanes].astype(BF16), pw_ref[g].astype(BF16)))
    return jnp.concatenate(outs, axis=1) + pb_ref[...]


def _in_proj(x, w, name, plan=None):
    t, d = x.shape
    n = w.shape[0]
    tm, tn = min(t, 1024), 1536
    nm, nn = t // tm, n // tn

    def body(x_ref, w_ref, o_ref, xb_ref):
        @pl.when(pl.program_id(1) == 0)
        def _():
            xb_ref[...] = x_ref[...].astype(BF16)
        o_ref[...] = _dot_nt(xb_ref[...], w_ref[...]).astype(BF16)

    def phase():
        step = pl.program_id(0) * nn + pl.program_id(1)
        return step == 0, step == (nm * nn) // 2, step == nm * nn - 1

    host = _Hosted(plan, n_in=2, n_out=1, n_scratch=1)
    return pl.pallas_call(
        host.wrap(body, phase), name=name, grid=(nm, nn),
        in_specs=[pl.BlockSpec((tm, d), lambda i, j: (i, 0)), pl.BlockSpec((tn, d), lambda i, j: (j, 0))]
        + host.in_specs,
        out_specs=[pl.BlockSpec((tm, tn), lambda i, j: (i, j))] + host.out_specs,
        out_shape=[jax.ShapeDtypeStruct((t, n), BF16)] + host.out_shape,
        scratch_shapes=[pltpu.VMEM((tm, d), BF16)] + host.scratch,
        compiler_params=_cparams(("arbitrary", "arbitrary")),
    )(x, w, *host.operands)


def _in_proj_first(x, plan, chip_order, name):
    t, d = x.shape
    k = len(plan.arrs)
    nb = plan.arrs[0].shape[0]
    tm = min(t, 1024)
    nm = t // tm

    def body(order_ref, x_ref, *refs):
        src, o_ref, dst = refs[:k], refs[k], refs[k + 1:2 * k + 1]
        wbuf, wsem = refs[2 * k + 1:2 * k + 3]
        sems = refs[2 * k + 3:]
        j, i = pl.program_id(0), pl.program_id(1)

        @pl.when((j == 0) & (i == 0))
        def _():
            plan.start(src, dst, sems)

        for jj in range(N_CHIP):
            @pl.when((j == jj) & (i == 0))
            def _(jj=jj):
                plan.chip_ready(jj, src, dst, sems)
                load = pltpu.make_async_copy(dst[0].at[pl.ds(2 * order_ref[jj], 2)], wbuf, wsem)
                load.start()
                load.wait()

        o_ref[...] = _dot_nt(x_ref[...].astype(BF16), wbuf[...].reshape(2 * nb, d)).astype(BF16)

        @pl.when((j == N_CHIP - 1) & (i == nm - 1))
        def _():
            plan.sent(src, dst, sems)

    any_spec = pl.BlockSpec(memory_space=pl.ANY)
    return pl.pallas_call(
        body, name=name,
        grid_spec=pltpu.PrefetchScalarGridSpec(
            num_scalar_prefetch=1, grid=(N_CHIP, nm),
            in_specs=[pl.BlockSpec((tm, d), lambda j, i, order: (i, 0))] + [any_spec] * k,
            out_specs=[pl.BlockSpec((tm, 2 * nb), lambda j, i, order: (i, order[j]))] + [any_spec] * k,
            scratch_shapes=[pltpu.VMEM((2, nb, d), BF16), pltpu.SemaphoreType.DMA(())] + plan.scratch),
        out_shape=[jax.ShapeDtypeStruct((t, N_DEV * nb), BF16)] + plan.out_shape,
        compiler_params=_cparams(("arbitrary", "arbitrary")),
    )(chip_order, x, *plan.arrs)


def _mixer_fwd(h, x, caw, cab, cbw, cbb, lbg, lbb, pw, pb, ps, w_out, lng, lnb, name, plan=None, target=None):
    t = h.shape[0]
    tt = TT
    n = t // tt
    with_loss = target is not None

    def body(h_ref, x_ref, *refs):
        if with_loss:
            t_ref, refs = refs[0], refs[1:]
        (caw_ref, cab_ref, cbw_ref, cbb_ref, lbg_ref, lbb_ref, pw_ref, pb_ref, ps_ref, wo_ref, lng_ref, lnb_ref,
         y_ref, u2_ref, z_ref, xn_ref) = refs[:16]
        refs = refs[16:]
        if with_loss:
            l_ref, refs = refs[0], refs[1:]
        exta, extb, extc, lasta, lastb, lastc, tmp, inv_ref = refs
        i = pl.program_id(0)

        @pl.when(i == 0)
        def _():
            for e in (lasta, lastb, lastc):
                e[...] = jnp.zeros_like(e)
            if with_loss:
                l_ref[...] = jnp.zeros_like(l_ref)

        @pl.when(i <= 1)
        def _():
            inv_ref[...] = _inv_count(i)

        a_bg, a_cg, a_v, a_z, b_v, b_g, b_z, c_u, c_z = _groups(h_ref)
        sgg = _sigmoid(b_g)
        for ext, last, cur in ((exta, lasta, a_cg * a_v), (extb, lastb, b_v * sgg), (extc, lastc, c_u)):
            _look_back(ext, cur, last[...])
            last[...] = _last_segment(cur)

        _conv_rows(exta, caw_ref, tmp, nk=KA, off=SEG - (KA - 1), reverse=False)
        ca = tmp[...] + cab_ref[...]
        sga = _sigmoid(a_z)
        y_ref[:, 0:W_MIX] = (a_bg * ca * (a_z * sga)).astype(BF16)

        _window_sums(extc, tmp, forward=False)
        pooled = tmp[...] * inv_ref[...] - c_u
        p = _pool_linear(pooled, pw_ref, pb_ref)
        sc = _sigmoid(c_z)
        y_ref[:, 2 * W_MIX:3 * W_MIX] = (p * ps_ref[...] * (c_z * sc)).astype(BF16)

        _conv_rows(extb, cbw_ref, tmp, nk=KB, off=SEG - (KB - 1), reverse=False)
        u2 = tmp[...] + cbb_ref[...]
        ln, _, _ = _layer_norm(u2, lbg_ref[...], lbb_ref[...])
        sl, sz = _sigmoid(ln), _sigmoid(b_z)
        y_ref[:, W_MIX:2 * W_MIX] = ((ln * sl) * (b_z * sz)).astype(BF16)

        for k, kept in enumerate((u2, sga, ca, sgg, sl, sz, sc, pooled, p)):
            u2_ref[:, k * W_MIX:(k + 1) * W_MIX] = kept

        out = _dot(y_ref[...], wo_ref[...])
        z = ALPHA * x_ref[...] + out
        z_ref[...] = z
        xn, _, _ = _layer_norm(z, lng_ref[...], lnb_ref[...])
        if with_loss:
            e = xn - t_ref[...]
            xn_ref[...] = e * (1.0 / D_MODEL)
            l_ref[...] += _fold8(e * e) * (0.5 / D_MODEL)
        else:
            xn_ref[...] = xn

    def phase():
        i = pl.program_id(0)
        return i == 0, i == n // 2, i == n - 1

    row = lambda wd: pl.BlockSpec((tt, wd), lambda i: (i, 0))
    full = lambda a: pl.BlockSpec(a.shape, lambda i: (0,) * a.ndim)
    params = (caw, cab, cbw, cbb, lbg, lbb, pw, pb, ps, w_out, lng, lnb)
    extra_in = [target] if with_loss else []
    extra_out = [jax.ShapeDtypeStruct((SUB, D_MODEL), F32)] if with_loss else []
    host = _Hosted(plan, n_in=2 + len(extra_in) + len(params), n_out=4 + len(extra_out), n_scratch=8)
    return pl.pallas_call(
        host.wrap(body, phase), name=name, grid=(n,),
        in_specs=[row(D_IN), row(D_MODEL)] + [row(D_MODEL)] * len(extra_in) + [full(a) for a in params]
        + host.in_specs,
        out_specs=[row(D_MIX), row(len(KEPT) * W_MIX), row(D_MODEL), row(D_MODEL)] + [full(o) for o in extra_out]
        + host.out_specs,
        out_shape=[jax.ShapeDtypeStruct((t, D_MIX), BF16), jax.ShapeDtypeStruct((t, len(KEPT) * W_MIX), F32),
                   jax.ShapeDtypeStruct((t, D_MODEL), F32), jax.ShapeDtypeStruct((t, D_MODEL), F32)]
        + extra_out + host.out_shape,
        scratch_shapes=[pltpu.VMEM((2 * tt, W_MIX), F32)] * 3 + [pltpu.VMEM((tt, W_MIX), F32)] * 5 + host.scratch,
        compiler_params=_cparams(("arbitrary",)),
    )(h, x, *extra_in, *params, *host.operands)


def _out_proj_bwd(g_xn, z, y, w_out_t, lng, name):
    t = z.shape[0]
    tt = min(t, 512)
    n = t // tt

    def body(g_ref, z_ref, y_ref, wo_ref, lng_ref, gz_ref, gy_ref, gwo_ref, gln_ref, accg, accb, accw, gzb):
        i = pl.program_id(0)

        @pl.when(i == 0)
        def _():
            accw[...] = jnp.zeros_like(accw)
            accg[...] = jnp.zeros_like(accg)
            accb[...] = jnp.zeros_like(accb)
            gzb[...] = jnp.zeros_like(gzb)

        before = gzb[(i + 1) % 2]
        gy_ref[...] = _dot(before, wo_ref[...])
        accw[...] += _dot_tn(y_ref[...], before)

        counts = (i < n).astype(F32)
        g = g_ref[...]
        _, zhat, rstd = _layer_norm(z_ref[...], lng_ref[...], 0.0)
        accg[...] += _fold8(g * zhat) * counts
        accb[...] += _fold8(g) * counts
        g_z = _layer_norm_bwd(g, zhat, rstd, lng_ref[...])
        gz_ref[...] = g_z
        gzb[i % 2] = g_z.astype(BF16)

        @pl.when(i == n)
        def _():
            gwo_ref[...] = accw[...].astype(BF16)
            gln_ref[...] = jnp.zeros_like(gln_ref)
            gln_ref[0:1, :] = jnp.sum(accg[...], axis=0, keepdims=True)
            gln_ref[1:2, :] = jnp.sum(accb[...], axis=0, keepdims=True)

    this = lambda wd: pl.BlockSpec((tt, wd), lambda i: (jnp.minimum(i, n - 1), 0))
    last = lambda wd: pl.BlockSpec((tt, wd), lambda i: (jnp.maximum(i - 1, 0), 0))
    full = lambda shape: pl.BlockSpec(shape, lambda i: (0,) * len(shape))
    return pl.pallas_call(
        body, name=name, grid=(n + 1,),
        in_specs=[this(D_MODEL), this(D_MODEL), last(D_MIX), full(w_out_t.shape), full(lng.shape)],
        out_specs=[this(D_MODEL), last(D_MIX), full((D_MIX, D_MODEL)), full((SUB, D_MODEL))],
        out_shape=[jax.ShapeDtypeStruct((t, D_MODEL), F32), jax.ShapeDtypeStruct((t, D_MIX), F32),
                   jax.ShapeDtypeStruct((D_MIX, D_MODEL), BF16), jax.ShapeDtypeStruct((SUB, D_MODEL), F32)],
        scratch_shapes=[pltpu.VMEM((SUB, D_MODEL), F32)] * 2 + [pltpu.VMEM((D_MIX, D_MODEL), F32),
                                                                 pltpu.VMEM((2, tt, D_MODEL), BF16)],
        compiler_params=_cparams(("arbitrary",)),
    )(g_xn, z, y, w_out_t, lng)


def _mixer_bwd(h, u2, g_y, caw, cab, cbw, lbg, lbb, pw, pb, ps, name, plan=None):
    t = h.shape[0]
    tt = TT
    n = t // tt
    before_groups = (1, 2, 4, 5)

    def body(h_ref, p_cg, p_av, p_bv, p_bg, u2_ref, gy_ref, caw_ref, cab_ref, cbw_ref, lbg_ref, lbb_ref,
             pw_ref, pb_ref, ps_ref, gh_ref, rows_ref, gpw_ref,
             exta, extb, gca, gu2, qx, nexta, nextb, nextc, tmp, tmp2, inv_ref, acc):
        s = pl.program_id(0)
        i = n - 1 - s

        @pl.when(s == 0)
        def _():
            acc[...] = jnp.zeros_like(acc)
            gpw_ref[...] = jnp.zeros_like(gpw_ref)
            for e in (nexta, nextb, nextc):
                e[...] = jnp.zeros_like(e)

        live = (i > 0).astype(F32)
        f32 = lambda ref: ref[...].astype(F32)
        before_a = _last_segment(f32(p_cg) * f32(p_av)) * live
        before_b = _last_segment(f32(p_bv) * _sigmoid(f32(p_bg))) * live

        a_bg, a_cg, a_v, a_z, b_v, b_g, b_z, c_u, c_z = _groups(h_ref)
        u2, sg, ca, sgg, sl, sz, sc, pooled, p = (u2_ref[:, k * W_MIX:(k + 1) * W_MIX] for k in range(len(KEPT)))
        g_ya = gy_ref[:, 0:W_MIX]
        g_yb = gy_ref[:, W_MIX:2 * W_MIX]
        g_yc = gy_ref[:, 2 * W_MIX:3 * W_MIX]

        def add_row(r, v):
            acc[r * SUB:(r + 1) * SUB, :] += _fold8(v)

        _look_back(exta, a_cg * a_v, before_a)
        s_az = a_z * sg
        t_a = g_ya * a_bg
        gh_ref[:, 0:W_MIX] = (g_ya * ca * s_az).astype(BF16)
        gh_ref[:, 3 * W_MIX:4 * W_MIX] = (t_a * ca * (sg * (1.0 + a_z * (1.0 - sg)))).astype(BF16)
        g_ca = t_a * s_az
        _look_ahead(gca, g_ca, nexta[...])
        nexta[...] = _first_segment(g_ca)
        add_row(R_CAB, g_ca)
        _conv_wgrad(gca, exta, acc, nk=KA, off=SEG - (KA - 1), row0=R_CAW)
        _conv_rows(gca, caw_ref, tmp, nk=KA, off=0, reverse=True)
        g_pa = tmp[...]
        gh_ref[:, W_MIX:2 * W_MIX] = (g_pa * a_v).astype(BF16)
        gh_ref[:, 2 * W_MIX:3 * W_MIX] = (g_pa * a_cg).astype(BF16)

        _look_back(extb, b_v * sgg, before_b)
        ln, u2hat, rstd = _layer_norm(u2, lbg_ref[...], lbb_ref[...])
        u3 = ln * sl
        s_bz = b_z * sz
        gh_ref[:, 6 * W_MIX:7 * W_MIX] = (g_yb * u3 * (sz * (1.0 + b_z * (1.0 - sz)))).astype(BF16)
        g_ln = g_yb * s_bz * (sl * (1.0 + ln * (1.0 - sl)))
        add_row(R_LBG, g_ln * u2hat)
        add_row(R_LBB, g_ln)
        g_u2 = _layer_norm_bwd(g_ln, u2hat, rstd, lbg_ref[...])
        _look_ahead(gu2, g_u2, nextb[...])
        nextb[...] = _first_segment(g_u2)
        add_row(R_CBB, g_u2)
        _conv_wgrad(gu2, extb, acc, nk=KB, off=SEG - (KB - 1), row0=R_CBW)
        _conv_rows(gu2, cbw_ref, tmp, nk=KB, off=0, reverse=True)
        g_u1 = tmp[...]
        gh_ref[:, 4 * W_MIX:5 * W_MIX] = (g_u1 * sgg).astype(BF16)
        gh_ref[:, 5 * W_MIX:6 * W_MIX] = (g_u1 * b_v * sgg * (1.0 - sgg)).astype(BF16)

        @pl.when((s == 0) | (i == 0))
        def _():
            inv_ref[...] = _inv_count(i)
        inv = inv_ref[...]
        s_cz = c_z * sc
        scale = ps_ref[...]
        gh_ref[:, 8 * W_MIX:9 * W_MIX] = (g_yc * p * scale * (sc * (1.0 + c_z * (1.0 - sc)))).astype(BF16)
        t_c = g_yc * s_cz
        add_row(R_PS, t_c * p)
        g_p = t_c * scale
        add_row(R_PB, g_p)
        g_pooled = []
        for g in range(len(POOL_WINDOWS)):
            lanes = slice(g * POOL_DIM, (g + 1) * POOL_DIM)
            gpg = g_p[:, lanes].astype(BF16)
            gpw_ref[g] += _dot_tn(pooled[:, lanes].astype(BF16), gpg)
            g_pooled.append(_dot_nt(gpg, pw_ref[g].astype(BF16)))
        g_pooled = jnp.concatenate(g_pooled, axis=1)
        q = g_pooled * inv
        _look_ahead(qx, q, nextc[...])
        nextc[...] = _first_segment(q)
        _window_sums(qx, tmp2, forward=True)
        gh_ref[:, 7 * W_MIX:8 * W_MIX] = (tmp2[...] - g_pooled).astype(BF16)

        @pl.when(s == n - 1)
        def _():
            for r in range(N_ROWS):
                rows_ref[r:r + 1, :] = jnp.sum(acc[r * SUB:(r + 1) * SUB, :], axis=0, keepdims=True)

    def phase():
        s = pl.program_id(0)
        return s == 0, s == n // 2, s == n - 1

    row = lambda wd: pl.BlockSpec((tt, wd), lambda s: (n - 1 - s, 0))
    before = [pl.BlockSpec((tt, W_MIX), lambda s, k=k: (jnp.maximum(n - 2 - s, 0), k)) for k in before_groups]
    full = lambda shape: pl.BlockSpec(shape, lambda s: (0,) * len(shape))
    params = (caw, cab, cbw, lbg, lbb, pw, pb, ps)
    host = _Hosted(plan, n_in=3 + len(before) + len(params), n_out=3, n_scratch=12)
    return pl.pallas_call(
        host.wrap(body, phase), name=name, grid=(n,),
        in_specs=[row(D_IN)] + before + [row(len(KEPT) * W_MIX), row(D_MIX)] + [full(a.shape) for a in params]
        + host.in_specs,
        out_specs=[row(D_IN), full((N_ROWS, W_MIX)), full(pw.shape)] + host.out_specs,
        out_shape=[jax.ShapeDtypeStruct((t, D_IN), BF16), jax.ShapeDtypeStruct((N_ROWS, W_MIX), F32),
                   jax.ShapeDtypeStruct(pw.shape, F32)] + host.out_shape,
        scratch_shapes=[pltpu.VMEM((2 * tt, W_MIX), F32)] * 5 + [pltpu.VMEM((tt, W_MIX), F32)] * 6
        + [pltpu.VMEM((N_ROWS * SUB, W_MIX), F32)] + host.scratch,
        compiler_params=_cparams(("arbitrary",)),
    )(h, *([h] * len(before)), u2, g_y, *params, *host.operands)


def _in_proj_wgrad(x, g_h, name, plan=None):
    t, d = x.shape
    n = g_h.shape[1]
    tk, tn = min(t, 1024), n // 2
    nk = t // tk

    def body(x_ref, g_ref, o_ref, acc):
        k = pl.program_id(1)

        @pl.when(k == 0)
        def _():
            acc[...] = jnp.zeros_like(acc)
        acc[...] += _dot_tn(x_ref[...].astype(BF16), g_ref[...])

        @pl.when(k == nk - 1)
        def _():
            o_ref[...] = acc[...].T.astype(BF16)

    def phase():
        step = pl.program_id(0) * nk + pl.program_id(1)
        return step == 0, step == nk, step == 2 * nk - 1

    host = _Hosted(plan, n_in=2, n_out=1, n_scratch=1)
    return pl.pallas_call(
        host.wrap(body, phase), name=name, grid=(n // tn, nk),
        in_specs=[pl.BlockSpec((tk, d), lambda j, k: (k, 0)), pl.BlockSpec((tk, tn), lambda j, k: (k, j))]
        + host.in_specs,
        out_specs=[pl.BlockSpec((tn, d), lambda j, k: (j, 0))] + host.out_specs,
        out_shape=[jax.ShapeDtypeStruct((n, d), BF16)] + host.out_shape,
        scratch_shapes=[pltpu.VMEM((d, tn), F32)] + host.scratch,
        compiler_params=_cparams(("arbitrary", "arbitrary")),
    )(x, g_h, *host.operands)


def _in_proj_dgrad(g_h, w, g_z, name, plan=None):
    t, n = g_h.shape
    d = w.shape[1]
    tm, tk = min(t, 1024), 1536
    nm, nk = t // tm, n // tk

    def body(g_ref, w_ref, gz_ref, o_ref):
        @pl.when(pl.program_id(1) == 0)
        def _():
            o_ref[...] = ALPHA * gz_ref[...]
        o_ref[...] += _dot(g_ref[...], w_ref[...])

    def phase():
        step = pl.program_id(0) * nk + pl.program_id(1)
        return step == 0, step == (nm * nk) // 2, step == nm * nk - 1

    host = _Hosted(plan, n_in=3, n_out=1, n_scratch=0)
    return pl.pallas_call(
        host.wrap(body, phase), name=name, grid=(nm, nk),
        in_specs=[pl.BlockSpec((tm, tk), lambda i, k: (i, k)), pl.BlockSpec((tk, d), lambda i, k: (k, 0)),
                  pl.BlockSpec((tm, d), lambda i, k: (i, 0))] + host.in_specs,
        out_specs=[pl.BlockSpec((tm, d), lambda i, k: (i, 0))] + host.out_specs,
        out_shape=[jax.ShapeDtypeStruct((t, d), F32)] + host.out_shape,
        scratch_shapes=host.scratch,
        compiler_params=_cparams(("arbitrary", "arbitrary")),
    )(g_h, w, g_z, *host.operands)


BC1 = 1.0 - ADAM_B1 ** ADAM_STEP
BC2 = 1.0 - ADAM_B2 ** ADAM_STEP


def _adamw_math(g, w, m, v):
    nm = ADAM_B1 * m + (1.0 - ADAM_B1) * g
    nv = ADAM_B2 * v + (1.0 - ADAM_B2) * (g * g)
    delta = -ADAM_LR * ((nm / BC1) / (jnp.sqrt(nv / BC2) + ADAM_EPS) + ADAM_WD * w)
    return delta, nm, nv


def _total(ref):
    g = ref[0].astype(F32)
    for k in range(1, ref.shape[0]):
        g = g + ref[k].astype(F32)
    return g


def _adamw_layers(parts, w, m, v, name):
    depth, r, c = w.shape
    p = parts[0].shape[0]
    tr = _row_tile(r)
    nr = r // tr

    def body(*refs):
        p_refs = refs[:depth]
        w_ref, m_ref, v_ref, g_ref, d_ref, nm_ref, nv_ref = refs[depth:]
        for l in range(depth):
            @pl.when(pl.program_id(0) == l)
            def _(l=l):
                g = _total(p_refs[l])
                delta, nm, nv = _adamw_math(g, w_ref[0], m_ref[0], v_ref[0])
                g_ref[0], d_ref[0], nm_ref[0], nv_ref[0] = g, delta, nm, nv

    def part_spec(l):
        return pl.BlockSpec((p, tr, c), lambda li, i: (0, jnp.where(li == l, i, jnp.where(li < l, 0, nr - 1)), 0))

    blk = pl.BlockSpec((1, tr, c), lambda li, i: (li, i, 0))
    out = jax.ShapeDtypeStruct((depth, r, c), F32)
    return pl.pallas_call(
        body, name=name, grid=(depth, nr),
        in_specs=[part_spec(l) for l in range(depth)] + [blk] * 3,
        out_specs=[blk] * 4, out_shape=[out] * 4,
        compiler_params=_cparams(("arbitrary", "arbitrary")),
    )(*parts, w, m, v)


SMALL_PARAMS = ("conv_a_w", "conv_a_b", "conv_b_w", "conv_b_b", "ln_b_g", "ln_b_b", "pool_w", "pool_b", "pool_scale",
                "ln_g", "ln_b")


def _adamw_small(rows_parts, gln_parts, gpw_parts, loss_parts, params, name):
    depth = len(rows_parts)
    cs = params["conv_a_w"][0].shape[2]
    operands = [*rows_parts, *gln_parts, *gpw_parts, loss_parts] + [a for n in SMALL_PARAMS for a in params[n]]
    n_in = len(operands)
    out_shape = [jax.ShapeDtypeStruct((1, 1), F32)]
    out_shape += [jax.ShapeDtypeStruct(params[n][0].shape, F32) for n in SMALL_PARAMS for _ in range(4)]

    def body(*refs):
        rows_p, gln_p, gpw_p = refs[:depth], refs[depth:2 * depth], refs[2 * depth:3 * depth]
        loss_p = refs[3 * depth]
        prm, outs = refs[3 * depth + 1:n_in], refs[n_in + 1:]
        refs[n_in][...] = jnp.sum(_total(loss_p)).reshape(1, 1)
        x, y, c = _position()
        to_front = (W_MIX - (4 * x + 2 * y + c) * cs) % W_MIX

        def update(name, g, at):
            k = SMALL_PARAMS.index(name)
            w_ref, m_ref, v_ref = prm[3 * k:3 * k + 3]
            g_ref, d_ref, nm_ref, nv_ref = outs[4 * k:4 * k + 4]
            delta, nm, nv = _adamw_math(g, w_ref[at], m_ref[at], v_ref[at])
            g_ref[at], d_ref[at], nm_ref[at], nv_ref[at] = g, delta, nm, nv

        for l in range(depth):
            rows = _total(rows_p[l])
            mine = pltpu.roll(rows, to_front, axis=1)
            gln = _total(gln_p[l])
            one = (slice(l, l + 1), slice(None))
            for name, r in (("conv_a_b", R_CAB), ("conv_b_b", R_CBB), ("ln_b_g", R_LBG), ("ln_b_b", R_LBB),
                            ("pool_scale", R_PS)):
                update(name, rows[r:r + 1, :], one)
            for g in range(len(POOL_WINDOWS)):
                update("pool_b", rows[R_PB:R_PB + 1, g * POOL_DIM:(g + 1) * POOL_DIM], (l, slice(g, g + 1), slice(None)))
            update("ln_g", gln[0:1, :], one)
            update("ln_b", gln[1:2, :], one)
            update("conv_a_w", mine[R_CAW:R_CAW + KA, 0:cs], (slice(None), l, slice(None)))
            update("conv_b_w", mine[R_CBW:R_CBW + KB, 0:cs], (slice(None), l, slice(None)))
            update("pool_w", _total(gpw_p[l]), (l,))

    vmem = pl.BlockSpec(memory_space=pltpu.VMEM)
    return pl.pallas_call(
        body, name=name, in_specs=[vmem] * n_in, out_specs=[vmem] * len(out_shape), out_shape=out_shape,
        compiler_params=pltpu.CompilerParams(vmem_limit_bytes=VMEM_LIMIT),
    )(*operands)


def kernel(x, w_in, conv_a_w, conv_a_b, conv_b_w, conv_b_b, ln_b_g, ln_b_b, pool_w, pool_b, pool_scale, w_out, ln_g, ln_b, loss_target, m_w_in, m_conv_a_w, m_conv_a_b, m_conv_b_w, m_conv_b_b, m_ln_b_g, m_ln_b_b, m_pool_w, m_pool_b, m_pool_scale, m_w_out, m_ln_g, m_ln_b, v_w_in, v_conv_a_w, v_conv_a_b, v_conv_b_w, v_conv_b_b, v_ln_b_g, v_ln_b_b, v_pool_w, v_pool_b, v_pool_scale, v_w_out, v_ln_g, v_ln_b):
    depth = w_in.shape[0]
    x0 = _to_segments(x[0])
    target = _to_segments(loss_target[0])
    r2 = lambda a: a.reshape(1, -1)

    tr = lambda a: jnp.swapaxes(a, 1, 2)
    w_in_t, m_w_in_t, v_w_in_t = tr(w_in), tr(m_w_in), tr(v_w_in)
    w_in_b, w_out_b = w_in_t.astype(BF16), w_out.astype(BF16)
    taps = lambda a: jnp.swapaxes(a, 0, 1)
    conv_sh = jnp.concatenate([taps(conv_a_w), taps(conv_b_w)], axis=0)
    full_in = lambda g: g.reshape(D_IN, D_MODEL)
    full_out = lambda g: g.reshape(D_MIX, D_MODEL)

    cx, cy = lax.axis_index("x"), lax.axis_index("y")
    chip_order = jnp.stack([2 * cx + cy, 2 * (1 - cx) + cy, 2 * cx + (1 - cy), 2 * (1 - cx) + (1 - cy)])
    h, gin, gout, gconv = _in_proj_first(x0, _Gather([w_in_b[0], w_out_b[0], conv_sh]),
                                         chip_order.astype(jnp.int32), "in_proj_0")
    w_in_f, w_out_f = [full_in(gin)], [full_out(gout)]
    conv_f = gconv.transpose(2, 1, 0, 3).reshape(depth, KA + KB, W_MIX)
    caw_f, cbw_f = conv_f[:, :KA], conv_f[:, KA:]
    xs, hs, ys, u2s, zs = [x0], [], [], [], []
    for l in range(depth):
        if l > 0:
            h = _in_proj(xs[l], w_in_f[l], f"in_proj_{l}")[0]
        last = l + 1 == depth
        y, u2, z, xn, *got = _mixer_fwd(
            h, xs[l], caw_f[l], r2(conv_a_b[l]), cbw_f[l], r2(conv_b_b[l]), r2(ln_b_g[l]), r2(ln_b_b[l]),
            pool_w[l], r2(pool_b[l]), r2(pool_scale[l]), w_out_f[l], r2(ln_g[l]), r2(ln_b[l]), f"mixer_fwd_{l}",
            plan=None if last else _Gather([w_in_b[l + 1], w_out_b[l + 1]]), target=target if last else None)
        if last:
            g, loss_rows = xn, got[0]
        else:
            w_in_f.append(full_in(got[0])), w_out_f.append(full_out(got[1]))
        hs.append(h), ys.append(y), u2s.append(u2), zs.append(z), xs.append(xn)


    gi_parts, go_parts, g_rows, g_lns, g_pool_w = ([None] * depth for _ in range(5))
    waiting = []
    for l in reversed(range(depth)):
        g_z, g_y, g_w_out, g_lns[l] = _out_proj_bwd(g, zs[l], ys[l], w_out_f[l].T, r2(ln_g[l]), f"out_proj_bwd_{l}")
        go = g_w_out.reshape(N_DEV, D_MIX // N_DEV, D_MODEL)
        if l == 0:
            waiting += [go]
        g_h, g_rows[l], g_pool_w[l], *got = _mixer_bwd(
            hs[l], u2s[l], g_y, caw_f[l], r2(conv_a_b[l]), cbw_f[l], r2(ln_b_g[l]), r2(ln_b_b[l]),
            pool_w[l], r2(pool_b[l]), r2(pool_scale[l]), f"mixer_bwd_{l}",
            plan=_Exchange(waiting) if waiting else None)
        if got:
            if l + 1 < depth:
                gi_parts[l + 1], go_parts[l + 1] = got[0], got[1]
            if l == 0:
                go_parts[0] = got[-1]
        if l > 0:
            gi = _in_proj_wgrad(xs[l], g_h, f"in_proj_wgrad_{l}")[0]
            waiting = [gi.reshape(N_DEV, D_IN // N_DEV, D_MODEL), go]
            g = _in_proj_dgrad(g_h, w_in_f[l], g_z, f"in_proj_dgrad_{l}")[0]
        else:
            small = [*g_rows, *g_lns, *g_pool_w, loss_rows]
            gi, *small_parts = _in_proj_wgrad(xs[0], g_h, "in_proj_wgrad_0", plan=_Gather(small))
            g, gi_parts[0] = _in_proj_dgrad(g_h, w_in_f[0], g_z, "in_proj_dgrad_0",
                                            plan=_Exchange([gi.reshape(N_DEV, D_IN // N_DEV, D_MODEL)]))
    grad_x = _from_segments(g)[None]

    out_in = [tr(a) for a in _adamw_layers(gi_parts, w_in_t, m_w_in_t, v_w_in_t, "adamw_w_in")]
    out_out = _adamw_layers(go_parts, w_out, m_w_out, v_w_out, "adamw_w_out")
    params = dict(
        conv_a_w=(taps(conv_a_w), taps(m_conv_a_w), taps(v_conv_a_w)), conv_a_b=(conv_a_b, m_conv_a_b, v_conv_a_b),
        conv_b_w=(taps(conv_b_w), taps(m_conv_b_w), taps(v_conv_b_w)), conv_b_b=(conv_b_b, m_conv_b_b, v_conv_b_b),
        ln_b_g=(ln_b_g, m_ln_b_g, v_ln_b_g), ln_b_b=(ln_b_b, m_ln_b_b, v_ln_b_b),
        pool_w=(pool_w, m_pool_w, v_pool_w), pool_b=(pool_b, m_pool_b, v_pool_b),
        pool_scale=(pool_scale, m_pool_scale, v_pool_scale), ln_g=(ln_g, m_ln_g, v_ln_g), ln_b=(ln_b, m_ln_b, v_ln_b))
    loss, *small_out = _adamw_small(small_parts[:depth], small_parts[depth:2 * depth], small_parts[2 * depth:3 * depth],
                                    small_parts[3 * depth], params, "adamw_small")
    small_out = {n: small_out[4 * k:4 * k + 4] for k, n in enumerate(SMALL_PARAMS)}
    for n in ("conv_a_w", "conv_b_w"):
        small_out[n] = [taps(a) for a in small_out[n]]

    order = ("w_in", "conv_a_w", "conv_a_b", "conv_b_w", "conv_b_b", "ln_b_g", "ln_b_b", "pool_w", "pool_b",
             "pool_scale", "w_out", "ln_g", "ln_b")
    outs = []
    for k in range(4):
        outs += [out_in[k] if n == "w_in" else out_out[k] if n == "w_out" else small_out[n][k] for n in order]
    return (loss.reshape(()), grad_x, *outs)
```

```python
import jax
import jax.numpy as jnp
from jax import lax
from jax.experimental import pallas as pl
from jax.experimental.pallas import tpu as pltpu

F32 = jnp.float32
BF16 = jnp.bfloat16

DEPTH = 2
D_MODEL = 1024
W_MIX = 512
D_IN = 9 * W_MIX
D_MIX = 3 * W_MIX
POOL_WINDOWS = (2, 4, 8, 16)
POOL_DIM = 128
KA = 3
KB = 31
ALPHA = (2.0 * DEPTH) ** 0.25
LN_EPS = 1e-5
ADAM_LR, ADAM_B1, ADAM_B2, ADAM_EPS, ADAM_WD, ADAM_STEP = 0.001, 0.9, 0.999, 1e-08, 0.01, 10

N_DEV = 8
MESH = pl.DeviceIdType.MESH

CH = 32
SUB = 8
VMEM_LIMIT = 56 * 1024 * 1024

R_CAB, R_CBB, R_LBG, R_LBB, R_PB, R_PS, R_CAW, R_CBW = 0, 1, 2, 3, 4, 5, 6, 9
N_ROWS = R_CBW + KB


def _cparams(sem, **kw):
    return pltpu.CompilerParams(dimension_semantics=sem, vmem_limit_bytes=VMEM_LIMIT, **kw)


def _sigmoid(v):
    return 1.0 / (1.0 + jnp.exp(-v))


def _fold8(a):
    r, c = a.shape
    return a.reshape(r // SUB, SUB, c).sum(axis=0)


def _dot(a, b):
    return jnp.dot(a, b, preferred_element_type=F32)


def _dot_nt(a, b):
    return lax.dot_general(a, b, (((1,), (1,)), ((), ())), preferred_element_type=F32)


def _dot_tn(a, b):
    return lax.dot_general(a, b, (((0,), (0,)), ((), ())), preferred_element_type=F32)


def _row_tile(r):
    for cand in (512, 256, 128, 64, 32, 16, 8):
        if r % cand == 0 and r > cand:
            return cand
    return r


def _position():
    return lax.axis_index("x"), lax.axis_index("y"), lax.axis_index("c")


class _Gather:
    def __init__(self, arrs):
        self.arrs = list(arrs)
        na = len(self.arrs)
        self.out_shape = [jax.ShapeDtypeStruct((N_DEV,) + a.shape, a.dtype) for a in self.arrs]
        self.scratch = [pltpu.SemaphoreType.DMA((na, 7)), pltpu.SemaphoreType.DMA((na, 7)),
                        pltpu.SemaphoreType.DMA((na,))]

    def _copies(self, src, dst, sems):
        send_sems, recv_sems, local_sems = sems
        na = len(self.arrs)
        x, y, c = _position()
        me, sibling = (x, y, c), (x, y, 1 - c)
        chips = [(1 - x, y), (x, 1 - y), (1 - x, 1 - y)]

        def slot(a, dev):
            return dst[a].at[4 * dev[0] + 2 * dev[1] + dev[2]]

        def copy(a, k, block, to, from_src=False):
            return pltpu.make_async_remote_copy(
                src_ref=src[a] if from_src else slot(a, block), dst_ref=slot(a, block),
                send_sem=send_sems.at[a, k], recv_sem=recv_sems.at[a, k], device_id=to, device_id_type=MESH)

        mine = [pltpu.make_async_copy(src[a], slot(a, me), local_sems.at[a]) for a in range(na)]
        first, landed, passed, last = [], [], [], []
        for a in range(na):
            first.append(copy(a, 0, me, sibling, from_src=True))
            first += [copy(a, 1 + j, me, (*chip, c), from_src=True) for j, chip in enumerate(chips)]
        for j, chip in enumerate(chips):
            for a in range(na):
                landed.append(copy(a, 1 + j, (*chip, c), me))
                passed.append(copy(a, 4 + j, (*chip, c), sibling))
        for a in range(na):
            last.append(copy(a, 0, sibling, me))
            last += [copy(a, 4 + j, (*chip, 1 - c), me) for j, chip in enumerate(chips)]
        return mine, first, landed, passed, last

    def start(self, src, dst, sems):
        mine, first, _, _, _ = self._copies(src, dst, sems)
        for cp in mine + first:
            cp.start()

    def pass_on(self, src, dst, sems):
        _, _, landed, passed, _ = self._copies(src, dst, sems)
        for got, cp in zip(landed, passed):
            got.wait_recv()
            cp.start()

    def finish(self, src, dst, sems):
        mine, first, _, passed, last = self._copies(src, dst, sems)
        for cp in last:
            cp.wait_recv()
        for cp in first + passed:
            cp.wait_send()
        for cp in mine:
            cp.wait()


class _Exchange:
    def __init__(self, arrs):
        self.arrs = list(arrs)
        na = len(self.arrs)
        self.out_shape = [jax.ShapeDtypeStruct(a.shape, a.dtype) for a in self.arrs]
        self.scratch = [pltpu.SemaphoreType.DMA((na, N_DEV - 1)), pltpu.SemaphoreType.DMA((na, N_DEV - 1)),
                        pltpu.SemaphoreType.DMA((na,))]

    def _copies(self, src, dst, sems):
        send_sems, recv_sems, local_sems = sems
        na = len(self.arrs)
        x, y, c = _position()
        me = 4 * x + 2 * y + c
        mine = [pltpu.make_async_copy(src[a].at[me], dst[a].at[me], local_sems.at[a]) for a in range(na)]
        copies = []
        for a in range(na):
            for k in range(N_DEV - 1):
                flip = k + 1
                px, py, pc = x ^ (flip >> 2), y ^ ((flip >> 1) & 1), c ^ (flip & 1)
                copies.append(pltpu.make_async_remote_copy(
                    src_ref=src[a].at[4 * px + 2 * py + pc], dst_ref=dst[a].at[me],
                    send_sem=send_sems.at[a, k], recv_sem=recv_sems.at[a, k], device_id=(px, py, pc),
                    device_id_type=MESH))
        return mine, copies

    def start(self, src, dst, sems):
        mine, copies = self._copies(src, dst, sems)
        for cp in mine + copies:
            cp.start()

    def pass_on(self, src, dst, sems):
        pass

    def finish(self, src, dst, sems):
        mine, copies = self._copies(src, dst, sems)
        for cp in copies:
            cp.wait()
        for cp in mine:
            cp.wait()


def _run_plan(plan, name):
    na = len(plan.arrs)
    any_spec = pl.BlockSpec(memory_space=pl.ANY)

    def body(*refs):
        src, dst, sems = refs[:na], refs[na:2 * na], refs[2 * na:]
        plan.start(src, dst, sems)
        plan.pass_on(src, dst, sems)
        plan.finish(src, dst, sems)

    return pl.pallas_call(
        body, name=name, in_specs=[any_spec] * na, out_specs=[any_spec] * na,
        out_shape=plan.out_shape, scratch_shapes=plan.scratch,
    )(*plan.arrs)


class _Hosted:
    def __init__(self, plan, n_in, n_out, n_scratch):
        self.plan, self.n_in, self.n_out, self.n_scratch = plan, n_in, n_out, n_scratch
        any_spec = pl.BlockSpec(memory_space=pl.ANY)
        k = 0 if plan is None else len(plan.arrs)
        self.operands = [] if plan is None else plan.arrs
        self.in_specs = [any_spec] * k
        self.out_specs = [any_spec] * k
        self.out_shape = [] if plan is None else plan.out_shape
        self.scratch = [] if plan is None else plan.scratch

    def wrap(self, body, phase):
        if self.plan is None:
            return body
        plan, k = self.plan, len(self.plan.arrs)
        i0, o0 = self.n_in, self.n_in + k
        o1 = o0 + self.n_out
        s0 = o1 + k
        s1 = s0 + self.n_scratch

        def hosted(*refs):
            src, dst, sems = refs[i0:o0], refs[o1:s0], refs[s1:]
            first, middle, last = phase()
            pl.when(first)(lambda: plan.start(src, dst, sems))
            body(*refs[:i0], *refs[o0:o1], *refs[s0:s1])
            pl.when(middle)(lambda: plan.pass_on(src, dst, sems))
            pl.when(last)(lambda: plan.finish(src, dst, sems))

        return hosted


TT = 256
SEG = TT // SUB


def _to_segments(a):
    t, c = a.shape
    return a.reshape(t // TT, SUB, SEG, c).transpose(0, 2, 1, 3).reshape(t, c)


def _from_segments(a):
    t, c = a.shape
    return a.reshape(t // TT, SEG, SUB, c).transpose(0, 2, 1, 3).reshape(t, c)


def _sublane_is(s):
    return lax.broadcasted_iota(jnp.int32, (TT, W_MIX), 0) % SUB == s


def _look_back(ext_ref, cur, before):
    ext_ref[TT:, :] = cur
    ext_ref[0:TT, :] = jnp.where(_sublane_is(0), before, pltpu.roll(cur, 1, axis=0))


def _last_segment(cur):
    return pltpu.roll(cur, TT - (SUB - 1), axis=0)


def _look_ahead(ext_ref, cur, after):
    ext_ref[0:TT, :] = cur
    ext_ref[TT:, :] = jnp.where(_sublane_is(SUB - 1), after, pltpu.roll(cur, TT - 1, axis=0))


def _first_segment(cur):
    return pltpu.roll(cur, SUB - 1, axis=0)


def _tap_loop(body):
    lax.fori_loop(0, TT // CH, lambda c, carry: body(pl.multiple_of(c * CH, CH), carry), 0)


def _conv_rows(ext_ref, w_ref, out_ref, *, nk, off, reverse):
    def body(r0, carry):
        acc = jnp.zeros((CH, W_MIX), F32)
        for k in range(nk):
            kk = nk - 1 - k if reverse else k
            acc = acc + ext_ref[pl.ds(r0 + (off + k) * SUB, CH), :] * w_ref[kk:kk + 1, :]
        out_ref[pl.ds(r0, CH), :] = acc
        return carry
    _tap_loop(body)


def _conv_wgrad(g_ref, ext_ref, acc_ref, *, nk, off, row0):
    def body(r0, carry):
        g = g_ref[pl.ds(r0, CH), :]
        for k in range(nk):
            a = (row0 + k) * SUB
            acc_ref[a:a + SUB, :] += _fold8(g * ext_ref[pl.ds(r0 + (off + k) * SUB, CH), :])
        return carry
    _tap_loop(body)


def _window_sums(ext_ref, out_ref, *, forward):
    def body(r0, carry):
        for g, w in enumerate(POOL_WINDOWS):
            lanes = slice(g * POOL_DIM, (g + 1) * POOL_DIM)
            acc = jnp.zeros((CH, POOL_DIM), F32)
            for j in range(w):
                off = j if forward else SEG - j
                acc = acc + ext_ref[pl.ds(r0 + off * SUB, CH), lanes]
            out_ref[pl.ds(r0, CH), lanes] = acc
        return carry
    _tap_loop(body)


def _inv_count(tile):
    r = lax.broadcasted_iota(jnp.int32, (TT, POOL_DIM), 0)
    t1 = (tile * TT + (r % SUB) * SEG + r // SUB + 1).astype(F32)
    return jnp.concatenate([1.0 / jnp.minimum(t1, float(w)) for w in POOL_WINDOWS], axis=1)


KEPT = ("u2", "sigmoid(a_z)", "ca", "sigmoid(b_g)", "sigmoid(ln)", "sigmoid(b_z)", "sigmoid(c_z)", "pooled", "p")


def _groups(h_ref):
    return [h_ref[:, k * W_MIX:(k + 1) * W_MIX].astype(F32) for k in range(9)]


def _layer_norm(v, g, b):
    mu = jnp.mean(v, axis=-1, keepdims=True)
    vc = v - mu
    var = jnp.mean(vc * vc, axis=-1, keepdims=True)
    rstd = lax.rsqrt(var + LN_EPS)
    vhat = vc * rstd
    return vhat * g + b, vhat, rstd


def _layer_norm_bwd(g_out, vhat, rstd, g):
    gh = g_out * g
    m1 = jnp.mean(gh, axis=-1, keepdims=True)
    m2 = jnp.mean(gh * vhat, axis=-1, keepdims=True)
    return rstd * (gh - m1 - vhat * m2)


def _pool_linear(pooled, pw_ref, pb_ref):
    outs = []
    for g in range(len(POOL_WINDOWS)):
        lanes = slice(g * POOL_DIM, (g + 1) * POOL_DIM)
        outs.append(_dot(pooled[:, lanes].astype(BF16), pw_ref[g].astype(BF16)))
    return jnp.concatenate(outs, axis=1) + pb_ref[...]


def _in_proj(x, w, name, plan=None):
    t, d = x.shape
    n = w.shape[0]
    tm, tn = min(t, 1024), 2304
    nm, nn = t // tm, n // tn

    def body(x_ref, w_ref, o_ref, xb_ref):
        @pl.when(pl.program_id(1) == 0)
        def _():
            xb_ref[...] = x_ref[...].astype(BF16)
        o_ref[...] = _dot_nt(xb_ref[...], w_ref[...]).astype(BF16)

    def phase():
        step = pl.program_id(0) * nn + pl.program_id(1)
        return step == 0, step == (nm * nn) // 2, step == nm * nn - 1

    host = _Hosted(plan, n_in=2, n_out=1, n_scratch=1)
    return pl.pallas_call(
        host.wrap(body, phase), name=name, grid=(nm, nn),
        in_specs=[pl.BlockSpec((tm, d), lambda i, j: (i, 0)), pl.BlockSpec((tn, d), lambda i, j: (j, 0))]
        + host.in_specs,
        out_specs=[pl.BlockSpec((tm, tn), lambda i, j: (i, j))] + host.out_specs,
        out_shape=[jax.ShapeDtypeStruct((t, n), BF16)] + host.out_shape,
        scratch_shapes=[pltpu.VMEM((tm, d), BF16)] + host.scratch,
        compiler_params=_cparams(("arbitrary", "arbitrary")),
    )(x, w, *host.operands)


def _mixer_fwd(h, x, caw, cab, cbw, cbb, lbg, lbb, pw, pb, ps, w_out, lng, lnb, name, plan=None, target=None):
    t = h.shape[0]
    tt = TT
    n = t // tt
    with_loss = target is not None

    def body(h_ref, x_ref, *refs):
        if with_loss:
            t_ref, refs = refs[0], refs[1:]
        (caw_ref, cab_ref, cbw_ref, cbb_ref, lbg_ref, lbb_ref, pw_ref, pb_ref, ps_ref, wo_ref, lng_ref, lnb_ref,
         y_ref, u2_ref, z_ref, xn_ref) = refs[:16]
        refs = refs[16:]
        if with_loss:
            l_ref, refs = refs[0], refs[1:]
        exta, extb, extc, lasta, lastb, lastc, tmp, inv_ref = refs
        i = pl.program_id(0)

        @pl.when(i == 0)
        def _():
            for e in (lasta, lastb, lastc):
                e[...] = jnp.zeros_like(e)
            if with_loss:
                l_ref[...] = jnp.zeros_like(l_ref)

        @pl.when(i <= 1)
        def _():
            inv_ref[...] = _inv_count(i)

        group = lambda k: h_ref[:, k * W_MIX:(k + 1) * W_MIX].astype(F32)

        def keep(name, v):
            k = KEPT.index(name)
            u2_ref[:, k * W_MIX:(k + 1) * W_MIX] = v

        def look_back(ext, last, cur):
            _look_back(ext, cur, last[...])
            last[...] = _last_segment(cur)

        sgg = _sigmoid(group(5))
        keep("sigmoid(b_g)", sgg)
        look_back(extb, lastb, group(4) * sgg)
        _conv_rows(extb, cbw_ref, tmp, nk=KB, off=SEG - (KB - 1), reverse=False)
        u2 = tmp[...] + cbb_ref[...]
        keep("u2", u2)
        ln, _, _ = _layer_norm(u2, lbg_ref[...], lbb_ref[...])
        b_z = group(6)
        sl, sz = _sigmoid(ln), _sigmoid(b_z)
        keep("sigmoid(ln)", sl), keep("sigmoid(b_z)", sz)
        y_ref[:, W_MIX:2 * W_MIX] = ((ln * sl) * (b_z * sz)).astype(BF16)

        look_back(exta, lasta, group(1) * group(2))
        _conv_rows(exta, caw_ref, tmp, nk=KA, off=SEG - (KA - 1), reverse=False)
        ca = tmp[...] + cab_ref[...]
        a_z = group(3)
        sga = _sigmoid(a_z)
        keep("ca", ca), keep("sigmoid(a_z)", sga)
        y_ref[:, 0:W_MIX] = (group(0) * ca * (a_z * sga)).astype(BF16)

        c_u = group(7)
        look_back(extc, lastc, c_u)
        _window_sums(extc, tmp, forward=False)
        pooled = tmp[...] * inv_ref[...] - c_u
        p = _pool_linear(pooled, pw_ref, pb_ref)
        c_z = group(8)
        sc = _sigmoid(c_z)
        keep("pooled", pooled), keep("p", p), keep("sigmoid(c_z)", sc)
        y_ref[:, 2 * W_MIX:3 * W_MIX] = (p * ps_ref[...] * (c_z * sc)).astype(BF16)

        out = _dot(y_ref[...], wo_ref[...])
        z = ALPHA * x_ref[...] + out
        z_ref[...] = z
        xn, _, _ = _layer_norm(z, lng_ref[...], lnb_ref[...])
        if with_loss:
            e = xn - t_ref[...]
            xn_ref[...] = e * (1.0 / D_MODEL)
            l_ref[...] += _fold8(e * e) * (0.5 / D_MODEL)
        else:
            xn_ref[...] = xn

    def phase():
        i = pl.program_id(0)
        return i == 0, i == n // 2, i == n - 1

    row = lambda wd: pl.BlockSpec((tt, wd), lambda i: (i, 0))
    full = lambda a: pl.BlockSpec(a.shape, lambda i: (0,) * a.ndim)
    params = (caw, cab, cbw, cbb, lbg, lbb, pw, pb, ps, w_out, lng, lnb)
    extra_in = [target] if with_loss else []
    extra_out = [jax.ShapeDtypeStruct((SUB, D_MODEL), F32)] if with_loss else []
    host = _Hosted(plan, n_in=2 + len(extra_in) + len(params), n_out=4 + len(extra_out), n_scratch=8)
    return pl.pallas_call(
        host.wrap(body, phase), name=name, grid=(n,),
        in_specs=[row(D_IN), row(D_MODEL)] + [row(D_MODEL)] * len(extra_in) + [full(a) for a in params]
        + host.in_specs,
        out_specs=[row(D_MIX), row(len(KEPT) * W_MIX), row(D_MODEL), row(D_MODEL)] + [full(o) for o in extra_out]
        + host.out_specs,
        out_shape=[jax.ShapeDtypeStruct((t, D_MIX), BF16), jax.ShapeDtypeStruct((t, len(KEPT) * W_MIX), F32),
                   jax.ShapeDtypeStruct((t, D_MODEL), F32), jax.ShapeDtypeStruct((t, D_MODEL), F32)]
        + extra_out + host.out_shape,
        scratch_shapes=[pltpu.VMEM((2 * tt, W_MIX), F32)] * 3 + [pltpu.VMEM((tt, W_MIX), F32)] * 5 + host.scratch,
        compiler_params=_cparams(("arbitrary",)),
    )(h, x, *extra_in, *params, *host.operands)


def _out_proj_bwd(g_xn, z, y, w_out_t, lng, name):
    t = z.shape[0]
    tt = min(t, 512)
    n = t // tt

    def body(g_ref, z_ref, y_ref, wo_ref, lng_ref, gz_ref, gy_ref, gwo_ref, gln_ref, accg, accb, accw, gzb):
        i = pl.program_id(0)

        @pl.when(i == 0)
        def _():
            accw[...] = jnp.zeros_like(accw)
            accg[...] = jnp.zeros_like(accg)
            accb[...] = jnp.zeros_like(accb)
            gzb[...] = jnp.zeros_like(gzb)

        before = gzb[(i + 1) % 2]
        gy_ref[...] = _dot(before, wo_ref[...])
        accw[...] += _dot_tn(y_ref[...], before)

        counts = (i < n).astype(F32)
        g = g_ref[...]
        _, zhat, rstd = _layer_norm(z_ref[...], lng_ref[...], 0.0)
        accg[...] += _fold8(g * zhat) * counts
        accb[...] += _fold8(g) * counts
        g_z = _layer_norm_bwd(g, zhat, rstd, lng_ref[...])
        gz_ref[...] = g_z
        gzb[i % 2] = g_z.astype(BF16)

        @pl.when(i == n)
        def _():
            gwo_ref[...] = accw[...].astype(BF16)
            gln_ref[...] = jnp.zeros_like(gln_ref)
            gln_ref[0:1, :] = jnp.sum(accg[...], axis=0, keepdims=True)
            gln_ref[1:2, :] = jnp.sum(accb[...], axis=0, keepdims=True)

    this = lambda wd: pl.BlockSpec((tt, wd), lambda i: (jnp.minimum(i, n - 1), 0))
    last = lambda wd: pl.BlockSpec((tt, wd), lambda i: (jnp.maximum(i - 1, 0), 0))
    full = lambda shape: pl.BlockSpec(shape, lambda i: (0,) * len(shape))
    return pl.pallas_call(
        body, name=name, grid=(n + 1,),
        in_specs=[this(D_MODEL), this(D_MODEL), last(D_MIX), full(w_out_t.shape), full(lng.shape)],
        out_specs=[this(D_MODEL), last(D_MIX), full((D_MIX, D_MODEL)), full((SUB, D_MODEL))],
        out_shape=[jax.ShapeDtypeStruct((t, D_MODEL), F32), jax.ShapeDtypeStruct((t, D_MIX), F32),
                   jax.ShapeDtypeStruct((D_MIX, D_MODEL), BF16), jax.ShapeDtypeStruct((SUB, D_MODEL), F32)],
        scratch_shapes=[pltpu.VMEM((SUB, D_MODEL), F32)] * 2 + [pltpu.VMEM((D_MIX, D_MODEL), F32),
                                                                 pltpu.VMEM((2, tt, D_MODEL), BF16)],
        compiler_params=_cparams(("arbitrary",)),
    )(g_xn, z, y, w_out_t, lng)


def _mixer_bwd(h, u2, g_y, caw, cbw, lbg, lbb, pw, ps, name, plan=None):
    t = h.shape[0]
    tt = TT
    n = t // tt
    before_groups = (1, 2, 4, 5)

    def body(h_ref, p_cg, p_av, p_bv, p_bg, u2_ref, gy_ref, caw_ref, cbw_ref, lbg_ref, lbb_ref,
             pw_ref, ps_ref, gh_ref, rows_ref, gpw_ref,
             exta, extb, gca, gu2, qx, nexta, nextb, nextc, tmp, tmp2, inv_ref, acc):
        s = pl.program_id(0)
        i = n - 1 - s

        @pl.when(s == 0)
        def _():
            acc[...] = jnp.zeros_like(acc)
            gpw_ref[...] = jnp.zeros_like(gpw_ref)
            for e in (nexta, nextb, nextc):
                e[...] = jnp.zeros_like(e)

        live = (i > 0).astype(F32)
        f32 = lambda ref: ref[...].astype(F32)
        before_a = _last_segment(f32(p_cg) * f32(p_av)) * live
        before_b = _last_segment(f32(p_bv) * _sigmoid(f32(p_bg))) * live

        a_bg, a_cg, a_v, a_z, b_v, b_g, b_z, c_u, c_z = _groups(h_ref)
        u2, sg, ca, sgg, sl, sz, sc, pooled, p = (u2_ref[:, k * W_MIX:(k + 1) * W_MIX] for k in range(len(KEPT)))
        g_ya = gy_ref[:, 0:W_MIX]
        g_yb = gy_ref[:, W_MIX:2 * W_MIX]
        g_yc = gy_ref[:, 2 * W_MIX:3 * W_MIX]

        def add_row(r, v):
            acc[r * SUB:(r + 1) * SUB, :] += _fold8(v)

        _look_back(exta, a_cg * a_v, before_a)
        s_az = a_z * sg
        t_a = g_ya * a_bg
        gh_ref[:, 0:W_MIX] = (g_ya * ca * s_az).astype(BF16)
        gh_ref[:, 3 * W_MIX:4 * W_MIX] = (t_a * ca * (sg * (1.0 + a_z * (1.0 - sg)))).astype(BF16)
        g_ca = t_a * s_az
        _look_ahead(gca, g_ca, nexta[...])
        nexta[...] = _first_segment(g_ca)
        add_row(R_CAB, g_ca)
        _conv_wgrad(gca, exta, acc, nk=KA, off=SEG - (KA - 1), row0=R_CAW)
        _conv_rows(gca, caw_ref, tmp, nk=KA, off=0, reverse=True)
        g_pa = tmp[...]
        gh_ref[:, W_MIX:2 * W_MIX] = (g_pa * a_v).astype(BF16)
        gh_ref[:, 2 * W_MIX:3 * W_MIX] = (g_pa * a_cg).astype(BF16)

        _look_back(extb, b_v * sgg, before_b)
        ln, u2hat, rstd = _layer_norm(u2, lbg_ref[...], lbb_ref[...])
        u3 = ln * sl
        s_bz = b_z * sz
        gh_ref[:, 6 * W_MIX:7 * W_MIX] = (g_yb * u3 * (sz * (1.0 + b_z * (1.0 - sz)))).astype(BF16)
        g_ln = g_yb * s_bz * (sl * (1.0 + ln * (1.0 - sl)))
        add_row(R_LBG, g_ln * u2hat)
        add_row(R_LBB, g_ln)
        g_u2 = _layer_norm_bwd(g_ln, u2hat, rstd, lbg_ref[...])
        _look_ahead(gu2, g_u2, nextb[...])
        nextb[...] = _first_segment(g_u2)
        add_row(R_CBB, g_u2)
        _conv_wgrad(gu2, extb, acc, nk=KB, off=SEG - (KB - 1), row0=R_CBW)
        _conv_rows(gu2, cbw_ref, tmp, nk=KB, off=0, reverse=True)
        g_u1 = tmp[...]
        gh_ref[:, 4 * W_MIX:5 * W_MIX] = (g_u1 * sgg).astype(BF16)
        gh_ref[:, 5 * W_MIX:6 * W_MIX] = (g_u1 * b_v * sgg * (1.0 - sgg)).astype(BF16)

        @pl.when((s == 0) | (i == 0))
        def _():
            inv_ref[...] = _inv_count(i)
        inv = inv_ref[...]
        s_cz = c_z * sc
        scale = ps_ref[...]
        gh_ref[:, 8 * W_MIX:9 * W_MIX] = (g_yc * p * scale * (sc * (1.0 + c_z * (1.0 - sc)))).astype(BF16)
        t_c = g_yc * s_cz
        add_row(R_PS, t_c * p)
        g_p = t_c * scale
        add_row(R_PB, g_p)
        g_pooled = []
        for g in range(len(POOL_WINDOWS)):
            lanes = slice(g * POOL_DIM, (g + 1) * POOL_DIM)
            gpg = g_p[:, lanes].astype(BF16)
            gpw_ref[g] += _dot_tn(pooled[:, lanes].astype(BF16), gpg)
            g_pooled.append(_dot_nt(gpg, pw_ref[g].astype(BF16)))
        g_pooled = jnp.concatenate(g_pooled, axis=1)
        q = g_pooled * inv
        _look_ahead(qx, q, nextc[...])
        nextc[...] = _first_segment(q)
        _window_sums(qx, tmp2, forward=True)
        gh_ref[:, 7 * W_MIX:8 * W_MIX] = (tmp2[...] - g_pooled).astype(BF16)

        @pl.when(s == n - 1)
        def _():
            for r in range(N_ROWS):
                rows_ref[r:r + 1, :] = jnp.sum(acc[r * SUB:(r + 1) * SUB, :], axis=0, keepdims=True)

    def phase():
        s = pl.program_id(0)
        return s == 0, s == n // 2, s == n - 1

    row = lambda wd: pl.BlockSpec((tt, wd), lambda s: (n - 1 - s, 0))
    before = [pl.BlockSpec((tt, W_MIX), lambda s, k=k: (jnp.maximum(n - 2 - s, 0), k)) for k in before_groups]
    full = lambda shape: pl.BlockSpec(shape, lambda s: (0,) * len(shape))
    params = (caw, cbw, lbg, lbb, pw, ps)
    host = _Hosted(plan, n_in=3 + len(before) + len(params), n_out=3, n_scratch=12)
    return pl.pallas_call(
        host.wrap(body, phase), name=name, grid=(n,),
        in_specs=[row(D_IN)] + before + [row(len(KEPT) * W_MIX), row(D_MIX)] + [full(a.shape) for a in params]
        + host.in_specs,
        out_specs=[row(D_IN), full((N_ROWS, W_MIX)), full(pw.shape)] + host.out_specs,
        out_shape=[jax.ShapeDtypeStruct((t, D_IN), BF16), jax.ShapeDtypeStruct((N_ROWS, W_MIX), F32),
                   jax.ShapeDtypeStruct(pw.shape, F32)] + host.out_shape,
        scratch_shapes=[pltpu.VMEM((2 * tt, W_MIX), F32)] * 5 + [pltpu.VMEM((tt, W_MIX), F32)] * 6
        + [pltpu.VMEM((N_ROWS * SUB, W_MIX), F32)] + host.scratch,
        compiler_params=_cparams(("arbitrary",)),
    )(h, *([h] * len(before)), u2, g_y, *params, *host.operands)


def _in_proj_wgrad(x, g_h, name, plan=None):
    t, d = x.shape
    n = g_h.shape[1]
    tk, tn = min(t, 1024), n // 2
    nk = t // tk

    def body(x_ref, g_ref, o_ref, acc):
        k = pl.program_id(1)

        @pl.when(k == 0)
        def _():
            acc[...] = jnp.zeros_like(acc)
        acc[...] += _dot_tn(x_ref[...].astype(BF16), g_ref[...])

        @pl.when(k == nk - 1)
        def _():
            o_ref[...] = acc[...].T.astype(BF16)

    def phase():
        step = pl.program_id(0) * nk + pl.program_id(1)
        return step == 0, step == nk, step == 2 * nk - 1

    host = _Hosted(plan, n_in=2, n_out=1, n_scratch=1)
    return pl.pallas_call(
        host.wrap(body, phase), name=name, grid=(n // tn, nk),
        in_specs=[pl.BlockSpec((tk, d), lambda j, k: (k, 0)), pl.BlockSpec((tk, tn), lambda j, k: (k, j))]
        + host.in_specs,
        out_specs=[pl.BlockSpec((tn, d), lambda j, k: (j, 0))] + host.out_specs,
        out_shape=[jax.ShapeDtypeStruct((n, d), BF16)] + host.out_shape,
        scratch_shapes=[pltpu.VMEM((d, tn), F32)] + host.scratch,
        compiler_params=_cparams(("arbitrary", "arbitrary")),
    )(x, g_h, *host.operands)


def _in_proj_dgrad(g_h, w, g_z, name, plan=None):
    t, n = g_h.shape
    d = w.shape[1]
    tm, tk = min(t, 1024), 2304
    nm, nk = t // tm, n // tk

    def body(g_ref, w_ref, gz_ref, o_ref):
        @pl.when(pl.program_id(1) == 0)
        def _():
            o_ref[...] = ALPHA * gz_ref[...]
        o_ref[...] += _dot(g_ref[...], w_ref[...])

    def phase():
        step = pl.program_id(0) * nk + pl.program_id(1)
        return step == 0, step == (nm * nk) // 2, step == nm * nk - 1

    host = _Hosted(plan, n_in=3, n_out=1, n_scratch=0)
    return pl.pallas_call(
        host.wrap(body, phase), name=name, grid=(nm, nk),
        in_specs=[pl.BlockSpec((tm, tk), lambda i, k: (i, k)), pl.BlockSpec((tk, d), lambda i, k: (k, 0)),
                  pl.BlockSpec((tm, d), lambda i, k: (i, 0))] + host.in_specs,
        out_specs=[pl.BlockSpec((tm, d), lambda i, k: (i, 0))] + host.out_specs,
        out_shape=[jax.ShapeDtypeStruct((t, d), F32)] + host.out_shape,
        scratch_shapes=host.scratch,
        compiler_params=_cparams(("arbitrary", "arbitrary")),
    )(g_h, w, g_z, *host.operands)


BC1 = 1.0 - ADAM_B1 ** ADAM_STEP
BC2 = 1.0 - ADAM_B2 ** ADAM_STEP


def _adamw_math(g, w, m, v):
    nm = ADAM_B1 * m + (1.0 - ADAM_B1) * g
    nv = ADAM_B2 * v + (1.0 - ADAM_B2) * (g * g)
    delta = -ADAM_LR * ((nm / BC1) / (jnp.sqrt(nv / BC2) + ADAM_EPS) + ADAM_WD * w)
    return delta, nm, nv


def _total(ref):
    g = ref[0].astype(F32)
    for k in range(1, ref.shape[0]):
        g = g + ref[k].astype(F32)
    return g


def _adamw_layers(parts, w, m, v, name):
    depth, r, c = w.shape
    p = parts[0].shape[0]
    tr = _row_tile(r)
    nr = r // tr

    def body(*refs):
        p_refs = refs[:depth]
        w_ref, m_ref, v_ref, g_ref, d_ref, nm_ref, nv_ref = refs[depth:]
        for l in range(depth):
            @pl.when(pl.program_id(0) == l)
            def _(l=l):
                g = _total(p_refs[l])
                delta, nm, nv = _adamw_math(g, w_ref[0], m_ref[0], v_ref[0])
                g_ref[0], d_ref[0], nm_ref[0], nv_ref[0] = g, delta, nm, nv

    def part_spec(l):
        return pl.BlockSpec((p, tr, c), lambda li, i: (0, jnp.where(li == l, i, jnp.where(li < l, 0, nr - 1)), 0))

    blk = pl.BlockSpec((1, tr, c), lambda li, i: (li, i, 0))
    out = jax.ShapeDtypeStruct((depth, r, c), F32)
    return pl.pallas_call(
        body, name=name, grid=(depth, nr),
        in_specs=[part_spec(l) for l in range(depth)] + [blk] * 3,
        out_specs=[blk] * 4, out_shape=[out] * 4,
        compiler_params=_cparams(("arbitrary", "arbitrary")),
    )(*parts, w, m, v)


SMALL_PARAMS = ("conv_a_w", "conv_a_b", "conv_b_w", "conv_b_b", "ln_b_g", "ln_b_b", "pool_w", "pool_b", "pool_scale",
                "ln_g", "ln_b")


def _adamw_small(rows_parts, gln_parts, gpw_parts, loss_parts, params, name):
    depth = len(rows_parts)
    cs = params["conv_a_w"][0].shape[2]
    operands = [*rows_parts, *gln_parts, *gpw_parts, loss_parts] + [a for n in SMALL_PARAMS for a in params[n]]
    n_in = len(operands)
    out_shape = [jax.ShapeDtypeStruct((1, 1), F32)]
    out_shape += [jax.ShapeDtypeStruct(params[n][0].shape, F32) for n in SMALL_PARAMS for _ in range(4)]

    def body(*refs):
        rows_p, gln_p, gpw_p = refs[:depth], refs[depth:2 * depth], refs[2 * depth:3 * depth]
        loss_p = refs[3 * depth]
        prm, outs = refs[3 * depth + 1:n_in], refs[n_in + 1:]
        refs[n_in][...] = jnp.sum(_total(loss_p)).reshape(1, 1)
        x, y, c = _position()
        to_front = (W_MIX - (4 * x + 2 * y + c) * cs) % W_MIX

        def update(name, g, at):
            k = SMALL_PARAMS.index(name)
            w_ref, m_ref, v_ref = prm[3 * k:3 * k + 3]
            g_ref, d_ref, nm_ref, nv_ref = outs[4 * k:4 * k + 4]
            delta, nm, nv = _adamw_math(g, w_ref[at], m_ref[at], v_ref[at])
            g_ref[at], d_ref[at], nm_ref[at], nv_ref[at] = g, delta, nm, nv

        for l in range(depth):
            rows = _total(rows_p[l])
            mine = pltpu.roll(rows, to_front, axis=1)
            gln = _total(gln_p[l])
            one = (slice(l, l + 1), slice(None))
            for name, r in (("conv_a_b", R_CAB), ("conv_b_b", R_CBB), ("ln_b_g", R_LBG), ("ln_b_b", R_LBB),
                            ("pool_scale", R_PS)):
                update(name, rows[r:r + 1, :], one)
            for g in range(len(POOL_WINDOWS)):
                update("pool_b", rows[R_PB:R_PB + 1, g * POOL_DIM:(g + 1) * POOL_DIM], (l, slice(g, g + 1), slice(None)))
            update("ln_g", gln[0:1, :], one)
            update("ln_b", gln[1:2, :], one)
            update("conv_a_w", mine[R_CAW:R_CAW + KA, 0:cs], (slice(None), l, slice(None)))
            update("conv_b_w", mine[R_CBW:R_CBW + KB, 0:cs], (slice(None), l, slice(None)))
            update("pool_w", _total(gpw_p[l]), (l,))

    vmem = pl.BlockSpec(memory_space=pltpu.VMEM)
    return pl.pallas_call(
        body, name=name, in_specs=[vmem] * n_in, out_specs=[vmem] * len(out_shape), out_shape=out_shape,
        compiler_params=pltpu.CompilerParams(vmem_limit_bytes=VMEM_LIMIT),
    )(*operands)


def kernel(x, w_in, conv_a_w, conv_a_b, conv_b_w, conv_b_b, ln_b_g, ln_b_b, pool_w, pool_b, pool_scale, w_out, ln_g, ln_b, loss_target, m_w_in, m_conv_a_w, m_conv_a_b, m_conv_b_w, m_conv_b_b, m_ln_b_g, m_ln_b_b, m_pool_w, m_pool_b, m_pool_scale, m_w_out, m_ln_g, m_ln_b, v_w_in, v_conv_a_w, v_conv_a_b, v_conv_b_w, v_conv_b_b, v_ln_b_g, v_ln_b_b, v_pool_w, v_pool_b, v_pool_scale, v_w_out, v_ln_g, v_ln_b):
    depth = w_in.shape[0]
    x0 = _to_segments(x[0])
    target = _to_segments(loss_target[0])
    r2 = lambda a: a.reshape(1, -1)

    tr = lambda a: jnp.swapaxes(a, 1, 2)
    w_in_t, m_w_in_t, v_w_in_t = tr(w_in), tr(m_w_in), tr(v_w_in)
    w_in_b, w_out_b = w_in_t.astype(BF16), w_out.astype(BF16)
    taps = lambda a: jnp.swapaxes(a, 0, 1)
    conv_sh = jnp.concatenate([taps(conv_a_w), taps(conv_b_w)], axis=0)
    full_in = lambda g: g.reshape(D_IN, D_MODEL)
    full_out = lambda g: g.reshape(D_MIX, D_MODEL)
    w_in_f = [full_in(_run_plan(_Gather([w_in_b[0]]), "gather_w_in_0")[0])]
    w_out_f = []

    xs, hs, ys, u2s, zs = [x0], [], [], [], []
    for l in range(depth):
        h, *got = _in_proj(xs[l], w_in_f[l], f"in_proj_{l}",
                           plan=_Gather([w_out_b[0], conv_sh]) if l == 0 else None)
        if got:
            w_out_f.append(full_out(got[0]))
            conv_f = got[1].transpose(2, 1, 0, 3).reshape(depth, KA + KB, W_MIX)
            caw_f, cbw_f = conv_f[:, :KA], conv_f[:, KA:]
        last = l + 1 == depth
        y, u2, z, xn, *got = _mixer_fwd(
            h, xs[l], caw_f[l], r2(conv_a_b[l]), cbw_f[l], r2(conv_b_b[l]), r2(ln_b_g[l]), r2(ln_b_b[l]),
            pool_w[l], r2(pool_b[l]), r2(pool_scale[l]), w_out_f[l], r2(ln_g[l]), r2(ln_b[l]), f"mixer_fwd_{l}",
            plan=None if last else _Gather([w_in_b[l + 1], w_out_b[l + 1]]), target=target if last else None)
        if last:
            g, loss_rows = xn, got[0]
        else:
            w_in_f.append(full_in(got[0])), w_out_f.append(full_out(got[1]))
        hs.append(h), ys.append(y), u2s.append(u2), zs.append(z), xs.append(xn)


    gi_parts, go_parts, g_rows, g_lns, g_pool_w = ([None] * depth for _ in range(5))
    waiting = []
    for l in reversed(range(depth)):
        g_z, g_y, g_w_out, g_lns[l] = _out_proj_bwd(g, zs[l], ys[l], w_out_f[l].T, r2(ln_g[l]), f"out_proj_bwd_{l}")
        go = g_w_out.reshape(N_DEV, D_MIX // N_DEV, D_MODEL)
        if l == 0:
            waiting += [go]
        g_h, g_rows[l], g_pool_w[l], *got = _mixer_bwd(
            hs[l], u2s[l], g_y, caw_f[l], cbw_f[l], r2(ln_b_g[l]), r2(ln_b_b[l]), pool_w[l], r2(pool_scale[l]),
            f"mixer_bwd_{l}",
            plan=_Exchange(waiting) if waiting else None)
        if got:
            if l + 1 < depth:
                gi_parts[l + 1], go_parts[l + 1] = got[0], got[1]
            if l == 0:
                go_parts[0] = got[-1]
        if l > 0:
            gi = _in_proj_wgrad(xs[l], g_h, f"in_proj_wgrad_{l}")[0]
            waiting = [gi.reshape(N_DEV, D_IN // N_DEV, D_MODEL), go]
            g = _in_proj_dgrad(g_h, w_in_f[l], g_z, f"in_proj_dgrad_{l}")[0]
        else:
            small = [*g_rows, *g_lns, *g_pool_w, loss_rows]
            gi, *small_parts = _in_proj_wgrad(xs[0], g_h, "in_proj_wgrad_0", plan=_Gather(small))
            g, gi_parts[0] = _in_proj_dgrad(g_h, w_in_f[0], g_z, "in_proj_dgrad_0",
                                            plan=_Exchange([gi.reshape(N_DEV, D_IN // N_DEV, D_MODEL)]))
    grad_x = _from_segments(g)[None]

    out_in = [tr(a) for a in _adamw_layers(gi_parts, w_in_t, m_w_in_t, v_w_in_t, "adamw_w_in")]
    out_out = _adamw_layers(go_parts, w_out, m_w_out, v_w_out, "adamw_w_out")
    params = dict(
        conv_a_w=(taps(conv_a_w), taps(m_conv_a_w), taps(v_conv_a_w)), conv_a_b=(conv_a_b, m_conv_a_b, v_conv_a_b),
        conv_b_w=(taps(conv_b_w), taps(m_conv_b_w), taps(v_conv_b_w)), conv_b_b=(conv_b_b, m_conv_b_b, v_conv_b_b),
        ln_b_g=(ln_b_g, m_ln_b_g, v_ln_b_g), ln_b_b=(ln_b_b, m_ln_b_b, v_ln_b_b),
        pool_w=(pool_w, m_pool_w, v_pool_w), pool_b=(pool_b, m_pool_b, v_pool_b),
        pool_scale=(pool_scale, m_pool_scale, v_pool_scale), ln_g=(ln_g, m_ln_g, v_ln_g), ln_b=(ln_b, m_ln_b, v_ln_b))
    loss, *small_out = _adamw_small(small_parts[:depth], small_parts[depth:2 * depth], small_parts[2 * depth:3 * depth],
                                    small_parts[3 * depth], params, "adamw_small")
    small_out = {n: small_out[4 * k:4 * k + 4] for k, n in enumerate(SMALL_PARAMS)}
    for n in ("conv_a_w", "conv_b_w"):
        small_out[n] = [taps(a) for a in small_out[n]]

    order = ("w_in", "conv_a_w", "conv_a_b", "conv_b_w", "conv_b_b", "ln_b_g", "ln_b_b", "pool_w", "pool_b",
             "pool_scale", "w_out", "ln_g", "ln_b")
    outs = []
    for k in range(4):
        outs += [out_in[k] if n == "w_in" else out_out[k] if n == "w_out" else small_out[n][k] for n in order]
    return (loss.reshape(()), grad_x, *outs)
```

```python
import jax
import jax.numpy as jnp
from jax import lax
from jax.experimental import pallas as pl
from jax.experimental.pallas import tpu as pltpu

F32 = jnp.float32
BF16 = jnp.bfloat16

DEPTH = 2
D_MODEL = 1024
W_MIX = 512
D_IN = 9 * W_MIX
D_MIX = 3 * W_MIX
POOL_WINDOWS = (2, 4, 8, 16)
POOL_DIM = 128
KA = 3
KB = 31
ALPHA = (2.0 * DEPTH) ** 0.25
LN_EPS = 1e-5
ADAM_LR, ADAM_B1, ADAM_B2, ADAM_EPS, ADAM_WD, ADAM_STEP = 0.001, 0.9, 0.999, 1e-08, 0.01, 10

N_DEV = 8
MESH = pl.DeviceIdType.MESH

CH = 32
SUB = 8
VMEM_LIMIT = 56 * 1024 * 1024

R_CAB, R_CBB, R_LBG, R_LBB, R_PB, R_PS, R_CAW, R_CBW = 0, 1, 2, 3, 4, 5, 6, 9
N_ROWS = R_CBW + KB


def _cparams(sem, **kw):
    return pltpu.CompilerParams(dimension_semantics=sem, vmem_limit_bytes=VMEM_LIMIT, **kw)


def _sigmoid(v):
    return 1.0 / (1.0 + jnp.exp(-v))


def _fold8(a):
    r, c = a.shape
    return a.reshape(r // SUB, SUB, c).sum(axis=0)


def _dot(a, b):
    return jnp.dot(a, b, preferred_element_type=F32)


def _dot_nt(a, b):
    return lax.dot_general(a, b, (((1,), (1,)), ((), ())), preferred_element_type=F32)


def _dot_tn(a, b):
    return lax.dot_general(a, b, (((0,), (0,)), ((), ())), preferred_element_type=F32)


def _row_tile(r, most=192):
    return max(c for c in range(SUB, min(r, most) + 1, SUB) if r % c == 0)


def _position():
    return lax.axis_index("x"), lax.axis_index("y"), lax.axis_index("c")


class _Gather:
    def __init__(self, arrs):
        self.arrs = list(arrs)
        na = len(self.arrs)
        self.out_shape = [jax.ShapeDtypeStruct((N_DEV,) + a.shape, a.dtype) for a in self.arrs]
        self.scratch = [pltpu.SemaphoreType.DMA((na, 7)), pltpu.SemaphoreType.DMA((na, 7)),
                        pltpu.SemaphoreType.DMA((na,))]

    def _copies(self, src, dst, sems):
        send_sems, recv_sems, local_sems = sems
        na = len(self.arrs)
        x, y, c = _position()
        me, sibling = (x, y, c), (x, y, 1 - c)
        chips = [(1 - x, y), (x, 1 - y), (1 - x, 1 - y)]

        def slot(a, dev):
            return dst[a].at[4 * dev[0] + 2 * dev[1] + dev[2]]

        def copy(a, k, block, to, from_src=False):
            return pltpu.make_async_remote_copy(
                src_ref=src[a] if from_src else slot(a, block), dst_ref=slot(a, block),
                send_sem=send_sems.at[a, k], recv_sem=recv_sems.at[a, k], device_id=to, device_id_type=MESH)

        mine = [pltpu.make_async_copy(src[a], slot(a, me), local_sems.at[a]) for a in range(na)]
        first, landed, passed, last = [], [], [], []
        for a in range(na):
            first.append(copy(a, 0, me, sibling, from_src=True))
            first += [copy(a, 1 + j, me, (*chip, c), from_src=True) for j, chip in enumerate(chips)]
        for j, chip in enumerate(chips):
            for a in range(na):
                landed.append(copy(a, 1 + j, (*chip, c), me))
                passed.append(copy(a, 4 + j, (*chip, c), sibling))
        for a in range(na):
            last.append(copy(a, 0, sibling, me))
            last += [copy(a, 4 + j, (*chip, 1 - c), me) for j, chip in enumerate(chips)]
        return mine, first, landed, passed, last

    def start(self, src, dst, sems):
        mine, first, _, _, _ = self._copies(src, dst, sems)
        for cp in mine + first:
            cp.start()

    def pass_on(self, src, dst, sems):
        _, _, landed, passed, _ = self._copies(src, dst, sems)
        for got, cp in zip(landed, passed):
            got.wait_recv()
            cp.start()

    def finish(self, src, dst, sems):
        mine, first, _, passed, last = self._copies(src, dst, sems)
        for cp in last:
            cp.wait_recv()
        for cp in first + passed:
            cp.wait_send()
        for cp in mine:
            cp.wait()


class _Exchange:
    def __init__(self, arrs):
        self.arrs = list(arrs)
        na = len(self.arrs)
        self.out_shape = [jax.ShapeDtypeStruct(a.shape, a.dtype) for a in self.arrs]
        self.scratch = [pltpu.SemaphoreType.DMA((na, N_DEV - 1)), pltpu.SemaphoreType.DMA((na, N_DEV - 1)),
                        pltpu.SemaphoreType.DMA((na,))]

    def _copies(self, src, dst, sems):
        send_sems, recv_sems, local_sems = sems
        na = len(self.arrs)
        x, y, c = _position()
        me = 4 * x + 2 * y + c
        mine = [pltpu.make_async_copy(src[a].at[me], dst[a].at[me], local_sems.at[a]) for a in range(na)]
        copies = []
        for a in range(na):
            for k in range(N_DEV - 1):
                flip = k + 1
                px, py, pc = x ^ (flip >> 2), y ^ ((flip >> 1) & 1), c ^ (flip & 1)
                copies.append(pltpu.make_async_remote_copy(
                    src_ref=src[a].at[4 * px + 2 * py + pc], dst_ref=dst[a].at[me],
                    send_sem=send_sems.at[a, k], recv_sem=recv_sems.at[a, k], device_id=(px, py, pc),
                    device_id_type=MESH))
        return mine, copies

    def start(self, src, dst, sems):
        mine, copies = self._copies(src, dst, sems)
        for cp in mine + copies:
            cp.start()

    def pass_on(self, src, dst, sems):
        pass

    def finish(self, src, dst, sems):
        mine, copies = self._copies(src, dst, sems)
        for cp in copies:
            cp.wait()
        for cp in mine:
            cp.wait()


def _run_plan(plan, name):
    na = len(plan.arrs)
    any_spec = pl.BlockSpec(memory_space=pl.ANY)

    def body(*refs):
        src, dst, sems = refs[:na], refs[na:2 * na], refs[2 * na:]
        plan.start(src, dst, sems)
        plan.pass_on(src, dst, sems)
        plan.finish(src, dst, sems)

    return pl.pallas_call(
        body, name=name, in_specs=[any_spec] * na, out_specs=[any_spec] * na,
        out_shape=plan.out_shape, scratch_shapes=plan.scratch,
    )(*plan.arrs)


class _Hosted:
    def __init__(self, plan, n_in, n_out, n_scratch):
        self.plan, self.n_in, self.n_out, self.n_scratch = plan, n_in, n_out, n_scratch
        any_spec = pl.BlockSpec(memory_space=pl.ANY)
        k = 0 if plan is None else len(plan.arrs)
        self.operands = [] if plan is None else plan.arrs
        self.in_specs = [any_spec] * k
        self.out_specs = [any_spec] * k
        self.out_shape = [] if plan is None else plan.out_shape
        self.scratch = [] if plan is None else plan.scratch

    def wrap(self, body, phase):
        if self.plan is None:
            return body
        plan, k = self.plan, len(self.plan.arrs)
        i0, o0 = self.n_in, self.n_in + k
        o1 = o0 + self.n_out
        s0 = o1 + k
        s1 = s0 + self.n_scratch

        def hosted(*refs):
            src, dst, sems = refs[i0:o0], refs[o1:s0], refs[s1:]
            first, middle, last = phase()
            pl.when(first)(lambda: plan.start(src, dst, sems))
            body(*refs[:i0], *refs[o0:o1], *refs[s0:s1])
            pl.when(middle)(lambda: plan.pass_on(src, dst, sems))
            pl.when(last)(lambda: plan.finish(src, dst, sems))

        return hosted


TT = 256
SEG = TT // SUB


def _to_segments(a):
    t, c = a.shape
    return a.reshape(t // TT, SUB, SEG, c).transpose(0, 2, 1, 3).reshape(t, c)


def _from_segments(a):
    t, c = a.shape
    return a.reshape(t // TT, SEG, SUB, c).transpose(0, 2, 1, 3).reshape(t, c)


def _sublane_is(s):
    return lax.broadcasted_iota(jnp.int32, (TT, W_MIX), 0) % SUB == s


def _look_back(ext_ref, cur, before):
    ext_ref[TT:, :] = cur
    ext_ref[0:TT, :] = jnp.where(_sublane_is(0), before, pltpu.roll(cur, 1, axis=0))


def _last_segment(cur):
    return pltpu.roll(cur, TT - (SUB - 1), axis=0)


def _look_ahead(ext_ref, cur, after):
    ext_ref[0:TT, :] = cur
    ext_ref[TT:, :] = jnp.where(_sublane_is(SUB - 1), after, pltpu.roll(cur, TT - 1, axis=0))


def _first_segment(cur):
    return pltpu.roll(cur, SUB - 1, axis=0)


def _tap_loop(body):
    lax.fori_loop(0, TT // CH, lambda c, carry: body(pl.multiple_of(c * CH, CH), carry), 0)


def _conv_rows(ext_ref, w_ref, out_ref, *, nk, off, reverse):
    def body(r0, carry):
        acc = jnp.zeros((CH, W_MIX), F32)
        for k in range(nk):
            kk = nk - 1 - k if reverse else k
            acc = acc + ext_ref[pl.ds(r0 + (off + k) * SUB, CH), :] * w_ref[kk:kk + 1, :]
        out_ref[pl.ds(r0, CH), :] = acc
        return carry
    _tap_loop(body)


def _conv_wgrad(g_ref, ext_ref, acc_ref, *, nk, off, row0):
    def body(r0, carry):
        g = g_ref[pl.ds(r0, CH), :]
        for k in range(nk):
            a = (row0 + k) * SUB
            acc_ref[a:a + SUB, :] += _fold8(g * ext_ref[pl.ds(r0 + (off + k) * SUB, CH), :])
        return carry
    _tap_loop(body)


def _window_sums(ext_ref, out_ref, *, forward):
    def body(r0, carry):
        for g, w in enumerate(POOL_WINDOWS):
            lanes = slice(g * POOL_DIM, (g + 1) * POOL_DIM)
            acc = jnp.zeros((CH, POOL_DIM), F32)
            for j in range(w):
                off = j if forward else SEG - j
                acc = acc + ext_ref[pl.ds(r0 + off * SUB, CH), lanes]
            out_ref[pl.ds(r0, CH), lanes] = acc
        return carry
    _tap_loop(body)


def _inv_count(tile):
    r = lax.broadcasted_iota(jnp.int32, (TT, POOL_DIM), 0)
    t1 = (tile * TT + (r % SUB) * SEG + r // SUB + 1).astype(F32)
    return jnp.concatenate([1.0 / jnp.minimum(t1, float(w)) for w in POOL_WINDOWS], axis=1)


KEPT = ("u2", "sigmoid(a_z)", "ca", "sigmoid(b_g)", "sigmoid(ln)", "sigmoid(b_z)", "sigmoid(c_z)", "pooled", "p")


def _groups(h_ref):
    return [h_ref[:, k * W_MIX:(k + 1) * W_MIX].astype(F32) for k in range(9)]


def _layer_norm(v, g, b):
    mu = jnp.mean(v, axis=-1, keepdims=True)
    vc = v - mu
    var = jnp.mean(vc * vc, axis=-1, keepdims=True)
    rstd = lax.rsqrt(var + LN_EPS)
    vhat = vc * rstd
    return vhat * g + b, vhat, rstd


def _layer_norm_bwd(g_out, vhat, rstd, g):
    gh = g_out * g
    m1 = jnp.mean(gh, axis=-1, keepdims=True)
    m2 = jnp.mean(gh * vhat, axis=-1, keepdims=True)
    return rstd * (gh - m1 - vhat * m2)


def _pool_linear(pooled, pw_ref, pb_ref):
    outs = []
    for g in range(len(POOL_WINDOWS)):
        lanes = slice(g * POOL_DIM, (g + 1) * POOL_DIM)
        outs.append(_dot(pooled[:, lanes].astype(BF16), pw_ref[g].astype(BF16)))
    return jnp.concatenate(outs, axis=1) + pb_ref[...]


def _in_proj(x, w, name, plan=None):
    t, d = x.shape
    n = w.shape[0]
    tm, tn = min(t, 1024), 2304
    nm, nn = t // tm, n // tn

    def body(x_ref, w_ref, o_ref, xb_ref):
        @pl.when(pl.program_id(1) == 0)
        def _():
            xb_ref[...] = x_ref[...].astype(BF16)
        o_ref[...] = _dot_nt(xb_ref[...], w_ref[...]).astype(BF16)

    def phase():
        step = pl.program_id(0) * nn + pl.program_id(1)
        return step == 0, step == (nm * nn) // 2, step == nm * nn - 1

    host = _Hosted(plan, n_in=2, n_out=1, n_scratch=1)
    return pl.pallas_call(
        host.wrap(body, phase), name=name, grid=(nm, nn),
        in_specs=[pl.BlockSpec((tm, d), lambda i, j: (i, 0)), pl.BlockSpec((tn, d), lambda i, j: (j, 0))]
        + host.in_specs,
        out_specs=[pl.BlockSpec((tm, tn), lambda i, j: (i, j))] + host.out_specs,
        out_shape=[jax.ShapeDtypeStruct((t, n), BF16)] + host.out_shape,
        scratch_shapes=[pltpu.VMEM((tm, d), BF16)] + host.scratch,
        compiler_params=_cparams(("arbitrary", "arbitrary")),
    )(x, w, *host.operands)


def _mixer_fwd(h, x, caw, cab, cbw, cbb, lbg, lbb, pw, pb, ps, w_out, lng, lnb, name, plan=None, target=None):
    t = h.shape[0]
    tt = TT
    n = t // tt
    with_loss = target is not None

    def body(h_ref, x_ref, *refs):
        if with_loss:
            t_ref, refs = refs[0], refs[1:]
        (caw_ref, cab_ref, cbw_ref, cbb_ref, lbg_ref, lbb_ref, pw_ref, pb_ref, ps_ref, wo_ref, lng_ref, lnb_ref,
         y_ref, u2_ref, z_ref, xn_ref) = refs[:16]
        refs = refs[16:]
        if with_loss:
            l_ref, refs = refs[0], refs[1:]
        exta, extb, extc, lasta, lastb, lastc, tmp, inv_ref = refs
        i = pl.program_id(0)

        @pl.when(i == 0)
        def _():
            for e in (lasta, lastb, lastc):
                e[...] = jnp.zeros_like(e)
            if with_loss:
                l_ref[...] = jnp.zeros_like(l_ref)

        @pl.when(i <= 1)
        def _():
            inv_ref[...] = _inv_count(i)

        group = lambda k: h_ref[:, k * W_MIX:(k + 1) * W_MIX].astype(F32)

        def keep(name, v):
            k = KEPT.index(name)
            u2_ref[:, k * W_MIX:(k + 1) * W_MIX] = v

        def look_back(ext, last, cur):
            _look_back(ext, cur, last[...])
            last[...] = _last_segment(cur)

        sgg = _sigmoid(group(5))
        keep("sigmoid(b_g)", sgg)
        look_back(extb, lastb, group(4) * sgg)
        _conv_rows(extb, cbw_ref, tmp, nk=KB, off=SEG - (KB - 1), reverse=False)
        u2 = tmp[...] + cbb_ref[...]
        keep("u2", u2)
        ln, _, _ = _layer_norm(u2, lbg_ref[...], lbb_ref[...])
        b_z = group(6)
        sl, sz = _sigmoid(ln), _sigmoid(b_z)
        keep("sigmoid(ln)", sl), keep("sigmoid(b_z)", sz)
        y_ref[:, W_MIX:2 * W_MIX] = ((ln * sl) * (b_z * sz)).astype(BF16)

        look_back(exta, lasta, group(1) * group(2))
        _conv_rows(exta, caw_ref, tmp, nk=KA, off=SEG - (KA - 1), reverse=False)
        ca = tmp[...] + cab_ref[...]
        a_z = group(3)
        sga = _sigmoid(a_z)
        keep("ca", ca), keep("sigmoid(a_z)", sga)
        y_ref[:, 0:W_MIX] = (group(0) * ca * (a_z * sga)).astype(BF16)

        c_u = group(7)
        look_back(extc, lastc, c_u)
        _window_sums(extc, tmp, forward=False)
        pooled = tmp[...] * inv_ref[...] - c_u
        p = _pool_linear(pooled, pw_ref, pb_ref)
        c_z = group(8)
        sc = _sigmoid(c_z)
        keep("pooled", pooled), keep("p", p), keep("sigmoid(c_z)", sc)
        y_ref[:, 2 * W_MIX:3 * W_MIX] = (p * ps_ref[...] * (c_z * sc)).astype(BF16)

        out = _dot(y_ref[...], wo_ref[...])
        z = ALPHA * x_ref[...] + out
        z_ref[...] = z
        xn, _, _ = _layer_norm(z, lng_ref[...], lnb_ref[...])
        if with_loss:
            e = xn - t_ref[...]
            xn_ref[...] = e * (1.0 / D_MODEL)
            l_ref[...] += _fold8(e * e) * (0.5 / D_MODEL)
        else:
            xn_ref[...] = xn

    def phase():
        i = pl.program_id(0)
        return i == 0, i == n // 2, i == n - 1

    row = lambda wd: pl.BlockSpec((tt, wd), lambda i: (i, 0))
    full = lambda a: pl.BlockSpec(a.shape, lambda i: (0,) * a.ndim)
    params = (caw, cab, cbw, cbb, lbg, lbb, pw, pb, ps, w_out, lng, lnb)
    extra_in = [target] if with_loss else []
    extra_out = [jax.ShapeDtypeStruct((SUB, D_MODEL), F32)] if with_loss else []
    host = _Hosted(plan, n_in=2 + len(extra_in) + len(params), n_out=4 + len(extra_out), n_scratch=8)
    return pl.pallas_call(
        host.wrap(body, phase), name=name, grid=(n,),
        in_specs=[row(D_IN), row(D_MODEL)] + [row(D_MODEL)] * len(extra_in) + [full(a) for a in params]
        + host.in_specs,
        out_specs=[row(D_MIX), row(len(KEPT) * W_MIX), row(D_MODEL), row(D_MODEL)] + [full(o) for o in extra_out]
        + host.out_specs,
        out_shape=[jax.ShapeDtypeStruct((t, D_MIX), BF16), jax.ShapeDtypeStruct((t, len(KEPT) * W_MIX), F32),
                   jax.ShapeDtypeStruct((t, D_MODEL), F32), jax.ShapeDtypeStruct((t, D_MODEL), F32)]
        + extra_out + host.out_shape,
        scratch_shapes=[pltpu.VMEM((2 * tt, W_MIX), F32)] * 3 + [pltpu.VMEM((tt, W_MIX), F32)] * 5 + host.scratch,
        compiler_params=_cparams(("arbitrary",)),
    )(h, x, *extra_in, *params, *host.operands)


def _out_proj_bwd(g_xn, z, y, w_out, lng, name):
    t = z.shape[0]
    tt = min(t, 512)
    n = t // tt

    def body(g_ref, z_ref, y_ref, wo_ref, lng_ref, gz_ref, gy_ref, gwo_ref, gln_ref, accg, accb, accw, gzb):
        i = pl.program_id(0)

        @pl.when(i == 0)
        def _():
            accw[...] = jnp.zeros_like(accw)
            accg[...] = jnp.zeros_like(accg)
            accb[...] = jnp.zeros_like(accb)
            gzb[...] = jnp.zeros_like(gzb)

        before = gzb[(i + 1) % 2]
        gy_ref[...] = _dot_nt(before, wo_ref[...])
        accw[...] += _dot_tn(y_ref[...], before)

        counts = (i < n).astype(F32)
        g = g_ref[...]
        _, zhat, rstd = _layer_norm(z_ref[...], lng_ref[...], 0.0)
        accg[...] += _fold8(g * zhat) * counts
        accb[...] += _fold8(g) * counts
        g_z = _layer_norm_bwd(g, zhat, rstd, lng_ref[...])
        gz_ref[...] = g_z
        gzb[i % 2] = g_z.astype(BF16)

        @pl.when(i == n)
        def _():
            gwo_ref[...] = accw[...].astype(BF16)
            gln_ref[...] = jnp.zeros_like(gln_ref)
            gln_ref[0:1, :] = jnp.sum(accg[...], axis=0, keepdims=True)
            gln_ref[1:2, :] = jnp.sum(accb[...], axis=0, keepdims=True)

    this = lambda wd: pl.BlockSpec((tt, wd), lambda i: (jnp.minimum(i, n - 1), 0))
    last = lambda wd: pl.BlockSpec((tt, wd), lambda i: (jnp.maximum(i - 1, 0), 0))
    full = lambda shape: pl.BlockSpec(shape, lambda i: (0,) * len(shape))
    return pl.pallas_call(
        body, name=name, grid=(n + 1,),
        in_specs=[this(D_MODEL), this(D_MODEL), last(D_MIX), full(w_out.shape), full(lng.shape)],
        out_specs=[this(D_MODEL), last(D_MIX), full((D_MIX, D_MODEL)), full((SUB, D_MODEL))],
        out_shape=[jax.ShapeDtypeStruct((t, D_MODEL), F32), jax.ShapeDtypeStruct((t, D_MIX), F32),
                   jax.ShapeDtypeStruct((D_MIX, D_MODEL), BF16), jax.ShapeDtypeStruct((SUB, D_MODEL), F32)],
        scratch_shapes=[pltpu.VMEM((SUB, D_MODEL), F32)] * 2 + [pltpu.VMEM((D_MIX, D_MODEL), F32),
                                                                 pltpu.VMEM((2, tt, D_MODEL), BF16)],
        compiler_params=_cparams(("arbitrary",)),
    )(g_xn, z, y, w_out, lng)


def _mixer_bwd(h, u2, g_y, caw, cbw, lbg, lbb, pw, ps, name, plan=None):
    t = h.shape[0]
    tt = TT
    n = t // tt
    before_groups = (1, 2, 4, 5)

    def body(h_ref, p_cg, p_av, p_bv, p_bg, u2_ref, gy_ref, caw_ref, cbw_ref, lbg_ref, lbb_ref,
             pw_ref, ps_ref, gh_ref, rows_ref, gpw_ref,
             exta, extb, gca, gu2, qx, nexta, nextb, nextc, tmp, tmp2, inv_ref, acc):
        s = pl.program_id(0)
        i = n - 1 - s

        @pl.when(s == 0)
        def _():
            acc[...] = jnp.zeros_like(acc)
            gpw_ref[...] = jnp.zeros_like(gpw_ref)
            for e in (nexta, nextb, nextc):
                e[...] = jnp.zeros_like(e)

        live = (i > 0).astype(F32)
        f32 = lambda ref: ref[...].astype(F32)
        before_a = _last_segment(f32(p_cg) * f32(p_av)) * live
        before_b = _last_segment(f32(p_bv) * _sigmoid(f32(p_bg))) * live

        a_bg, a_cg, a_v, a_z, b_v, b_g, b_z, c_u, c_z = _groups(h_ref)
        u2, sg, ca, sgg, sl, sz, sc, pooled, p = (u2_ref[:, k * W_MIX:(k + 1) * W_MIX] for k in range(len(KEPT)))
        g_ya = gy_ref[:, 0:W_MIX]
        g_yb = gy_ref[:, W_MIX:2 * W_MIX]
        g_yc = gy_ref[:, 2 * W_MIX:3 * W_MIX]

        def add_row(r, v):
            acc[r * SUB:(r + 1) * SUB, :] += _fold8(v)

        _look_back(exta, a_cg * a_v, before_a)
        s_az = a_z * sg
        t_a = g_ya * a_bg
        gh_ref[:, 0:W_MIX] = (g_ya * ca * s_az).astype(BF16)
        gh_ref[:, 3 * W_MIX:4 * W_MIX] = (t_a * ca * (sg * (1.0 + a_z * (1.0 - sg)))).astype(BF16)
        g_ca = t_a * s_az
        _look_ahead(gca, g_ca, nexta[...])
        nexta[...] = _first_segment(g_ca)
        add_row(R_CAB, g_ca)
        _conv_wgrad(gca, exta, acc, nk=KA, off=SEG - (KA - 1), row0=R_CAW)
        _conv_rows(gca, caw_ref, tmp, nk=KA, off=0, reverse=True)
        g_pa = tmp[...]
        gh_ref[:, W_MIX:2 * W_MIX] = (g_pa * a_v).astype(BF16)
        gh_ref[:, 2 * W_MIX:3 * W_MIX] = (g_pa * a_cg).astype(BF16)

        _look_back(extb, b_v * sgg, before_b)
        ln, u2hat, rstd = _layer_norm(u2, lbg_ref[...], lbb_ref[...])
        u3 = ln * sl
        s_bz = b_z * sz
        gh_ref[:, 6 * W_MIX:7 * W_MIX] = (g_yb * u3 * (sz * (1.0 + b_z * (1.0 - sz)))).astype(BF16)
        g_ln = g_yb * s_bz * (sl * (1.0 + ln * (1.0 - sl)))
        add_row(R_LBG, g_ln * u2hat)
        add_row(R_LBB, g_ln)
        g_u2 = _layer_norm_bwd(g_ln, u2hat, rstd, lbg_ref[...])
        _look_ahead(gu2, g_u2, nextb[...])
        nextb[...] = _first_segment(g_u2)
        add_row(R_CBB, g_u2)
        _conv_wgrad(gu2, extb, acc, nk=KB, off=SEG - (KB - 1), row0=R_CBW)
        _conv_rows(gu2, cbw_ref, tmp, nk=KB, off=0, reverse=True)
        g_u1 = tmp[...]
        gh_ref[:, 4 * W_MIX:5 * W_MIX] = (g_u1 * sgg).astype(BF16)
        gh_ref[:, 5 * W_MIX:6 * W_MIX] = (g_u1 * b_v * sgg * (1.0 - sgg)).astype(BF16)

        @pl.when((s == 0) | (i == 0))
        def _():
            inv_ref[...] = _inv_count(i)
        inv = inv_ref[...]
        s_cz = c_z * sc
        scale = ps_ref[...]
        gh_ref[:, 8 * W_MIX:9 * W_MIX] = (g_yc * p * scale * (sc * (1.0 + c_z * (1.0 - sc)))).astype(BF16)
        t_c = g_yc * s_cz
        add_row(R_PS, t_c * p)
        g_p = t_c * scale
        add_row(R_PB, g_p)
        g_pooled = []
        for g in range(len(POOL_WINDOWS)):
            lanes = slice(g * POOL_DIM, (g + 1) * POOL_DIM)
            gpg = g_p[:, lanes].astype(BF16)
            gpw_ref[g] += _dot_tn(pooled[:, lanes].astype(BF16), gpg)
            g_pooled.append(_dot_nt(gpg, pw_ref[g].astype(BF16)))
        g_pooled = jnp.concatenate(g_pooled, axis=1)
        q = g_pooled * inv
        _look_ahead(qx, q, nextc[...])
        nextc[...] = _first_segment(q)
        _window_sums(qx, tmp2, forward=True)
        gh_ref[:, 7 * W_MIX:8 * W_MIX] = (tmp2[...] - g_pooled).astype(BF16)

        @pl.when(s == n - 1)
        def _():
            for r in range(N_ROWS):
                rows_ref[r:r + 1, :] = jnp.sum(acc[r * SUB:(r + 1) * SUB, :], axis=0, keepdims=True)

    def phase():
        s = pl.program_id(0)
        return s == 0, s == n // 2, s == n - 1

    row = lambda wd: pl.BlockSpec((tt, wd), lambda s: (n - 1 - s, 0))
    before = [pl.BlockSpec((tt, W_MIX), lambda s, k=k: (jnp.maximum(n - 2 - s, 0), k)) for k in before_groups]
    full = lambda shape: pl.BlockSpec(shape, lambda s: (0,) * len(shape))
    params = (caw, cbw, lbg, lbb, pw, ps)
    host = _Hosted(plan, n_in=3 + len(before) + len(params), n_out=3, n_scratch=12)
    return pl.pallas_call(
        host.wrap(body, phase), name=name, grid=(n,),
        in_specs=[row(D_IN)] + before + [row(len(KEPT) * W_MIX), row(D_MIX)] + [full(a.shape) for a in params]
        + host.in_specs,
        out_specs=[row(D_IN), full((N_ROWS, W_MIX)), full(pw.shape)] + host.out_specs,
        out_shape=[jax.ShapeDtypeStruct((t, D_IN), BF16), jax.ShapeDtypeStruct((N_ROWS, W_MIX), F32),
                   jax.ShapeDtypeStruct(pw.shape, F32)] + host.out_shape,
        scratch_shapes=[pltpu.VMEM((2 * tt, W_MIX), F32)] * 5 + [pltpu.VMEM((tt, W_MIX), F32)] * 6
        + [pltpu.VMEM((N_ROWS * SUB, W_MIX), F32)] + host.scratch,
        compiler_params=_cparams(("arbitrary",)),
    )(h, *([h] * len(before)), u2, g_y, *params, *host.operands)


def _in_proj_wgrad(x, g_h, name, plan=None):
    t, d = x.shape
    n = g_h.shape[1]
    tk, tn = min(t, 1024), n // 2
    nk = t // tk

    def body(x_ref, g_ref, o_ref, acc):
        k = pl.program_id(1)

        @pl.when(k == 0)
        def _():
            acc[...] = jnp.zeros_like(acc)
        acc[...] += _dot_tn(x_ref[...].astype(BF16), g_ref[...])

        @pl.when(k == nk - 1)
        def _():
            o_ref[...] = acc[...].T.astype(BF16)

    def phase():
        step = pl.program_id(0) * nk + pl.program_id(1)
        return step == 0, step == nk, step == 2 * nk - 1

    host = _Hosted(plan, n_in=2, n_out=1, n_scratch=1)
    return pl.pallas_call(
        host.wrap(body, phase), name=name, grid=(n // tn, nk),
        in_specs=[pl.BlockSpec((tk, d), lambda j, k: (k, 0)), pl.BlockSpec((tk, tn), lambda j, k: (k, j))]
        + host.in_specs,
        out_specs=[pl.BlockSpec((tn, d), lambda j, k: (j, 0))] + host.out_specs,
        out_shape=[jax.ShapeDtypeStruct((n, d), BF16)] + host.out_shape,
        scratch_shapes=[pltpu.VMEM((d, tn), F32)] + host.scratch,
        compiler_params=_cparams(("arbitrary", "arbitrary")),
    )(x, g_h, *host.operands)


def _in_proj_dgrad(g_h, w, g_z, name, plan=None):
    t, n = g_h.shape
    d = w.shape[1]
    tm, tk = min(t, 1024), 2304
    nm, nk = t // tm, n // tk

    def body(g_ref, w_ref, gz_ref, o_ref):
        @pl.when(pl.program_id(1) == 0)
        def _():
            o_ref[...] = ALPHA * gz_ref[...]
        o_ref[...] += _dot(g_ref[...], w_ref[...])

    def phase():
        step = pl.program_id(0) * nk + pl.program_id(1)
        return step == 0, step == (nm * nk) // 2, step == nm * nk - 1

    host = _Hosted(plan, n_in=3, n_out=1, n_scratch=0)
    return pl.pallas_call(
        host.wrap(body, phase), name=name, grid=(nm, nk),
        in_specs=[pl.BlockSpec((tm, tk), lambda i, k: (i, k)), pl.BlockSpec((tk, d), lambda i, k: (k, 0)),
                  pl.BlockSpec((tm, d), lambda i, k: (i, 0))] + host.in_specs,
        out_specs=[pl.BlockSpec((tm, d), lambda i, k: (i, 0))] + host.out_specs,
        out_shape=[jax.ShapeDtypeStruct((t, d), F32)] + host.out_shape,
        scratch_shapes=host.scratch,
        compiler_params=_cparams(("arbitrary", "arbitrary")),
    )(g_h, w, g_z, *host.operands)


BC1 = 1.0 - ADAM_B1 ** ADAM_STEP
BC2 = 1.0 - ADAM_B2 ** ADAM_STEP


def _adamw_math(g, w, m, v):
    nm = ADAM_B1 * m + (1.0 - ADAM_B1) * g
    nv = ADAM_B2 * v + (1.0 - ADAM_B2) * (g * g)
    delta = -ADAM_LR * ((nm / BC1) / (jnp.sqrt(nv / BC2) + ADAM_EPS) + ADAM_WD * w)
    return delta, nm, nv


def _total(ref):
    g = ref[0].astype(F32)
    for k in range(1, ref.shape[0]):
        g = g + ref[k].astype(F32)
    return g


def _adamw_layers(parts, w, m, v, name):
    depth, r, c = w.shape
    p = parts[0].shape[0]
    tr = _row_tile(r)
    nr = r // tr

    def body(*refs):
        p_refs = refs[:depth]
        w_ref, m_ref, v_ref, g_ref, d_ref, nm_ref, nv_ref = refs[depth:]
        for l in range(depth):
            @pl.when(pl.program_id(0) == l)
            def _(l=l):
                g = _total(p_refs[l])
                delta, nm, nv = _adamw_math(g, w_ref[0], m_ref[0], v_ref[0])
                g_ref[0], d_ref[0], nm_ref[0], nv_ref[0] = g, delta, nm, nv

    def part_spec(l):
        return pl.BlockSpec((p, tr, c), lambda li, i: (0, jnp.where(li == l, i, jnp.where(li < l, 0, nr - 1)), 0))

    blk = pl.BlockSpec((1, tr, c), lambda li, i: (li, i, 0))
    out = jax.ShapeDtypeStruct((depth, r, c), F32)
    return pl.pallas_call(
        body, name=name, grid=(depth, nr),
        in_specs=[part_spec(l) for l in range(depth)] + [blk] * 3,
        out_specs=[blk] * 4, out_shape=[out] * 4,
        compiler_params=_cparams(("arbitrary", "arbitrary")),
    )(*parts, w, m, v)


SMALL_PARAMS = ("conv_a_w", "conv_a_b", "conv_b_w", "conv_b_b", "ln_b_g", "ln_b_b", "pool_w", "pool_b", "pool_scale",
                "ln_g", "ln_b")


def _adamw_small(rows_parts, gln_parts, gpw_parts, loss_parts, params, name):
    depth = len(rows_parts)
    cs = params["conv_a_w"][0].shape[2]
    operands = [*rows_parts, *gln_parts, *gpw_parts, loss_parts] + [a for n in SMALL_PARAMS for a in params[n]]
    n_in = len(operands)
    out_shape = [jax.ShapeDtypeStruct((1, 1), F32)]
    out_shape += [jax.ShapeDtypeStruct(params[n][0].shape, F32) for n in SMALL_PARAMS for _ in range(4)]

    def body(*refs):
        rows_p, gln_p, gpw_p = refs[:depth], refs[depth:2 * depth], refs[2 * depth:3 * depth]
        loss_p = refs[3 * depth]
        prm, outs = refs[3 * depth + 1:n_in], refs[n_in + 1:]
        refs[n_in][...] = jnp.sum(_total(loss_p)).reshape(1, 1)
        x, y, c = _position()
        to_front = (W_MIX - (4 * x + 2 * y + c) * cs) % W_MIX

        def update(name, g, at):
            k = SMALL_PARAMS.index(name)
            w_ref, m_ref, v_ref = prm[3 * k:3 * k + 3]
            g_ref, d_ref, nm_ref, nv_ref = outs[4 * k:4 * k + 4]
            delta, nm, nv = _adamw_math(g, w_ref[at], m_ref[at], v_ref[at])
            g_ref[at], d_ref[at], nm_ref[at], nv_ref[at] = g, delta, nm, nv

        for l in range(depth):
            rows = _total(rows_p[l])
            mine = pltpu.roll(rows, to_front, axis=1)
            gln = _total(gln_p[l])
            one = (slice(l, l + 1), slice(None))
            for name, r in (("conv_a_b", R_CAB), ("conv_b_b", R_CBB), ("ln_b_g", R_LBG), ("ln_b_b", R_LBB),
                            ("pool_scale", R_PS)):
                update(name, rows[r:r + 1, :], one)
            for g in range(len(POOL_WINDOWS)):
                update("pool_b", rows[R_PB:R_PB + 1, g * POOL_DIM:(g + 1) * POOL_DIM], (l, slice(g, g + 1), slice(None)))
            update("ln_g", gln[0:1, :], one)
            update("ln_b", gln[1:2, :], one)
            update("conv_a_w", mine[R_CAW:R_CAW + KA, 0:cs], (slice(None), l, slice(None)))
            update("conv_b_w", mine[R_CBW:R_CBW + KB, 0:cs], (slice(None), l, slice(None)))
            update("pool_w", _total(gpw_p[l]), (l,))

    vmem = pl.BlockSpec(memory_space=pltpu.VMEM)
    return pl.pallas_call(
        body, name=name, in_specs=[vmem] * n_in, out_specs=[vmem] * len(out_shape), out_shape=out_shape,
        compiler_params=pltpu.CompilerParams(vmem_limit_bytes=VMEM_LIMIT),
    )(*operands)


def kernel(x, w_in, conv_a_w, conv_a_b, conv_b_w, conv_b_b, ln_b_g, ln_b_b, pool_w, pool_b, pool_scale, w_out, ln_g, ln_b, loss_target, m_w_in, m_conv_a_w, m_conv_a_b, m_conv_b_w, m_conv_b_b, m_ln_b_g, m_ln_b_b, m_pool_w, m_pool_b, m_pool_scale, m_w_out, m_ln_g, m_ln_b, v_w_in, v_conv_a_w, v_conv_a_b, v_conv_b_w, v_conv_b_b, v_ln_b_g, v_ln_b_b, v_pool_w, v_pool_b, v_pool_scale, v_w_out, v_ln_g, v_ln_b):
    depth = w_in.shape[0]
    x0 = _to_segments(x[0])
    target = _to_segments(loss_target[0])
    r2 = lambda a: a.reshape(1, -1)

    tr = lambda a: jnp.swapaxes(a, 1, 2)
    w_in_t, m_w_in_t, v_w_in_t = tr(w_in), tr(m_w_in), tr(v_w_in)
    w_in_b, w_out_b = w_in_t.astype(BF16), w_out.astype(BF16)
    taps = lambda a: jnp.swapaxes(a, 0, 1)
    conv_sh = jnp.concatenate([taps(conv_a_w), taps(conv_b_w)], axis=0)
    full_in = lambda g: g.reshape(D_IN, D_MODEL)
    full_out = lambda g: g.reshape(D_MIX, D_MODEL)
    w_in_f = [full_in(_run_plan(_Gather([w_in_b[0]]), "gather_w_in_0")[0])]
    w_out_f = []

    xs, hs, ys, u2s, zs = [x0], [], [], [], []
    for l in range(depth):
        h, *got = _in_proj(xs[l], w_in_f[l], f"in_proj_{l}",
                           plan=_Gather([w_out_b[0], conv_sh]) if l == 0 else None)
        if got:
            w_out_f.append(full_out(got[0]))
            conv_f = got[1].transpose(2, 1, 0, 3).reshape(depth, KA + KB, W_MIX)
            caw_f, cbw_f = conv_f[:, :KA], conv_f[:, KA:]
        last = l + 1 == depth
        y, u2, z, xn, *got = _mixer_fwd(
            h, xs[l], caw_f[l], r2(conv_a_b[l]), cbw_f[l], r2(conv_b_b[l]), r2(ln_b_g[l]), r2(ln_b_b[l]),
            pool_w[l], r2(pool_b[l]), r2(pool_scale[l]), w_out_f[l], r2(ln_g[l]), r2(ln_b[l]), f"mixer_fwd_{l}",
            plan=None if last else _Gather([w_in_b[l + 1], w_out_b[l + 1]]), target=target if last else None)
        if last:
            g, loss_rows = xn, got[0]
        else:
            w_in_f.append(full_in(got[0])), w_out_f.append(full_out(got[1]))
        hs.append(h), ys.append(y), u2s.append(u2), zs.append(z), xs.append(xn)


    gi_parts, go_parts, g_rows, g_lns, g_pool_w = ([None] * depth for _ in range(5))
    waiting = []
    for l in reversed(range(depth)):
        g_z, g_y, g_w_out, g_lns[l] = _out_proj_bwd(g, zs[l], ys[l], w_out_f[l], r2(ln_g[l]), f"out_proj_bwd_{l}")
        go = g_w_out.reshape(N_DEV, D_MIX // N_DEV, D_MODEL)
        if l == 0:
            waiting += [go]
        g_h, g_rows[l], g_pool_w[l], *got = _mixer_bwd(
            hs[l], u2s[l], g_y, caw_f[l], cbw_f[l], r2(ln_b_g[l]), r2(ln_b_b[l]), pool_w[l], r2(pool_scale[l]),
            f"mixer_bwd_{l}",
            plan=_Exchange(waiting) if waiting else None)
        if got:
            if l + 1 < depth:
                gi_parts[l + 1], go_parts[l + 1] = got[0], got[1]
            if l == 0:
                go_parts[0] = got[-1]
        if l > 0:
            gi = _in_proj_wgrad(xs[l], g_h, f"in_proj_wgrad_{l}")[0]
            waiting = [gi.reshape(N_DEV, D_IN // N_DEV, D_MODEL), go]
            g = _in_proj_dgrad(g_h, w_in_f[l], g_z, f"in_proj_dgrad_{l}")[0]
        else:
            small = [*g_rows, *g_lns, *g_pool_w, loss_rows]
            gi, *small_parts = _in_proj_wgrad(xs[0], g_h, "in_proj_wgrad_0", plan=_Gather(small))
            g, gi_parts[0] = _in_proj_dgrad(g_h, w_in_f[0], g_z, "in_proj_dgrad_0",
                                            plan=_Exchange([gi.reshape(N_DEV, D_IN // N_DEV, D_MODEL)]))
    grad_x = _from_segments(g)[None]

    out_in = [tr(a) for a in _adamw_layers(gi_parts, w_in_t, m_w_in_t, v_w_in_t, "adamw_w_in")]
    out_out = _adamw_layers(go_parts, w_out, m_w_out, v_w_out, "adamw_w_out")
    params = dict(
        conv_a_w=(taps(conv_a_w), taps(m_conv_a_w), taps(v_conv_a_w)), conv_a_b=(conv_a_b, m_conv_a_b, v_conv_a_b),
        conv_b_w=(taps(conv_b_w), taps(m_conv_b_w), taps(v_conv_b_w)), conv_b_b=(conv_b_b, m_conv_b_b, v_conv_b_b),
        ln_b_g=(ln_b_g, m_ln_b_g, v_ln_b_g), ln_b_b=(ln_b_b, m_ln_b_b, v_ln_b_b),
        pool_w=(pool_w, m_pool_w, v_pool_w), pool_b=(pool_b, m_pool_b, v_pool_b),
        pool_scale=(pool_scale, m_pool_scale, v_pool_scale), ln_g=(ln_g, m_ln_g, v_ln_g), ln_b=(ln_b, m_ln_b, v_ln_b))
    loss, *small_out = _adamw_small(small_parts[:depth], small_parts[depth:2 * depth], small_parts[2 * depth:3 * depth],
                                    small_parts[3 * depth], params, "adamw_small")
    small_out = {n: small_out[4 * k:4 * k + 4] for k, n in enumerate(SMALL_PARAMS)}
    for n in ("conv_a_w", "conv_b_w"):
        small_out[n] = [taps(a) for a in small_out[n]]

    order = ("w_in", "conv_a_w", "conv_a_b", "conv_b_w", "conv_b_b", "ln_b_g", "ln_b_b", "pool_w", "pool_b",
             "pool_scale", "w_out", "ln_g", "ln_b")
    outs = []
    for k in range(4):
        outs += [out_in[k] if n == "w_in" else out_out[k] if n == "w_out" else small_out[n][k] for n in order]
    return (loss.reshape(()), grad_x, *outs)
```

```python
import jax
import jax.numpy as jnp
from jax import lax
from jax.experimental import pallas as pl
from jax.experimental.pallas import tpu as pltpu

F32 = jnp.float32
BF16 = jnp.bfloat16

DEPTH = 2
D_MODEL = 1024
W_MIX = 512
D_IN = 9 * W_MIX
D_MIX = 3 * W_MIX
POOL_WINDOWS = (2, 4, 8, 16)
POOL_DIM = 128
KA = 3
KB = 31
ALPHA = (2.0 * DEPTH) ** 0.25
LN_EPS = 1e-5
ADAM_LR, ADAM_B1, ADAM_B2, ADAM_EPS, ADAM_WD, ADAM_STEP = 0.001, 0.9, 0.999, 1e-08, 0.01, 10

N_DEV = 8
MESH = pl.DeviceIdType.MESH

CH = 32
SUB = 8
VMEM_LIMIT = 56 * 1024 * 1024

R_CAB, R_CBB, R_LBG, R_LBB, R_PB, R_PS, R_CAW, R_CBW = 0, 1, 2, 3, 4, 5, 6, 9
N_ROWS = R_CBW + KB


def _cparams(sem, **kw):
    return pltpu.CompilerParams(dimension_semantics=sem, vmem_limit_bytes=VMEM_LIMIT, **kw)


def _sigmoid(v):
    return 1.0 / (1.0 + jnp.exp(-v))


def _fold8(a):
    r, c = a.shape
    return a.reshape(r // SUB, SUB, c).sum(axis=0)


def _dot(a, b):
    return jnp.dot(a, b, preferred_element_type=F32)


def _dot_nt(a, b):
    return lax.dot_general(a, b, (((1,), (1,)), ((), ())), preferred_element_type=F32)


def _dot_tn(a, b):
    return lax.dot_general(a, b, (((0,), (0,)), ((), ())), preferred_element_type=F32)


def _row_tile(r, most=192):
    return max(c for c in range(SUB, min(r, most) + 1, SUB) if r % c == 0)


def _position():
    return lax.axis_index("x"), lax.axis_index("y"), lax.axis_index("c")


class _Gather:
    def __init__(self, arrs):
        self.arrs = list(arrs)
        na = len(self.arrs)
        self.out_shape = [jax.ShapeDtypeStruct((N_DEV,) + a.shape, a.dtype) for a in self.arrs]
        self.scratch = [pltpu.SemaphoreType.DMA((na, 7)), pltpu.SemaphoreType.DMA((na, 7)),
                        pltpu.SemaphoreType.DMA((na,))]

    def _copies(self, src, dst, sems):
        send_sems, recv_sems, local_sems = sems
        na = len(self.arrs)
        x, y, c = _position()
        me, sibling = (x, y, c), (x, y, 1 - c)
        chips = [(1 - x, y), (x, 1 - y), (1 - x, 1 - y)]

        def slot(a, dev):
            return dst[a].at[4 * dev[0] + 2 * dev[1] + dev[2]]

        def copy(a, k, block, to, from_src=False):
            return pltpu.make_async_remote_copy(
                src_ref=src[a] if from_src else slot(a, block), dst_ref=slot(a, block),
                send_sem=send_sems.at[a, k], recv_sem=recv_sems.at[a, k], device_id=to, device_id_type=MESH)

        mine = [pltpu.make_async_copy(src[a], slot(a, me), local_sems.at[a]) for a in range(na)]
        first, landed, passed, last = [], [], [], []
        for a in range(na):
            first.append(copy(a, 0, me, sibling, from_src=True))
            first += [copy(a, 1 + j, me, (*chip, c), from_src=True) for j, chip in enumerate(chips)]
        for j, chip in enumerate(chips):
            for a in range(na):
                landed.append(copy(a, 1 + j, (*chip, c), me))
                passed.append(copy(a, 4 + j, (*chip, c), sibling))
        for a in range(na):
            last.append(copy(a, 0, sibling, me))
            last += [copy(a, 4 + j, (*chip, 1 - c), me) for j, chip in enumerate(chips)]
        return mine, first, landed, passed, last

    def start(self, src, dst, sems):
        mine, first, _, _, _ = self._copies(src, dst, sems)
        for cp in mine + first:
            cp.start()

    def pass_on(self, src, dst, sems):
        _, _, landed, passed, _ = self._copies(src, dst, sems)
        for got, cp in zip(landed, passed):
            got.wait_recv()
            cp.start()

    def finish(self, src, dst, sems):
        mine, first, _, passed, last = self._copies(src, dst, sems)
        for cp in last:
            cp.wait_recv()
        for cp in first + passed:
            cp.wait_send()
        for cp in mine:
            cp.wait()


class _Exchange:
    def __init__(self, arrs):
        self.arrs = list(arrs)
        na = len(self.arrs)
        self.out_shape = [jax.ShapeDtypeStruct(a.shape, a.dtype) for a in self.arrs]
        self.scratch = [pltpu.SemaphoreType.DMA((na, N_DEV - 1)), pltpu.SemaphoreType.DMA((na, N_DEV - 1)),
                        pltpu.SemaphoreType.DMA((na,))]

    def _copies(self, src, dst, sems):
        send_sems, recv_sems, local_sems = sems
        na = len(self.arrs)
        x, y, c = _position()
        me = 4 * x + 2 * y + c
        mine = [pltpu.make_async_copy(src[a].at[me], dst[a].at[me], local_sems.at[a]) for a in range(na)]
        copies = []
        for a in range(na):
            for k in range(N_DEV - 1):
                flip = k + 1
                px, py, pc = x ^ (flip >> 2), y ^ ((flip >> 1) & 1), c ^ (flip & 1)
                copies.append(pltpu.make_async_remote_copy(
                    src_ref=src[a].at[4 * px + 2 * py + pc], dst_ref=dst[a].at[me],
                    send_sem=send_sems.at[a, k], recv_sem=recv_sems.at[a, k], device_id=(px, py, pc),
                    device_id_type=MESH))
        return mine, copies

    def start(self, src, dst, sems):
        mine, copies = self._copies(src, dst, sems)
        for cp in mine + copies:
            cp.start()

    def pass_on(self, src, dst, sems):
        pass

    def finish(self, src, dst, sems):
        mine, copies = self._copies(src, dst, sems)
        for cp in copies:
            cp.wait()
        for cp in mine:
            cp.wait()


def _run_plan(plan, name):
    na = len(plan.arrs)
    any_spec = pl.BlockSpec(memory_space=pl.ANY)

    def body(*refs):
        src, dst, sems = refs[:na], refs[na:2 * na], refs[2 * na:]
        plan.start(src, dst, sems)
        plan.pass_on(src, dst, sems)
        plan.finish(src, dst, sems)

    return pl.pallas_call(
        body, name=name, in_specs=[any_spec] * na, out_specs=[any_spec] * na,
        out_shape=plan.out_shape, scratch_shapes=plan.scratch,
    )(*plan.arrs)


class _Hosted:
    def __init__(self, plan, n_in, n_out, n_scratch):
        self.plan, self.n_in, self.n_out, self.n_scratch = plan, n_in, n_out, n_scratch
        any_spec = pl.BlockSpec(memory_space=pl.ANY)
        k = 0 if plan is None else len(plan.arrs)
        self.operands = [] if plan is None else plan.arrs
        self.in_specs = [any_spec] * k
        self.out_specs = [any_spec] * k
        self.out_shape = [] if plan is None else plan.out_shape
        self.scratch = [] if plan is None else plan.scratch

    def wrap(self, body, phase):
        if self.plan is None:
            return body
        plan, k = self.plan, len(self.plan.arrs)
        i0, o0 = self.n_in, self.n_in + k
        o1 = o0 + self.n_out
        s0 = o1 + k
        s1 = s0 + self.n_scratch

        def hosted(*refs):
            src, dst, sems = refs[i0:o0], refs[o1:s0], refs[s1:]
            first, middle, last = phase()
            pl.when(first)(lambda: plan.start(src, dst, sems))
            body(*refs[:i0], *refs[o0:o1], *refs[s0:s1])
            pl.when(middle)(lambda: plan.pass_on(src, dst, sems))
            pl.when(last)(lambda: plan.finish(src, dst, sems))

        return hosted


TT = 256
SEG = TT // SUB


def _to_segments(a):
    t, c = a.shape
    return a.reshape(t // TT, SUB, SEG, c).transpose(0, 2, 1, 3).reshape(t, c)


def _from_segments(a):
    t, c = a.shape
    return a.reshape(t // TT, SEG, SUB, c).transpose(0, 2, 1, 3).reshape(t, c)


def _sublane_is(s):
    return lax.broadcasted_iota(jnp.int32, (TT, W_MIX), 0) % SUB == s


def _look_back(ext_ref, cur, before):
    ext_ref[TT:, :] = cur
    ext_ref[0:TT, :] = jnp.where(_sublane_is(0), before, pltpu.roll(cur, 1, axis=0))


def _last_segment(cur):
    return pltpu.roll(cur, TT - (SUB - 1), axis=0)


def _look_ahead(ext_ref, cur, after):
    ext_ref[0:TT, :] = cur
    ext_ref[TT:, :] = jnp.where(_sublane_is(SUB - 1), after, pltpu.roll(cur, TT - 1, axis=0))


def _first_segment(cur):
    return pltpu.roll(cur, SUB - 1, axis=0)


def _tap_loop(body):
    lax.fori_loop(0, TT // CH, lambda c, carry: body(pl.multiple_of(c * CH, CH), carry), 0)


TAPS = 8
HALVES = (slice(0, W_MIX // 2), slice(W_MIX // 2, W_MIX))


def _windows(ext_ref, r0, off, nk, lanes):
    for k0 in range(0, nk, TAPS):
        n = min(TAPS, nk - k0)
        win = ext_ref[pl.ds(r0 + (off + k0) * SUB, CH + (n - 1) * SUB), lanes]
        for t in range(n):
            yield k0 + t, win[t * SUB:t * SUB + CH, :]


def _conv_rows(ext_ref, w_ref, out_ref, *, nk, off, reverse):
    def body(r0, carry):
        for lanes in HALVES:
            acc = jnp.zeros((CH, W_MIX // 2), F32)
            for k, rows in _windows(ext_ref, r0, off, nk, lanes):
                kk = nk - 1 - k if reverse else k
                acc = acc + rows * w_ref[kk:kk + 1, lanes]
            out_ref[pl.ds(r0, CH), lanes] = acc
        return carry
    _tap_loop(body)


def _conv_wgrad(g_ref, ext_ref, acc_ref, *, nk, off, row0):
    def body(r0, carry):
        for lanes in HALVES:
            g = g_ref[pl.ds(r0, CH), lanes]
            for k, rows in _windows(ext_ref, r0, off, nk, lanes):
                a = (row0 + k) * SUB
                acc_ref[a:a + SUB, lanes] += _fold8(g * rows)
        return carry
    _tap_loop(body)


def _window_sums(ext_ref, out_ref, *, forward):
    def body(r0, carry):
        for g, w in enumerate(POOL_WINDOWS):
            lanes = slice(g * POOL_DIM, (g + 1) * POOL_DIM)
            acc = jnp.zeros((CH, POOL_DIM), F32)
            for j in range(w):
                off = j if forward else SEG - j
                acc = acc + ext_ref[pl.ds(r0 + off * SUB, CH), lanes]
            out_ref[pl.ds(r0, CH), lanes] = acc
        return carry
    _tap_loop(body)


def _inv_count(tile):
    r = lax.broadcasted_iota(jnp.int32, (TT, POOL_DIM), 0)
    t1 = (tile * TT + (r % SUB) * SEG + r // SUB + 1).astype(F32)
    return jnp.concatenate([1.0 / jnp.minimum(t1, float(w)) for w in POOL_WINDOWS], axis=1)


KEPT = ("u2", "sigmoid(a_z)", "ca", "sigmoid(b_g)", "sigmoid(ln)", "sigmoid(b_z)", "sigmoid(c_z)", "pooled", "p")


def _groups(h_ref):
    return [h_ref[:, k * W_MIX:(k + 1) * W_MIX].astype(F32) for k in range(9)]


def _layer_norm(v, g, b):
    mu = jnp.mean(v, axis=-1, keepdims=True)
    vc = v - mu
    var = jnp.mean(vc * vc, axis=-1, keepdims=True)
    rstd = lax.rsqrt(var + LN_EPS)
    vhat = vc * rstd
    return vhat * g + b, vhat, rstd


def _layer_norm_bwd(g_out, vhat, rstd, g):
    gh = g_out * g
    m1 = jnp.mean(gh, axis=-1, keepdims=True)
    m2 = jnp.mean(gh * vhat, axis=-1, keepdims=True)
    return rstd * (gh - m1 - vhat * m2)


def _pool_linear(pooled, pw_ref, pb_ref):
    outs = []
    for g in range(len(POOL_WINDOWS)):
        lanes = slice(g * POOL_DIM, (g + 1) * POOL_DIM)
        outs.append(_dot(pooled[:, lanes].astype(BF16), pw_ref[g].astype(BF16)))
    return jnp.concatenate(outs, axis=1) + pb_ref[...]


def _in_proj(x, w, name, plan=None):
    t, d = x.shape
    n = w.shape[0]
    tm, tn = min(t, 1024), 2304
    nm, nn = t // tm, n // tn

    def body(x_ref, w_ref, o_ref, xb_ref):
        @pl.when(pl.program_id(1) == 0)
        def _():
            xb_ref[...] = x_ref[...].astype(BF16)
        o_ref[...] = _dot_nt(xb_ref[...], w_ref[...]).astype(BF16)

    def phase():
        step = pl.program_id(0) * nn + pl.program_id(1)
        return step == 0, step == (nm * nn) // 2, step == nm * nn - 1

    host = _Hosted(plan, n_in=2, n_out=1, n_scratch=1)
    return pl.pallas_call(
        host.wrap(body, phase), name=name, grid=(nm, nn),
        in_specs=[pl.BlockSpec((tm, d), lambda i, j: (i, 0)), pl.BlockSpec((tn, d), lambda i, j: (j, 0))]
        + host.in_specs,
        out_specs=[pl.BlockSpec((tm, tn), lambda i, j: (i, j))] + host.out_specs,
        out_shape=[jax.ShapeDtypeStruct((t, n), BF16)] + host.out_shape,
        scratch_shapes=[pltpu.VMEM((tm, d), BF16)] + host.scratch,
        compiler_params=_cparams(("arbitrary", "arbitrary")),
    )(x, w, *host.operands)


def _mixer_fwd(h, x, caw, cab, cbw, cbb, lbg, lbb, pw, pb, ps, w_out, lng, lnb, name, plan=None, target=None):
    t = h.shape[0]
    tt = TT
    n = t // tt
    with_loss = target is not None

    def body(h_ref, x_ref, *refs):
        if with_loss:
            t_ref, refs = refs[0], refs[1:]
        (caw_ref, cab_ref, cbw_ref, cbb_ref, lbg_ref, lbb_ref, pw_ref, pb_ref, ps_ref, wo_ref, lng_ref, lnb_ref,
         y_ref, u2_ref, z_ref, xn_ref) = refs[:16]
        refs = refs[16:]
        if with_loss:
            l_ref, refs = refs[0], refs[1:]
        exta, extb, extc, lasta, lastb, lastc, tmp, inv_ref = refs
        i = pl.program_id(0)

        @pl.when(i == 0)
        def _():
            for e in (lasta, lastb, lastc):
                e[...] = jnp.zeros_like(e)
            if with_loss:
                l_ref[...] = jnp.zeros_like(l_ref)

        @pl.when(i <= 1)
        def _():
            inv_ref[...] = _inv_count(i)

        group = lambda k: h_ref[:, k * W_MIX:(k + 1) * W_MIX].astype(F32)

        def keep(name, v):
            k = KEPT.index(name)
            u2_ref[:, k * W_MIX:(k + 1) * W_MIX] = v

        def look_back(ext, last, cur):
            _look_back(ext, cur, last[...])
            last[...] = _last_segment(cur)

        sgg = _sigmoid(group(5))
        keep("sigmoid(b_g)", sgg)
        look_back(extb, lastb, group(4) * sgg)
        _conv_rows(extb, cbw_ref, tmp, nk=KB, off=SEG - (KB - 1), reverse=False)
        u2 = tmp[...] + cbb_ref[...]
        keep("u2", u2)
        ln, _, _ = _layer_norm(u2, lbg_ref[...], lbb_ref[...])
        b_z = group(6)
        sl, sz = _sigmoid(ln), _sigmoid(b_z)
        keep("sigmoid(ln)", sl), keep("sigmoid(b_z)", sz)
        y_ref[:, W_MIX:2 * W_MIX] = ((ln * sl) * (b_z * sz)).astype(BF16)

        look_back(exta, lasta, group(1) * group(2))
        _conv_rows(exta, caw_ref, tmp, nk=KA, off=SEG - (KA - 1), reverse=False)
        ca = tmp[...] + cab_ref[...]
        a_z = group(3)
        sga = _sigmoid(a_z)
        keep("ca", ca), keep("sigmoid(a_z)", sga)
        y_ref[:, 0:W_MIX] = (group(0) * ca * (a_z * sga)).astype(BF16)

        c_u = group(7)
        look_back(extc, lastc, c_u)
        _window_sums(extc, tmp, forward=False)
        pooled = tmp[...] * inv_ref[...] - c_u
        p = _pool_linear(pooled, pw_ref, pb_ref)
        c_z = group(8)
        sc = _sigmoid(c_z)
        keep("pooled", pooled), keep("p", p), keep("sigmoid(c_z)", sc)
        y_ref[:, 2 * W_MIX:3 * W_MIX] = (p * ps_ref[...] * (c_z * sc)).astype(BF16)

        out = _dot(y_ref[...], wo_ref[...])
        z = ALPHA * x_ref[...] + out
        z_ref[...] = z
        xn, _, _ = _layer_norm(z, lng_ref[...], lnb_ref[...])
        if with_loss:
            e = xn - t_ref[...]
            xn_ref[...] = e * (1.0 / D_MODEL)
            l_ref[...] += _fold8(e * e) * (0.5 / D_MODEL)
        else:
            xn_ref[...] = xn

    def phase():
        i = pl.program_id(0)
        return i == 0, i == n // 2, i == n - 1

    row = lambda wd: pl.BlockSpec((tt, wd), lambda i: (i, 0))
    full = lambda a: pl.BlockSpec(a.shape, lambda i: (0,) * a.ndim)
    params = (caw, cab, cbw, cbb, lbg, lbb, pw, pb, ps, w_out, lng, lnb)
    extra_in = [target] if with_loss else []
    extra_out = [jax.ShapeDtypeStruct((SUB, D_MODEL), F32)] if with_loss else []
    host = _Hosted(plan, n_in=2 + len(extra_in) + len(params), n_out=4 + len(extra_out), n_scratch=8)
    return pl.pallas_call(
        host.wrap(body, phase), name=name, grid=(n,),
        in_specs=[row(D_IN), row(D_MODEL)] + [row(D_MODEL)] * len(extra_in) + [full(a) for a in params]
        + host.in_specs,
        out_specs=[row(D_MIX), row(len(KEPT) * W_MIX), row(D_MODEL), row(D_MODEL)] + [full(o) for o in extra_out]
        + host.out_specs,
        out_shape=[jax.ShapeDtypeStruct((t, D_MIX), BF16), jax.ShapeDtypeStruct((t, len(KEPT) * W_MIX), F32),
                   jax.ShapeDtypeStruct((t, D_MODEL), F32), jax.ShapeDtypeStruct((t, D_MODEL), F32)]
        + extra_out + host.out_shape,
        scratch_shapes=[pltpu.VMEM((2 * tt, W_MIX), F32)] * 3 + [pltpu.VMEM((tt, W_MIX), F32)] * 5 + host.scratch,
        compiler_params=_cparams(("arbitrary",)),
    )(h, x, *extra_in, *params, *host.operands)


def _out_proj_bwd(g_xn, z, y, w_out, lng, name):
    t = z.shape[0]
    tt = min(t, 512)
    n = t // tt

    def body(g_ref, z_ref, y_ref, wo_ref, lng_ref, gz_ref, gy_ref, gwo_ref, gln_ref, accg, accb, accw, gzb):
        i = pl.program_id(0)

        @pl.when(i == 0)
        def _():
            accw[...] = jnp.zeros_like(accw)
            accg[...] = jnp.zeros_like(accg)
            accb[...] = jnp.zeros_like(accb)
            gzb[...] = jnp.zeros_like(gzb)

        before = gzb[(i + 1) % 2]
        gy_ref[...] = _dot_nt(before, wo_ref[...])
        accw[...] += _dot_tn(y_ref[...], before)

        counts = (i < n).astype(F32)
        g = g_ref[...]
        _, zhat, rstd = _layer_norm(z_ref[...], lng_ref[...], 0.0)
        accg[...] += _fold8(g * zhat) * counts
        accb[...] += _fold8(g) * counts
        g_z = _layer_norm_bwd(g, zhat, rstd, lng_ref[...])
        gz_ref[...] = g_z
        gzb[i % 2] = g_z.astype(BF16)

        @pl.when(i == n)
        def _():
            gwo_ref[...] = accw[...].astype(BF16)
            gln_ref[...] = jnp.zeros_like(gln_ref)
            gln_ref[0:1, :] = jnp.sum(accg[...], axis=0, keepdims=True)
            gln_ref[1:2, :] = jnp.sum(accb[...], axis=0, keepdims=True)

    this = lambda wd: pl.BlockSpec((tt, wd), lambda i: (jnp.minimum(i, n - 1), 0))
    last = lambda wd: pl.BlockSpec((tt, wd), lambda i: (jnp.maximum(i - 1, 0), 0))
    full = lambda shape: pl.BlockSpec(shape, lambda i: (0,) * len(shape))
    return pl.pallas_call(
        body, name=name, grid=(n + 1,),
        in_specs=[this(D_MODEL), this(D_MODEL), last(D_MIX), full(w_out.shape), full(lng.shape)],
        out_specs=[this(D_MODEL), last(D_MIX), full((D_MIX, D_MODEL)), full((SUB, D_MODEL))],
        out_shape=[jax.ShapeDtypeStruct((t, D_MODEL), F32), jax.ShapeDtypeStruct((t, D_MIX), F32),
                   jax.ShapeDtypeStruct((D_MIX, D_MODEL), BF16), jax.ShapeDtypeStruct((SUB, D_MODEL), F32)],
        scratch_shapes=[pltpu.VMEM((SUB, D_MODEL), F32)] * 2 + [pltpu.VMEM((D_MIX, D_MODEL), F32),
                                                                 pltpu.VMEM((2, tt, D_MODEL), BF16)],
        compiler_params=_cparams(("arbitrary",)),
    )(g_xn, z, y, w_out, lng)


def _mixer_bwd(h, u2, g_y, caw, cbw, lbg, lbb, pw, ps, name, plan=None):
    t = h.shape[0]
    tt = TT
    n = t // tt
    before_groups = (1, 2, 4, 5)

    def body(h_ref, p_cg, p_av, p_bv, p_bg, u2_ref, gy_ref, caw_ref, cbw_ref, lbg_ref, lbb_ref,
             pw_ref, ps_ref, gh_ref, rows_ref, gpw_ref,
             exta, extb, gca, gu2, qx, nexta, nextb, nextc, tmp, tmp2, inv_ref, acc):
        s = pl.program_id(0)
        i = n - 1 - s

        @pl.when(s == 0)
        def _():
            acc[...] = jnp.zeros_like(acc)
            gpw_ref[...] = jnp.zeros_like(gpw_ref)
            for e in (nexta, nextb, nextc):
                e[...] = jnp.zeros_like(e)

        live = (i > 0).astype(F32)
        f32 = lambda ref: ref[...].astype(F32)
        before_a = _last_segment(f32(p_cg) * f32(p_av)) * live
        before_b = _last_segment(f32(p_bv) * _sigmoid(f32(p_bg))) * live

        a_bg, a_cg, a_v, a_z, b_v, b_g, b_z, c_u, c_z = _groups(h_ref)
        u2, sg, ca, sgg, sl, sz, sc, pooled, p = (u2_ref[:, k * W_MIX:(k + 1) * W_MIX] for k in range(len(KEPT)))
        g_ya = gy_ref[:, 0:W_MIX]
        g_yb = gy_ref[:, W_MIX:2 * W_MIX]
        g_yc = gy_ref[:, 2 * W_MIX:3 * W_MIX]

        def add_row(r, v):
            acc[r * SUB:(r + 1) * SUB, :] += _fold8(v)

        _look_back(exta, a_cg * a_v, before_a)
        s_az = a_z * sg
        t_a = g_ya * a_bg
        gh_ref[:, 0:W_MIX] = (g_ya * ca * s_az).astype(BF16)
        gh_ref[:, 3 * W_MIX:4 * W_MIX] = (t_a * ca * (sg * (1.0 + a_z * (1.0 - sg)))).astype(BF16)
        g_ca = t_a * s_az
        _look_ahead(gca, g_ca, nexta[...])
        nexta[...] = _first_segment(g_ca)
        add_row(R_CAB, g_ca)
        _conv_wgrad(gca, exta, acc, nk=KA, off=SEG - (KA - 1), row0=R_CAW)
        _conv_rows(gca, caw_ref, tmp, nk=KA, off=0, reverse=True)
        g_pa = tmp[...]
        gh_ref[:, W_MIX:2 * W_MIX] = (g_pa * a_v).astype(BF16)
        gh_ref[:, 2 * W_MIX:3 * W_MIX] = (g_pa * a_cg).astype(BF16)

        _look_back(extb, b_v * sgg, before_b)
        ln, u2hat, rstd = _layer_norm(u2, lbg_ref[...], lbb_ref[...])
        u3 = ln * sl
        s_bz = b_z * sz
        gh_ref[:, 6 * W_MIX:7 * W_MIX] = (g_yb * u3 * (sz * (1.0 + b_z * (1.0 - sz)))).astype(BF16)
        g_ln = g_yb * s_bz * (sl * (1.0 + ln * (1.0 - sl)))
        add_row(R_LBG, g_ln * u2hat)
        add_row(R_LBB, g_ln)
        g_u2 = _layer_norm_bwd(g_ln, u2hat, rstd, lbg_ref[...])
        _look_ahead(gu2, g_u2, nextb[...])
        nextb[...] = _first_segment(g_u2)
        add_row(R_CBB, g_u2)
        _conv_wgrad(gu2, extb, acc, nk=KB, off=SEG - (KB - 1), row0=R_CBW)
        _conv_rows(gu2, cbw_ref, tmp, nk=KB, off=0, reverse=True)
        g_u1 = tmp[...]
        gh_ref[:, 4 * W_MIX:5 * W_MIX] = (g_u1 * sgg).astype(BF16)
        gh_ref[:, 5 * W_MIX:6 * W_MIX] = (g_u1 * b_v * sgg * (1.0 - sgg)).astype(BF16)

        @pl.when((s == 0) | (i == 0))
        def _():
            inv_ref[...] = _inv_count(i)
        inv = inv_ref[...]
        s_cz = c_z * sc
        scale = ps_ref[...]
        gh_ref[:, 8 * W_MIX:9 * W_MIX] = (g_yc * p * scale * (sc * (1.0 + c_z * (1.0 - sc)))).astype(BF16)
        t_c = g_yc * s_cz
        add_row(R_PS, t_c * p)
        g_p = t_c * scale
        add_row(R_PB, g_p)
        g_pooled = []
        for g in range(len(POOL_WINDOWS)):
            lanes = slice(g * POOL_DIM, (g + 1) * POOL_DIM)
            gpg = g_p[:, lanes].astype(BF16)
            gpw_ref[g] += _dot_tn(pooled[:, lanes].astype(BF16), gpg)
            g_pooled.append(_dot_nt(gpg, pw_ref[g].astype(BF16)))
        g_pooled = jnp.concatenate(g_pooled, axis=1)
        q = g_pooled * inv
        _look_ahead(qx, q, nextc[...])
        nextc[...] = _first_segment(q)
        _window_sums(qx, tmp2, forward=True)
        gh_ref[:, 7 * W_MIX:8 * W_MIX] = (tmp2[...] - g_pooled).astype(BF16)

        @pl.when(s == n - 1)
        def _():
            for r in range(N_ROWS):
                rows_ref[r:r + 1, :] = jnp.sum(acc[r * SUB:(r + 1) * SUB, :], axis=0, keepdims=True)

    def phase():
        s = pl.program_id(0)
        return s == 0, s == n // 2, s == n - 1

    row = lambda wd: pl.BlockSpec((tt, wd), lambda s: (n - 1 - s, 0))
    before = [pl.BlockSpec((tt, W_MIX), lambda s, k=k: (jnp.maximum(n - 2 - s, 0), k)) for k in before_groups]
    full = lambda shape: pl.BlockSpec(shape, lambda s: (0,) * len(shape))
    params = (caw, cbw, lbg, lbb, pw, ps)
    host = _Hosted(plan, n_in=3 + len(before) + len(params), n_out=3, n_scratch=12)
    return pl.pallas_call(
        host.wrap(body, phase), name=name, grid=(n,),
        in_specs=[row(D_IN)] + before + [row(len(KEPT) * W_MIX), row(D_MIX)] + [full(a.shape) for a in params]
        + host.in_specs,
        out_specs=[row(D_IN), full((N_ROWS, W_MIX)), full(pw.shape)] + host.out_specs,
        out_shape=[jax.ShapeDtypeStruct((t, D_IN), BF16), jax.ShapeDtypeStruct((N_ROWS, W_MIX), F32),
                   jax.ShapeDtypeStruct(pw.shape, F32)] + host.out_shape,
        scratch_shapes=[pltpu.VMEM((2 * tt, W_MIX), F32)] * 5 + [pltpu.VMEM((tt, W_MIX), F32)] * 6
        + [pltpu.VMEM((N_ROWS * SUB, W_MIX), F32)] + host.scratch,
        compiler_params=_cparams(("arbitrary",)),
    )(h, *([h] * len(before)), u2, g_y, *params, *host.operands)


def _in_proj_wgrad(x, g_h, name, plan=None):
    t, d = x.shape
    n = g_h.shape[1]
    tk, tn = min(t, 1024), n // 2
    nk = t // tk

    def body(x_ref, g_ref, o_ref, acc):
        k = pl.program_id(1)

        @pl.when(k == 0)
        def _():
            acc[...] = jnp.zeros_like(acc)
        acc[...] += _dot_tn(x_ref[...].astype(BF16), g_ref[...])

        @pl.when(k == nk - 1)
        def _():
            o_ref[...] = acc[...].T.astype(BF16)

    def phase():
        step = pl.program_id(0) * nk + pl.program_id(1)
        return step == 0, step == nk, step == 2 * nk - 1

    host = _Hosted(plan, n_in=2, n_out=1, n_scratch=1)
    return pl.pallas_call(
        host.wrap(body, phase), name=name, grid=(n // tn, nk),
        in_specs=[pl.BlockSpec((tk, d), lambda j, k: (k, 0)), pl.BlockSpec((tk, tn), lambda j, k: (k, j))]
        + host.in_specs,
        out_specs=[pl.BlockSpec((tn, d), lambda j, k: (j, 0))] + host.out_specs,
        out_shape=[jax.ShapeDtypeStruct((n, d), BF16)] + host.out_shape,
        scratch_shapes=[pltpu.VMEM((d, tn), F32)] + host.scratch,
        compiler_params=_cparams(("arbitrary", "arbitrary")),
    )(x, g_h, *host.operands)


def _in_proj_dgrad(g_h, w, g_z, name, plan=None):
    t, n = g_h.shape
    d = w.shape[1]
    tm, tk = min(t, 1024), 2304
    nm, nk = t // tm, n // tk

    def body(g_ref, w_ref, gz_ref, o_ref):
        @pl.when(pl.program_id(1) == 0)
        def _():
            o_ref[...] = ALPHA * gz_ref[...]
        o_ref[...] += _dot(g_ref[...], w_ref[...])

    def phase():
        step = pl.program_id(0) * nk + pl.program_id(1)
        return step == 0, step == (nm * nk) // 2, step == nm * nk - 1

    host = _Hosted(plan, n_in=3, n_out=1, n_scratch=0)
    return pl.pallas_call(
        host.wrap(body, phase), name=name, grid=(nm, nk),
        in_specs=[pl.BlockSpec((tm, tk), lambda i, k: (i, k)), pl.BlockSpec((tk, d), lambda i, k: (k, 0)),
                  pl.BlockSpec((tm, d), lambda i, k: (i, 0))] + host.in_specs,
        out_specs=[pl.BlockSpec((tm, d), lambda i, k: (i, 0))] + host.out_specs,
        out_shape=[jax.ShapeDtypeStruct((t, d), F32)] + host.out_shape,
        scratch_shapes=host.scratch,
        compiler_params=_cparams(("arbitrary", "arbitrary")),
    )(g_h, w, g_z, *host.operands)


BC1 = 1.0 - ADAM_B1 ** ADAM_STEP
BC2 = 1.0 - ADAM_B2 ** ADAM_STEP


def _adamw_math(g, w, m, v):
    nm = ADAM_B1 * m + (1.0 - ADAM_B1) * g
    nv = ADAM_B2 * v + (1.0 - ADAM_B2) * (g * g)
    delta = -ADAM_LR * ((nm / BC1) / (jnp.sqrt(nv / BC2) + ADAM_EPS) + ADAM_WD * w)
    return delta, nm, nv


def _total(ref):
    g = ref[0].astype(F32)
    for k in range(1, ref.shape[0]):
        g = g + ref[k].astype(F32)
    return g


def _adamw_layers(parts, w, m, v, name):
    depth, r, c = w.shape
    p = parts[0].shape[0]
    tr = _row_tile(r)
    nr = r // tr

    def body(*refs):
        p_refs = refs[:depth]
        w_ref, m_ref, v_ref, g_ref, d_ref, nm_ref, nv_ref = refs[depth:]
        for l in range(depth):
            @pl.when(pl.program_id(0) == l)
            def _(l=l):
                g = _total(p_refs[l])
                delta, nm, nv = _adamw_math(g, w_ref[0], m_ref[0], v_ref[0])
                g_ref[0], d_ref[0], nm_ref[0], nv_ref[0] = g, delta, nm, nv

    def part_spec(l):
        return pl.BlockSpec((p, tr, c), lambda li, i: (0, jnp.where(li == l, i, jnp.where(li < l, 0, nr - 1)), 0))

    blk = pl.BlockSpec((1, tr, c), lambda li, i: (li, i, 0))
    out = jax.ShapeDtypeStruct((depth, r, c), F32)
    return pl.pallas_call(
        body, name=name, grid=(depth, nr),
        in_specs=[part_spec(l) for l in range(depth)] + [blk] * 3,
        out_specs=[blk] * 4, out_shape=[out] * 4,
        compiler_params=_cparams(("arbitrary", "arbitrary")),
    )(*parts, w, m, v)


SMALL_PARAMS = ("conv_a_w", "conv_a_b", "conv_b_w", "conv_b_b", "ln_b_g", "ln_b_b", "pool_w", "pool_b", "pool_scale",
                "ln_g", "ln_b")


def _adamw_small(rows_parts, gln_parts, gpw_parts, loss_parts, params, name):
    depth = len(rows_parts)
    cs = params["conv_a_w"][0].shape[2]
    operands = [*rows_parts, *gln_parts, *gpw_parts, loss_parts] + [a for n in SMALL_PARAMS for a in params[n]]
    n_in = len(operands)
    out_shape = [jax.ShapeDtypeStruct((1, 1), F32)]
    out_shape += [jax.ShapeDtypeStruct(params[n][0].shape, F32) for n in SMALL_PARAMS for _ in range(4)]

    def body(*refs):
        rows_p, gln_p, gpw_p = refs[:depth], refs[depth:2 * depth], refs[2 * depth:3 * depth]
        loss_p = refs[3 * depth]
        prm, outs = refs[3 * depth + 1:n_in], refs[n_in + 1:]
        refs[n_in][...] = jnp.sum(_total(loss_p)).reshape(1, 1)
        x, y, c = _position()
        to_front = (W_MIX - (4 * x + 2 * y + c) * cs) % W_MIX

        def update(name, g, at):
            k = SMALL_PARAMS.index(name)
            w_ref, m_ref, v_ref = prm[3 * k:3 * k + 3]
            g_ref, d_ref, nm_ref, nv_ref = outs[4 * k:4 * k + 4]
            delta, nm, nv = _adamw_math(g, w_ref[at], m_ref[at], v_ref[at])
            g_ref[at], d_ref[at], nm_ref[at], nv_ref[at] = g, delta, nm, nv

        for l in range(depth):
            rows = _total(rows_p[l])
            mine = pltpu.roll(rows, to_front, axis=1)
            gln = _total(gln_p[l])
            one = (slice(l, l + 1), slice(None))
            for name, r in (("conv_a_b", R_CAB), ("conv_b_b", R_CBB), ("ln_b_g", R_LBG), ("ln_b_b", R_LBB),
                            ("pool_scale", R_PS)):
                update(name, rows[r:r + 1, :], one)
            for g in range(len(POOL_WINDOWS)):
                update("pool_b", rows[R_PB:R_PB + 1, g * POOL_DIM:(g + 1) * POOL_DIM], (l, slice(g, g + 1), slice(None)))
            update("ln_g", gln[0:1, :], one)
            update("ln_b", gln[1:2, :], one)
            update("conv_a_w", mine[R_CAW:R_CAW + KA, 0:cs], (slice(None), l, slice(None)))
            update("conv_b_w", mine[R_CBW:R_CBW + KB, 0:cs], (slice(None), l, slice(None)))
            update("pool_w", _total(gpw_p[l]), (l,))

    vmem = pl.BlockSpec(memory_space=pltpu.VMEM)
    return pl.pallas_call(
        body, name=name, in_specs=[vmem] * n_in, out_specs=[vmem] * len(out_shape), out_shape=out_shape,
        compiler_params=pltpu.CompilerParams(vmem_limit_bytes=VMEM_LIMIT),
    )(*operands)


def kernel(x, w_in, conv_a_w, conv_a_b, conv_b_w, conv_b_b, ln_b_g, ln_b_b, pool_w, pool_b, pool_scale, w_out, ln_g, ln_b, loss_target, m_w_in, m_conv_a_w, m_conv_a_b, m_conv_b_w, m_conv_b_b, m_ln_b_g, m_ln_b_b, m_pool_w, m_pool_b, m_pool_scale, m_w_out, m_ln_g, m_ln_b, v_w_in, v_conv_a_w, v_conv_a_b, v_conv_b_w, v_conv_b_b, v_ln_b_g, v_ln_b_b, v_pool_w, v_pool_b, v_pool_scale, v_w_out, v_ln_g, v_ln_b):
    depth = w_in.shape[0]
    x0 = _to_segments(x[0])
    target = _to_segments(loss_target[0])
    r2 = lambda a: a.reshape(1, -1)

    tr = lambda a: jnp.swapaxes(a, 1, 2)
    w_in_t, m_w_in_t, v_w_in_t = tr(w_in), tr(m_w_in), tr(v_w_in)
    w_in_b, w_out_b = w_in_t.astype(BF16), w_out.astype(BF16)
    taps = lambda a: jnp.swapaxes(a, 0, 1)
    conv_sh = jnp.concatenate([taps(conv_a_w), taps(conv_b_w)], axis=0)
    full_in = lambda g: g.reshape(D_IN, D_MODEL)
    full_out = lambda g: g.reshape(D_MIX, D_MODEL)
    w_in_f = [full_in(_run_plan(_Gather([w_in_b[0]]), "gather_w_in_0")[0])]
    w_out_f = []

    xs, hs, ys, u2s, zs = [x0], [], [], [], []
    for l in range(depth):
        h, *got = _in_proj(xs[l], w_in_f[l], f"in_proj_{l}",
                           plan=_Gather([w_out_b[0], conv_sh]) if l == 0 else None)
        if got:
            w_out_f.append(full_out(got[0]))
            conv_f = got[1].transpose(2, 1, 0, 3).reshape(depth, KA + KB, W_MIX)
            caw_f, cbw_f = conv_f[:, :KA], conv_f[:, KA:]
        last = l + 1 == depth
        y, u2, z, xn, *got = _mixer_fwd(
            h, xs[l], caw_f[l], r2(conv_a_b[l]), cbw_f[l], r2(conv_b_b[l]), r2(ln_b_g[l]), r2(ln_b_b[l]),
            pool_w[l], r2(pool_b[l]), r2(pool_scale[l]), w_out_f[l], r2(ln_g[l]), r2(ln_b[l]), f"mixer_fwd_{l}",
            plan=None if last else _Gather([w_in_b[l + 1], w_out_b[l + 1]]), target=target if last else None)
        if last:
            g, loss_rows = xn, got[0]
        else:
            w_in_f.append(full_in(got[0])), w_out_f.append(full_out(got[1]))
        hs.append(h), ys.append(y), u2s.append(u2), zs.append(z), xs.append(xn)


    gi_parts, go_parts, g_rows, g_lns, g_pool_w = ([None] * depth for _ in range(5))
    waiting = []
    for l in reversed(range(depth)):
        g_z, g_y, g_w_out, g_lns[l] = _out_proj_bwd(g, zs[l], ys[l], w_out_f[l], r2(ln_g[l]), f"out_proj_bwd_{l}")
        go = g_w_out.reshape(N_DEV, D_MIX // N_DEV, D_MODEL)
        if l == 0:
            waiting += [go]
        g_h, g_rows[l], g_pool_w[l], *got = _mixer_bwd(
            hs[l], u2s[l], g_y, caw_f[l], cbw_f[l], r2(ln_b_g[l]), r2(ln_b_b[l]), pool_w[l], r2(pool_scale[l]),
            f"mixer_bwd_{l}",
            plan=_Exchange(waiting) if waiting else None)
        if got:
            if l + 1 < depth:
                gi_parts[l + 1], go_parts[l + 1] = got[0], got[1]
            if l == 0:
                go_parts[0] = got[-1]
        if l > 0:
            gi = _in_proj_wgrad(xs[l], g_h, f"in_proj_wgrad_{l}")[0]
            waiting = [gi.reshape(N_DEV, D_IN // N_DEV, D_MODEL), go]
            g = _in_proj_dgrad(g_h, w_in_f[l], g_z, f"in_proj_dgrad_{l}")[0]
        else:
            small = [*g_rows, *g_lns, *g_pool_w, loss_rows]
            gi, *small_parts = _in_proj_wgrad(xs[0], g_h, "in_proj_wgrad_0", plan=_Gather(small))
            g, gi_parts[0] = _in_proj_dgrad(g_h, w_in_f[0], g_z, "in_proj_dgrad_0",
                                            plan=_Exchange([gi.reshape(N_DEV, D_IN // N_DEV, D_MODEL)]))
    grad_x = _from_segments(g)[None]

    out_in = [tr(a) for a in _adamw_layers(gi_parts, w_in_t, m_w_in_t, v_w_in_t, "adamw_w_in")]
    out_out = _adamw_layers(go_parts, w_out, m_w_out, v_w_out, "adamw_w_out")
    params = dict(
        conv_a_w=(taps(conv_a_w), taps(m_conv_a_w), taps(v_conv_a_w)), conv_a_b=(conv_a_b, m_conv_a_b, v_conv_a_b),
        conv_b_w=(taps(conv_b_w), taps(m_conv_b_w), taps(v_conv_b_w)), conv_b_b=(conv_b_b, m_conv_b_b, v_conv_b_b),
        ln_b_g=(ln_b_g, m_ln_b_g, v_ln_b_g), ln_b_b=(ln_b_b, m_ln_b_b, v_ln_b_b),
        pool_w=(pool_w, m_pool_w, v_pool_w), pool_b=(pool_b, m_pool_b, v_pool_b),
        pool_scale=(pool_scale, m_pool_scale, v_pool_scale), ln_g=(ln_g, m_ln_g, v_ln_g), ln_b=(ln_b, m_ln_b, v_ln_b))
    loss, *small_out = _adamw_small(small_parts[:depth], small_parts[depth:2 * depth], small_parts[2 * depth:3 * depth],
                                    small_parts[3 * depth], params, "adamw_small")
    small_out = {n: small_out[4 * k:4 * k + 4] for k, n in enumerate(SMALL_PARAMS)}
    for n in ("conv_a_w", "conv_b_w"):
        small_out[n] = [taps(a) for a in small_out[n]]

    order = ("w_in", "conv_a_w", "conv_a_b", "conv_b_w", "conv_b_b", "ln_b_g", "ln_b_b", "pool_w", "pool_b",
             "pool_scale", "w_out", "ln_g", "ln_b")
    outs = []
    for k in range(4):
        outs += [out_in[k] if n == "w_in" else out_out[k] if n == "w_out" else small_out[n][k] for n in order]
    return (loss.reshape(()), grad_x, *outs)
```

```python
import jax
import jax.numpy as jnp
from jax import lax
from jax.experimental import pallas as pl
from jax.experimental.pallas import tpu as pltpu

F32 = jnp.float32
BF16 = jnp.bfloat16

DEPTH = 2
D_MODEL = 1024
W_MIX = 512
D_IN = 9 * W_MIX
D_MIX = 3 * W_MIX
POOL_WINDOWS = (2, 4, 8, 16)
POOL_DIM = 128
KA = 3
KB = 31
ALPHA = (2.0 * DEPTH) ** 0.25
LN_EPS = 1e-5
ADAM_LR, ADAM_B1, ADAM_B2, ADAM_EPS, ADAM_WD, ADAM_STEP = 0.001, 0.9, 0.999, 1e-08, 0.01, 10

N_DEV = 8
MESH = pl.DeviceIdType.MESH

CH = 32
SUB = 8
VMEM_LIMIT = 56 * 1024 * 1024

R_CAB, R_CBB, R_LBG, R_LBB, R_PB, R_PS, R_CAW, R_CBW = 0, 1, 2, 3, 4, 5, 6, 9
N_ROWS = R_CBW + KB


def _cparams(sem, **kw):
    return pltpu.CompilerParams(dimension_semantics=sem, vmem_limit_bytes=VMEM_LIMIT, **kw)


def _sigmoid(v):
    return 1.0 / (1.0 + jnp.exp(-v))


def _fold8(a):
    r, c = a.shape
    return a.reshape(r // SUB, SUB, c).sum(axis=0)


def _dot(a, b):
    return jnp.dot(a, b, preferred_element_type=F32)


def _dot_nt(a, b):
    return lax.dot_general(a, b, (((1,), (1,)), ((), ())), preferred_element_type=F32)


def _dot_tn(a, b):
    return lax.dot_general(a, b, (((0,), (0,)), ((), ())), preferred_element_type=F32)


def _row_tile(r, most=192):
    return max(c for c in range(SUB, min(r, most) + 1, SUB) if r % c == 0)


def _position():
    return lax.axis_index("x"), lax.axis_index("y"), lax.axis_index("c")


class _Gather:
    def __init__(self, arrs):
        self.arrs = list(arrs)
        na = len(self.arrs)
        self.out_shape = [jax.ShapeDtypeStruct((N_DEV,) + a.shape, a.dtype) for a in self.arrs]
        self.scratch = [pltpu.SemaphoreType.DMA((na, 7)), pltpu.SemaphoreType.DMA((na, 7)),
                        pltpu.SemaphoreType.DMA((na,))]

    def _copies(self, src, dst, sems):
        send_sems, recv_sems, local_sems = sems
        na = len(self.arrs)
        x, y, c = _position()
        me, sibling = (x, y, c), (x, y, 1 - c)
        chips = [(1 - x, y), (x, 1 - y), (1 - x, 1 - y)]

        def slot(a, dev):
            return dst[a].at[4 * dev[0] + 2 * dev[1] + dev[2]]

        def copy(a, k, block, to, from_src=False):
            return pltpu.make_async_remote_copy(
                src_ref=src[a] if from_src else slot(a, block), dst_ref=slot(a, block),
                send_sem=send_sems.at[a, k], recv_sem=recv_sems.at[a, k], device_id=to, device_id_type=MESH)

        mine = [pltpu.make_async_copy(src[a], slot(a, me), local_sems.at[a]) for a in range(na)]
        first, landed, passed, last = [], [], [], []
        for a in range(na):
            first.append(copy(a, 0, me, sibling, from_src=True))
            first += [copy(a, 1 + j, me, (*chip, c), from_src=True) for j, chip in enumerate(chips)]
        for j, chip in enumerate(chips):
            for a in range(na):
                landed.append(copy(a, 1 + j, (*chip, c), me))
                passed.append(copy(a, 4 + j, (*chip, c), sibling))
        for a in range(na):
            last.append(copy(a, 0, sibling, me))
            last += [copy(a, 4 + j, (*chip, 1 - c), me) for j, chip in enumerate(chips)]
        return mine, first, landed, passed, last

    def start(self, src, dst, sems):
        mine, first, _, _, _ = self._copies(src, dst, sems)
        for cp in mine + first:
            cp.start()

    def pass_on(self, src, dst, sems):
        _, _, landed, passed, _ = self._copies(src, dst, sems)
        for got, cp in zip(landed, passed):
            got.wait_recv()
            cp.start()

    def finish(self, src, dst, sems):
        mine, first, _, passed, last = self._copies(src, dst, sems)
        for cp in last:
            cp.wait_recv()
        for cp in first + passed:
            cp.wait_send()
        for cp in mine:
            cp.wait()


class _Exchange:
    def __init__(self, arrs):
        self.arrs = list(arrs)
        na = len(self.arrs)
        self.out_shape = [jax.ShapeDtypeStruct(a.shape, a.dtype) for a in self.arrs]
        self.scratch = [pltpu.SemaphoreType.DMA((na, N_DEV - 1)), pltpu.SemaphoreType.DMA((na, N_DEV - 1)),
                        pltpu.SemaphoreType.DMA((na,))]

    def _copies(self, src, dst, sems):
        send_sems, recv_sems, local_sems = sems
        na = len(self.arrs)
        x, y, c = _position()
        me = 4 * x + 2 * y + c
        mine = [pltpu.make_async_copy(src[a].at[me], dst[a].at[me], local_sems.at[a]) for a in range(na)]
        copies = []
        for a in range(na):
            for k in range(N_DEV - 1):
                flip = k + 1
                px, py, pc = x ^ (flip >> 2), y ^ ((flip >> 1) & 1), c ^ (flip & 1)
                copies.append(pltpu.make_async_remote_copy(
                    src_ref=src[a].at[4 * px + 2 * py + pc], dst_ref=dst[a].at[me],
                    send_sem=send_sems.at[a, k], recv_sem=recv_sems.at[a, k], device_id=(px, py, pc),
                    device_id_type=MESH))
        return mine, copies

    def start(self, src, dst, sems):
        mine, copies = self._copies(src, dst, sems)
        for cp in mine + copies:
            cp.start()

    def pass_on(self, src, dst, sems):
        pass

    def finish(self, src, dst, sems):
        mine, copies = self._copies(src, dst, sems)
        for cp in copies:
            cp.wait()
        for cp in mine:
            cp.wait()


def _run_plan(plan, name):
    na = len(plan.arrs)
    any_spec = pl.BlockSpec(memory_space=pl.ANY)

    def body(*refs):
        src, dst, sems = refs[:na], refs[na:2 * na], refs[2 * na:]
        plan.start(src, dst, sems)
        plan.pass_on(src, dst, sems)
        plan.finish(src, dst, sems)

    return pl.pallas_call(
        body, name=name, in_specs=[any_spec] * na, out_specs=[any_spec] * na,
        out_shape=plan.out_shape, scratch_shapes=plan.scratch,
    )(*plan.arrs)


class _Hosted:
    def __init__(self, plan, n_in, n_out, n_scratch):
        self.plan, self.n_in, self.n_out, self.n_scratch = plan, n_in, n_out, n_scratch
        any_spec = pl.BlockSpec(memory_space=pl.ANY)
        k = 0 if plan is None else len(plan.arrs)
        self.operands = [] if plan is None else plan.arrs
        self.in_specs = [any_spec] * k
        self.out_specs = [any_spec] * k
        self.out_shape = [] if plan is None else plan.out_shape
        self.scratch = [] if plan is None else plan.scratch

    def wrap(self, body, phase):
        if self.plan is None:
            return body
        plan, k = self.plan, len(self.plan.arrs)
        i0, o0 = self.n_in, self.n_in + k
        o1 = o0 + self.n_out
        s0 = o1 + k
        s1 = s0 + self.n_scratch

        def hosted(*refs):
            src, dst, sems = refs[i0:o0], refs[o1:s0], refs[s1:]
            first, middle, last = phase()
            pl.when(first)(lambda: plan.start(src, dst, sems))
            body(*refs[:i0], *refs[o0:o1], *refs[s0:s1])
            pl.when(middle)(lambda: plan.pass_on(src, dst, sems))
            pl.when(last)(lambda: plan.finish(src, dst, sems))

        return hosted


TT = 256
SEG = TT // SUB


def _to_segments(a):
    t, c = a.shape
    return a.reshape(t // TT, SUB, SEG, c).transpose(0, 2, 1, 3).reshape(t, c)


def _from_segments(a):
    t, c = a.shape
    return a.reshape(t // TT, SEG, SUB, c).transpose(0, 2, 1, 3).reshape(t, c)


def _sublane_is(s):
    return lax.broadcasted_iota(jnp.int32, (TT, W_MIX), 0) % SUB == s


def _look_back(ext_ref, cur, before):
    ext_ref[TT:, :] = cur
    ext_ref[0:TT, :] = jnp.where(_sublane_is(0), before, pltpu.roll(cur, 1, axis=0))


def _last_segment(cur):
    return pltpu.roll(cur, TT - (SUB - 1), axis=0)


def _look_ahead(ext_ref, cur, after):
    ext_ref[0:TT, :] = cur
    ext_ref[TT:, :] = jnp.where(_sublane_is(SUB - 1), after, pltpu.roll(cur, TT - 1, axis=0))


def _first_segment(cur):
    return pltpu.roll(cur, SUB - 1, axis=0)


def _tap_loop(body):
    lax.fori_loop(0, TT // CH, lambda c, carry: body(pl.multiple_of(c * CH, CH), carry), 0)


TAPS = 8
HALVES = (slice(0, W_MIX // 2), slice(W_MIX // 2, W_MIX))


def _windows(ext_ref, r0, off, nk, lanes):
    for k0 in range(0, nk, TAPS):
        n = min(TAPS, nk - k0)
        win = ext_ref[pl.ds(r0 + (off + k0) * SUB, CH + (n - 1) * SUB), lanes]
        for t in range(n):
            yield k0 + t, win[t * SUB:t * SUB + CH, :]


def _conv_rows(ext_ref, w_ref, out_ref, *, nk, off, reverse):
    def body(r0, carry):
        for lanes in HALVES:
            acc = jnp.zeros((CH, W_MIX // 2), F32)
            for k, rows in _windows(ext_ref, r0, off, nk, lanes):
                kk = nk - 1 - k if reverse else k
                acc = acc + rows * w_ref[kk:kk + 1, lanes]
            out_ref[pl.ds(r0, CH), lanes] = acc
        return carry
    _tap_loop(body)


def _conv_wgrad(g_ref, ext_ref, acc_ref, *, nk, off, row0):
    def body(r0, carry):
        for lanes in HALVES:
            g = g_ref[pl.ds(r0, CH), lanes]
            for k, rows in _windows(ext_ref, r0, off, nk, lanes):
                a = (row0 + k) * SUB
                acc_ref[a:a + SUB, lanes] += _fold8(g * rows)
        return carry
    _tap_loop(body)


def _window_sums(ext_ref, out_ref, *, forward):
    def body(r0, carry):
        for g, w in enumerate(POOL_WINDOWS):
            lanes = slice(g * POOL_DIM, (g + 1) * POOL_DIM)
            first = 0 if forward else SEG - (w - 1)
            win = ext_ref[pl.ds(r0 + first * SUB, CH + (w - 1) * SUB), lanes]
            acc = win[0:CH, :]
            for j in range(1, w):
                acc = acc + win[j * SUB:j * SUB + CH, :]
            out_ref[pl.ds(r0, CH), lanes] = acc
        return carry
    _tap_loop(body)


def _inv_count(tile):
    r = lax.broadcasted_iota(jnp.int32, (TT, POOL_DIM), 0)
    t1 = (tile * TT + (r % SUB) * SEG + r // SUB + 1).astype(F32)
    return jnp.concatenate([1.0 / jnp.minimum(t1, float(w)) for w in POOL_WINDOWS], axis=1)


KEPT = ("u2", "sigmoid(a_z)", "ca", "sigmoid(b_g)", "sigmoid(ln)", "sigmoid(b_z)", "sigmoid(c_z)", "pooled", "p")


def _groups(h_ref):
    return [h_ref[:, k * W_MIX:(k + 1) * W_MIX].astype(F32) for k in range(9)]


def _layer_norm(v, g, b):
    mu = jnp.mean(v, axis=-1, keepdims=True)
    vc = v - mu
    var = jnp.mean(vc * vc, axis=-1, keepdims=True)
    rstd = lax.rsqrt(var + LN_EPS)
    vhat = vc * rstd
    return vhat * g + b, vhat, rstd


def _layer_norm_bwd(g_out, vhat, rstd, g):
    gh = g_out * g
    m1 = jnp.mean(gh, axis=-1, keepdims=True)
    m2 = jnp.mean(gh * vhat, axis=-1, keepdims=True)
    return rstd * (gh - m1 - vhat * m2)


def _pool_linear(pooled, pw_ref, pb_ref):
    outs = []
    for g in range(len(POOL_WINDOWS)):
        lanes = slice(g * POOL_DIM, (g + 1) * POOL_DIM)
        outs.append(_dot(pooled[:, lanes].astype(BF16), pw_ref[g].astype(BF16)))
    return jnp.concatenate(outs, axis=1) + pb_ref[...]


def _in_proj(x, w, name, plan=None):
    t, d = x.shape
    n = w.shape[0]
    tm, tn = min(t, 1024), 2304
    nm, nn = t // tm, n // tn

    def body(x_ref, w_ref, o_ref, xb_ref):
        @pl.when(pl.program_id(1) == 0)
        def _():
            xb_ref[...] = x_ref[...].astype(BF16)
        o_ref[...] = _dot_nt(xb_ref[...], w_ref[...]).astype(BF16)

    def phase():
        step = pl.program_id(0) * nn + pl.program_id(1)
        return step == 0, step == (nm * nn) // 2, step == nm * nn - 1

    host = _Hosted(plan, n_in=2, n_out=1, n_scratch=1)
    return pl.pallas_call(
        host.wrap(body, phase), name=name, grid=(nm, nn),
        in_specs=[pl.BlockSpec((tm, d), lambda i, j: (i, 0)), pl.BlockSpec((tn, d), lambda i, j: (j, 0))]
        + host.in_specs,
        out_specs=[pl.BlockSpec((tm, tn), lambda i, j: (i, j))] + host.out_specs,
        out_shape=[jax.ShapeDtypeStruct((t, n), BF16)] + host.out_shape,
        scratch_shapes=[pltpu.VMEM((tm, d), BF16)] + host.scratch,
        compiler_params=_cparams(("arbitrary", "arbitrary")),
    )(x, w, *host.operands)


def _mixer_fwd(h, x, caw, cab, cbw, cbb, lbg, lbb, pw, pb, ps, w_out, lng, lnb, name, plan=None, target=None):
    t = h.shape[0]
    tt = TT
    n = t // tt
    with_loss = target is not None

    def body(h_ref, x_ref, *refs):
        if with_loss:
            t_ref, refs = refs[0], refs[1:]
        (caw_ref, cab_ref, cbw_ref, cbb_ref, lbg_ref, lbb_ref, pw_ref, pb_ref, ps_ref, wo_ref, lng_ref, lnb_ref,
         y_ref, u2_ref, z_ref, xn_ref) = refs[:16]
        refs = refs[16:]
        if with_loss:
            l_ref, refs = refs[0], refs[1:]
        exta, extb, extc, lasta, lastb, lastc, tmp, inv_ref = refs
        i = pl.program_id(0)

        @pl.when(i == 0)
        def _():
            for e in (lasta, lastb, lastc):
                e[...] = jnp.zeros_like(e)
            if with_loss:
                l_ref[...] = jnp.zeros_like(l_ref)

        @pl.when(i <= 1)
        def _():
            inv_ref[...] = _inv_count(i)

        group = lambda k: h_ref[:, k * W_MIX:(k + 1) * W_MIX].astype(F32)

        def keep(name, v):
            k = KEPT.index(name)
            u2_ref[:, k * W_MIX:(k + 1) * W_MIX] = v

        def look_back(ext, last, cur):
            _look_back(ext, cur, last[...])
            last[...] = _last_segment(cur)

        sgg = _sigmoid(group(5))
        keep("sigmoid(b_g)", sgg)
        look_back(extb, lastb, group(4) * sgg)
        _conv_rows(extb, cbw_ref, tmp, nk=KB, off=SEG - (KB - 1), reverse=False)
        u2 = tmp[...] + cbb_ref[...]
        keep("u2", u2)
        ln, _, _ = _layer_norm(u2, lbg_ref[...], lbb_ref[...])
        b_z = group(6)
        sl, sz = _sigmoid(ln), _sigmoid(b_z)
        keep("sigmoid(ln)", sl), keep("sigmoid(b_z)", sz)
        y_ref[:, W_MIX:2 * W_MIX] = ((ln * sl) * (b_z * sz)).astype(BF16)

        look_back(exta, lasta, group(1) * group(2))
        _conv_rows(exta, caw_ref, tmp, nk=KA, off=SEG - (KA - 1), reverse=False)
        ca = tmp[...] + cab_ref[...]
        a_z = group(3)
        sga = _sigmoid(a_z)
        keep("ca", ca), keep("sigmoid(a_z)", sga)
        y_ref[:, 0:W_MIX] = (group(0) * ca * (a_z * sga)).astype(BF16)

        c_u = group(7)
        look_back(extc, lastc, c_u)
        _window_sums(extc, tmp, forward=False)
        pooled = tmp[...] * inv_ref[...] - c_u
        p = _pool_linear(pooled, pw_ref, pb_ref)
        c_z = group(8)
        sc = _sigmoid(c_z)
        keep("pooled", pooled), keep("p", p), keep("sigmoid(c_z)", sc)
        y_ref[:, 2 * W_MIX:3 * W_MIX] = (p * ps_ref[...] * (c_z * sc)).astype(BF16)

        out = _dot(y_ref[...], wo_ref[...])
        z = ALPHA * x_ref[...] + out
        z_ref[...] = z
        xn, _, _ = _layer_norm(z, lng_ref[...], lnb_ref[...])
        if with_loss:
            e = xn - t_ref[...]
            xn_ref[...] = e * (1.0 / D_MODEL)
            l_ref[...] += _fold8(e * e) * (0.5 / D_MODEL)
        else:
            xn_ref[...] = xn

    def phase():
        i = pl.program_id(0)
        return i == 0, i == n // 2, i == n - 1

    row = lambda wd: pl.BlockSpec((tt, wd), lambda i: (i, 0))
    full = lambda a: pl.BlockSpec(a.shape, lambda i: (0,) * a.ndim)
    params = (caw, cab, cbw, cbb, lbg, lbb, pw, pb, ps, w_out, lng, lnb)
    extra_in = [target] if with_loss else []
    extra_out = [jax.ShapeDtypeStruct((SUB, D_MODEL), F32)] if with_loss else []
    host = _Hosted(plan, n_in=2 + len(extra_in) + len(params), n_out=4 + len(extra_out), n_scratch=8)
    return pl.pallas_call(
        host.wrap(body, phase), name=name, grid=(n,),
        in_specs=[row(D_IN), row(D_MODEL)] + [row(D_MODEL)] * len(extra_in) + [full(a) for a in params]
        + host.in_specs,
        out_specs=[row(D_MIX), row(len(KEPT) * W_MIX), row(D_MODEL), row(D_MODEL)] + [full(o) for o in extra_out]
        + host.out_specs,
        out_shape=[jax.ShapeDtypeStruct((t, D_MIX), BF16), jax.ShapeDtypeStruct((t, len(KEPT) * W_MIX), F32),
                   jax.ShapeDtypeStruct((t, D_MODEL), F32), jax.ShapeDtypeStruct((t, D_MODEL), F32)]
        + extra_out + host.out_shape,
        scratch_shapes=[pltpu.VMEM((2 * tt, W_MIX), F32)] * 3 + [pltpu.VMEM((tt, W_MIX), F32)] * 5 + host.scratch,
        compiler_params=_cparams(("arbitrary",)),
    )(h, x, *extra_in, *params, *host.operands)


def _out_proj_bwd(g_xn, z, y, w_out, lng, name):
    t = z.shape[0]
    tt = min(t, 512)
    n = t // tt

    def body(g_ref, z_ref, y_ref, wo_ref, lng_ref, gz_ref, gy_ref, gwo_ref, gln_ref, accg, accb, accw, gzb):
        i = pl.program_id(0)

        @pl.when(i == 0)
        def _():
            accw[...] = jnp.zeros_like(accw)
            accg[...] = jnp.zeros_like(accg)
            accb[...] = jnp.zeros_like(accb)
            gzb[...] = jnp.zeros_like(gzb)

        before = gzb[(i + 1) % 2]
        gy_ref[...] = _dot_nt(before, wo_ref[...])
        accw[...] += _dot_tn(y_ref[...], before)

        counts = (i < n).astype(F32)
        g = g_ref[...]
        _, zhat, rstd = _layer_norm(z_ref[...], lng_ref[...], 0.0)
        accg[...] += _fold8(g * zhat) * counts
        accb[...] += _fold8(g) * counts
        g_z = _layer_norm_bwd(g, zhat, rstd, lng_ref[...])
        gz_ref[...] = g_z
        gzb[i % 2] = g_z.astype(BF16)

        @pl.when(i == n)
        def _():
            gwo_ref[...] = accw[...].astype(BF16)
            gln_ref[...] = jnp.zeros_like(gln_ref)
            gln_ref[0:1, :] = jnp.sum(accg[...], axis=0, keepdims=True)
            gln_ref[1:2, :] = jnp.sum(accb[...], axis=0, keepdims=True)

    this = lambda wd: pl.BlockSpec((tt, wd), lambda i: (jnp.minimum(i, n - 1), 0))
    last = lambda wd: pl.BlockSpec((tt, wd), lambda i: (jnp.maximum(i - 1, 0), 0))
    full = lambda shape: pl.BlockSpec(shape, lambda i: (0,) * len(shape))
    return pl.pallas_call(
        body, name=name, grid=(n + 1,),
        in_specs=[this(D_MODEL), this(D_MODEL), last(D_MIX), full(w_out.shape), full(lng.shape)],
        out_specs=[this(D_MODEL), last(D_MIX), full((D_MIX, D_MODEL)), full((SUB, D_MODEL))],
        out_shape=[jax.ShapeDtypeStruct((t, D_MODEL), F32), jax.ShapeDtypeStruct((t, D_MIX), F32),
                   jax.ShapeDtypeStruct((D_MIX, D_MODEL), BF16), jax.ShapeDtypeStruct((SUB, D_MODEL), F32)],
        scratch_shapes=[pltpu.VMEM((SUB, D_MODEL), F32)] * 2 + [pltpu.VMEM((D_MIX, D_MODEL), F32),
                                                                 pltpu.VMEM((2, tt, D_MODEL), BF16)],
        compiler_params=_cparams(("arbitrary",)),
    )(g_xn, z, y, w_out, lng)


def _mixer_bwd(h, u2, g_y, caw, cbw, lbg, lbb, pw, ps, name, plan=None):
    t = h.shape[0]
    tt = TT
    n = t // tt
    before_groups = (1, 2, 4, 5)

    def body(h_ref, p_cg, p_av, p_bv, p_bg, u2_ref, gy_ref, caw_ref, cbw_ref, lbg_ref, lbb_ref,
             pw_ref, ps_ref, gh_ref, rows_ref, gpw_ref,
             exta, extb, gca, gu2, qx, nexta, nextb, nextc, tmp, tmp2, inv_ref, acc):
        s = pl.program_id(0)
        i = n - 1 - s

        @pl.when(s == 0)
        def _():
            acc[...] = jnp.zeros_like(acc)
            gpw_ref[...] = jnp.zeros_like(gpw_ref)
            for e in (nexta, nextb, nextc):
                e[...] = jnp.zeros_like(e)

        live = (i > 0).astype(F32)
        f32 = lambda ref: ref[...].astype(F32)
        before_a = _last_segment(f32(p_cg) * f32(p_av)) * live
        before_b = _last_segment(f32(p_bv) * _sigmoid(f32(p_bg))) * live

        a_bg, a_cg, a_v, a_z, b_v, b_g, b_z, c_u, c_z = _groups(h_ref)
        u2, sg, ca, sgg, sl, sz, sc, pooled, p = (u2_ref[:, k * W_MIX:(k + 1) * W_MIX] for k in range(len(KEPT)))
        g_ya = gy_ref[:, 0:W_MIX]
        g_yb = gy_ref[:, W_MIX:2 * W_MIX]
        g_yc = gy_ref[:, 2 * W_MIX:3 * W_MIX]

        def add_row(r, v):
            acc[r * SUB:(r + 1) * SUB, :] += _fold8(v)

        _look_back(exta, a_cg * a_v, before_a)
        s_az = a_z * sg
        t_a = g_ya * a_bg
        gh_ref[:, 0:W_MIX] = (g_ya * ca * s_az).astype(BF16)
        gh_ref[:, 3 * W_MIX:4 * W_MIX] = (t_a * ca * (sg * (1.0 + a_z * (1.0 - sg)))).astype(BF16)
        g_ca = t_a * s_az
        _look_ahead(gca, g_ca, nexta[...])
        nexta[...] = _first_segment(g_ca)
        add_row(R_CAB, g_ca)
        _conv_wgrad(gca, exta, acc, nk=KA, off=SEG - (KA - 1), row0=R_CAW)
        _conv_rows(gca, caw_ref, tmp, nk=KA, off=0, reverse=True)
        g_pa = tmp[...]
        gh_ref[:, W_MIX:2 * W_MIX] = (g_pa * a_v).astype(BF16)
        gh_ref[:, 2 * W_MIX:3 * W_MIX] = (g_pa * a_cg).astype(BF16)

        _look_back(extb, b_v * sgg, before_b)
        ln, u2hat, rstd = _layer_norm(u2, lbg_ref[...], lbb_ref[...])
        u3 = ln * sl
        s_bz = b_z * sz
        gh_ref[:, 6 * W_MIX:7 * W_MIX] = (g_yb * u3 * (sz * (1.0 + b_z * (1.0 - sz)))).astype(BF16)
        g_ln = g_yb * s_bz * (sl * (1.0 + ln * (1.0 - sl)))
        add_row(R_LBG, g_ln * u2hat)
        add_row(R_LBB, g_ln)
        g_u2 = _layer_norm_bwd(g_ln, u2hat, rstd, lbg_ref[...])
        _look_ahead(gu2, g_u2, nextb[...])
        nextb[...] = _first_segment(g_u2)
        add_row(R_CBB, g_u2)
        _conv_wgrad(gu2, extb, acc, nk=KB, off=SEG - (KB - 1), row0=R_CBW)
        _conv_rows(gu2, cbw_ref, tmp, nk=KB, off=0, reverse=True)
        g_u1 = tmp[...]
        gh_ref[:, 4 * W_MIX:5 * W_MIX] = (g_u1 * sgg).astype(BF16)
        gh_ref[:, 5 * W_MIX:6 * W_MIX] = (g_u1 * b_v * sgg * (1.0 - sgg)).astype(BF16)

        @pl.when((s == 0) | (i == 0))
        def _():
            inv_ref[...] = _inv_count(i)
        inv = inv_ref[...]
        s_cz = c_z * sc
        scale = ps_ref[...]
        gh_ref[:, 8 * W_MIX:9 * W_MIX] = (g_yc * p * scale * (sc * (1.0 + c_z * (1.0 - sc)))).astype(BF16)
        t_c = g_yc * s_cz
        add_row(R_PS, t_c * p)
        g_p = t_c * scale
        add_row(R_PB, g_p)
        g_pooled = []
        for g in range(len(POOL_WINDOWS)):
            lanes = slice(g * POOL_DIM, (g + 1) * POOL_DIM)
            gpg = g_p[:, lanes].astype(BF16)
            gpw_ref[g] += _dot_tn(pooled[:, lanes].astype(BF16), gpg)
            g_pooled.append(_dot_nt(gpg, pw_ref[g].astype(BF16)))
        g_pooled = jnp.concatenate(g_pooled, axis=1)
        q = g_pooled * inv
        _look_ahead(qx, q, nextc[...])
        nextc[...] = _first_segment(q)
        _window_sums(qx, tmp2, forward=True)
        gh_ref[:, 7 * W_MIX:8 * W_MIX] = (tmp2[...] - g_pooled).astype(BF16)

        @pl.when(s == n - 1)
        def _():
            for r in range(N_ROWS):
                rows_ref[r:r + 1, :] = jnp.sum(acc[r * SUB:(r + 1) * SUB, :], axis=0, keepdims=True)

    def phase():
        s = pl.program_id(0)
        return s == 0, s == n // 2, s == n - 1

    row = lambda wd: pl.BlockSpec((tt, wd), lambda s: (n - 1 - s, 0))
    before = [pl.BlockSpec((tt, W_MIX), lambda s, k=k: (jnp.maximum(n - 2 - s, 0), k)) for k in before_groups]
    full = lambda shape: pl.BlockSpec(shape, lambda s: (0,) * len(shape))
    params = (caw, cbw, lbg, lbb, pw, ps)
    host = _Hosted(plan, n_in=3 + len(before) + len(params), n_out=3, n_scratch=12)
    return pl.pallas_call(
        host.wrap(body, phase), name=name, grid=(n,),
        in_specs=[row(D_IN)] + before + [row(len(KEPT) * W_MIX), row(D_MIX)] + [full(a.shape) for a in params]
        + host.in_specs,
        out_specs=[row(D_IN), full((N_ROWS, W_MIX)), full(pw.shape)] + host.out_specs,
        out_shape=[jax.ShapeDtypeStruct((t, D_IN), BF16), jax.ShapeDtypeStruct((N_ROWS, W_MIX), F32),
                   jax.ShapeDtypeStruct(pw.shape, F32)] + host.out_shape,
        scratch_shapes=[pltpu.VMEM((2 * tt, W_MIX), F32)] * 5 + [pltpu.VMEM((tt, W_MIX), F32)] * 6
        + [pltpu.VMEM((N_ROWS * SUB, W_MIX), F32)] + host.scratch,
        compiler_params=_cparams(("arbitrary",)),
    )(h, *([h] * len(before)), u2, g_y, *params, *host.operands)


def _in_proj_wgrad(x, g_h, name, plan=None):
    t, d = x.shape
    n = g_h.shape[1]
    tk, tn = min(t, 1024), n // 2
    nk = t // tk

    def body(x_ref, g_ref, o_ref, acc):
        k = pl.program_id(1)

        @pl.when(k == 0)
        def _():
            acc[...] = jnp.zeros_like(acc)
        acc[...] += _dot_tn(x_ref[...].astype(BF16), g_ref[...])

        @pl.when(k == nk - 1)
        def _():
            o_ref[...] = acc[...].T.astype(BF16)

    def phase():
        step = pl.program_id(0) * nk + pl.program_id(1)
        return step == 0, step == nk, step == 2 * nk - 1

    host = _Hosted(plan, n_in=2, n_out=1, n_scratch=1)
    return pl.pallas_call(
        host.wrap(body, phase), name=name, grid=(n // tn, nk),
        in_specs=[pl.BlockSpec((tk, d), lambda j, k: (k, 0)), pl.BlockSpec((tk, tn), lambda j, k: (k, j))]
        + host.in_specs,
        out_specs=[pl.BlockSpec((tn, d), lambda j, k: (j, 0))] + host.out_specs,
        out_shape=[jax.ShapeDtypeStruct((n, d), BF16)] + host.out_shape,
        scratch_shapes=[pltpu.VMEM((d, tn), F32)] + host.scratch,
        compiler_params=_cparams(("arbitrary", "arbitrary")),
    )(x, g_h, *host.operands)


def _in_proj_dgrad(g_h, w, g_z, name, plan=None):
    t, n = g_h.shape
    d = w.shape[1]
    tm, tk = min(t, 1024), 2304
    nm, nk = t // tm, n // tk

    def body(g_ref, w_ref, gz_ref, o_ref):
        @pl.when(pl.program_id(1) == 0)
        def _():
            o_ref[...] = ALPHA * gz_ref[...]
        o_ref[...] += _dot(g_ref[...], w_ref[...])

    def phase():
        step = pl.program_id(0) * nk + pl.program_id(1)
        return step == 0, step == (nm * nk) // 2, step == nm * nk - 1

    host = _Hosted(plan, n_in=3, n_out=1, n_scratch=0)
    return pl.pallas_call(
        host.wrap(body, phase), name=name, grid=(nm, nk),
        in_specs=[pl.BlockSpec((tm, tk), lambda i, k: (i, k)), pl.BlockSpec((tk, d), lambda i, k: (k, 0)),
                  pl.BlockSpec((tm, d), lambda i, k: (i, 0))] + host.in_specs,
        out_specs=[pl.BlockSpec((tm, d), lambda i, k: (i, 0))] + host.out_specs,
        out_shape=[jax.ShapeDtypeStruct((t, d), F32)] + host.out_shape,
        scratch_shapes=host.scratch,
        compiler_params=_cparams(("arbitrary", "arbitrary")),
    )(g_h, w, g_z, *host.operands)


BC1 = 1.0 - ADAM_B1 ** ADAM_STEP
BC2 = 1.0 - ADAM_B2 ** ADAM_STEP


def _adamw_math(g, w, m, v):
    nm = ADAM_B1 * m + (1.0 - ADAM_B1) * g
    nv = ADAM_B2 * v + (1.0 - ADAM_B2) * (g * g)
    delta = -ADAM_LR * ((nm / BC1) / (jnp.sqrt(nv / BC2) + ADAM_EPS) + ADAM_WD * w)
    return delta, nm, nv


def _total(ref):
    g = ref[0].astype(F32)
    for k in range(1, ref.shape[0]):
        g = g + ref[k].astype(F32)
    return g


def _adamw_layers(parts, w, m, v, name):
    depth, r, c = w.shape
    p = parts[0].shape[0]
    tr = _row_tile(r)
    nr = r // tr

    def body(*refs):
        p_refs = refs[:depth]
        w_ref, m_ref, v_ref, g_ref, d_ref, nm_ref, nv_ref = refs[depth:]
        for l in range(depth):
            @pl.when(pl.program_id(0) == l)
            def _(l=l):
                g = _total(p_refs[l])
                delta, nm, nv = _adamw_math(g, w_ref[0], m_ref[0], v_ref[0])
                g_ref[0], d_ref[0], nm_ref[0], nv_ref[0] = g, delta, nm, nv

    def part_spec(l):
        return pl.BlockSpec((p, tr, c), lambda li, i: (0, jnp.where(li == l, i, jnp.where(li < l, 0, nr - 1)), 0))

    blk = pl.BlockSpec((1, tr, c), lambda li, i: (li, i, 0))
    out = jax.ShapeDtypeStruct((depth, r, c), F32)
    return pl.pallas_call(
        body, name=name, grid=(depth, nr),
        in_specs=[part_spec(l) for l in range(depth)] + [blk] * 3,
        out_specs=[blk] * 4, out_shape=[out] * 4,
        compiler_params=_cparams(("arbitrary", "arbitrary")),
    )(*parts, w, m, v)


SMALL_PARAMS = ("conv_a_w", "conv_a_b", "conv_b_w", "conv_b_b", "ln_b_g", "ln_b_b", "pool_w", "pool_b", "pool_scale",
                "ln_g", "ln_b")


def _adamw_small(rows_parts, gln_parts, gpw_parts, loss_parts, params, name):
    depth = len(rows_parts)
    cs = params["conv_a_w"][0].shape[2]
    operands = [*rows_parts, *gln_parts, *gpw_parts, loss_parts] + [a for n in SMALL_PARAMS for a in params[n]]
    n_in = len(operands)
    out_shape = [jax.ShapeDtypeStruct((1, 1), F32)]
    out_shape += [jax.ShapeDtypeStruct(params[n][0].shape, F32) for n in SMALL_PARAMS for _ in range(4)]

    def body(*refs):
        rows_p, gln_p, gpw_p = refs[:depth], refs[depth:2 * depth], refs[2 * depth:3 * depth]
        loss_p = refs[3 * depth]
        prm, outs = refs[3 * depth + 1:n_in], refs[n_in + 1:]
        refs[n_in][...] = jnp.sum(_total(loss_p)).reshape(1, 1)
        x, y, c = _position()
        to_front = (W_MIX - (4 * x + 2 * y + c) * cs) % W_MIX

        def update(name, g, at):
            k = SMALL_PARAMS.index(name)
            w_ref, m_ref, v_ref = prm[3 * k:3 * k + 3]
            g_ref, d_ref, nm_ref, nv_ref = outs[4 * k:4 * k + 4]
            delta, nm, nv = _adamw_math(g, w_ref[at], m_ref[at], v_ref[at])
            g_ref[at], d_ref[at], nm_ref[at], nv_ref[at] = g, delta, nm, nv

        for l in range(depth):
            rows = _total(rows_p[l])
            mine = pltpu.roll(rows, to_front, axis=1)
            gln = _total(gln_p[l])
            one = (slice(l, l + 1), slice(None))
            for name, r in (("conv_a_b", R_CAB), ("conv_b_b", R_CBB), ("ln_b_g", R_LBG), ("ln_b_b", R_LBB),
                            ("pool_scale", R_PS)):
                update(name, rows[r:r + 1, :], one)
            for g in range(len(POOL_WINDOWS)):
                update("pool_b", rows[R_PB:R_PB + 1, g * POOL_DIM:(g + 1) * POOL_DIM], (l, slice(g, g + 1), slice(None)))
            update("ln_g", gln[0:1, :], one)
            update("ln_b", gln[1:2, :], one)
            update("conv_a_w", mine[R_CAW:R_CAW + KA, 0:cs], (slice(None), l, slice(None)))
            update("conv_b_w", mine[R_CBW:R_CBW + KB, 0:cs], (slice(None), l, slice(None)))
            update("pool_w", _total(gpw_p[l]), (l,))

    vmem = pl.BlockSpec(memory_space=pltpu.VMEM)
    return pl.pallas_call(
        body, name=name, in_specs=[vmem] * n_in, out_specs=[vmem] * len(out_shape), out_shape=out_shape,
        compiler_params=pltpu.CompilerParams(vmem_limit_bytes=VMEM_LIMIT),
    )(*operands)


def kernel(x, w_in, conv_a_w, conv_a_b, conv_b_w, conv_b_b, ln_b_g, ln_b_b, pool_w, pool_b, pool_scale, w_out, ln_g, ln_b, loss_target, m_w_in, m_conv_a_w, m_conv_a_b, m_conv_b_w, m_conv_b_b, m_ln_b_g, m_ln_b_b, m_pool_w, m_pool_b, m_pool_scale, m_w_out, m_ln_g, m_ln_b, v_w_in, v_conv_a_w, v_conv_a_b, v_conv_b_w, v_conv_b_b, v_ln_b_g, v_ln_b_b, v_pool_w, v_pool_b, v_pool_scale, v_w_out, v_ln_g, v_ln_b):
    depth = w_in.shape[0]
    x0 = _to_segments(x[0])
    target = _to_segments(loss_target[0])
    r2 = lambda a: a.reshape(1, -1)

    tr = lambda a: jnp.swapaxes(a, 1, 2)
    w_in_t, m_w_in_t, v_w_in_t = tr(w_in), tr(m_w_in), tr(v_w_in)
    w_in_b, w_out_b = w_in_t.astype(BF16), w_out.astype(BF16)
    taps = lambda a: jnp.swapaxes(a, 0, 1)
    conv_sh = jnp.concatenate([taps(conv_a_w), taps(conv_b_w)], axis=0)
    full_in = lambda g: g.reshape(D_IN, D_MODEL)
    full_out = lambda g: g.reshape(D_MIX, D_MODEL)
    w_in_f = [full_in(_run_plan(_Gather([w_in_b[0]]), "gather_w_in_0")[0])]
    w_out_f = []

    xs, hs, ys, u2s, zs = [x0], [], [], [], []
    for l in range(depth):
        h, *got = _in_proj(xs[l], w_in_f[l], f"in_proj_{l}",
                           plan=_Gather([w_out_b[0], conv_sh]) if l == 0 else None)
        if got:
            w_out_f.append(full_out(got[0]))
            conv_f = got[1].transpose(2, 1, 0, 3).reshape(depth, KA + KB, W_MIX)
            caw_f, cbw_f = conv_f[:, :KA], conv_f[:, KA:]
        last = l + 1 == depth
        y, u2, z, xn, *got = _mixer_fwd(
            h, xs[l], caw_f[l], r2(conv_a_b[l]), cbw_f[l], r2(conv_b_b[l]), r2(ln_b_g[l]), r2(ln_b_b[l]),
            pool_w[l], r2(pool_b[l]), r2(pool_scale[l]), w_out_f[l], r2(ln_g[l]), r2(ln_b[l]), f"mixer_fwd_{l}",
            plan=None if last else _Gather([w_in_b[l + 1], w_out_b[l + 1]]), target=target if last else None)
        if last:
            g, loss_rows = xn, got[0]
        else:
            w_in_f.append(full_in(got[0])), w_out_f.append(full_out(got[1]))
        hs.append(h), ys.append(y), u2s.append(u2), zs.append(z), xs.append(xn)


    gi_parts, go_parts, g_rows, g_lns, g_pool_w = ([None] * depth for _ in range(5))
    waiting = []
    for l in reversed(range(depth)):
        g_z, g_y, g_w_out, g_lns[l] = _out_proj_bwd(g, zs[l], ys[l], w_out_f[l], r2(ln_g[l]), f"out_proj_bwd_{l}")
        go = g_w_out.reshape(N_DEV, D_MIX // N_DEV, D_MODEL)
        if l == 0:
            waiting += [go]
        g_h, g_rows[l], g_pool_w[l], *got = _mixer_bwd(
            hs[l], u2s[l], g_y, caw_f[l], cbw_f[l], r2(ln_b_g[l]), r2(ln_b_b[l]), pool_w[l], r2(pool_scale[l]),
            f"mixer_bwd_{l}",
            plan=_Exchange(waiting) if waiting else None)
        if got:
            if l + 1 < depth:
                gi_parts[l + 1], go_parts[l + 1] = got[0], got[1]
            if l == 0:
                go_parts[0] = got[-1]
        if l > 0:
            gi = _in_proj_wgrad(xs[l], g_h, f"in_proj_wgrad_{l}")[0]
            waiting = [gi.reshape(N_DEV, D_IN // N_DEV, D_MODEL), go]
            g = _in_proj_dgrad(g_h, w_in_f[l], g_z, f"in_proj_dgrad_{l}")[0]
        else:
            small = [*g_rows, *g_lns, *g_pool_w, loss_rows]
            gi, *small_parts = _in_proj_wgrad(xs[0], g_h, "in_proj_wgrad_0", plan=_Gather(small))
            g, gi_parts[0] = _in_proj_dgrad(g_h, w_in_f[0], g_z, "in_proj_dgrad_0",
                                            plan=_Exchange([gi.reshape(N_DEV, D_IN // N_DEV, D_MODEL)]))
    grad_x = _from_segments(g)[None]

    out_in = [tr(a) for a in _adamw_layers(gi_parts, w_in_t, m_w_in_t, v_w_in_t, "adamw_w_in")]
    out_out = _adamw_layers(go_parts, w_out, m_w_out, v_w_out, "adamw_w_out")
    params = dict(
        conv_a_w=(taps(conv_a_w), taps(m_conv_a_w), taps(v_conv_a_w)), conv_a_b=(conv_a_b, m_conv_a_b, v_conv_a_b),
        conv_b_w=(taps(conv_b_w), taps(m_conv_b_w), taps(v_conv_b_w)), conv_b_b=(conv_b_b, m_conv_b_b, v_conv_b_b),
        ln_b_g=(ln_b_g, m_ln_b_g, v_ln_b_g), ln_b_b=(ln_b_b, m_ln_b_b, v_ln_b_b),
        pool_w=(pool_w, m_pool_w, v_pool_w), pool_b=(pool_b, m_pool_b, v_pool_b),
        pool_scale=(pool_scale, m_pool_scale, v_pool_scale), ln_g=(ln_g, m_ln_g, v_ln_g), ln_b=(ln_b, m_ln_b, v_ln_b))
    loss, *small_out = _adamw_small(small_parts[:depth], small_parts[depth:2 * depth], small_parts[2 * depth:3 * depth],
                                    small_parts[3 * depth], params, "adamw_small")
    small_out = {n: small_out[4 * k:4 * k + 4] for k, n in enumerate(SMALL_PARAMS)}
    for n in ("conv_a_w", "conv_b_w"):
        small_out[n] = [taps(a) for a in small_out[n]]

    order = ("w_in", "conv_a_w", "conv_a_b", "conv_b_w", "conv_b_b", "ln_b_g", "ln_b_b", "pool_w", "pool_b",
             "pool_scale", "w_out", "ln_g", "ln_b")
    outs = []
    for k in range(4):
        outs += [out_in[k] if n == "w_in" else out_out[k] if n == "w_out" else small_out[n][k] for n in order]
    return (loss.reshape(()), grad_x, *outs)
```

```python
import jax
import jax.numpy as jnp
from jax import lax
from jax.experimental import pallas as pl
from jax.experimental.pallas import tpu as pltpu

F32 = jnp.float32
BF16 = jnp.bfloat16

DEPTH = 2
D_MODEL = 1024
W_MIX = 512
D_IN = 9 * W_MIX
D_MIX = 3 * W_MIX
POOL_WINDOWS = (2, 4, 8, 16)
POOL_DIM = 128
KA = 3
KB = 31
ALPHA = (2.0 * DEPTH) ** 0.25
LN_EPS = 1e-5
ADAM_LR, ADAM_B1, ADAM_B2, ADAM_EPS, ADAM_WD, ADAM_STEP = 0.001, 0.9, 0.999, 1e-08, 0.01, 10

N_DEV = 8
MESH = pl.DeviceIdType.MESH

CH = 32
SUB = 8
VMEM_LIMIT = 56 * 1024 * 1024

R_CAB, R_CBB, R_LBG, R_LBB, R_PB, R_PS, R_CAW, R_CBW = 0, 1, 2, 3, 4, 5, 6, 9
N_ROWS = R_CBW + KB


def _cparams(sem, **kw):
    return pltpu.CompilerParams(dimension_semantics=sem, vmem_limit_bytes=VMEM_LIMIT, **kw)


def _sigmoid(v):
    return 1.0 / (1.0 + jnp.exp(-v))


def _fold8(a):
    r, c = a.shape
    return a.reshape(r // SUB, SUB, c).sum(axis=0)


def _dot(a, b):
    return jnp.dot(a, b, preferred_element_type=F32)


def _dot_nt(a, b):
    return lax.dot_general(a, b, (((1,), (1,)), ((), ())), preferred_element_type=F32)


def _dot_tn(a, b):
    return lax.dot_general(a, b, (((0,), (0,)), ((), ())), preferred_element_type=F32)


def _row_tile(r, most=192):
    return max(c for c in range(SUB, min(r, most) + 1, SUB) if r % c == 0)


def _position():
    return lax.axis_index("x"), lax.axis_index("y"), lax.axis_index("c")


class _Gather:
    def __init__(self, arrs):
        self.arrs = list(arrs)
        na = len(self.arrs)
        self.out_shape = [jax.ShapeDtypeStruct((N_DEV,) + a.shape, a.dtype) for a in self.arrs]
        self.scratch = [pltpu.SemaphoreType.DMA((na, 7)), pltpu.SemaphoreType.DMA((na, 7)),
                        pltpu.SemaphoreType.DMA((na,))]

    def _copies(self, src, dst, sems):
        send_sems, recv_sems, local_sems = sems
        na = len(self.arrs)
        x, y, c = _position()
        me, sibling = (x, y, c), (x, y, 1 - c)
        chips = [(1 - x, y), (x, 1 - y), (1 - x, 1 - y)]

        def slot(a, dev):
            return dst[a].at[4 * dev[0] + 2 * dev[1] + dev[2]]

        def copy(a, k, block, to, from_src=False):
            return pltpu.make_async_remote_copy(
                src_ref=src[a] if from_src else slot(a, block), dst_ref=slot(a, block),
                send_sem=send_sems.at[a, k], recv_sem=recv_sems.at[a, k], device_id=to, device_id_type=MESH)

        mine = [pltpu.make_async_copy(src[a], slot(a, me), local_sems.at[a]) for a in range(na)]
        first, landed, passed, last = [], [], [], []
        for a in range(na):
            first.append(copy(a, 0, me, sibling, from_src=True))
            first += [copy(a, 1 + j, me, (*chip, c), from_src=True) for j, chip in enumerate(chips)]
        for j, chip in enumerate(chips):
            for a in range(na):
                landed.append(copy(a, 1 + j, (*chip, c), me))
                passed.append(copy(a, 4 + j, (*chip, c), sibling))
        for a in range(na):
            last.append(copy(a, 0, sibling, me))
            last += [copy(a, 4 + j, (*chip, 1 - c), me) for j, chip in enumerate(chips)]
        return mine, first, landed, passed, last

    def start(self, src, dst, sems):
        mine, first, _, _, _ = self._copies(src, dst, sems)
        for cp in mine + first:
            cp.start()

    def pass_on(self, src, dst, sems):
        _, _, landed, passed, _ = self._copies(src, dst, sems)
        for got, cp in zip(landed, passed):
            got.wait_recv()
            cp.start()

    def finish(self, src, dst, sems):
        mine, first, _, passed, last = self._copies(src, dst, sems)
        for cp in last:
            cp.wait_recv()
        for cp in first + passed:
            cp.wait_send()
        for cp in mine:
            cp.wait()


class _Exchange:
    def __init__(self, arrs):
        self.arrs = list(arrs)
        na = len(self.arrs)
        self.out_shape = [jax.ShapeDtypeStruct(a.shape, a.dtype) for a in self.arrs]
        self.scratch = [pltpu.SemaphoreType.DMA((na, N_DEV - 1)), pltpu.SemaphoreType.DMA((na, N_DEV - 1)),
                        pltpu.SemaphoreType.DMA((na,))]

    def _copies(self, src, dst, sems):
        send_sems, recv_sems, local_sems = sems
        na = len(self.arrs)
        x, y, c = _position()
        me = 4 * x + 2 * y + c
        mine = [pltpu.make_async_copy(src[a].at[me], dst[a].at[me], local_sems.at[a]) for a in range(na)]
        copies = []
        for a in range(na):
            for k in range(N_DEV - 1):
                flip = k + 1
                px, py, pc = x ^ (flip >> 2), y ^ ((flip >> 1) & 1), c ^ (flip & 1)
                copies.append(pltpu.make_async_remote_copy(
                    src_ref=src[a].at[4 * px + 2 * py + pc], dst_ref=dst[a].at[me],
                    send_sem=send_sems.at[a, k], recv_sem=recv_sems.at[a, k], device_id=(px, py, pc),
                    device_id_type=MESH))
        return mine, copies

    def start(self, src, dst, sems):
        mine, copies = self._copies(src, dst, sems)
        for cp in mine + copies:
            cp.start()

    def pass_on(self, src, dst, sems):
        pass

    def finish(self, src, dst, sems):
        mine, copies = self._copies(src, dst, sems)
        for cp in copies:
            cp.wait()
        for cp in mine:
            cp.wait()


def _run_plan(plan, name):
    na = len(plan.arrs)
    any_spec = pl.BlockSpec(memory_space=pl.ANY)

    def body(*refs):
        src, dst, sems = refs[:na], refs[na:2 * na], refs[2 * na:]
        plan.start(src, dst, sems)
        plan.pass_on(src, dst, sems)
        plan.finish(src, dst, sems)

    return pl.pallas_call(
        body, name=name, in_specs=[any_spec] * na, out_specs=[any_spec] * na,
        out_shape=plan.out_shape, scratch_shapes=plan.scratch,
    )(*plan.arrs)


class _Hosted:
    def __init__(self, plan, n_in, n_out, n_scratch):
        self.plan, self.n_in, self.n_out, self.n_scratch = plan, n_in, n_out, n_scratch
        any_spec = pl.BlockSpec(memory_space=pl.ANY)
        k = 0 if plan is None else len(plan.arrs)
        self.operands = [] if plan is None else plan.arrs
        self.in_specs = [any_spec] * k
        self.out_specs = [any_spec] * k
        self.out_shape = [] if plan is None else plan.out_shape
        self.scratch = [] if plan is None else plan.scratch

    def wrap(self, body, phase):
        if self.plan is None:
            return body
        plan, k = self.plan, len(self.plan.arrs)
        i0, o0 = self.n_in, self.n_in + k
        o1 = o0 + self.n_out
        s0 = o1 + k
        s1 = s0 + self.n_scratch

        def hosted(*refs):
            src, dst, sems = refs[i0:o0], refs[o1:s0], refs[s1:]
            first, middle, last = phase()
            pl.when(first)(lambda: plan.start(src, dst, sems))
            body(*refs[:i0], *refs[o0:o1], *refs[s0:s1])
            pl.when(middle)(lambda: plan.pass_on(src, dst, sems))
            pl.when(last)(lambda: plan.finish(src, dst, sems))

        return hosted


TT = 256
SEG = TT // SUB


def _to_segments(a):
    t, c = a.shape
    return a.reshape(t // TT, SUB, SEG, c).transpose(0, 2, 1, 3).reshape(t, c)


def _from_segments(a):
    t, c = a.shape
    return a.reshape(t // TT, SEG, SUB, c).transpose(0, 2, 1, 3).reshape(t, c)


def _sublane_is(s):
    return lax.broadcasted_iota(jnp.int32, (TT, W_MIX), 0) % SUB == s


def _look_back(ext_ref, cur, before):
    ext_ref[TT:, :] = cur
    ext_ref[0:TT, :] = jnp.where(_sublane_is(0), before, pltpu.roll(cur, 1, axis=0))


def _last_segment(cur):
    return pltpu.roll(cur, TT - (SUB - 1), axis=0)


def _look_ahead(ext_ref, cur, after):
    ext_ref[0:TT, :] = cur
    ext_ref[TT:, :] = jnp.where(_sublane_is(SUB - 1), after, pltpu.roll(cur, TT - 1, axis=0))


def _first_segment(cur):
    return pltpu.roll(cur, SUB - 1, axis=0)


def _tap_loop(body):
    lax.fori_loop(0, TT // CH, lambda c, carry: body(pl.multiple_of(c * CH, CH), carry), 0)


TAPS = 8
HALVES = tuple(slice(q * 128, (q + 1) * 128) for q in range(W_MIX // 128))


def _windows(ext_ref, r0, off, nk, lanes):
    for k0 in range(0, nk, TAPS):
        n = min(TAPS, nk - k0)
        win = ext_ref[pl.ds(r0 + (off + k0) * SUB, CH + (n - 1) * SUB), lanes]
        for t in range(n):
            yield k0 + t, win[t * SUB:t * SUB + CH, :]


def _conv_rows(ext_ref, w_ref, out_ref, *, nk, off, reverse):
    def body(r0, carry):
        for lanes in HALVES:
            acc = jnp.zeros((CH, lanes.stop - lanes.start), F32)
            for k, rows in _windows(ext_ref, r0, off, nk, lanes):
                kk = nk - 1 - k if reverse else k
                acc = acc + rows * w_ref[kk:kk + 1, lanes]
            out_ref[pl.ds(r0, CH), lanes] = acc
        return carry
    _tap_loop(body)


def _conv_wgrad(g_ref, ext_ref, acc_ref, *, nk, off, row0):
    def body(r0, carry):
        for lanes in HALVES:
            g = g_ref[pl.ds(r0, CH), lanes]
            for k, rows in _windows(ext_ref, r0, off, nk, lanes):
                a = (row0 + k) * SUB
                acc_ref[a:a + SUB, lanes] += _fold8(g * rows)
        return carry
    _tap_loop(body)


def _window_sums(ext_ref, out_ref, *, forward):
    def body(r0, carry):
        for g, w in enumerate(POOL_WINDOWS):
            lanes = slice(g * POOL_DIM, (g + 1) * POOL_DIM)
            first = 0 if forward else SEG - (w - 1)
            win = ext_ref[pl.ds(r0 + first * SUB, CH + (w - 1) * SUB), lanes]
            acc = win[0:CH, :]
            for j in range(1, w):
                acc = acc + win[j * SUB:j * SUB + CH, :]
            out_ref[pl.ds(r0, CH), lanes] = acc
        return carry
    _tap_loop(body)


def _inv_count(tile):
    r = lax.broadcasted_iota(jnp.int32, (TT, POOL_DIM), 0)
    t1 = (tile * TT + (r % SUB) * SEG + r // SUB + 1).astype(F32)
    return jnp.concatenate([1.0 / jnp.minimum(t1, float(w)) for w in POOL_WINDOWS], axis=1)


KEPT = ("u2", "sigmoid(a_z)", "ca", "sigmoid(b_g)", "sigmoid(ln)", "sigmoid(b_z)", "sigmoid(c_z)", "pooled", "p")


def _groups(h_ref):
    return [h_ref[:, k * W_MIX:(k + 1) * W_MIX].astype(F32) for k in range(9)]


def _layer_norm(v, g, b):
    mu = jnp.mean(v, axis=-1, keepdims=True)
    vc = v - mu
    var = jnp.mean(vc * vc, axis=-1, keepdims=True)
    rstd = lax.rsqrt(var + LN_EPS)
    vhat = vc * rstd
    return vhat * g + b, vhat, rstd


def _layer_norm_bwd(g_out, vhat, rstd, g):
    gh = g_out * g
    m1 = jnp.mean(gh, axis=-1, keepdims=True)
    m2 = jnp.mean(gh * vhat, axis=-1, keepdims=True)
    return rstd * (gh - m1 - vhat * m2)


def _pool_linear(pooled, pw_ref, pb_ref):
    outs = []
    for g in range(len(POOL_WINDOWS)):
        lanes = slice(g * POOL_DIM, (g + 1) * POOL_DIM)
        outs.append(_dot(pooled[:, lanes].astype(BF16), pw_ref[g].astype(BF16)))
    return jnp.concatenate(outs, axis=1) + pb_ref[...]


def _in_proj(x, w, name, plan=None):
    t, d = x.shape
    n = w.shape[0]
    tm, tn = min(t, 1024), 2304
    nm, nn = t // tm, n // tn

    def body(x_ref, w_ref, o_ref, xb_ref):
        @pl.when(pl.program_id(1) == 0)
        def _():
            xb_ref[...] = x_ref[...].astype(BF16)
        o_ref[...] = _dot_nt(xb_ref[...], w_ref[...]).astype(BF16)

    def phase():
        step = pl.program_id(0) * nn + pl.program_id(1)
        return step == 0, step == (nm * nn) // 2, step == nm * nn - 1

    host = _Hosted(plan, n_in=2, n_out=1, n_scratch=1)
    return pl.pallas_call(
        host.wrap(body, phase), name=name, grid=(nm, nn),
        in_specs=[pl.BlockSpec((tm, d), lambda i, j: (i, 0)), pl.BlockSpec((tn, d), lambda i, j: (j, 0))]
        + host.in_specs,
        out_specs=[pl.BlockSpec((tm, tn), lambda i, j: (i, j))] + host.out_specs,
        out_shape=[jax.ShapeDtypeStruct((t, n), BF16)] + host.out_shape,
        scratch_shapes=[pltpu.VMEM((tm, d), BF16)] + host.scratch,
        compiler_params=_cparams(("arbitrary", "arbitrary")),
    )(x, w, *host.operands)


def _mixer_fwd(h, x, caw, cab, cbw, cbb, lbg, lbb, pw, pb, ps, w_out, lng, lnb, name, plan=None, target=None):
    t = h.shape[0]
    tt = TT
    n = t // tt
    with_loss = target is not None

    def body(h_ref, x_ref, *refs):
        if with_loss:
            t_ref, refs = refs[0], refs[1:]
        (caw_ref, cab_ref, cbw_ref, cbb_ref, lbg_ref, lbb_ref, pw_ref, pb_ref, ps_ref, wo_ref, lng_ref, lnb_ref,
         y_ref, u2_ref, z_ref, xn_ref) = refs[:16]
        refs = refs[16:]
        if with_loss:
            l_ref, refs = refs[0], refs[1:]
        exta, extb, extc, lasta, lastb, lastc, tmp, inv_ref = refs
        i = pl.program_id(0)

        @pl.when(i == 0)
        def _():
            for e in (lasta, lastb, lastc):
                e[...] = jnp.zeros_like(e)
            if with_loss:
                l_ref[...] = jnp.zeros_like(l_ref)

        @pl.when(i <= 1)
        def _():
            inv_ref[...] = _inv_count(i)

        group = lambda k: h_ref[:, k * W_MIX:(k + 1) * W_MIX].astype(F32)

        def keep(name, v):
            k = KEPT.index(name)
            u2_ref[:, k * W_MIX:(k + 1) * W_MIX] = v

        def look_back(ext, last, cur):
            _look_back(ext, cur, last[...])
            last[...] = _last_segment(cur)

        sgg = _sigmoid(group(5))
        keep("sigmoid(b_g)", sgg)
        look_back(extb, lastb, group(4) * sgg)
        _conv_rows(extb, cbw_ref, tmp, nk=KB, off=SEG - (KB - 1), reverse=False)
        u2 = tmp[...] + cbb_ref[...]
        keep("u2", u2)
        ln, _, _ = _layer_norm(u2, lbg_ref[...], lbb_ref[...])
        b_z = group(6)
        sl, sz = _sigmoid(ln), _sigmoid(b_z)
        keep("sigmoid(ln)", sl), keep("sigmoid(b_z)", sz)
        y_ref[:, W_MIX:2 * W_MIX] = ((ln * sl) * (b_z * sz)).astype(BF16)

        look_back(exta, lasta, group(1) * group(2))
        _conv_rows(exta, caw_ref, tmp, nk=KA, off=SEG - (KA - 1), reverse=False)
        ca = tmp[...] + cab_ref[...]
        a_z = group(3)
        sga = _sigmoid(a_z)
        keep("ca", ca), keep("sigmoid(a_z)", sga)
        y_ref[:, 0:W_MIX] = (group(0) * ca * (a_z * sga)).astype(BF16)

        c_u = group(7)
        look_back(extc, lastc, c_u)
        _window_sums(extc, tmp, forward=False)
        pooled = tmp[...] * inv_ref[...] - c_u
        p = _pool_linear(pooled, pw_ref, pb_ref)
        c_z = group(8)
        sc = _sigmoid(c_z)
        keep("pooled", pooled), keep("p", p), keep("sigmoid(c_z)", sc)
        y_ref[:, 2 * W_MIX:3 * W_MIX] = (p * ps_ref[...] * (c_z * sc)).astype(BF16)

        out = _dot(y_ref[...], wo_ref[...])
        z = ALPHA * x_ref[...] + out
        z_ref[...] = z
        xn, _, _ = _layer_norm(z, lng_ref[...], lnb_ref[...])
        if with_loss:
            e = xn - t_ref[...]
            xn_ref[...] = e * (1.0 / D_MODEL)
            l_ref[...] += _fold8(e * e) * (0.5 / D_MODEL)
        else:
            xn_ref[...] = xn

    def phase():
        i = pl.program_id(0)
        return i == 0, i == n // 2, i == n - 1

    row = lambda wd: pl.BlockSpec((tt, wd), lambda i: (i, 0))
    full = lambda a: pl.BlockSpec(a.shape, lambda i: (0,) * a.ndim)
    params = (caw, cab, cbw, cbb, lbg, lbb, pw, pb, ps, w_out, lng, lnb)
    extra_in = [target] if with_loss else []
    extra_out = [jax.ShapeDtypeStruct((SUB, D_MODEL), F32)] if with_loss else []
    host = _Hosted(plan, n_in=2 + len(extra_in) + len(params), n_out=4 + len(extra_out), n_scratch=8)
    return pl.pallas_call(
        host.wrap(body, phase), name=name, grid=(n,),
        in_specs=[row(D_IN), row(D_MODEL)] + [row(D_MODEL)] * len(extra_in) + [full(a) for a in params]
        + host.in_specs,
        out_specs=[row(D_MIX), row(len(KEPT) * W_MIX), row(D_MODEL), row(D_MODEL)] + [full(o) for o in extra_out]
        + host.out_specs,
        out_shape=[jax.ShapeDtypeStruct((t, D_MIX), BF16), jax.ShapeDtypeStruct((t, len(KEPT) * W_MIX), F32),
                   jax.ShapeDtypeStruct((t, D_MODEL), F32), jax.ShapeDtypeStruct((t, D_MODEL), F32)]
        + extra_out + host.out_shape,
        scratch_shapes=[pltpu.VMEM((2 * tt, W_MIX), F32)] * 3 + [pltpu.VMEM((tt, W_MIX), F32)] * 5 + host.scratch,
        compiler_params=_cparams(("arbitrary",)),
    )(h, x, *extra_in, *params, *host.operands)


def _out_proj_bwd(g_xn, z, y, w_out, lng, name):
    t = z.shape[0]
    tt = min(t, 512)
    n = t // tt

    def body(g_ref, z_ref, y_ref, wo_ref, lng_ref, gz_ref, gy_ref, gwo_ref, gln_ref, accg, accb, accw, gzb):
        i = pl.program_id(0)

        @pl.when(i == 0)
        def _():
            accw[...] = jnp.zeros_like(accw)
            accg[...] = jnp.zeros_like(accg)
            accb[...] = jnp.zeros_like(accb)
            gzb[...] = jnp.zeros_like(gzb)

        before = gzb[(i + 1) % 2]
        gy_ref[...] = _dot_nt(before, wo_ref[...])
        accw[...] += _dot_tn(y_ref[...], before)

        counts = (i < n).astype(F32)
        g = g_ref[...]
        _, zhat, rstd = _layer_norm(z_ref[...], lng_ref[...], 0.0)
        accg[...] += _fold8(g * zhat) * counts
        accb[...] += _fold8(g) * counts
        g_z = _layer_norm_bwd(g, zhat, rstd, lng_ref[...])
        gz_ref[...] = g_z
        gzb[i % 2] = g_z.astype(BF16)

        @pl.when(i == n)
        def _():
            gwo_ref[...] = accw[...].astype(BF16)
            gln_ref[...] = jnp.zeros_like(gln_ref)
            gln_ref[0:1, :] = jnp.sum(accg[...], axis=0, keepdims=True)
            gln_ref[1:2, :] = jnp.sum(accb[...], axis=0, keepdims=True)

    this = lambda wd: pl.BlockSpec((tt, wd), lambda i: (jnp.minimum(i, n - 1), 0))
    last = lambda wd: pl.BlockSpec((tt, wd), lambda i: (jnp.maximum(i - 1, 0), 0))
    full = lambda shape: pl.BlockSpec(shape, lambda i: (0,) * len(shape))
    return pl.pallas_call(
        body, name=name, grid=(n + 1,),
        in_specs=[this(D_MODEL), this(D_MODEL), last(D_MIX), full(w_out.shape), full(lng.shape)],
        out_specs=[this(D_MODEL), last(D_MIX), full((D_MIX, D_MODEL)), full((SUB, D_MODEL))],
        out_shape=[jax.ShapeDtypeStruct((t, D_MODEL), F32), jax.ShapeDtypeStruct((t, D_MIX), F32),
                   jax.ShapeDtypeStruct((D_MIX, D_MODEL), BF16), jax.ShapeDtypeStruct((SUB, D_MODEL), F32)],
        scratch_shapes=[pltpu.VMEM((SUB, D_MODEL), F32)] * 2 + [pltpu.VMEM((D_MIX, D_MODEL), F32),
                                                                 pltpu.VMEM((2, tt, D_MODEL), BF16)],
        compiler_params=_cparams(("arbitrary",)),
    )(g_xn, z, y, w_out, lng)


def _mixer_bwd(h, u2, g_y, caw, cbw, lbg, lbb, pw, ps, name, plan=None):
    t = h.shape[0]
    tt = TT
    n = t // tt
    before_groups = (1, 2, 4, 5)

    def body(h_ref, p_cg, p_av, p_bv, p_bg, u2_ref, gy_ref, caw_ref, cbw_ref, lbg_ref, lbb_ref,
             pw_ref, ps_ref, gh_ref, rows_ref, gpw_ref,
             exta, extb, gca, gu2, qx, nexta, nextb, nextc, tmp, tmp2, inv_ref, acc):
        s = pl.program_id(0)
        i = n - 1 - s

        @pl.when(s == 0)
        def _():
            acc[...] = jnp.zeros_like(acc)
            gpw_ref[...] = jnp.zeros_like(gpw_ref)
            for e in (nexta, nextb, nextc):
                e[...] = jnp.zeros_like(e)

        live = (i > 0).astype(F32)
        f32 = lambda ref: ref[...].astype(F32)
        before_a = _last_segment(f32(p_cg) * f32(p_av)) * live
        before_b = _last_segment(f32(p_bv) * _sigmoid(f32(p_bg))) * live

        a_bg, a_cg, a_v, a_z, b_v, b_g, b_z, c_u, c_z = _groups(h_ref)
        u2, sg, ca, sgg, sl, sz, sc, pooled, p = (u2_ref[:, k * W_MIX:(k + 1) * W_MIX] for k in range(len(KEPT)))
        g_ya = gy_ref[:, 0:W_MIX]
        g_yb = gy_ref[:, W_MIX:2 * W_MIX]
        g_yc = gy_ref[:, 2 * W_MIX:3 * W_MIX]

        def add_row(r, v):
            acc[r * SUB:(r + 1) * SUB, :] += _fold8(v)

        _look_back(exta, a_cg * a_v, before_a)
        s_az = a_z * sg
        t_a = g_ya * a_bg
        gh_ref[:, 0:W_MIX] = (g_ya * ca * s_az).astype(BF16)
        gh_ref[:, 3 * W_MIX:4 * W_MIX] = (t_a * ca * (sg * (1.0 + a_z * (1.0 - sg)))).astype(BF16)
        g_ca = t_a * s_az
        _look_ahead(gca, g_ca, nexta[...])
        nexta[...] = _first_segment(g_ca)
        add_row(R_CAB, g_ca)
        _conv_wgrad(gca, exta, acc, nk=KA, off=SEG - (KA - 1), row0=R_CAW)
        _conv_rows(gca, caw_ref, tmp, nk=KA, off=0, reverse=True)
        g_pa = tmp[...]
        gh_ref[:, W_MIX:2 * W_MIX] = (g_pa * a_v).astype(BF16)
        gh_ref[:, 2 * W_MIX:3 * W_MIX] = (g_pa * a_cg).astype(BF16)

        _look_back(extb, b_v * sgg, before_b)
        ln, u2hat, rstd = _layer_norm(u2, lbg_ref[...], lbb_ref[...])
        u3 = ln * sl
        s_bz = b_z * sz
        gh_ref[:, 6 * W_MIX:7 * W_MIX] = (g_yb * u3 * (sz * (1.0 + b_z * (1.0 - sz)))).astype(BF16)
        g_ln = g_yb * s_bz * (sl * (1.0 + ln * (1.0 - sl)))
        add_row(R_LBG, g_ln * u2hat)
        add_row(R_LBB, g_ln)
        g_u2 = _layer_norm_bwd(g_ln, u2hat, rstd, lbg_ref[...])
        _look_ahead(gu2, g_u2, nextb[...])
        nextb[...] = _first_segment(g_u2)
        add_row(R_CBB, g_u2)
        _conv_wgrad(gu2, extb, acc, nk=KB, off=SEG - (KB - 1), row0=R_CBW)
        _conv_rows(gu2, cbw_ref, tmp, nk=KB, off=0, reverse=True)
        g_u1 = tmp[...]
        gh_ref[:, 4 * W_MIX:5 * W_MIX] = (g_u1 * sgg).astype(BF16)
        gh_ref[:, 5 * W_MIX:6 * W_MIX] = (g_u1 * b_v * sgg * (1.0 - sgg)).astype(BF16)

        @pl.when((s == 0) | (i == 0))
        def _():
            inv_ref[...] = _inv_count(i)
        inv = inv_ref[...]
        s_cz = c_z * sc
        scale = ps_ref[...]
        gh_ref[:, 8 * W_MIX:9 * W_MIX] = (g_yc * p * scale * (sc * (1.0 + c_z * (1.0 - sc)))).astype(BF16)
        t_c = g_yc * s_cz
        add_row(R_PS, t_c * p)
        g_p = t_c * scale
        add_row(R_PB, g_p)
        g_pooled = []
        for g in range(len(POOL_WINDOWS)):
            lanes = slice(g * POOL_DIM, (g + 1) * POOL_DIM)
            gpg = g_p[:, lanes].astype(BF16)
            gpw_ref[g] += _dot_tn(pooled[:, lanes].astype(BF16), gpg)
            g_pooled.append(_dot_nt(gpg, pw_ref[g].astype(BF16)))
        g_pooled = jnp.concatenate(g_pooled, axis=1)
        q = g_pooled * inv
        _look_ahead(qx, q, nextc[...])
        nextc[...] = _first_segment(q)
        _window_sums(qx, tmp2, forward=True)
        gh_ref[:, 7 * W_MIX:8 * W_MIX] = (tmp2[...] - g_pooled).astype(BF16)

        @pl.when(s == n - 1)
        def _():
            for r in range(N_ROWS):
                rows_ref[r:r + 1, :] = jnp.sum(acc[r * SUB:(r + 1) * SUB, :], axis=0, keepdims=True)

    def phase():
        s = pl.program_id(0)
        return s == 0, s == n // 2, s == n - 1

    row = lambda wd: pl.BlockSpec((tt, wd), lambda s: (n - 1 - s, 0))
    before = [pl.BlockSpec((tt, W_MIX), lambda s, k=k: (jnp.maximum(n - 2 - s, 0), k)) for k in before_groups]
    full = lambda shape: pl.BlockSpec(shape, lambda s: (0,) * len(shape))
    params = (caw, cbw, lbg, lbb, pw, ps)
    host = _Hosted(plan, n_in=3 + len(before) + len(params), n_out=3, n_scratch=12)
    return pl.pallas_call(
        host.wrap(body, phase), name=name, grid=(n,),
        in_specs=[row(D_IN)] + before + [row(len(KEPT) * W_MIX), row(D_MIX)] + [full(a.shape) for a in params]
        + host.in_specs,
        out_specs=[row(D_IN), full((N_ROWS, W_MIX)), full(pw.shape)] + host.out_specs,
        out_shape=[jax.ShapeDtypeStruct((t, D_IN), BF16), jax.ShapeDtypeStruct((N_ROWS, W_MIX), F32),
                   jax.ShapeDtypeStruct(pw.shape, F32)] + host.out_shape,
        scratch_shapes=[pltpu.VMEM((2 * tt, W_MIX), F32)] * 5 + [pltpu.VMEM((tt, W_MIX), F32)] * 6
        + [pltpu.VMEM((N_ROWS * SUB, W_MIX), F32)] + host.scratch,
        compiler_params=_cparams(("arbitrary",)),
    )(h, *([h] * len(before)), u2, g_y, *params, *host.operands)


def _in_proj_wgrad(x, g_h, name, plan=None):
    t, d = x.shape
    n = g_h.shape[1]
    tk, tn = min(t, 1024), n // 2
    nk = t // tk

    def body(x_ref, g_ref, o_ref, acc):
        k = pl.program_id(1)

        @pl.when(k == 0)
        def _():
            acc[...] = jnp.zeros_like(acc)
        acc[...] += _dot_tn(x_ref[...].astype(BF16), g_ref[...])

        @pl.when(k == nk - 1)
        def _():
            o_ref[...] = acc[...].T.astype(BF16)

    def phase():
        step = pl.program_id(0) * nk + pl.program_id(1)
        return step == 0, step == nk, step == 2 * nk - 1

    host = _Hosted(plan, n_in=2, n_out=1, n_scratch=1)
    return pl.pallas_call(
        host.wrap(body, phase), name=name, grid=(n // tn, nk),
        in_specs=[pl.BlockSpec((tk, d), lambda j, k: (k, 0)), pl.BlockSpec((tk, tn), lambda j, k: (k, j))]
        + host.in_specs,
        out_specs=[pl.BlockSpec((tn, d), lambda j, k: (j, 0))] + host.out_specs,
        out_shape=[jax.ShapeDtypeStruct((n, d), BF16)] + host.out_shape,
        scratch_shapes=[pltpu.VMEM((d, tn), F32)] + host.scratch,
        compiler_params=_cparams(("arbitrary", "arbitrary")),
    )(x, g_h, *host.operands)


def _in_proj_dgrad(g_h, w, g_z, name, plan=None):
    t, n = g_h.shape
    d = w.shape[1]
    tm, tk = min(t, 1024), 2304
    nm, nk = t // tm, n // tk

    def body(g_ref, w_ref, gz_ref, o_ref):
        @pl.when(pl.program_id(1) == 0)
        def _():
            o_ref[...] = ALPHA * gz_ref[...]
        o_ref[...] += _dot(g_ref[...], w_ref[...])

    def phase():
        step = pl.program_id(0) * nk + pl.program_id(1)
        return step == 0, step == (nm * nk) // 2, step == nm * nk - 1

    host = _Hosted(plan, n_in=3, n_out=1, n_scratch=0)
    return pl.pallas_call(
        host.wrap(body, phase), name=name, grid=(nm, nk),
        in_specs=[pl.BlockSpec((tm, tk), lambda i, k: (i, k)), pl.BlockSpec((tk, d), lambda i, k: (k, 0)),
                  pl.BlockSpec((tm, d), lambda i, k: (i, 0))] + host.in_specs,
        out_specs=[pl.BlockSpec((tm, d), lambda i, k: (i, 0))] + host.out_specs,
        out_shape=[jax.ShapeDtypeStruct((t, d), F32)] + host.out_shape,
        scratch_shapes=host.scratch,
        compiler_params=_cparams(("arbitrary", "arbitrary")),
    )(g_h, w, g_z, *host.operands)


BC1 = 1.0 - ADAM_B1 ** ADAM_STEP
BC2 = 1.0 - ADAM_B2 ** ADAM_STEP


def _adamw_math(g, w, m, v):
    nm = ADAM_B1 * m + (1.0 - ADAM_B1) * g
    nv = ADAM_B2 * v + (1.0 - ADAM_B2) * (g * g)
    delta = -ADAM_LR * ((nm / BC1) / (jnp.sqrt(nv / BC2) + ADAM_EPS) + ADAM_WD * w)
    return delta, nm, nv


def _total(ref):
    g = ref[0].astype(F32)
    for k in range(1, ref.shape[0]):
        g = g + ref[k].astype(F32)
    return g


def _adamw_layers(parts, w, m, v, name):
    depth, r, c = w.shape
    p = parts[0].shape[0]
    tr = _row_tile(r)
    nr = r // tr

    def body(*refs):
        p_refs = refs[:depth]
        w_ref, m_ref, v_ref, g_ref, d_ref, nm_ref, nv_ref = refs[depth:]
        for l in range(depth):
            @pl.when(pl.program_id(0) == l)
            def _(l=l):
                g = _total(p_refs[l])
                delta, nm, nv = _adamw_math(g, w_ref[0], m_ref[0], v_ref[0])
                g_ref[0], d_ref[0], nm_ref[0], nv_ref[0] = g, delta, nm, nv

    def part_spec(l):
        return pl.BlockSpec((p, tr, c), lambda li, i: (0, jnp.where(li == l, i, jnp.where(li < l, 0, nr - 1)), 0))

    blk = pl.BlockSpec((1, tr, c), lambda li, i: (li, i, 0))
    out = jax.ShapeDtypeStruct((depth, r, c), F32)
    return pl.pallas_call(
        body, name=name, grid=(depth, nr),
        in_specs=[part_spec(l) for l in range(depth)] + [blk] * 3,
        out_specs=[blk] * 4, out_shape=[out] * 4,
        compiler_params=_cparams(("arbitrary", "arbitrary")),
    )(*parts, w, m, v)


SMALL_PARAMS = ("conv_a_w", "conv_a_b", "conv_b_w", "conv_b_b", "ln_b_g", "ln_b_b", "pool_w", "pool_b", "pool_scale",
                "ln_g", "ln_b")


def _adamw_small(rows_parts, gln_parts, gpw_parts, loss_parts, params, name):
    depth = len(rows_parts)
    cs = params["conv_a_w"][0].shape[2]
    operands = [*rows_parts, *gln_parts, *gpw_parts, loss_parts] + [a for n in SMALL_PARAMS for a in params[n]]
    n_in = len(operands)
    out_shape = [jax.ShapeDtypeStruct((1, 1), F32)]
    out_shape += [jax.ShapeDtypeStruct(params[n][0].shape, F32) for n in SMALL_PARAMS for _ in range(4)]

    def body(*refs):
        rows_p, gln_p, gpw_p = refs[:depth], refs[depth:2 * depth], refs[2 * depth:3 * depth]
        loss_p = refs[3 * depth]
        prm, outs = refs[3 * depth + 1:n_in], refs[n_in + 1:]
        refs[n_in][...] = jnp.sum(_total(loss_p)).reshape(1, 1)
        x, y, c = _position()
        to_front = (W_MIX - (4 * x + 2 * y + c) * cs) % W_MIX

        def update(name, g, at):
            k = SMALL_PARAMS.index(name)
            w_ref, m_ref, v_ref = prm[3 * k:3 * k + 3]
            g_ref, d_ref, nm_ref, nv_ref = outs[4 * k:4 * k + 4]
            delta, nm, nv = _adamw_math(g, w_ref[at], m_ref[at], v_ref[at])
            g_ref[at], d_ref[at], nm_ref[at], nv_ref[at] = g, delta, nm, nv

        for l in range(depth):
            rows = _total(rows_p[l])
            mine = pltpu.roll(rows, to_front, axis=1)
            gln = _total(gln_p[l])
            one = (slice(l, l + 1), slice(None))
            for name, r in (("conv_a_b", R_CAB), ("conv_b_b", R_CBB), ("ln_b_g", R_LBG), ("ln_b_b", R_LBB),
                            ("pool_scale", R_PS)):
                update(name, rows[r:r + 1, :], one)
            for g in range(len(POOL_WINDOWS)):
                update("pool_b", rows[R_PB:R_PB + 1, g * POOL_DIM:(g + 1) * POOL_DIM], (l, slice(g, g + 1), slice(None)))
            update("ln_g", gln[0:1, :], one)
            update("ln_b", gln[1:2, :], one)
            update("conv_a_w", mine[R_CAW:R_CAW + KA, 0:cs], (slice(None), l, slice(None)))
            update("conv_b_w", mine[R_CBW:R_CBW + KB, 0:cs], (slice(None), l, slice(None)))
            update("pool_w", _total(gpw_p[l]), (l,))

    vmem = pl.BlockSpec(memory_space=pltpu.VMEM)
    return pl.pallas_call(
        body, name=name, in_specs=[vmem] * n_in, out_specs=[vmem] * len(out_shape), out_shape=out_shape,
        compiler_params=pltpu.CompilerParams(vmem_limit_bytes=VMEM_LIMIT),
    )(*operands)


def kernel(x, w_in, conv_a_w, conv_a_b, conv_b_w, conv_b_b, ln_b_g, ln_b_b, pool_w, pool_b, pool_scale, w_out, ln_g, ln_b, loss_target, m_w_in, m_conv_a_w, m_conv_a_b, m_conv_b_w, m_conv_b_b, m_ln_b_g, m_ln_b_b, m_pool_w, m_pool_b, m_pool_scale, m_w_out, m_ln_g, m_ln_b, v_w_in, v_conv_a_w, v_conv_a_b, v_conv_b_w, v_conv_b_b, v_ln_b_g, v_ln_b_b, v_pool_w, v_pool_b, v_pool_scale, v_w_out, v_ln_g, v_ln_b):
    depth = w_in.shape[0]
    x0 = _to_segments(x[0])
    target = _to_segments(loss_target[0])
    r2 = lambda a: a.reshape(1, -1)

    tr = lambda a: jnp.swapaxes(a, 1, 2)
    w_in_t, m_w_in_t, v_w_in_t = tr(w_in), tr(m_w_in), tr(v_w_in)
    w_in_b, w_out_b = w_in_t.astype(BF16), w_out.astype(BF16)
    taps = lambda a: jnp.swapaxes(a, 0, 1)
    conv_sh = jnp.concatenate([taps(conv_a_w), taps(conv_b_w)], axis=0)
    full_in = lambda g: g.reshape(D_IN, D_MODEL)
    full_out = lambda g: g.reshape(D_MIX, D_MODEL)
    w_in_f = [full_in(_run_plan(_Gather([w_in_b[0]]), "gather_w_in_0")[0])]
    w_out_f = []

    xs, hs, ys, u2s, zs = [x0], [], [], [], []
    for l in range(depth):
        h, *got = _in_proj(xs[l], w_in_f[l], f"in_proj_{l}",
                           plan=_Gather([w_out_b[0], conv_sh]) if l == 0 else None)
        if got:
            w_out_f.append(full_out(got[0]))
            conv_f = got[1].transpose(2, 1, 0, 3).reshape(depth, KA + KB, W_MIX)
            caw_f, cbw_f = conv_f[:, :KA], conv_f[:, KA:]
        last = l + 1 == depth
        y, u2, z, xn, *got = _mixer_fwd(
            h, xs[l], caw_f[l], r2(conv_a_b[l]), cbw_f[l], r2(conv_b_b[l]), r2(ln_b_g[l]), r2(ln_b_b[l]),
            pool_w[l], r2(pool_b[l]), r2(pool_scale[l]), w_out_f[l], r2(ln_g[l]), r2(ln_b[l]), f"mixer_fwd_{l}",
            plan=None if last else _Gather([w_in_b[l + 1], w_out_b[l + 1]]), target=target if last else None)
        if last:
            g, loss_rows = xn, got[0]
        else:
            w_in_f.append(full_in(got[0])), w_out_f.append(full_out(got[1]))
        hs.append(h), ys.append(y), u2s.append(u2), zs.append(z), xs.append(xn)


    gi_parts, go_parts, g_rows, g_lns, g_pool_w = ([None] * depth for _ in range(5))
    waiting = []
    for l in reversed(range(depth)):
        g_z, g_y, g_w_out, g_lns[l] = _out_proj_bwd(g, zs[l], ys[l], w_out_f[l], r2(ln_g[l]), f"out_proj_bwd_{l}")
        go = g_w_out.reshape(N_DEV, D_MIX // N_DEV, D_MODEL)
        if l == 0:
            waiting += [go]
        g_h, g_rows[l], g_pool_w[l], *got = _mixer_bwd(
            hs[l], u2s[l], g_y, caw_f[l], cbw_f[l], r2(ln_b_g[l]), r2(ln_b_b[l]), pool_w[l], r2(pool_scale[l]),
            f"mixer_bwd_{l}",
            plan=_Exchange(waiting) if waiting else None)
        if got:
            if l + 1 < depth:
                gi_parts[l + 1], go_parts[l + 1] = got[0], got[1]
            if l == 0:
                go_parts[0] = got[-1]
        if l > 0:
            gi = _in_proj_wgrad(xs[l], g_h, f"in_proj_wgrad_{l}")[0]
            waiting = [gi.reshape(N_DEV, D_IN // N_DEV, D_MODEL), go]
            g = _in_proj_dgrad(g_h, w_in_f[l], g_z, f"in_proj_dgrad_{l}")[0]
        else:
            small = [*g_rows, *g_lns, *g_pool_w, loss_rows]
            gi, *small_parts = _in_proj_wgrad(xs[0], g_h, "in_proj_wgrad_0", plan=_Gather(small))
            g, gi_parts[0] = _in_proj_dgrad(g_h, w_in_f[0], g_z, "in_proj_dgrad_0",
                                            plan=_Exchange([gi.reshape(N_DEV, D_IN // N_DEV, D_MODEL)]))
    grad_x = _from_segments(g)[None]

    out_in = [tr(a) for a in _adamw_layers(gi_parts, w_in_t, m_w_in_t, v_w_in_t, "adamw_w_in")]
    out_out = _adamw_layers(go_parts, w_out, m_w_out, v_w_out, "adamw_w_out")
    params = dict(
        conv_a_w=(taps(conv_a_w), taps(m_conv_a_w), taps(v_conv_a_w)), conv_a_b=(conv_a_b, m_conv_a_b, v_conv_a_b),
        conv_b_w=(taps(conv_b_w), taps(m_conv_b_w), taps(v_conv_b_w)), conv_b_b=(conv_b_b, m_conv_b_b, v_conv_b_b),
        ln_b_g=(ln_b_g, m_ln_b_g, v_ln_b_g), ln_b_b=(ln_b_b, m_ln_b_b, v_ln_b_b),
        pool_w=(pool_w, m_pool_w, v_pool_w), pool_b=(pool_b, m_pool_b, v_pool_b),
        pool_scale=(pool_scale, m_pool_scale, v_pool_scale), ln_g=(ln_g, m_ln_g, v_ln_g), ln_b=(ln_b, m_ln_b, v_ln_b))
    loss, *small_out = _adamw_small(small_parts[:depth], small_parts[depth:2 * depth], small_parts[2 * depth:3 * depth],
                                    small_parts[3 * depth], params, "adamw_small")
    small_out = {n: small_out[4 * k:4 * k + 4] for k, n in enumerate(SMALL_PARAMS)}
    for n in ("conv_a_w", "conv_b_w"):
        small_out[n] = [taps(a) for a in small_out[n]]

    order = ("w_in", "conv_a_w", "conv_a_b", "conv_b_w", "conv_b_b", "ln_b_g", "ln_b_b", "pool_w", "pool_b",
             "pool_scale", "w_out", "ln_g", "ln_b")
    outs = []
    for k in range(4):
        outs += [out_in[k] if n == "w_in" else out_out[k] if n == "w_out" else small_out[n][k] for n in order]
    return (loss.reshape(()), grad_x, *outs)
```
